```python
import jax, jax.numpy as jnp
from jax import lax
import numpy as np

D_MODEL = 2048
BATCH = 1
SEQ = 8192
DEPTH = 2

GRID_W = 64
CTX_LEN = 256
N_HEADS = 16
QK_NOPE_DIM = 128
QK_ROPE_DIM = 64
V_HEAD_DIM = 128
Q_LORA_RANK = 512
KV_LORA_RANK = 512
ROPE_THETA = 10000.0
Q_BLOCK = 128
ATTN_SCALE = (QK_NOPE_DIM + QK_ROPE_DIM) ** -0.5
CONV_WIDTH = 2048
CONV_K = 3
D_FF = 5632
N_EXPERTS = 8
TOP_K = 2
D_FF_EXPERT = 2816
MOE_BLOCK = 128
N_DENSE = (DEPTH + 1) // 2
N_MOE = DEPTH // 2
NORM_EPS = 1e-6
IN_SIZES = (Q_LORA_RANK, KV_LORA_RANK, QK_ROPE_DIM, CONV_WIDTH, CONV_WIDTH, CONV_WIDTH, D_MODEL, D_MODEL)
KV_LO = Q_LORA_RANK
KV_HI = Q_LORA_RANK + KV_LORA_RANK + QK_ROPE_DIM
D_IN = Q_LORA_RANK + KV_LORA_RANK + QK_ROPE_DIM + 3 * CONV_WIDTH + 2 * D_MODEL

kernel_name = 'hybrid_mla_shortconv_moe_dit_prefix'


def rmsnorm(x, g):
    xf = x.astype(jnp.float32)
    y = xf * lax.rsqrt(jnp.mean(xf * xf, axis=-1, keepdims=True) + NORM_EPS)
    return (y * g.astype(jnp.float32)).astype(x.dtype)


def modulate(h, shift, scale):
    return h * (1 + scale) + shift


def axial_rope_tables(n_tokens, dtype):
    rows = n_tokens // GRID_W
    row = jnp.repeat(jnp.arange(rows), GRID_W).astype(jnp.float32)
    col = jnp.tile(jnp.arange(GRID_W), rows).astype(jnp.float32)
    half = QK_ROPE_DIM // 2
    inv_freq = jnp.power(ROPE_THETA, -jnp.arange(0, half, 2, dtype=jnp.float32) / half)
    ang_r = row[:, None] * inv_freq
    ang_c = col[:, None] * inv_freq
    return (jnp.cos(ang_r).astype(dtype), jnp.sin(ang_r).astype(dtype),
            jnp.cos(ang_c).astype(dtype), jnp.sin(ang_c).astype(dtype))


def rotate(x, cos, sin):
    x1, x2 = jnp.split(x, 2, axis=-1)
    cos = cos[None, :, None, :]
    sin = sin[None, :, None, :]
    return jnp.concatenate([x1 * cos - x2 * sin, x2 * cos + x1 * sin], axis=-1)


def rope_2d(x, rot):
    if rot is None:
        return x
    cos_r, sin_r, cos_c, sin_c = rot
    x_row, x_col = jnp.split(x, 2, axis=-1)
    return jnp.concatenate([rotate(x_row, cos_r, sin_r), rotate(x_col, cos_c, sin_c)], axis=-1)


def split_in(p):
    parts, o = [], 0
    for s in IN_SIZES:
        parts.append(p[..., o:o + s])
        o += s
    return parts


def mla_query(q_a, g_qa, w_qb, rot):
    B, L, _ = q_a.shape
    q = (rmsnorm(q_a, g_qa) @ w_qb).reshape(B, L, N_HEADS, QK_NOPE_DIM + QK_ROPE_DIM)
    q_nope, q_rope = q[..., :QK_NOPE_DIM], q[..., QK_NOPE_DIM:]
    return jnp.concatenate([q_nope, rope_2d(q_rope, rot)], axis=-1)


def mla_keys_values(c_kv, k_rope, g_kva, w_kvb, rot):
    B, L, _ = c_kv.shape
    kv = (rmsnorm(c_kv, g_kva) @ w_kvb).reshape(B, L, N_HEADS, QK_NOPE_DIM + V_HEAD_DIM)
    k_nope, v = kv[..., :QK_NOPE_DIM], kv[..., QK_NOPE_DIM:]
    k_r = rope_2d(k_rope[:, :, None, :], rot)
    k_r = jnp.broadcast_to(k_r, (B, L, N_HEADS, QK_ROPE_DIM))
    return jnp.concatenate([k_nope, k_r], axis=-1), v


def block_attention(q, k, v):
    B, L, H, dq = q.shape
    nb = L // Q_BLOCK
    qb = q.reshape(B, nb, Q_BLOCK, H, dq).swapaxes(0, 1)

    def one_block(q_blk):
        s = jnp.einsum('bqhd,bkhd->bhqk', q_blk, k, preferred_element_type=jnp.float32) * ATTN_SCALE
        p = jax.nn.softmax(s, axis=-1).astype(v.dtype)
        return jnp.einsum('bhqk,bkhd->bqhd', p, v)

    o = lax.map(one_block, qb)
    return o.swapaxes(0, 1).reshape(B, L, H, v.shape[-1])


def dwconv_centred(u, w):
    pad = CONV_K // 2
    L = u.shape[1]
    up = jnp.pad(u, ((0, 0), (pad, pad), (0, 0)))
    out = w[0] * up[:, 0:L]
    for j in range(1, CONV_K):
        out = out + w[j] * up[:, j:j + L]
    return out


def mixer(p, rot, ctx_kv, lp):
    B, L, _ = p.shape
    q_a, c_kv, k_rope, cx, cb, cc, ga, gb = split_in(p)
    q = mla_query(q_a, lp['g_qa'], lp['w_qb'], rot)
    k, v = mla_keys_values(c_kv, k_rope, lp['g_kva'], lp['w_kvb'], rot)
    if ctx_kv is None:
        k_all, v_all = k, v
    else:
        k_all = jnp.concatenate([ctx_kv[0], k], axis=1)
        v_all = jnp.concatenate([ctx_kv[1], v], axis=1)
    attn = block_attention(q, k_all, v_all)
    o_a = attn.reshape(B, L, N_HEADS * V_HEAD_DIM) @ lp['w_oa']
    o_b = (cb * dwconv_centred(cc * cx, lp['w_conv'])) @ lp['w_ob']
    merged = jax.nn.sigmoid(ga) * o_a + jax.nn.sigmoid(gb) * o_b
    return merged @ lp['w_o'], (k, v)


def swiglu(h, w_gate, w_up, w_down):
    return (jax.nn.silu(h @ w_gate) * (h @ w_up)) @ w_down


def moe_swiglu(h, w_router, w_gate, w_up, w_down):
    N, D = h.shape
    logits = (h @ w_router).astype(jnp.float32)
    top_val, top_idx = lax.top_k(logits, TOP_K)
    gates = jax.nn.softmax(top_val, axis=-1)
    A = N * TOP_K
    e_flat = top_idx.reshape(A)
    tok_flat = jnp.repeat(jnp.arange(N), TOP_K)
    g_flat = gates.reshape(A)
    order = jnp.argsort(e_flat)
    e_s, tok_s, g_s = e_flat[order], tok_flat[order], g_flat[order]
    counts = jnp.bincount(e_flat, length=N_EXPERTS)
    start = jnp.cumsum(counts) - counts
    padded = (counts + MOE_BLOCK - 1) // MOE_BLOCK * MOE_BLOCK
    p_end = jnp.cumsum(padded)
    p_start = p_end - padded
    dest = p_start[e_s] + jnp.arange(A) - start[e_s]
    n_blocks = -(-A // MOE_BLOCK) + N_EXPERTS
    buf = jnp.zeros((n_blocks * MOE_BLOCK, D), h.dtype).at[dest].set(h[tok_s])
    block_expert = jnp.clip(jnp.searchsorted(p_end, jnp.arange(n_blocks) * MOE_BLOCK, side='right'),
                            0, N_EXPERTS - 1)

    def one_block(args):
        xb, e = args
        return swiglu(xb, w_gate[e], w_up[e], w_down[e])

    ybuf = lax.map(one_block, (buf.reshape(n_blocks, MOE_BLOCK, D), block_expert))
    y_s = ybuf.reshape(-1, D)[dest] * g_s[:, None].astype(h.dtype)
    return jnp.zeros((N, D), h.dtype).at[tok_s].add(y_s)


def setup_inputs(seed: int = 0) -> dict:
    key = jax.random.key(seed)
    ks = iter(jax.random.split(key, 32))
    f32 = jnp.float32

    def nrm(shape, scale):
        return jax.random.normal(next(ks), shape, f32) * scale

    def gain(shape):
        return 1.0 + nrm(shape, 0.02)

    D = D_MODEL
    return {
        'x': nrm((BATCH, SEQ, D), 1.0),
        'c': nrm((BATCH, D), 1.0),
        'ctx': nrm((BATCH, CTX_LEN, D), 1.0),
        'c_ctx': nrm((D,), 1.0),
        'w_ada': nrm((DEPTH, D, 6 * D), 0.5 * D ** -0.5),
        'b_ada': nrm((DEPTH, 6 * D), 0.01),
        'g_attn': gain((DEPTH, D)),
        'w_in': nrm((DEPTH, D, D_IN), D ** -0.5),
        'g_qa': gain((DEPTH, Q_LORA_RANK)),
        'w_qb': nrm((DEPTH, Q_LORA_RANK, N_HEADS * (QK_NOPE_DIM + QK_ROPE_DIM)), Q_LORA_RANK ** -0.5),
        'g_kva': gain((DEPTH, KV_LORA_RANK)),
        'w_kvb': nrm((DEPTH, KV_LORA_RANK, N_HEADS * (QK_NOPE_DIM + V_HEAD_DIM)), KV_LORA_RANK ** -0.5),
        'w_conv': nrm((DEPTH, CONV_K, CONV_WIDTH), CONV_K ** -0.5),
        'w_oa': nrm((DEPTH, N_HEADS * V_HEAD_DIM, D), (N_HEADS * V_HEAD_DIM) ** -0.5),
        'w_ob': nrm((DEPTH, CONV_WIDTH, D), CONV_WIDTH ** -0.5),
        'w_o': nrm((DEPTH, D, D), D ** -0.5),
        'g_ffn': gain((DEPTH, D)),
        'w_gate_dense': nrm((N_DENSE, D, D_FF), D ** -0.5),
        'w_up_dense': nrm((N_DENSE, D, D_FF), D ** -0.5),
        'w_down_dense': nrm((N_DENSE, D_FF, D), D_FF ** -0.5),
        'w_router': nrm((N_MOE, D, N_EXPERTS), D ** -0.5),
        'w_gate_exp': nrm((N_MOE, N_EXPERTS, D, D_FF_EXPERT), D ** -0.5),
        'w_up_exp': nrm((N_MOE, N_EXPERTS, D, D_FF_EXPERT), D ** -0.5),
        'w_down_exp': nrm((N_MOE, N_EXPERTS, D_FF_EXPERT, D), D_FF_EXPERT ** -0.5),
        'g_final': gain((D,)),
    }


def reference(x, c, ctx, c_ctx, w_ada, b_ada, g_attn, w_in, g_qa, w_qb, g_kva, w_kvb, w_conv,
              w_oa, w_ob, w_o, g_ffn, w_gate_dense, w_up_dense, w_down_dense, w_router,
              w_gate_exp, w_up_exp, w_down_exp, g_final):
    B, S, D = x.shape
    rot = axial_rope_tables(S, x.dtype)
    xc = ctx
    Lc = ctx.shape[1]
    for i in range(DEPTH):
        last = i == DEPTH - 1
        mod = jnp.split(jax.nn.silu(c) @ w_ada[i] + b_ada[i], 6, axis=-1)
        sh1, sc1, gt1, sh2, sc2, gt2 = [m[:, None, :] for m in mod]
        sh1c, sc1c, gt1c, sh2c, sc2c, gt2c = jnp.split(jax.nn.silu(c_ctx) @ w_ada[i] + b_ada[i], 6, axis=-1)
        lp = {'g_qa': g_qa[i], 'w_qb': w_qb[i], 'g_kva': g_kva[i], 'w_kvb': w_kvb[i],
              'w_conv': w_conv[i], 'w_oa': w_oa[i], 'w_ob': w_ob[i], 'w_o': w_o[i]}

        h = modulate(rmsnorm(x, g_attn[i]), sh1, sc1)
        hc = modulate(rmsnorm(xc, g_attn[i]), sh1c, sc1c)
        if last:
            kv_c = hc @ w_in[i][:, KV_LO:KV_HI]
            k_c, v_c = mla_keys_values(kv_c[..., :KV_LORA_RANK], kv_c[..., KV_LORA_RANK:],
                                       g_kva[i], w_kvb[i], None)
        else:
            out_c, (k_c, v_c) = mixer(hc @ w_in[i], None, None, lp)
            xc = xc + gt1c * out_c
        out, _ = mixer(h @ w_in[i], rot, (k_c, v_c), lp)
        x = x + gt1 * out

        h2 = modulate(rmsnorm(x, g_ffn[i]), sh2, sc2)
        if not last:
            h2c = modulate(rmsnorm(xc, g_ffn[i]), sh2c, sc2c)
            h2 = jnp.concatenate([h2c, h2], axis=1)
        if i % 2 == 0:
            j = i // 2
            y = swiglu(h2, w_gate_dense[j], w_up_dense[j], w_down_dense[j])
        else:
            j = i // 2
            y = moe_swiglu(h2.reshape(-1, D), w_router[j], w_gate_exp[j], w_up_exp[j],
                           w_down_exp[j]).reshape(h2.shape)
        if not last:
            xc = xc + gt2c * y[:, :Lc]
            y = y[:, Lc:]
        x = x + gt2 * y
    return rmsnorm(x, g_final)
```

```python
import functools

import jax
import jax.numpy as jnp
from jax import lax
from jax.experimental import pallas as pl
from jax.experimental.pallas import tpu as pltpu

F32 = jnp.float32
BF16 = jnp.bfloat16

N_HEADS = 16
QK_NOPE_DIM = 128
QK_ROPE_DIM = 64
V_HEAD_DIM = 128
Q_LORA_RANK = 512
KV_LORA_RANK = 512
GRID_W = 64
ROPE_THETA = 10000.0
ATTN_SCALE = (QK_NOPE_DIM + QK_ROPE_DIM) ** -0.5
CONV_K = 3
N_EXPERTS = 8
TOP_K = 2
NORM_EPS = 1e-6

LANES = 128
SUBLANES = 8
HEAD_W = 2 * LANES
VMEM_LIMIT_BYTES = 56 * 1024 * 1024

MOE_ROWS = 256
DMA_WINDOW = 32


def _tile(n, *cands):
    for c in cands:
        if n % c == 0:
            return c
    return n


def _cparams(n_axes):
    return pltpu.CompilerParams(dimension_semantics=("arbitrary",) * n_axes,
                                vmem_limit_bytes=VMEM_LIMIT_BYTES)


def _dot(a, b):
    return jnp.dot(a, b, preferred_element_type=F32)


def _ada_kernel(xt_ref, w_ref, b_ref, o_ref, *, k_chunk):
    d = xt_ref.shape[0]
    tn = o_ref.shape[-1]

    def body(k, acc):
        a0, a1 = acc
        ks = pl.multiple_of(k * k_chunk, k_chunk)
        xt = xt_ref[pl.ds(ks, k_chunk), :]
        s = xt * jax.nn.sigmoid(xt)
        w = w_ref[0, pl.ds(ks, k_chunk), :]
        a0 = a0 + jnp.sum(w * s[:, 0:1], axis=0, keepdims=True)
        a1 = a1 + jnp.sum(w * s[:, 1:2], axis=0, keepdims=True)
        return a0, a1

    z = jnp.zeros((1, tn), F32)
    a0, a1 = lax.fori_loop(0, d // k_chunk, body, (z, z))
    o_ref[0, 0:1, :] = a0 + b_ref[0]
    o_ref[0, 1:2, :] = a1 + b_ref[0]


def _ada_mod(c, c_ctx, w_ada, b_ada):
    depth, d, n = w_ada.shape
    xt = jnp.stack([c[0], c_ctx], axis=1)
    tn = _tile(n, 1024, 512, LANES)
    k_chunk = _tile(d, 256, SUBLANES)
    return pl.pallas_call(
        functools.partial(_ada_kernel, k_chunk=k_chunk),
        grid=(depth, n // tn),
        in_specs=[pl.BlockSpec((d, 2), lambda l, j: (0, 0)),
                  pl.BlockSpec((1, d, tn), lambda l, j: (l, 0, j)),
                  pl.BlockSpec((1, 1, tn), lambda l, j: (l, 0, j))],
        out_specs=pl.BlockSpec((1, 2, tn), lambda l, j: (l, 0, j)),
        out_shape=jax.ShapeDtypeStruct((depth, 2, n), F32),
        compiler_params=_cparams(2),
        name="ada_mod",
    )(xt, w_ada, b_ada.reshape(depth, 1, n))


def _norm_mod(x, g_ref, sh_ref, sc_ref, is_ctx):
    y = x * lax.rsqrt(jnp.mean(x * x, axis=-1, keepdims=True) + NORM_EPS) * g_ref[...]
    sh = jnp.where(is_ctx, sh_ref[1:2, :], sh_ref[0:1, :])
    sc = jnp.where(is_ctx, sc_ref[1:2, :], sc_ref[0:1, :])
    return y * (1.0 + sc) + sh


def _norm_kernel(x_ref, g_ref, sh_ref, sc_ref, o_ref, *, n_lat_tiles):
    is_ctx = pl.program_id(0) >= n_lat_tiles
    o_ref[...] = _norm_mod(x_ref[...], g_ref, sh_ref, sc_ref, is_ctx).astype(o_ref.dtype)


def _norm_router_kernel(x_ref, g_ref, sh_ref, sc_ref, wr_ref, h_ref, idx_ref, gate_ref):
    h = _norm_mod(x_ref[...], g_ref, sh_ref, sc_ref, False)
    h_ref[...] = h
    logits = jnp.dot(h, wr_ref[...], preferred_element_type=F32, precision=lax.Precision.HIGHEST)
    lane = lax.broadcasted_iota(jnp.int32, logits.shape, 1).astype(F32)
    neg = jnp.float32(-jnp.inf)
    l1 = jnp.where(lane < N_EXPERTS, logits, neg)
    v1 = jnp.max(l1, axis=-1, keepdims=True)
    i1 = jnp.min(jnp.where(l1 == v1, lane, float(LANES)), axis=-1, keepdims=True)
    l2 = jnp.where(lane == i1, neg, l1)
    v2 = jnp.max(l2, axis=-1, keepdims=True)
    i2 = jnp.min(jnp.where(l2 == v2, lane, float(LANES)), axis=-1, keepdims=True)
    e = jnp.exp(v2 - v1)
    g1 = 1.0 / (1.0 + e)
    g2 = e / (1.0 + e)
    idx_ref[...] = jnp.where(lane == 0, i1, jnp.where(lane == 1, i2, 0.0)).astype(jnp.int32)
    gate_ref[...] = jnp.where(lane == 0, g1, jnp.where(lane == 1, g2, 0.0))


def _norm_modulate(x, g, mods, sh_blk, sc_blk, n_lat, out_dtype):
    r, d = x.shape
    tm = _tile(n_lat, 256, 128, 64, SUBLANES) if r > n_lat else _tile(r, 256, 128, 64, SUBLANES)
    if r > n_lat:
        tm = _tile(r - n_lat, tm, 128, 64, SUBLANES)
    return pl.pallas_call(
        functools.partial(_norm_kernel, n_lat_tiles=n_lat // tm),
        grid=(r // tm,),
        in_specs=[pl.BlockSpec((tm, d), lambda i: (i, 0)),
                  pl.BlockSpec((1, d), lambda i: (0, 0)),
                  pl.BlockSpec((2, d), lambda i: (0, sh_blk)),
                  pl.BlockSpec((2, d), lambda i: (0, sc_blk))],
        out_specs=pl.BlockSpec((tm, d), lambda i: (i, 0)),
        out_shape=jax.ShapeDtypeStruct((r, d), out_dtype),
        compiler_params=_cparams(1),
        name="norm_modulate",
    )(x, g.reshape(1, d), mods, mods)


def _norm_modulate_route(x, g, mods, sh_blk, sc_blk, w_router):
    r, d = x.shape
    tm = _tile(r, 256, 128, 64, SUBLANES)
    wr = jnp.pad(w_router, ((0, 0), (0, LANES - N_EXPERTS)))
    return pl.pallas_call(
        _norm_router_kernel,
        grid=(r // tm,),
        in_specs=[pl.BlockSpec((tm, d), lambda i: (i, 0)),
                  pl.BlockSpec((1, d), lambda i: (0, 0)),
                  pl.BlockSpec((2, d), lambda i: (0, sh_blk)),
                  pl.BlockSpec((2, d), lambda i: (0, sc_blk)),
                  pl.BlockSpec((d, LANES), lambda i: (0, 0))],
        out_specs=[pl.BlockSpec((tm, d), lambda i: (i, 0)),
                   pl.BlockSpec((tm, LANES), lambda i: (i, 0)),
                   pl.BlockSpec((tm, LANES), lambda i: (i, 0))],
        out_shape=[jax.ShapeDtypeStruct((r, d), F32),
                   jax.ShapeDtypeStruct((r, LANES), jnp.int32),
                   jax.ShapeDtypeStruct((r, LANES), F32)],
        compiler_params=_cparams(1),
        name="norm_modulate_route",
    )(x, g.reshape(1, d), mods, mods, wr)


def _in_a_kernel(h_ref, w_ref, g_ref, cos_ref, sin_ref, qa_ref, ckv_ref, kr_ref):
    acc = _dot(h_ref[...], w_ref[...])

    def rms(v, g):
        return v * lax.rsqrt(jnp.mean(v * v, axis=-1, keepdims=True) + NORM_EPS) * g

    q0, q1 = 0, Q_LORA_RANK
    c1 = q1 + KV_LORA_RANK
    qa_ref[...] = rms(acc[:, q0:q1], g_ref[:, q0:q1]).astype(qa_ref.dtype)
    ckv_ref[...] = rms(acc[:, q1:c1], g_ref[:, q1:c1]).astype(ckv_ref.dtype)
    kr = acc[:, c1:c1 + LANES] * cos_ref[...] + acc[:, c1 + LANES:c1 + 2 * LANES] * sin_ref[...]
    kr_ref[...] = kr.astype(kr_ref.dtype)


def _in_proj_a(h, w_a, g_a, cos_t, sin_t):
    m, d = h.shape
    n = w_a.shape[1]
    tm = _tile(m, 768, 640, 512, 256, 128)
    return pl.pallas_call(
        _in_a_kernel,
        grid=(m // tm,),
        in_specs=[pl.BlockSpec((tm, d), lambda i: (i, 0)),
                  pl.BlockSpec((d, n), lambda i: (0, 0)),
                  pl.BlockSpec((1, Q_LORA_RANK + KV_LORA_RANK), lambda i: (0, 0)),
                  pl.BlockSpec((tm, LANES), lambda i: (i, 0)),
                  pl.BlockSpec((tm, LANES), lambda i: (i, 0))],
        out_specs=[pl.BlockSpec((tm, Q_LORA_RANK), lambda i: (i, 0)),
                   pl.BlockSpec((tm, KV_LORA_RANK), lambda i: (i, 0)),
                   pl.BlockSpec((tm, LANES), lambda i: (i, 0))],
        out_shape=[jax.ShapeDtypeStruct((m, Q_LORA_RANK), BF16),
                   jax.ShapeDtypeStruct((m, KV_LORA_RANK), BF16),
                   jax.ShapeDtypeStruct((m, LANES), BF16)],
        compiler_params=_cparams(1),
        name="in_proj_a",
    )(h, w_a, g_a, cos_t, sin_t)


def _in_b_kernel(h_ref, w_ref, o_ref, *, first_gate_tile):
    acc = _dot(h_ref[...], w_ref[...])
    is_gate = pl.program_id(0) >= first_gate_tile

    @pl.when(is_gate)
    def _():
        o_ref[...] = jax.nn.sigmoid(acc).astype(o_ref.dtype)

    @pl.when(jnp.logical_not(is_gate))
    def _():
        o_ref[...] = acc.astype(o_ref.dtype)


def _in_proj_b(h, w_b, first_gate_col):
    m, d = h.shape
    n = w_b.shape[1]
    tm = _tile(m, 1408, 768, 640, 512, 256, 128)
    tn = _tile(first_gate_col, 1024, 512, 256, LANES)
    return pl.pallas_call(
        functools.partial(_in_b_kernel, first_gate_tile=first_gate_col // tn),
        grid=(n // tn, m // tm),
        in_specs=[pl.BlockSpec((tm, d), lambda j, i: (i, 0)),
                  pl.BlockSpec((d, tn), lambda j, i: (0, j))],
        out_specs=pl.BlockSpec((tm, tn), lambda j, i: (i, j)),
        out_shape=jax.ShapeDtypeStruct((m, n), BF16),
        compiler_params=_cparams(2),
        name="in_proj_b",
    )(h, w_b)


def _q_kernel(a_ref, wm_ref, ws_ref, cos_ref, sin_ref, q_ref):
    a = a_ref[...]
    cos = cos_ref[...]
    sin = sin_ref[...]
    for hd in range(N_HEADS):
        main = _dot(a, wm_ref[:, hd * HEAD_W:(hd + 1) * HEAD_W])
        swap = _dot(a, ws_ref[:, hd * LANES:(hd + 1) * LANES])
        q_ref[:, hd * HEAD_W:hd * HEAD_W + LANES] = (main[:, :LANES] * ATTN_SCALE).astype(q_ref.dtype)
        rot = (main[:, LANES:] * cos + swap * sin) * ATTN_SCALE
        q_ref[:, hd * HEAD_W + LANES:(hd + 1) * HEAD_W] = rot.astype(q_ref.dtype)


def _q_proj(qa_n, w_main, w_swap, cos_t, sin_t, rows):
    k = qa_n.shape[1]
    tm = _tile(rows, 768, 640, 512, 256, 128)
    return pl.pallas_call(
        _q_kernel,
        grid=(rows // tm,),
        in_specs=[pl.BlockSpec((tm, k), lambda i: (i, 0)),
                  pl.BlockSpec(w_main.shape, lambda i: (0, 0)),
                  pl.BlockSpec(w_swap.shape, lambda i: (0, 0)),
                  pl.BlockSpec((tm, LANES), lambda i: (i, 0)),
                  pl.BlockSpec((tm, LANES), lambda i: (i, 0))],
        out_specs=pl.BlockSpec((tm, N_HEADS * HEAD_W), lambda i: (i, 0)),
        out_shape=jax.ShapeDtypeStruct((rows, N_HEADS * HEAD_W), BF16),
        compiler_params=_cparams(1),
        name="q_proj",
    )(qa_n, w_main, w_swap, cos_t, sin_t)


def _kv_kernel(c_ref, kr_ref, wk_ref, wv_ref, k_ref, v_ref):
    c = c_ref[...]
    kr = kr_ref[...]
    for hd in range(N_HEADS):
        kn = _dot(c, wk_ref[:, hd * LANES:(hd + 1) * LANES])
        k_ref[:, hd * HEAD_W:hd * HEAD_W + LANES] = kn.astype(k_ref.dtype)
        k_ref[:, hd * HEAD_W + LANES:(hd + 1) * HEAD_W] = kr
    v_ref[...] = _dot(c, wv_ref[...]).astype(v_ref.dtype)


def _kv_proj(ckv_n, kr, w_k, w_v):
    m, k = ckv_n.shape
    tm = _tile(m, 768, 640, 512, 256, 128)
    return pl.pallas_call(
        _kv_kernel,
        grid=(m // tm,),
        in_specs=[pl.BlockSpec((tm, k), lambda i: (i, 0)),
                  pl.BlockSpec((tm, LANES), lambda i: (i, 0)),
                  pl.BlockSpec(w_k.shape, lambda i: (0, 0)),
                  pl.BlockSpec(w_v.shape, lambda i: (0, 0))],
        out_specs=[pl.BlockSpec((tm, N_HEADS * HEAD_W), lambda i: (i, 0)),
                   pl.BlockSpec((tm, N_HEADS * V_HEAD_DIM), lambda i: (i, 0))],
        out_shape=[jax.ShapeDtypeStruct((m, N_HEADS * HEAD_W), BF16),
                   jax.ShapeDtypeStruct((m, N_HEADS * V_HEAD_DIM), BF16)],
        compiler_params=_cparams(1),
        name="kv_proj",
    )(ckv_n, kr, w_k, w_v)


def _attn_step(q, k, v, carry):
    m, l, acc = carry
    s = lax.dot_general(q, k, (((1,), (1,)), ((), ())), preferred_element_type=F32)
    m_new = jnp.maximum(m, jnp.max(s, axis=-1, keepdims=True))
    alpha = jnp.exp(m - m_new)
    p = jnp.exp(s - m_new)
    l = alpha * l + jnp.sum(p, axis=-1, keepdims=True)
    acc = alpha * acc + _dot(p.astype(v.dtype), v)
    return m_new, l, acc


def _attn_init(tq):
    return (jnp.full((tq, 1), -jnp.inf, F32), jnp.zeros((tq, 1), F32), jnp.zeros((tq, V_HEAD_DIM), F32))


def _attn_lat_kernel(q_ref, kl_ref, vl_ref, kc_ref, vc_ref, o_ref, *, tk):
    q = q_ref[...]
    n_chunks = kl_ref.shape[0] // tk

    def body(c, carry):
        ks = pl.multiple_of(c * tk, tk)
        return _attn_step(q, kl_ref[pl.ds(ks, tk), :], vl_ref[pl.ds(ks, tk), :], carry)

    carry = lax.fori_loop(0, n_chunks, body, _attn_init(q.shape[0]))
    _, l, acc = _attn_step(q, kc_ref[...], vc_ref[...], carry)
    o_ref[...] = (acc / l).astype(o_ref.dtype)


def _attn_ctx_kernel(q_ref, kc_ref, vc_ref, o_ref):
    q = q_ref[...]
    _, l, acc = _attn_step(q, kc_ref[...], vc_ref[...], _attn_init(q.shape[0]))
    o_ref[...] = (acc / l).astype(o_ref.dtype)


def _attention_latent(q, k, v, s_len, c_len):
    tq = _tile(s_len, 512, 256, 128)
    tk = _tile(s_len, 512, 256, 128)
    c_blk = s_len // c_len
    return pl.pallas_call(
        functools.partial(_attn_lat_kernel, tk=tk),
        grid=(N_HEADS, s_len // tq),
        in_specs=[pl.BlockSpec((tq, HEAD_W), lambda h, i: (i, h)),
                  pl.BlockSpec((s_len, HEAD_W), lambda h, i: (0, h)),
                  pl.BlockSpec((s_len, V_HEAD_DIM), lambda h, i: (0, h)),
                  pl.BlockSpec((c_len, HEAD_W), lambda h, i: (c_blk, h)),
                  pl.BlockSpec((c_len, V_HEAD_DIM), lambda h, i: (c_blk, h))],
        out_specs=pl.BlockSpec((tq, V_HEAD_DIM), lambda h, i: (i, h)),
        out_shape=jax.ShapeDtypeStruct((s_len, N_HEADS * V_HEAD_DIM), BF16),
        compiler_params=_cparams(2),
        name="attention_latent",
    )(q, k, v, k, v)


def _attention_context(q, k, v, s_len, c_len):
    c_blk = s_len // c_len
    return pl.pallas_call(
        _attn_ctx_kernel,
        grid=(N_HEADS,),
        in_specs=[pl.BlockSpec((c_len, HEAD_W), lambda h: (c_blk, h)),
                  pl.BlockSpec((c_len, HEAD_W), lambda h: (c_blk, h)),
                  pl.BlockSpec((c_len, V_HEAD_DIM), lambda h: (c_blk, h))],
        out_specs=pl.BlockSpec((c_len, V_HEAD_DIM), lambda h: (0, h)),
        out_shape=jax.ShapeDtypeStruct((c_len, N_HEADS * V_HEAD_DIM), BF16),
        compiler_params=_cparams(1),
        name="attention_context",
    )(q, k, v)


def _mix_kernel(attn_ref, cx_ref, cb_ref, cc_ref, cxp_ref, ccp_ref, cxn_ref, ccn_ref, sga_ref, sgb_ref,
                wc_ref, woa_ref, wob_ref, o_ref, z_ref, *, seg_starts, seg_ends, col_chunk):
    i = pl.program_id(0)
    tm, width = z_ref.shape

    @pl.when(pl.program_id(1) == 0)
    def _():
        loc = lax.broadcasted_iota(jnp.int32, (tm, 1), 0)
        row = loc + i * tm
        first = functools.reduce(jnp.logical_or, [row == r for r in seg_starts])
        last = functools.reduce(jnp.logical_or, [row == r for r in seg_ends])
        for c0 in range(0, width, col_chunk):
            cs = slice(c0, c0 + col_chunk)
            u = cx_ref[:, cs].astype(F32) * cc_ref[:, cs].astype(F32)
            u_halo_prev = (cxp_ref[SUBLANES - 1:SUBLANES, cs].astype(F32)
                           * ccp_ref[SUBLANES - 1:SUBLANES, cs].astype(F32))
            u_halo_next = cxn_ref[0:1, cs].astype(F32) * ccn_ref[0:1, cs].astype(F32)
            u_prev = jnp.where(loc == 0, u_halo_prev, pltpu.roll(u, 1, 0))
            u_prev = jnp.where(first, 0.0, u_prev)
            u_next = jnp.where(loc == tm - 1, u_halo_next, pltpu.roll(u, tm - 1, 0))
            u_next = jnp.where(last, 0.0, u_next)
            conv = wc_ref[0:1, cs] * u_prev + wc_ref[1:2, cs] * u + wc_ref[2:3, cs] * u_next
            z_ref[:, cs] = (cb_ref[:, cs].astype(F32) * conv).astype(z_ref.dtype)

    o_a = _dot(attn_ref[...], woa_ref[...])
    o_b = _dot(z_ref[...], wob_ref[...])
    o_ref[...] = (sga_ref[...].astype(F32) * o_a + sgb_ref[...].astype(F32) * o_b).astype(o_ref.dtype)


def _mix(attn, p, w_conv, w_oa, w_ob, rows, s_len, m_len):
    attn_w = w_oa.shape[0]
    width, d = w_ob.shape
    tm = _tile(rows, 384, 256, 128, 64)
    tn = _tile(d, 512, 256, LANES)
    ga_blk = 3 * width // tn
    gb_blk = (3 * width + d) // tn
    hb = tm // SUBLANES
    last_hb = p.shape[0] // SUBLANES - 1

    def prev_map(col):
        return lambda i, j: (jnp.maximum(i * hb - 1, 0), col)

    def next_map(col):
        return lambda i, j: (jnp.minimum((i + 1) * hb, last_hb), col)

    kern = functools.partial(_mix_kernel, seg_starts=(0, s_len), seg_ends=(s_len - 1, m_len - 1),
                             col_chunk=_tile(width, 512, LANES))
    return pl.pallas_call(
        kern,
        grid=(rows // tm, d // tn),
        in_specs=[pl.BlockSpec((tm, attn_w), lambda i, j: (i, 0)),
                  pl.BlockSpec((tm, width), lambda i, j: (i, 0)),
                  pl.BlockSpec((tm, width), lambda i, j: (i, 1)),
                  pl.BlockSpec((tm, width), lambda i, j: (i, 2)),
                  pl.BlockSpec((SUBLANES, width), prev_map(0)),
                  pl.BlockSpec((SUBLANES, width), prev_map(2)),
                  pl.BlockSpec((SUBLANES, width), next_map(0)),
                  pl.BlockSpec((SUBLANES, width), next_map(2)),
                  pl.BlockSpec((tm, tn), lambda i, j: (i, ga_blk + j)),
                  pl.BlockSpec((tm, tn), lambda i, j: (i, gb_blk + j)),
                  pl.BlockSpec((CONV_K, width), lambda i, j: (0, 0)),
                  pl.BlockSpec((attn_w, tn), lambda i, j: (0, j)),
                  pl.BlockSpec((width, tn), lambda i, j: (0, j))],
        out_specs=pl.BlockSpec((tm, tn), lambda i, j: (i, j)),
        out_shape=jax.ShapeDtypeStruct((rows, d), BF16),
        scratch_shapes=[pltpu.VMEM((tm, width), BF16)],
        compiler_params=_cparams(2),
        name="mix",
    )(attn, p, p, p, p, p, p, p, p, p, w_conv, w_oa, w_ob)


def _mm_res_kernel(a_ref, w_ref, x_ref, gt_ref, o_ref, *, n_lat):
    tm = a_ref.shape[0]
    row = lax.broadcasted_iota(jnp.int32, (tm, 1), 0) + pl.program_id(1) * tm
    gate = jnp.where(row >= n_lat, gt_ref[1:2, :], gt_ref[0:1, :])
    o_ref[...] = x_ref[...] + gate * _dot(a_ref[...], w_ref[...])


def _matmul_residual(a, w, x, mods, gate_blk, rows, n_lat):
    k, n = w.shape
    tm = _tile(rows, 768, 640, 512, 256, 128)
    tn = _tile(n, 512, 256, LANES)
    nt = n // tn
    return pl.pallas_call(
        functools.partial(_mm_res_kernel, n_lat=n_lat),
        grid=(nt, rows // tm),
        in_specs=[pl.BlockSpec((tm, k), lambda j, i: (i, 0)),
                  pl.BlockSpec((k, tn), lambda j, i: (0, j)),
                  pl.BlockSpec((tm, tn), lambda j, i: (i, j)),
                  pl.BlockSpec((2, tn), lambda j, i: (0, gate_blk * nt + j))],
        out_specs=pl.BlockSpec((tm, tn), lambda j, i: (i, j)),
        out_shape=jax.ShapeDtypeStruct((rows, n), F32),
        compiler_params=_cparams(2),
        name="matmul_residual",
    )(a, w, x, mods)


def _glu_kernel(eid_ref, a_ref, wg_ref, wu_ref, o_ref):
    del eid_ref
    a = a_ref[...].astype(BF16)
    gate = _dot(a, wg_ref[0])
    up = _dot(a, wu_ref[0])
    o_ref[...] = (gate * jax.nn.sigmoid(gate) * up).astype(o_ref.dtype)


def _glu(a, w_gate, w_up, eid, tm):
    r, k = a.shape
    f = w_gate.shape[2]
    tn = _tile(f, 512, 1408, 256, LANES)
    grid_spec = pltpu.PrefetchScalarGridSpec(
        num_scalar_prefetch=1,
        grid=(f // tn, r // tm),
        in_specs=[pl.BlockSpec((tm, k), lambda j, i, e: (i, 0)),
                  pl.BlockSpec((1, k, tn), lambda j, i, e: (e[i], 0, j)),
                  pl.BlockSpec((1, k, tn), lambda j, i, e: (e[i], 0, j))],
        out_specs=pl.BlockSpec((tm, tn), lambda j, i, e: (i, j)),
    )
    return pl.pallas_call(
        _glu_kernel,
        grid_spec=grid_spec,
        out_shape=jax.ShapeDtypeStruct((r, f), BF16),
        compiler_params=_cparams(2),
        name="glu",
    )(eid, a, w_gate, w_up)


def _down_kernel(eid_ref, a_ref, w_ref, o_ref):
    del eid_ref
    o_ref[...] = _dot(a_ref[...], w_ref[0])


def _down_grouped(a, w_down, eid, tm):
    r, f = a.shape
    d = w_down.shape[2]
    tn = _tile(d, 512, 256, LANES)
    grid_spec = pltpu.PrefetchScalarGridSpec(
        num_scalar_prefetch=1,
        grid=(d // tn, r // tm),
        in_specs=[pl.BlockSpec((tm, f), lambda j, i, e: (i, 0)),
                  pl.BlockSpec((1, f, tn), lambda j, i, e: (e[i], 0, j))],
        out_specs=pl.BlockSpec((tm, tn), lambda j, i, e: (i, j)),
    )
    return pl.pallas_call(
        _down_kernel,
        grid_spec=grid_spec,
        out_shape=jax.ShapeDtypeStruct((r, d), F32),
        compiler_params=_cparams(2),
        name="down_grouped",
    )(eid, a, w_down)


def _row_move_kernel(sidx_ref, didx_ref, src_ref, init_ref, out_ref, sem, *, n, window):
    del init_ref

    def copy(i):
        return pltpu.make_async_copy(src_ref.at[pl.ds(sidx_ref[i], 1)], out_ref.at[pl.ds(didx_ref[i], 1)], sem)

    def prime(i, c):
        copy(i).start()
        return c

    def steady(i, c):
        copy(i - window).wait()
        copy(i).start()
        return c

    def drain(i, c):
        copy(i).wait()
        return c

    lax.fori_loop(0, window, prime, 0)
    lax.fori_loop(window, n, steady, 0)
    lax.fori_loop(n - window, n, drain, 0)


def _row_move(src, src_idx, dst_idx, init):
    n = src_idx.shape[0]
    window = min(DMA_WINDOW, n)
    grid_spec = pltpu.PrefetchScalarGridSpec(
        num_scalar_prefetch=2,
        grid=(1,),
        in_specs=[pl.BlockSpec(memory_space=pl.ANY), pl.BlockSpec(memory_space=pl.ANY)],
        out_specs=pl.BlockSpec(memory_space=pl.ANY),
        scratch_shapes=[pltpu.SemaphoreType.DMA(())],
    )
    return pl.pallas_call(
        functools.partial(_row_move_kernel, n=n, window=window),
        grid_spec=grid_spec,
        out_shape=jax.ShapeDtypeStruct(init.shape, init.dtype),
        input_output_aliases={3: 0},
        compiler_params=_cparams(1),
        name="row_move",
    )(src_idx, dst_idx, src, init)


def _final_kernel(x_ref, y_ref, gate_ref, gt_ref, g_ref, o_ref):
    d = x_ref.shape[1]
    gates = gate_ref[...]
    y = gates[:, 0:1] * y_ref[:, :d] + gates[:, 1:2] * y_ref[:, d:]
    x = x_ref[...] + gt_ref[0:1, :] * y
    o_ref[...] = x * lax.rsqrt(jnp.mean(x * x, axis=-1, keepdims=True) + NORM_EPS) * g_ref[...]


def _combine_final(x, y_pairs, gates, mods, gate_blk, g_final):
    r, d = x.shape
    tm = _tile(r, 256, 128, 64, SUBLANES)
    return pl.pallas_call(
        _final_kernel,
        grid=(r // tm,),
        in_specs=[pl.BlockSpec((tm, d), lambda i: (i, 0)),
                  pl.BlockSpec((tm, TOP_K * d), lambda i: (i, 0)),
                  pl.BlockSpec((tm, LANES), lambda i: (i, 0)),
                  pl.BlockSpec((2, d), lambda i: (0, gate_blk)),
                  pl.BlockSpec((1, d), lambda i: (0, 0))],
        out_specs=pl.BlockSpec((tm, d), lambda i: (i, 0)),
        out_shape=jax.ShapeDtypeStruct((r, d), F32),
        compiler_params=_cparams(1),
        name="combine_final",
    )(x, y_pairs, gates, mods, g_final.reshape(1, d))


def _rope_tables(s_len, c_len):
    rows = s_len // GRID_W
    row = jnp.repeat(jnp.arange(rows), GRID_W).astype(F32)
    col = jnp.tile(jnp.arange(GRID_W), rows).astype(F32)
    half = QK_ROPE_DIM // 2
    inv_freq = jnp.power(ROPE_THETA, -jnp.arange(0, half, 2, dtype=F32) / half)
    ang_r = row[:, None] * inv_freq
    ang_c = col[:, None] * inv_freq
    pad = jnp.zeros((s_len, LANES - QK_ROPE_DIM), F32)
    cos = jnp.concatenate([jnp.cos(ang_r), jnp.cos(ang_r), jnp.cos(ang_c), jnp.cos(ang_c), pad], axis=1)
    sin = jnp.concatenate([-jnp.sin(ang_r), jnp.sin(ang_r), -jnp.sin(ang_c), jnp.sin(ang_c), pad], axis=1)
    cos_c = jnp.concatenate([jnp.ones((c_len, QK_ROPE_DIM), F32), jnp.zeros((c_len, LANES - QK_ROPE_DIM), F32)], 1)
    return jnp.concatenate([cos, cos_c], axis=0), jnp.concatenate([sin, jnp.zeros((c_len, LANES), F32)], axis=0)


def _rope_swap_perm():
    q = QK_ROPE_DIM // 4
    return jnp.concatenate([jnp.arange(q, 2 * q), jnp.arange(0, q), jnp.arange(3 * q, 4 * q), jnp.arange(2 * q, 3 * q)])


def _layer_weights(w_in, w_qb, w_kvb):
    d = w_in.shape[0]
    perm = _rope_swap_perm()
    kv_lo = Q_LORA_RANK
    kr_lo = Q_LORA_RANK + KV_LORA_RANK
    kr_hi = kr_lo + QK_ROPE_DIM
    w_kr = w_in[:, kr_lo:kr_hi]
    zpad = jnp.zeros((d, LANES - QK_ROPE_DIM), w_in.dtype)
    w_a = jnp.concatenate([w_in[:, :kv_lo], w_in[:, kv_lo:kr_lo], w_kr, zpad, w_kr[:, perm], zpad], axis=1)
    w_b = w_in[:, kr_hi:]
    qb = w_qb.reshape(Q_LORA_RANK, N_HEADS, QK_NOPE_DIM + QK_ROPE_DIM)
    q_rope = qb[:, :, QK_NOPE_DIM:]
    hpad = jnp.zeros((Q_LORA_RANK, N_HEADS, LANES - QK_ROPE_DIM), w_qb.dtype)
    w_q_main = jnp.concatenate([qb, hpad], axis=2).reshape(Q_LORA_RANK, N_HEADS * HEAD_W)
    w_q_swap = jnp.concatenate([q_rope[:, :, perm], hpad], axis=2).reshape(Q_LORA_RANK, N_HEADS * LANES)
    kvb = w_kvb.reshape(KV_LORA_RANK, N_HEADS, QK_NOPE_DIM + V_HEAD_DIM)
    w_k = kvb[:, :, :QK_NOPE_DIM].reshape(KV_LORA_RANK, N_HEADS * QK_NOPE_DIM)
    w_v = kvb[:, :, QK_NOPE_DIM:].reshape(KV_LORA_RANK, N_HEADS * V_HEAD_DIM)
    return tuple(w.astype(BF16) for w in (w_a, w_b, w_q_main, w_q_swap, w_k, w_v))


def _moe_slots(idx, n_blocks):
    e_flat = idx.reshape(-1)
    onehot = (e_flat[:, None] == jnp.arange(N_EXPERTS)[None, :]).astype(jnp.int32)
    csum = jnp.cumsum(onehot, axis=0)
    counts = csum[-1]
    rank = jnp.sum((csum - onehot) * onehot, axis=1)
    padded = (counts + MOE_ROWS - 1) // MOE_ROWS * MOE_ROWS
    p_end = jnp.cumsum(padded)
    p_start = p_end - padded
    dest = jnp.sum(onehot * p_start[None, :], axis=1) + rank
    blk_lo = jnp.arange(n_blocks) * MOE_ROWS
    block_expert = jnp.minimum(jnp.sum((blk_lo[:, None] >= p_end[None, :]).astype(jnp.int32), axis=1),
                               N_EXPERTS - 1)
    return dest.astype(jnp.int32), block_expert.astype(jnp.int32)


def kernel(x, c, ctx, c_ctx, w_ada, b_ada, g_attn, w_in, g_qa, w_qb, g_kva, w_kvb, w_conv, w_oa, w_ob, w_o,
           g_ffn, w_gate_dense, w_up_dense, w_down_dense, w_router, w_gate_exp, w_up_exp, w_down_exp, g_final):
    _, s_len, d = x.shape
    c_len = ctx.shape[1]
    m_len = s_len + c_len
    depth = w_in.shape[0]
    width = w_conv.shape[2]

    mods_all = _ada_mod(c, c_ctx, w_ada, b_ada)
    cos_t, sin_t = _rope_tables(s_len, c_len)
    xa = jnp.concatenate([x[0], ctx[0]], axis=0)

    for li in range(depth):
        last = li == depth - 1
        mods = mods_all[li]
        w_a, w_b, w_q_main, w_q_swap, w_k, w_v = _layer_weights(w_in[li], w_qb[li], w_kvb[li])
        g_a = jnp.concatenate([g_qa[li], g_kva[li]]).reshape(1, -1)
        rows = s_len if last else m_len

        h = _norm_modulate(xa, g_attn[li], mods, 0, 1, s_len, BF16)
        qa_n, ckv_n, kr = _in_proj_a(h, w_a, g_a, cos_t, sin_t)
        p = _in_proj_b(h, w_b, 3 * width)
        q = _q_proj(qa_n, w_q_main, w_q_swap, cos_t, sin_t, rows)
        k, v = _kv_proj(ckv_n, kr, w_k, w_v)
        attn = _attention_latent(q, k, v, s_len, c_len)
        if not last:
            attn = jnp.concatenate([attn, _attention_context(q, k, v, s_len, c_len)], axis=0)
        merged = _mix(attn, p, w_conv[li], w_oa[li].astype(BF16), w_ob[li].astype(BF16), rows, s_len, m_len)
        xa = _matmul_residual(merged, w_o[li].astype(BF16), xa, mods, 2, rows, s_len)

        j = li // 2
        if li % 2 == 0:
            h2 = _norm_modulate(xa, g_ffn[li], mods, 3, 4, s_len, BF16)
            tm = _tile(rows, 768, 640, 512, 256, 128)
            eid = jnp.zeros((rows // tm,), jnp.int32)
            hid = _glu(h2, w_gate_dense[j][None].astype(BF16), w_up_dense[j][None].astype(BF16), eid, tm)
            xa = _matmul_residual(hid, w_down_dense[j].astype(BF16), xa, mods, 5, rows, s_len)
        else:
            assert last, "expert layers are only supported as the final layer"
            h2, idx, gates = _norm_modulate_route(xa, g_ffn[li], mods, 3, 4, w_router[j])
            n_assign = rows * TOP_K
            n_blocks = -(-n_assign // MOE_ROWS) + N_EXPERTS
            dest, block_expert = _moe_slots(idx[:, :TOP_K], n_blocks)
            tok = jnp.arange(n_assign, dtype=jnp.int32) // TOP_K
            buf = _row_move(h2, tok, dest, jnp.zeros((n_blocks * MOE_ROWS, d), F32))
            hid = _glu(buf, w_gate_exp[j].astype(BF16), w_up_exp[j].astype(BF16), block_expert, MOE_ROWS)
            ybuf = _down_grouped(hid, w_down_exp[j].astype(BF16), block_expert, MOE_ROWS)
            y_pairs = _row_move(ybuf, dest, jnp.arange(n_assign, dtype=jnp.int32),
                                jnp.zeros((n_assign, d), F32))
            return _combine_final(xa, y_pairs.reshape(rows, TOP_K * d), gates, mods, 5, g_final)[None]
    raise NotImplementedError("a dense final layer is not needed for this problem")
```

```python
import functools

import jax
import jax.numpy as jnp
from jax import lax
from jax.experimental import pallas as pl
from jax.experimental.pallas import tpu as pltpu

F32 = jnp.float32
BF16 = jnp.bfloat16

N_HEADS = 16
QK_NOPE_DIM = 128
QK_ROPE_DIM = 64
V_HEAD_DIM = 128
Q_LORA_RANK = 512
KV_LORA_RANK = 512
GRID_W = 64
ROPE_THETA = 10000.0
ATTN_SCALE = (QK_NOPE_DIM + QK_ROPE_DIM) ** -0.5
Q_SCALE = ATTN_SCALE * 1.4426950408889634
CONV_K = 3
N_EXPERTS = 8
TOP_K = 2
NORM_EPS = 1e-6

LANES = 128
SUBLANES = 8
HEAD_W = 2 * LANES
VMEM_LIMIT_BYTES = 56 * 1024 * 1024

MOE_ROWS = 256


def _tile(n, *cands):
    for c in cands:
        if n % c == 0:
            return c
    return n


def _cparams(n_axes):
    return pltpu.CompilerParams(dimension_semantics=("arbitrary",) * n_axes,
                                vmem_limit_bytes=VMEM_LIMIT_BYTES)


def _dot(a, b):
    return jnp.dot(a, b, preferred_element_type=F32)


def _ada_kernel(xt_ref, w_ref, b_ref, o_ref, *, k_chunk):
    d = xt_ref.shape[0]
    tn = o_ref.shape[-1]

    def body(k, acc):
        a0, a1 = acc
        ks = pl.multiple_of(k * k_chunk, k_chunk)
        xt = xt_ref[pl.ds(ks, k_chunk), :]
        s = xt * jax.nn.sigmoid(xt)
        w = w_ref[0, pl.ds(ks, k_chunk), :]
        a0 = a0 + jnp.sum(w * s[:, 0:1], axis=0, keepdims=True)
        a1 = a1 + jnp.sum(w * s[:, 1:2], axis=0, keepdims=True)
        return a0, a1

    z = jnp.zeros((1, tn), F32)
    a0, a1 = lax.fori_loop(0, d // k_chunk, body, (z, z))
    o_ref[0, 0:1, :] = a0 + b_ref[0]
    o_ref[0, 1:2, :] = a1 + b_ref[0]


def _ada_mod(c, c_ctx, w_ada, b_ada):
    depth, d, n = w_ada.shape
    xt = jnp.stack([c[0], c_ctx], axis=1)
    tn = _tile(n, 1024, 512, LANES)
    k_chunk = _tile(d, 256, SUBLANES)
    return pl.pallas_call(
        functools.partial(_ada_kernel, k_chunk=k_chunk),
        grid=(depth, n // tn),
        in_specs=[pl.BlockSpec((d, 2), lambda l, j: (0, 0)),
                  pl.BlockSpec((1, d, tn), lambda l, j: (l, 0, j)),
                  pl.BlockSpec((1, 1, tn), lambda l, j: (l, 0, j))],
        out_specs=pl.BlockSpec((1, 2, tn), lambda l, j: (l, 0, j)),
        out_shape=jax.ShapeDtypeStruct((depth, 2, n), F32),
        compiler_params=_cparams(2),
        name="ada_mod",
    )(xt, w_ada, b_ada.reshape(depth, 1, n))


def _norm_mod(x, g_ref, sh_ref, sc_ref, is_ctx):
    y = x * lax.rsqrt(jnp.mean(x * x, axis=-1, keepdims=True) + NORM_EPS) * g_ref[...]
    sh = jnp.where(is_ctx, sh_ref[1:2, :], sh_ref[0:1, :])
    sc = jnp.where(is_ctx, sc_ref[1:2, :], sc_ref[0:1, :])
    return y * (1.0 + sc) + sh


def _norm_kernel(x_ref, g_ref, sh_ref, sc_ref, o_ref, *, n_lat_tiles):
    is_ctx = pl.program_id(0) >= n_lat_tiles
    o_ref[...] = _norm_mod(x_ref[...], g_ref, sh_ref, sc_ref, is_ctx).astype(o_ref.dtype)


def _norm_router_kernel(x_ref, g_ref, sh_ref, sc_ref, wr_ref, h_ref, idx_ref, gate_ref):
    h = _norm_mod(x_ref[...], g_ref, sh_ref, sc_ref, False)
    h_ref[...] = h
    logits = jnp.dot(h, wr_ref[...], preferred_element_type=F32, precision=lax.Precision.HIGHEST)
    lane = lax.broadcasted_iota(jnp.int32, logits.shape, 1).astype(F32)
    neg = jnp.float32(-jnp.inf)
    l1 = jnp.where(lane < N_EXPERTS, logits, neg)
    v1 = jnp.max(l1, axis=-1, keepdims=True)
    i1 = jnp.min(jnp.where(l1 == v1, lane, float(LANES)), axis=-1, keepdims=True)
    l2 = jnp.where(lane == i1, neg, l1)
    v2 = jnp.max(l2, axis=-1, keepdims=True)
    i2 = jnp.min(jnp.where(l2 == v2, lane, float(LANES)), axis=-1, keepdims=True)
    e = jnp.exp(v2 - v1)
    g1 = 1.0 / (1.0 + e)
    g2 = e / (1.0 + e)
    idx_ref[...] = jnp.where(lane == 0, i1, jnp.where(lane == 1, i2, 0.0)).astype(jnp.int32)
    gate_ref[...] = jnp.where(lane == 0, g1, jnp.where(lane == 1, g2, 0.0))


def _norm_modulate(x, g, mods, sh_blk, sc_blk, n_lat, out_dtype):
    r, d = x.shape
    tm = _tile(n_lat, 256, 128, 64, SUBLANES) if r > n_lat else _tile(r, 256, 128, 64, SUBLANES)
    if r > n_lat:
        tm = _tile(r - n_lat, tm, 128, 64, SUBLANES)
    return pl.pallas_call(
        functools.partial(_norm_kernel, n_lat_tiles=n_lat // tm),
        grid=(r // tm,),
        in_specs=[pl.BlockSpec((tm, d), lambda i: (i, 0)),
                  pl.BlockSpec((1, d), lambda i: (0, 0)),
                  pl.BlockSpec((2, d), lambda i: (0, sh_blk)),
                  pl.BlockSpec((2, d), lambda i: (0, sc_blk))],
        out_specs=pl.BlockSpec((tm, d), lambda i: (i, 0)),
        out_shape=jax.ShapeDtypeStruct((r, d), out_dtype),
        compiler_params=_cparams(1),
        name="norm_modulate",
    )(x, g.reshape(1, d), mods, mods)


def _norm_modulate_route(x, g, mods, sh_blk, sc_blk, w_router):
    r, d = x.shape
    tm = _tile(r, 256, 128, 64, SUBLANES)
    wr = jnp.pad(w_router, ((0, 0), (0, LANES - N_EXPERTS)))
    return pl.pallas_call(
        _norm_router_kernel,
        grid=(r // tm,),
        in_specs=[pl.BlockSpec((tm, d), lambda i: (i, 0)),
                  pl.BlockSpec((1, d), lambda i: (0, 0)),
                  pl.BlockSpec((2, d), lambda i: (0, sh_blk)),
                  pl.BlockSpec((2, d), lambda i: (0, sc_blk)),
                  pl.BlockSpec((d, LANES), lambda i: (0, 0))],
        out_specs=[pl.BlockSpec((tm, d), lambda i: (i, 0)),
                   pl.BlockSpec((tm, LANES), lambda i: (i, 0)),
                   pl.BlockSpec((tm, LANES), lambda i: (i, 0))],
        out_shape=[jax.ShapeDtypeStruct((r, d), F32),
                   jax.ShapeDtypeStruct((r, LANES), jnp.int32),
                   jax.ShapeDtypeStruct((r, LANES), F32)],
        compiler_params=_cparams(1),
        name="norm_modulate_route",
    )(x, g.reshape(1, d), mods, mods, wr)


def _in_a_kernel(h_ref, w_ref, g_ref, cos_ref, sin_ref, qa_ref, ckv_ref, kr_ref):
    acc = _dot(h_ref[...], w_ref[...])

    def rms(v, g):
        return v * lax.rsqrt(jnp.mean(v * v, axis=-1, keepdims=True) + NORM_EPS) * g

    q0, q1 = 0, Q_LORA_RANK
    c1 = q1 + KV_LORA_RANK
    qa_ref[...] = rms(acc[:, q0:q1], g_ref[:, q0:q1]).astype(qa_ref.dtype)
    ckv_ref[...] = rms(acc[:, q1:c1], g_ref[:, q1:c1]).astype(ckv_ref.dtype)
    kr = acc[:, c1:c1 + LANES] * cos_ref[...] + acc[:, c1 + LANES:c1 + 2 * LANES] * sin_ref[...]
    kr_ref[...] = kr.astype(kr_ref.dtype)


def _in_proj_a(h, w_a, g_a, cos_t, sin_t):
    m, d = h.shape
    n = w_a.shape[1]
    tm = _tile(m, 768, 640, 512, 256, 128)
    return pl.pallas_call(
        _in_a_kernel,
        grid=(m // tm,),
        in_specs=[pl.BlockSpec((tm, d), lambda i: (i, 0)),
                  pl.BlockSpec((d, n), lambda i: (0, 0)),
                  pl.BlockSpec((1, Q_LORA_RANK + KV_LORA_RANK), lambda i: (0, 0)),
                  pl.BlockSpec((tm, LANES), lambda i: (i, 0)),
                  pl.BlockSpec((tm, LANES), lambda i: (i, 0))],
        out_specs=[pl.BlockSpec((tm, Q_LORA_RANK), lambda i: (i, 0)),
                   pl.BlockSpec((tm, KV_LORA_RANK), lambda i: (i, 0)),
                   pl.BlockSpec((tm, LANES), lambda i: (i, 0))],
        out_shape=[jax.ShapeDtypeStruct((m, Q_LORA_RANK), BF16),
                   jax.ShapeDtypeStruct((m, KV_LORA_RANK), BF16),
                   jax.ShapeDtypeStruct((m, LANES), BF16)],
        compiler_params=_cparams(1),
        name="in_proj_a",
    )(h, w_a, g_a, cos_t, sin_t)


def _in_b_kernel(h_ref, w_ref, o_ref, *, first_gate_tile):
    acc = _dot(h_ref[...], w_ref[...])
    is_gate = pl.program_id(0) >= first_gate_tile

    @pl.when(is_gate)
    def _():
        o_ref[...] = jax.nn.sigmoid(acc).astype(o_ref.dtype)

    @pl.when(jnp.logical_not(is_gate))
    def _():
        o_ref[...] = acc.astype(o_ref.dtype)


def _in_proj_b(h, w_b, first_gate_col):
    m, d = h.shape
    n = w_b.shape[1]
    tm = _tile(m, 1408, 768, 640, 512, 256, 128)
    tn = _tile(first_gate_col, 1024, 512, 256, LANES)
    return pl.pallas_call(
        functools.partial(_in_b_kernel, first_gate_tile=first_gate_col // tn),
        grid=(n // tn, m // tm),
        in_specs=[pl.BlockSpec((tm, d), lambda j, i: (i, 0)),
                  pl.BlockSpec((d, tn), lambda j, i: (0, j))],
        out_specs=pl.BlockSpec((tm, tn), lambda j, i: (i, j)),
        out_shape=jax.ShapeDtypeStruct((m, n), BF16),
        compiler_params=_cparams(2),
        name="in_proj_b",
    )(h, w_b)


def _q_kernel(a_ref, wm_ref, ws_ref, cos_ref, sin_ref, q_ref):
    a = a_ref[...]
    cos = cos_ref[...]
    sin = sin_ref[...]
    for hd in range(N_HEADS):
        main = _dot(a, wm_ref[:, hd * HEAD_W:(hd + 1) * HEAD_W])
        swap = _dot(a, ws_ref[:, hd * LANES:(hd + 1) * LANES])
        q_ref[:, hd * HEAD_W:hd * HEAD_W + LANES] = (main[:, :LANES] * Q_SCALE).astype(q_ref.dtype)
        rot = (main[:, LANES:] * cos + swap * sin) * Q_SCALE
        q_ref[:, hd * HEAD_W + LANES:(hd + 1) * HEAD_W] = rot.astype(q_ref.dtype)


def _q_proj(qa_n, w_main, w_swap, cos_t, sin_t, rows):
    k = qa_n.shape[1]
    tm = _tile(rows, 768, 640, 512, 256, 128)
    return pl.pallas_call(
        _q_kernel,
        grid=(rows // tm,),
        in_specs=[pl.BlockSpec((tm, k), lambda i: (i, 0)),
                  pl.BlockSpec(w_main.shape, lambda i: (0, 0)),
                  pl.BlockSpec(w_swap.shape, lambda i: (0, 0)),
                  pl.BlockSpec((tm, LANES), lambda i: (i, 0)),
                  pl.BlockSpec((tm, LANES), lambda i: (i, 0))],
        out_specs=pl.BlockSpec((tm, N_HEADS * HEAD_W), lambda i: (i, 0)),
        out_shape=jax.ShapeDtypeStruct((rows, N_HEADS * HEAD_W), BF16),
        compiler_params=_cparams(1),
        name="q_proj",
    )(qa_n, w_main, w_swap, cos_t, sin_t)


def _kv_kernel(c_ref, kr_ref, wk_ref, wv_ref, k_ref, v_ref):
    c = c_ref[...]
    kr = kr_ref[...]
    lane = lax.broadcasted_iota(jnp.int32, (c.shape[0], LANES), 1)
    ones_col = jnp.where(lane == 0, 1.0, 0.0).astype(v_ref.dtype)
    for hd in range(N_HEADS):
        kn = _dot(c, wk_ref[:, hd * LANES:(hd + 1) * LANES])
        k_ref[:, hd * HEAD_W:hd * HEAD_W + LANES] = kn.astype(k_ref.dtype)
        k_ref[:, hd * HEAD_W + LANES:(hd + 1) * HEAD_W] = kr
        vh = _dot(c, wv_ref[:, hd * LANES:(hd + 1) * LANES])
        v_ref[:, hd * HEAD_W:hd * HEAD_W + LANES] = vh.astype(v_ref.dtype)
        v_ref[:, hd * HEAD_W + LANES:(hd + 1) * HEAD_W] = ones_col


def _kv_proj(ckv_n, kr, w_k, w_v):
    m, k = ckv_n.shape
    tm = _tile(m, 768, 640, 512, 256, 128)
    return pl.pallas_call(
        _kv_kernel,
        grid=(m // tm,),
        in_specs=[pl.BlockSpec((tm, k), lambda i: (i, 0)),
                  pl.BlockSpec((tm, LANES), lambda i: (i, 0)),
                  pl.BlockSpec(w_k.shape, lambda i: (0, 0)),
                  pl.BlockSpec(w_v.shape, lambda i: (0, 0))],
        out_specs=[pl.BlockSpec((tm, N_HEADS * HEAD_W), lambda i: (i, 0)),
                   pl.BlockSpec((tm, N_HEADS * HEAD_W), lambda i: (i, 0))],
        out_shape=[jax.ShapeDtypeStruct((m, N_HEADS * HEAD_W), BF16),
                   jax.ShapeDtypeStruct((m, N_HEADS * HEAD_W), BF16)],
        compiler_params=_cparams(1),
        name="kv_proj",
    )(ckv_n, kr, w_k, w_v)


def _attn_kernel(q_ref, k_ref, v_ref, o_ref, sa_ref, sb_ref, *, tk, n_chunks):
    q = q_ref[...]
    tq = q.shape[0]

    def scores(c):
        ks = pl.multiple_of(c * tk, tk)
        return lax.dot_general(q, k_ref[pl.ds(ks, tk), :], (((1,), (1,)), ((), ())),
                               preferred_element_type=F32)

    def absorb(s_ref, c, carry):
        m, acc = carry
        s = s_ref[...]
        m_new = jnp.maximum(m, jnp.max(s, axis=-1, keepdims=True))
        p = jnp.exp2(s - m_new).astype(v_ref.dtype)
        ks = pl.multiple_of(c * tk, tk)
        acc = jnp.exp2(m - m_new) * acc + _dot(p, v_ref[pl.ds(ks, tk), :])
        return m_new, acc

    def pair(c2, carry):
        c = 2 * c2
        sb_ref[...] = scores(c + 1)
        carry = absorb(sa_ref, c, carry)
        sa_ref[...] = scores(c + 2)
        return absorb(sb_ref, c + 1, carry)

    carry = (jnp.full((tq, 1), -jnp.inf, F32), jnp.zeros((tq, HEAD_W), F32))
    sa_ref[...] = scores(0)
    n_pairs = (n_chunks - 1) // 2
    carry = lax.fori_loop(0, n_pairs, pair, carry)
    c_last = 2 * n_pairs
    if n_chunks - c_last == 2:
        sb_ref[...] = scores(c_last + 1)
        carry = absorb(sa_ref, c_last, carry)
        carry = absorb(sb_ref, c_last + 1, carry)
    else:
        carry = absorb(sa_ref, c_last, carry)
    _, acc = carry
    o_ref[...] = (acc[:, :V_HEAD_DIM] / acc[:, V_HEAD_DIM:V_HEAD_DIM + 1]).astype(o_ref.dtype)


def _attention(q, k, v, q_row0, n_q, k_row0, n_k):
    tq = _tile(n_q, 512, 256, 128)
    tk = _tile(n_k, 768, 640, 512, 384, 256, 128)
    assert q_row0 % tq == 0 and k_row0 % n_k == 0
    q_blk0 = q_row0 // tq
    k_blk = k_row0 // n_k
    return pl.pallas_call(
        functools.partial(_attn_kernel, tk=tk, n_chunks=n_k // tk),
        grid=(N_HEADS, n_q // tq),
        in_specs=[pl.BlockSpec((tq, HEAD_W), lambda h, i: (q_blk0 + i, h)),
                  pl.BlockSpec((n_k, HEAD_W), lambda h, i: (k_blk, h)),
                  pl.BlockSpec((n_k, HEAD_W), lambda h, i: (k_blk, h))],
        out_specs=pl.BlockSpec((tq, V_HEAD_DIM), lambda h, i: (i, h)),
        out_shape=jax.ShapeDtypeStruct((n_q, N_HEADS * V_HEAD_DIM), BF16),
        scratch_shapes=[pltpu.VMEM((tq, tk), F32), pltpu.VMEM((tq, tk), F32)],
        compiler_params=_cparams(2),
        name="attention",
    )(q, k, v)


def _mix_kernel(attn_ref, cx_ref, cb_ref, cc_ref, cxp_ref, ccp_ref, cxn_ref, ccn_ref, sga_ref, sgb_ref,
                wc_ref, woa_ref, wob_ref, o_ref, z_ref, *, seg_starts, seg_ends, col_chunk):
    i = pl.program_id(0)
    tm, width = z_ref.shape

    @pl.when(pl.program_id(1) == 0)
    def _():
        loc = lax.broadcasted_iota(jnp.int32, (tm, 1), 0)
        row = loc + i * tm
        first = functools.reduce(jnp.logical_or, [row == r for r in seg_starts])
        last = functools.reduce(jnp.logical_or, [row == r for r in seg_ends])
        for c0 in range(0, width, col_chunk):
            cs = slice(c0, c0 + col_chunk)
            u = cx_ref[:, cs].astype(F32) * cc_ref[:, cs].astype(F32)
            u_halo_prev = (cxp_ref[SUBLANES - 1:SUBLANES, cs].astype(F32)
                           * ccp_ref[SUBLANES - 1:SUBLANES, cs].astype(F32))
            u_halo_next = cxn_ref[0:1, cs].astype(F32) * ccn_ref[0:1, cs].astype(F32)
            u_prev = jnp.where(loc == 0, u_halo_prev, pltpu.roll(u, 1, 0))
            u_prev = jnp.where(first, 0.0, u_prev)
            u_next = jnp.where(loc == tm - 1, u_halo_next, pltpu.roll(u, tm - 1, 0))
            u_next = jnp.where(last, 0.0, u_next)
            conv = wc_ref[0:1, cs] * u_prev + wc_ref[1:2, cs] * u + wc_ref[2:3, cs] * u_next
            z_ref[:, cs] = (cb_ref[:, cs].astype(F32) * conv).astype(z_ref.dtype)

    o_a = _dot(attn_ref[...], woa_ref[...])
    o_b = _dot(z_ref[...], wob_ref[...])
    o_ref[...] = (sga_ref[...].astype(F32) * o_a + sgb_ref[...].astype(F32) * o_b).astype(o_ref.dtype)


def _mix(attn, p, w_conv, w_oa, w_ob, rows, s_len, m_len):
    attn_w = w_oa.shape[0]
    width, d = w_ob.shape
    tm = _tile(rows, 768, 512, 384, 256, 128, 64)
    tn = _tile(d, 512, 256, LANES)
    ga_blk = 3 * width // tn
    gb_blk = (3 * width + d) // tn
    hb = tm // SUBLANES
    last_hb = p.shape[0] // SUBLANES - 1

    def prev_map(col):
        return lambda i, j: (jnp.maximum(i * hb - 1, 0), col)

    def next_map(col):
        return lambda i, j: (jnp.minimum((i + 1) * hb, last_hb), col)

    kern = functools.partial(_mix_kernel, seg_starts=(0, s_len), seg_ends=(s_len - 1, m_len - 1),
                             col_chunk=_tile(width, 512, LANES))
    return pl.pallas_call(
        kern,
        grid=(rows // tm, d // tn),
        in_specs=[pl.BlockSpec((tm, attn_w), lambda i, j: (i, 0)),
                  pl.BlockSpec((tm, width), lambda i, j: (i, 0)),
                  pl.BlockSpec((tm, width), lambda i, j: (i, 1)),
                  pl.BlockSpec((tm, width), lambda i, j: (i, 2)),
                  pl.BlockSpec((SUBLANES, width), prev_map(0)),
                  pl.BlockSpec((SUBLANES, width), prev_map(2)),
                  pl.BlockSpec((SUBLANES, width), next_map(0)),
                  pl.BlockSpec((SUBLANES, width), next_map(2)),
                  pl.BlockSpec((tm, tn), lambda i, j: (i, ga_blk + j)),
                  pl.BlockSpec((tm, tn), lambda i, j: (i, gb_blk + j)),
                  pl.BlockSpec((CONV_K, width), lambda i, j: (0, 0)),
                  pl.BlockSpec((attn_w, tn), lambda i, j: (0, j)),
                  pl.BlockSpec((width, tn), lambda i, j: (0, j))],
        out_specs=pl.BlockSpec((tm, tn), lambda i, j: (i, j)),
        out_shape=jax.ShapeDtypeStruct((rows, d), BF16),
        scratch_shapes=[pltpu.VMEM((tm, width), BF16)],
        compiler_params=_cparams(2),
        name="mix",
    )(attn, p, p, p, p, p, p, p, p, p, w_conv, w_oa, w_ob)


def _mm_res_kernel(a_ref, w_ref, x_ref, gt_ref, o_ref, *, n_lat):
    tm = a_ref.shape[0]
    row = lax.broadcasted_iota(jnp.int32, (tm, 1), 0) + pl.program_id(1) * tm
    gate = jnp.where(row >= n_lat, gt_ref[1:2, :], gt_ref[0:1, :])
    o_ref[...] = x_ref[...] + gate * _dot(a_ref[...], w_ref[...])


def _matmul_residual(a, w, x, mods, gate_blk, rows, n_lat):
    k, n = w.shape
    tm = _tile(rows, 768, 640, 512, 256, 128)
    tn = _tile(n, 512, 256, LANES)
    nt = n // tn
    return pl.pallas_call(
        functools.partial(_mm_res_kernel, n_lat=n_lat),
        grid=(nt, rows // tm),
        in_specs=[pl.BlockSpec((tm, k), lambda j, i: (i, 0)),
                  pl.BlockSpec((k, tn), lambda j, i: (0, j)),
                  pl.BlockSpec((tm, tn), lambda j, i: (i, j)),
                  pl.BlockSpec((2, tn), lambda j, i: (0, gate_blk * nt + j))],
        out_specs=pl.BlockSpec((tm, tn), lambda j, i: (i, j)),
        out_shape=jax.ShapeDtypeStruct((rows, n), F32),
        compiler_params=_cparams(2),
        name="matmul_residual",
    )(a, w, x, mods)


def _glu_kernel(eid_ref, a_ref, wg_ref, wu_ref, o_ref):
    del eid_ref
    a = a_ref[...].astype(BF16)
    gate = _dot(a, wg_ref[0])
    up = _dot(a, wu_ref[0])
    o_ref[...] = (gate * jax.nn.sigmoid(gate) * up).astype(o_ref.dtype)


def _glu(a, w_gate, w_up, eid, tm):
    r, k = a.shape
    f = w_gate.shape[2]
    tn = _tile(f, 512, 1408, 256, LANES)
    grid_spec = pltpu.PrefetchScalarGridSpec(
        num_scalar_prefetch=1,
        grid=(f // tn, r // tm),
        in_specs=[pl.BlockSpec((tm, k), lambda j, i, e: (i, 0)),
                  pl.BlockSpec((1, k, tn), lambda j, i, e: (e[i], 0, j)),
                  pl.BlockSpec((1, k, tn), lambda j, i, e: (e[i], 0, j))],
        out_specs=pl.BlockSpec((tm, tn), lambda j, i, e: (i, j)),
    )
    return pl.pallas_call(
        _glu_kernel,
        grid_spec=grid_spec,
        out_shape=jax.ShapeDtypeStruct((r, f), BF16),
        compiler_params=_cparams(2),
        name="glu",
    )(eid, a, w_gate, w_up)


def _down_kernel(eid_ref, a_ref, w_ref, o_ref):
    del eid_ref
    o_ref[...] = _dot(a_ref[...], w_ref[0])


def _down_grouped(a, w_down, eid, tm):
    r, f = a.shape
    d = w_down.shape[2]
    tn = _tile(d, 2048, 1024, 512, 256, LANES)
    grid_spec = pltpu.PrefetchScalarGridSpec(
        num_scalar_prefetch=1,
        grid=(d // tn, r // tm),
        in_specs=[pl.BlockSpec((tm, f), lambda j, i, e: (i, 0)),
                  pl.BlockSpec((1, f, tn), lambda j, i, e: (e[i], 0, j))],
        out_specs=pl.BlockSpec((tm, tn), lambda j, i, e: (i, j)),
    )
    return pl.pallas_call(
        _down_kernel,
        grid_spec=grid_spec,
        out_shape=jax.ShapeDtypeStruct((r, d), F32),
        compiler_params=_cparams(2),
        name="down_grouped",
    )(eid, a, w_down)


def _gather_block(idx_ref, idx_base, idx_stride, src_ref, dst_ref, sem, n_rows):
    def copy(r):
        row = idx_ref[idx_base + r * idx_stride]
        return pltpu.make_async_copy(src_ref.at[pl.ds(row, 1)], dst_ref.at[pl.ds(r, 1)], sem)

    def start(r, c):
        copy(r).start()
        return c

    def wait(r, c):
        copy(r).wait()
        return c

    lax.fori_loop(0, n_rows, start, 0)
    lax.fori_loop(0, n_rows, wait, 0)


def _gather_rows_kernel(idx_ref, src_ref, o_ref, sem):
    tm = o_ref.shape[0]
    _gather_block(idx_ref, pl.program_id(0) * tm, 1, src_ref, o_ref, sem, tm)


def _gather_rows(src, idx, tm):
    n = idx.shape[0]
    d = src.shape[1]
    grid_spec = pltpu.PrefetchScalarGridSpec(
        num_scalar_prefetch=1,
        grid=(n // tm,),
        in_specs=[pl.BlockSpec(memory_space=pl.ANY)],
        out_specs=pl.BlockSpec((tm, d), lambda i, idx_ref: (i, 0)),
        scratch_shapes=[pltpu.SemaphoreType.DMA(())],
    )
    return pl.pallas_call(
        _gather_rows_kernel,
        grid_spec=grid_spec,
        out_shape=jax.ShapeDtypeStruct((n, d), src.dtype),
        compiler_params=_cparams(1),
        name="gather_rows",
    )(idx, src)


def _final_kernel(dest_ref, x_ref, y_hbm_ref, gate_ref, gt_ref, g_ref, o_ref, y_ref, sem):
    tm = x_ref.shape[0]
    base = pl.program_id(0) * tm * TOP_K
    for kk in range(TOP_K):
        _gather_block(dest_ref, base + kk, TOP_K, y_hbm_ref, y_ref.at[kk], sem, tm)
    gates = gate_ref[...]
    y = gates[:, 0:1] * y_ref[0] + gates[:, 1:2] * y_ref[1]
    x = x_ref[...] + gt_ref[0:1, :] * y
    o_ref[...] = x * lax.rsqrt(jnp.mean(x * x, axis=-1, keepdims=True) + NORM_EPS) * g_ref[...]


def _combine_final(x, ybuf, dest, gates, mods, gate_blk, g_final):
    r, d = x.shape
    tm = _tile(r, 256, 128, 64, SUBLANES)
    grid_spec = pltpu.PrefetchScalarGridSpec(
        num_scalar_prefetch=1,
        grid=(r // tm,),
        in_specs=[pl.BlockSpec((tm, d), lambda i, dest_ref: (i, 0)),
                  pl.BlockSpec(memory_space=pl.ANY),
                  pl.BlockSpec((tm, LANES), lambda i, dest_ref: (i, 0)),
                  pl.BlockSpec((2, d), lambda i, dest_ref: (0, gate_blk)),
                  pl.BlockSpec((1, d), lambda i, dest_ref: (0, 0))],
        out_specs=pl.BlockSpec((tm, d), lambda i, dest_ref: (i, 0)),
        scratch_shapes=[pltpu.VMEM((TOP_K, tm, d), F32), pltpu.SemaphoreType.DMA(())],
    )
    return pl.pallas_call(
        _final_kernel,
        grid_spec=grid_spec,
        out_shape=jax.ShapeDtypeStruct((r, d), F32),
        compiler_params=_cparams(1),
        name="combine_final",
    )(dest, x, ybuf, gates, mods, g_final.reshape(1, d))


def _rope_tables(s_len, c_len):
    rows = s_len // GRID_W
    row = jnp.repeat(jnp.arange(rows), GRID_W).astype(F32)
    col = jnp.tile(jnp.arange(GRID_W), rows).astype(F32)
    half = QK_ROPE_DIM // 2
    inv_freq = jnp.power(ROPE_THETA, -jnp.arange(0, half, 2, dtype=F32) / half)
    ang_r = row[:, None] * inv_freq
    ang_c = col[:, None] * inv_freq
    pad = jnp.zeros((s_len, LANES - QK_ROPE_DIM), F32)
    cos = jnp.concatenate([jnp.cos(ang_r), jnp.cos(ang_r), jnp.cos(ang_c), jnp.cos(ang_c), pad], axis=1)
    sin = jnp.concatenate([-jnp.sin(ang_r), jnp.sin(ang_r), -jnp.sin(ang_c), jnp.sin(ang_c), pad], axis=1)
    cos_c = jnp.concatenate([jnp.ones((c_len, QK_ROPE_DIM), F32), jnp.zeros((c_len, LANES - QK_ROPE_DIM), F32)], 1)
    return jnp.concatenate([cos, cos_c], axis=0), jnp.concatenate([sin, jnp.zeros((c_len, LANES), F32)], axis=0)


def _rope_swap_perm():
    q = QK_ROPE_DIM // 4
    return jnp.concatenate([jnp.arange(q, 2 * q), jnp.arange(0, q), jnp.arange(3 * q, 4 * q), jnp.arange(2 * q, 3 * q)])


def _layer_weights(w_in, w_qb, w_kvb):
    d = w_in.shape[0]
    perm = _rope_swap_perm()
    kv_lo = Q_LORA_RANK
    kr_lo = Q_LORA_RANK + KV_LORA_RANK
    kr_hi = kr_lo + QK_ROPE_DIM
    w_kr = w_in[:, kr_lo:kr_hi]
    zpad = jnp.zeros((d, LANES - QK_ROPE_DIM), w_in.dtype)
    w_a = jnp.concatenate([w_in[:, :kv_lo], w_in[:, kv_lo:kr_lo], w_kr, zpad, w_kr[:, perm], zpad], axis=1)
    w_b = w_in[:, kr_hi:]
    qb = w_qb.reshape(Q_LORA_RANK, N_HEADS, QK_NOPE_DIM + QK_ROPE_DIM)
    q_rope = qb[:, :, QK_NOPE_DIM:]
    hpad = jnp.zeros((Q_LORA_RANK, N_HEADS, LANES - QK_ROPE_DIM), w_qb.dtype)
    w_q_main = jnp.concatenate([qb, hpad], axis=2).reshape(Q_LORA_RANK, N_HEADS * HEAD_W)
    w_q_swap = jnp.concatenate([q_rope[:, :, perm], hpad], axis=2).reshape(Q_LORA_RANK, N_HEADS * LANES)
    kvb = w_kvb.reshape(KV_LORA_RANK, N_HEADS, QK_NOPE_DIM + V_HEAD_DIM)
    w_k = kvb[:, :, :QK_NOPE_DIM].reshape(KV_LORA_RANK, N_HEADS * QK_NOPE_DIM)
    w_v = kvb[:, :, QK_NOPE_DIM:].reshape(KV_LORA_RANK, N_HEADS * V_HEAD_DIM)
    return tuple(w.astype(BF16) for w in (w_a, w_b, w_q_main, w_q_swap, w_k, w_v))


def _moe_slots(idx, n_blocks):
    e_flat = idx.reshape(-1)
    onehot = (e_flat[:, None] == jnp.arange(N_EXPERTS)[None, :]).astype(jnp.int32)
    csum = jnp.cumsum(onehot, axis=0)
    counts = csum[-1]
    rank = jnp.sum((csum - onehot) * onehot, axis=1)
    padded = (counts + MOE_ROWS - 1) // MOE_ROWS * MOE_ROWS
    p_end = jnp.cumsum(padded)
    p_start = p_end - padded
    dest = jnp.sum(onehot * p_start[None, :], axis=1) + rank
    blk_lo = jnp.arange(n_blocks) * MOE_ROWS
    block_expert = jnp.minimum(jnp.sum((blk_lo[:, None] >= p_end[None, :]).astype(jnp.int32), axis=1),
                               N_EXPERTS - 1)
    return dest.astype(jnp.int32), block_expert.astype(jnp.int32)


def kernel(x, c, ctx, c_ctx, w_ada, b_ada, g_attn, w_in, g_qa, w_qb, g_kva, w_kvb, w_conv, w_oa, w_ob, w_o,
           g_ffn, w_gate_dense, w_up_dense, w_down_dense, w_router, w_gate_exp, w_up_exp, w_down_exp, g_final):
    _, s_len, d = x.shape
    c_len = ctx.shape[1]
    m_len = s_len + c_len
    depth = w_in.shape[0]
    width = w_conv.shape[2]

    mods_all = _ada_mod(c, c_ctx, w_ada, b_ada)
    cos_t, sin_t = _rope_tables(s_len, c_len)
    xa = jnp.concatenate([x[0], ctx[0]], axis=0)

    for li in range(depth):
        last = li == depth - 1
        mods = mods_all[li]
        w_a, w_b, w_q_main, w_q_swap, w_k, w_v = _layer_weights(w_in[li], w_qb[li], w_kvb[li])
        g_a = jnp.concatenate([g_qa[li], g_kva[li]]).reshape(1, -1)
        rows = s_len if last else m_len

        h = _norm_modulate(xa, g_attn[li], mods, 0, 1, s_len, BF16)
        qa_n, ckv_n, kr = _in_proj_a(h, w_a, g_a, cos_t, sin_t)
        p = _in_proj_b(h, w_b, 3 * width)
        q = _q_proj(qa_n, w_q_main, w_q_swap, cos_t, sin_t, rows)
        k, v = _kv_proj(ckv_n, kr, w_k, w_v)
        attn = _attention(q, k, v, 0, s_len, 0, m_len)
        if not last:
            attn = jnp.concatenate([attn, _attention(q, k, v, s_len, c_len, s_len, c_len)], axis=0)
        merged = _mix(attn, p, w_conv[li], w_oa[li].astype(BF16), w_ob[li].astype(BF16), rows, s_len, m_len)
        xa = _matmul_residual(merged, w_o[li].astype(BF16), xa, mods, 2, rows, s_len)

        j = li // 2
        if li % 2 == 0:
            h2 = _norm_modulate(xa, g_ffn[li], mods, 3, 4, s_len, BF16)
            tm = _tile(rows, 768, 640, 512, 256, 128)
            eid = jnp.zeros((rows // tm,), jnp.int32)
            hid = _glu(h2, w_gate_dense[j][None].astype(BF16), w_up_dense[j][None].astype(BF16), eid, tm)
            xa = _matmul_residual(hid, w_down_dense[j].astype(BF16), xa, mods, 5, rows, s_len)
        else:
            assert last, "expert layers are only supported as the final layer"
            h2, idx, gates = _norm_modulate_route(xa, g_ffn[li], mods, 3, 4, w_router[j])
            n_assign = rows * TOP_K
            n_blocks = -(-n_assign // MOE_ROWS) + N_EXPERTS
            dest, block_expert = _moe_slots(idx[:, :TOP_K], n_blocks)
            tok = jnp.arange(n_assign, dtype=jnp.int32) // TOP_K
            slot_tok = jnp.zeros((n_blocks * MOE_ROWS,), jnp.int32).at[dest].set(tok, unique_indices=True)
            buf = _gather_rows(h2, slot_tok, MOE_ROWS)
            hid = _glu(buf, w_gate_exp[j].astype(BF16), w_up_exp[j].astype(BF16), block_expert, MOE_ROWS)
            ybuf = _down_grouped(hid, w_down_exp[j].astype(BF16), block_expert, MOE_ROWS)
            return _combine_final(xa, ybuf, dest, gates, mods, 5, g_final)[None]
    raise NotImplementedError("a dense final layer is not needed for this problem")
```

```python
import functools

import jax
import jax.numpy as jnp
from jax import lax
from jax.experimental import pallas as pl
from jax.experimental.pallas import tpu as pltpu

F32 = jnp.float32
BF16 = jnp.bfloat16

N_HEADS = 16
QK_NOPE_DIM = 128
QK_ROPE_DIM = 64
V_HEAD_DIM = 128
Q_LORA_RANK = 512
KV_LORA_RANK = 512
GRID_W = 64
ROPE_THETA = 10000.0
ATTN_SCALE = (QK_NOPE_DIM + QK_ROPE_DIM) ** -0.5
Q_SCALE = ATTN_SCALE * 1.4426950408889634
CONV_K = 3
N_EXPERTS = 8
TOP_K = 2
NORM_EPS = 1e-6

LANES = 128
SUBLANES = 8
HEAD_W = 2 * LANES
VMEM_LIMIT_BYTES = 56 * 1024 * 1024

MOE_ROWS = 256
DMA_ISSUE_UNROLL = 8


def _tile(n, *cands):
    for c in cands:
        if n % c == 0:
            return c
    return n


def _cparams(n_axes):
    return pltpu.CompilerParams(dimension_semantics=("arbitrary",) * n_axes,
                                vmem_limit_bytes=VMEM_LIMIT_BYTES)


def _dot(a, b):
    return jnp.dot(a, b, preferred_element_type=F32)


def _ada_kernel(xt_ref, w_ref, b_ref, o_ref, *, k_chunk):
    d = xt_ref.shape[0]
    tn = o_ref.shape[-1]

    def body(k, acc):
        a0, a1 = acc
        ks = pl.multiple_of(k * k_chunk, k_chunk)
        xt = xt_ref[pl.ds(ks, k_chunk), :]
        s = xt * jax.nn.sigmoid(xt)
        w = w_ref[0, pl.ds(ks, k_chunk), :]
        a0 = a0 + jnp.sum(w * s[:, 0:1], axis=0, keepdims=True)
        a1 = a1 + jnp.sum(w * s[:, 1:2], axis=0, keepdims=True)
        return a0, a1

    z = jnp.zeros((1, tn), F32)
    a0, a1 = lax.fori_loop(0, d // k_chunk, body, (z, z))
    o_ref[0, 0:1, :] = a0 + b_ref[0]
    o_ref[0, 1:2, :] = a1 + b_ref[0]


def _ada_mod(c, c_ctx, w_ada, b_ada):
    depth, d, n = w_ada.shape
    xt = jnp.stack([c[0], c_ctx], axis=1)
    tn = _tile(n, 1024, 512, LANES)
    k_chunk = _tile(d, 256, SUBLANES)
    return pl.pallas_call(
        functools.partial(_ada_kernel, k_chunk=k_chunk),
        grid=(depth, n // tn),
        in_specs=[pl.BlockSpec((d, 2), lambda l, j: (0, 0)),
                  pl.BlockSpec((1, d, tn), lambda l, j: (l, 0, j)),
                  pl.BlockSpec((1, 1, tn), lambda l, j: (l, 0, j))],
        out_specs=pl.BlockSpec((1, 2, tn), lambda l, j: (l, 0, j)),
        out_shape=jax.ShapeDtypeStruct((depth, 2, n), F32),
        compiler_params=_cparams(2),
        name="ada_mod",
    )(xt, w_ada, b_ada.reshape(depth, 1, n))


def _norm_mod(x, g_ref, sh_ref, sc_ref, is_ctx):
    y = x * lax.rsqrt(jnp.mean(x * x, axis=-1, keepdims=True) + NORM_EPS) * g_ref[...]
    sh = jnp.where(is_ctx, sh_ref[1:2, :], sh_ref[0:1, :])
    sc = jnp.where(is_ctx, sc_ref[1:2, :], sc_ref[0:1, :])
    return y * (1.0 + sc) + sh


def _norm_kernel(x_ref, g_ref, sh_ref, sc_ref, o_ref, *, n_lat_tiles):
    is_ctx = pl.program_id(0) >= n_lat_tiles
    o_ref[...] = _norm_mod(x_ref[...], g_ref, sh_ref, sc_ref, is_ctx).astype(o_ref.dtype)


def _norm_router_kernel(x_ref, g_ref, sh_ref, sc_ref, wr_ref, h_ref, idx_ref, gate_ref):
    h = _norm_mod(x_ref[...], g_ref, sh_ref, sc_ref, False)
    h_ref[...] = h
    logits = jnp.dot(h, wr_ref[...], preferred_element_type=F32, precision=lax.Precision.HIGHEST)
    lane = lax.broadcasted_iota(jnp.int32, logits.shape, 1).astype(F32)
    neg = jnp.float32(-jnp.inf)
    l1 = jnp.where(lane < N_EXPERTS, logits, neg)
    v1 = jnp.max(l1, axis=-1, keepdims=True)
    i1 = jnp.min(jnp.where(l1 == v1, lane, float(LANES)), axis=-1, keepdims=True)
    l2 = jnp.where(lane == i1, neg, l1)
    v2 = jnp.max(l2, axis=-1, keepdims=True)
    i2 = jnp.min(jnp.where(l2 == v2, lane, float(LANES)), axis=-1, keepdims=True)
    e = jnp.exp(v2 - v1)
    g1 = 1.0 / (1.0 + e)
    g2 = e / (1.0 + e)
    idx_ref[...] = jnp.where(lane == 0, i1, jnp.where(lane == 1, i2, 0.0)).astype(jnp.int32)
    gate_ref[...] = jnp.where(lane == 0, g1, jnp.where(lane == 1, g2, 0.0))


def _norm_modulate(x, g, mods, sh_blk, sc_blk, n_lat, out_dtype):
    r, d = x.shape
    tm = _tile(n_lat, 256, 128, 64, SUBLANES) if r > n_lat else _tile(r, 256, 128, 64, SUBLANES)
    if r > n_lat:
        tm = _tile(r - n_lat, tm, 128, 64, SUBLANES)
    return pl.pallas_call(
        functools.partial(_norm_kernel, n_lat_tiles=n_lat // tm),
        grid=(r // tm,),
        in_specs=[pl.BlockSpec((tm, d), lambda i: (i, 0)),
                  pl.BlockSpec((1, d), lambda i: (0, 0)),
                  pl.BlockSpec((2, d), lambda i: (0, sh_blk)),
                  pl.BlockSpec((2, d), lambda i: (0, sc_blk))],
        out_specs=pl.BlockSpec((tm, d), lambda i: (i, 0)),
        out_shape=jax.ShapeDtypeStruct((r, d), out_dtype),
        compiler_params=_cparams(1),
        name="norm_modulate",
    )(x, g.reshape(1, d), mods, mods)


def _norm_modulate_route(x, g, mods, sh_blk, sc_blk, w_router):
    r, d = x.shape
    tm = _tile(r, 256, 128, 64, SUBLANES)
    wr = jnp.pad(w_router, ((0, 0), (0, LANES - N_EXPERTS)))
    return pl.pallas_call(
        _norm_router_kernel,
        grid=(r // tm,),
        in_specs=[pl.BlockSpec((tm, d), lambda i: (i, 0)),
                  pl.BlockSpec((1, d), lambda i: (0, 0)),
                  pl.BlockSpec((2, d), lambda i: (0, sh_blk)),
                  pl.BlockSpec((2, d), lambda i: (0, sc_blk)),
                  pl.BlockSpec((d, LANES), lambda i: (0, 0))],
        out_specs=[pl.BlockSpec((tm, d), lambda i: (i, 0)),
                   pl.BlockSpec((tm, LANES), lambda i: (i, 0)),
                   pl.BlockSpec((tm, LANES), lambda i: (i, 0))],
        out_shape=[jax.ShapeDtypeStruct((r, d), F32),
                   jax.ShapeDtypeStruct((r, LANES), jnp.int32),
                   jax.ShapeDtypeStruct((r, LANES), F32)],
        compiler_params=_cparams(1),
        name="norm_modulate_route",
    )(x, g.reshape(1, d), mods, mods, wr)


def _in_a_kernel(h_ref, w_ref, g_ref, cos_ref, sin_ref, qa_ref, ckv_ref, kr_ref):
    acc = _dot(h_ref[...], w_ref[...])

    def rms(v, g):
        return v * lax.rsqrt(jnp.mean(v * v, axis=-1, keepdims=True) + NORM_EPS) * g

    q0, q1 = 0, Q_LORA_RANK
    c1 = q1 + KV_LORA_RANK
    qa_ref[...] = rms(acc[:, q0:q1], g_ref[:, q0:q1]).astype(qa_ref.dtype)
    ckv_ref[...] = rms(acc[:, q1:c1], g_ref[:, q1:c1]).astype(ckv_ref.dtype)
    kr = acc[:, c1:c1 + LANES] * cos_ref[...] + acc[:, c1 + LANES:c1 + 2 * LANES] * sin_ref[...]
    kr_ref[...] = kr.astype(kr_ref.dtype)


def _in_proj_a(h, w_a, g_a, cos_t, sin_t):
    m, d = h.shape
    n = w_a.shape[1]
    tm = _tile(m, 768, 640, 512, 256, 128)
    return pl.pallas_call(
        _in_a_kernel,
        grid=(m // tm,),
        in_specs=[pl.BlockSpec((tm, d), lambda i: (i, 0)),
                  pl.BlockSpec((d, n), lambda i: (0, 0)),
                  pl.BlockSpec((1, Q_LORA_RANK + KV_LORA_RANK), lambda i: (0, 0)),
                  pl.BlockSpec((tm, LANES), lambda i: (i, 0)),
                  pl.BlockSpec((tm, LANES), lambda i: (i, 0))],
        out_specs=[pl.BlockSpec((tm, Q_LORA_RANK), lambda i: (i, 0)),
                   pl.BlockSpec((tm, KV_LORA_RANK), lambda i: (i, 0)),
                   pl.BlockSpec((tm, LANES), lambda i: (i, 0))],
        out_shape=[jax.ShapeDtypeStruct((m, Q_LORA_RANK), BF16),
                   jax.ShapeDtypeStruct((m, KV_LORA_RANK), BF16),
                   jax.ShapeDtypeStruct((m, LANES), BF16)],
        compiler_params=_cparams(1),
        name="in_proj_a",
    )(h, w_a, g_a, cos_t, sin_t)


def _in_b_kernel(h_ref, w_ref, o_ref, *, first_gate_tile):
    acc = _dot(h_ref[...], w_ref[...])
    is_gate = pl.program_id(0) >= first_gate_tile

    @pl.when(is_gate)
    def _():
        o_ref[...] = jax.nn.sigmoid(acc).astype(o_ref.dtype)

    @pl.when(jnp.logical_not(is_gate))
    def _():
        o_ref[...] = acc.astype(o_ref.dtype)


def _in_proj_b(h, w_b, first_gate_col):
    m, d = h.shape
    n = w_b.shape[1]
    tm = _tile(m, 1408, 768, 640, 512, 256, 128)
    tn = _tile(first_gate_col, 1024, 512, 256, LANES)
    return pl.pallas_call(
        functools.partial(_in_b_kernel, first_gate_tile=first_gate_col // tn),
        grid=(n // tn, m // tm),
        in_specs=[pl.BlockSpec((tm, d), lambda j, i: (i, 0)),
                  pl.BlockSpec((d, tn), lambda j, i: (0, j))],
        out_specs=pl.BlockSpec((tm, tn), lambda j, i: (i, j)),
        out_shape=jax.ShapeDtypeStruct((m, n), BF16),
        compiler_params=_cparams(2),
        name="in_proj_b",
    )(h, w_b)


def _q_kernel(a_ref, wm_ref, ws_ref, cos_ref, sin_ref, q_ref):
    a = a_ref[...]
    cos = cos_ref[...]
    sin = sin_ref[...]
    for hd in range(N_HEADS):
        main = _dot(a, wm_ref[:, hd * HEAD_W:(hd + 1) * HEAD_W])
        swap = _dot(a, ws_ref[:, hd * LANES:(hd + 1) * LANES])
        q_ref[:, hd * HEAD_W:hd * HEAD_W + LANES] = (main[:, :LANES] * Q_SCALE).astype(q_ref.dtype)
        rot = (main[:, LANES:] * cos + swap * sin) * Q_SCALE
        q_ref[:, hd * HEAD_W + LANES:(hd + 1) * HEAD_W] = rot.astype(q_ref.dtype)


def _q_proj(qa_n, w_main, w_swap, cos_t, sin_t, rows):
    k = qa_n.shape[1]
    tm = _tile(rows, 768, 640, 512, 256, 128)
    return pl.pallas_call(
        _q_kernel,
        grid=(rows // tm,),
        in_specs=[pl.BlockSpec((tm, k), lambda i: (i, 0)),
                  pl.BlockSpec(w_main.shape, lambda i: (0, 0)),
                  pl.BlockSpec(w_swap.shape, lambda i: (0, 0)),
                  pl.BlockSpec((tm, LANES), lambda i: (i, 0)),
                  pl.BlockSpec((tm, LANES), lambda i: (i, 0))],
        out_specs=pl.BlockSpec((tm, N_HEADS * HEAD_W), lambda i: (i, 0)),
        out_shape=jax.ShapeDtypeStruct((rows, N_HEADS * HEAD_W), BF16),
        compiler_params=_cparams(1),
        name="q_proj",
    )(qa_n, w_main, w_swap, cos_t, sin_t)


def _kv_kernel(c_ref, kr_ref, wk_ref, wv_ref, k_ref, v_ref):
    c = c_ref[...]
    kr = kr_ref[...]
    lane = lax.broadcasted_iota(jnp.int32, (c.shape[0], LANES), 1)
    ones_col = jnp.where(lane == 0, 1.0, 0.0).astype(v_ref.dtype)
    for hd in range(N_HEADS):
        kn = _dot(c, wk_ref[:, hd * LANES:(hd + 1) * LANES])
        k_ref[:, hd * HEAD_W:hd * HEAD_W + LANES] = kn.astype(k_ref.dtype)
        k_ref[:, hd * HEAD_W + LANES:(hd + 1) * HEAD_W] = kr
        vh = _dot(c, wv_ref[:, hd * LANES:(hd + 1) * LANES])
        v_ref[:, hd * HEAD_W:hd * HEAD_W + LANES] = vh.astype(v_ref.dtype)
        v_ref[:, hd * HEAD_W + LANES:(hd + 1) * HEAD_W] = ones_col


def _kv_proj(ckv_n, kr, w_k, w_v):
    m, k = ckv_n.shape
    tm = _tile(m, 768, 640, 512, 256, 128)
    return pl.pallas_call(
        _kv_kernel,
        grid=(m // tm,),
        in_specs=[pl.BlockSpec((tm, k), lambda i: (i, 0)),
                  pl.BlockSpec((tm, LANES), lambda i: (i, 0)),
                  pl.BlockSpec(w_k.shape, lambda i: (0, 0)),
                  pl.BlockSpec(w_v.shape, lambda i: (0, 0))],
        out_specs=[pl.BlockSpec((tm, N_HEADS * HEAD_W), lambda i: (i, 0)),
                   pl.BlockSpec((tm, N_HEADS * HEAD_W), lambda i: (i, 0))],
        out_shape=[jax.ShapeDtypeStruct((m, N_HEADS * HEAD_W), BF16),
                   jax.ShapeDtypeStruct((m, N_HEADS * HEAD_W), BF16)],
        compiler_params=_cparams(1),
        name="kv_proj",
    )(ckv_n, kr, w_k, w_v)


def _attn_kernel(q_ref, k_ref, v_ref, o_ref, sa_ref, sb_ref, *, tk, n_chunks):
    q = q_ref[...]
    tq = q.shape[0]

    def scores(c):
        return lax.dot_general(q, k_ref[c * tk:(c + 1) * tk, :], (((1,), (1,)), ((), ())),
                               preferred_element_type=F32)

    def absorb(s_ref, c, carry):
        m, acc = carry
        s = s_ref[...]
        m_new = jnp.maximum(m, jnp.max(s, axis=-1, keepdims=True))
        p = jnp.exp2(s - m_new).astype(v_ref.dtype)
        acc = jnp.exp2(m - m_new) * acc + _dot(p, v_ref[c * tk:(c + 1) * tk, :])
        return m_new, acc

    s_refs = (sa_ref, sb_ref)
    carry = (jnp.full((tq, 1), -jnp.inf, F32), jnp.zeros((tq, HEAD_W), F32))
    sa_ref[...] = scores(0)
    for c in range(n_chunks):
        if c + 1 < n_chunks:
            s_refs[(c + 1) % 2][...] = scores(c + 1)
        carry = absorb(s_refs[c % 2], c, carry)
    _, acc = carry
    o_ref[...] = (acc[:, :V_HEAD_DIM] / acc[:, V_HEAD_DIM:V_HEAD_DIM + 1]).astype(o_ref.dtype)


def _attention(q, k, v, q_row0, n_q, k_row0, n_k):
    tq = _tile(n_q, 512, 256, 128)
    tk = _tile(n_k, 768, 640, 512, 384, 256, 128)
    assert q_row0 % tq == 0 and k_row0 % n_k == 0
    q_blk0 = q_row0 // tq
    k_blk = k_row0 // n_k
    return pl.pallas_call(
        functools.partial(_attn_kernel, tk=tk, n_chunks=n_k // tk),
        grid=(N_HEADS, n_q // tq),
        in_specs=[pl.BlockSpec((tq, HEAD_W), lambda h, i: (q_blk0 + i, h)),
                  pl.BlockSpec((n_k, HEAD_W), lambda h, i: (k_blk, h)),
                  pl.BlockSpec((n_k, HEAD_W), lambda h, i: (k_blk, h))],
        out_specs=pl.BlockSpec((tq, V_HEAD_DIM), lambda h, i: (i, h)),
        out_shape=jax.ShapeDtypeStruct((n_q, N_HEADS * V_HEAD_DIM), BF16),
        scratch_shapes=[pltpu.VMEM((tq, tk), F32), pltpu.VMEM((tq, tk), F32)],
        compiler_params=_cparams(2),
        name="attention",
    )(q, k, v)


def _mix_kernel(attn_ref, cx_ref, cb_ref, cc_ref, cxp_ref, ccp_ref, cxn_ref, ccn_ref, sga_ref, sgb_ref,
                wc_ref, woa_ref, wob_ref, o_ref, z_ref, *, seg_starts, seg_ends, col_chunk):
    i = pl.program_id(0)
    tm, width = z_ref.shape

    @pl.when(pl.program_id(1) == 0)
    def _():
        loc = lax.broadcasted_iota(jnp.int32, (tm, 1), 0)
        row = loc + i * tm
        first = functools.reduce(jnp.logical_or, [row == r for r in seg_starts])
        last = functools.reduce(jnp.logical_or, [row == r for r in seg_ends])
        for c0 in range(0, width, col_chunk):
            cs = slice(c0, c0 + col_chunk)
            u = cx_ref[:, cs].astype(F32) * cc_ref[:, cs].astype(F32)
            u_halo_prev = (cxp_ref[SUBLANES - 1:SUBLANES, cs].astype(F32)
                           * ccp_ref[SUBLANES - 1:SUBLANES, cs].astype(F32))
            u_halo_next = cxn_ref[0:1, cs].astype(F32) * ccn_ref[0:1, cs].astype(F32)
            u_prev = jnp.where(loc == 0, u_halo_prev, pltpu.roll(u, 1, 0))
            u_prev = jnp.where(first, 0.0, u_prev)
            u_next = jnp.where(loc == tm - 1, u_halo_next, pltpu.roll(u, tm - 1, 0))
            u_next = jnp.where(last, 0.0, u_next)
            conv = wc_ref[0:1, cs] * u_prev + wc_ref[1:2, cs] * u + wc_ref[2:3, cs] * u_next
            z_ref[:, cs] = (cb_ref[:, cs].astype(F32) * conv).astype(z_ref.dtype)

    o_a = _dot(attn_ref[...], woa_ref[...])
    o_b = _dot(z_ref[...], wob_ref[...])
    o_ref[...] = (sga_ref[...].astype(F32) * o_a + sgb_ref[...].astype(F32) * o_b).astype(o_ref.dtype)


def _mix(attn, p, w_conv, w_oa, w_ob, rows, s_len, m_len):
    attn_w = w_oa.shape[0]
    width, d = w_ob.shape
    tm = _tile(rows, 768, 512, 384, 256, 128, 64)
    tn = _tile(d, 512, 256, LANES)
    ga_blk = 3 * width // tn
    gb_blk = (3 * width + d) // tn
    hb = tm // SUBLANES
    last_hb = p.shape[0] // SUBLANES - 1

    def prev_map(col):
        return lambda i, j: (jnp.maximum(i * hb - 1, 0), col)

    def next_map(col):
        return lambda i, j: (jnp.minimum((i + 1) * hb, last_hb), col)

    kern = functools.partial(_mix_kernel, seg_starts=(0, s_len), seg_ends=(s_len - 1, m_len - 1),
                             col_chunk=_tile(width, 512, LANES))
    return pl.pallas_call(
        kern,
        grid=(rows // tm, d // tn),
        in_specs=[pl.BlockSpec((tm, attn_w), lambda i, j: (i, 0)),
                  pl.BlockSpec((tm, width), lambda i, j: (i, 0)),
                  pl.BlockSpec((tm, width), lambda i, j: (i, 1)),
                  pl.BlockSpec((tm, width), lambda i, j: (i, 2)),
                  pl.BlockSpec((SUBLANES, width), prev_map(0)),
                  pl.BlockSpec((SUBLANES, width), prev_map(2)),
                  pl.BlockSpec((SUBLANES, width), next_map(0)),
                  pl.BlockSpec((SUBLANES, width), next_map(2)),
                  pl.BlockSpec((tm, tn), lambda i, j: (i, ga_blk + j)),
                  pl.BlockSpec((tm, tn), lambda i, j: (i, gb_blk + j)),
                  pl.BlockSpec((CONV_K, width), lambda i, j: (0, 0)),
                  pl.BlockSpec((attn_w, tn), lambda i, j: (0, j)),
                  pl.BlockSpec((width, tn), lambda i, j: (0, j))],
        out_specs=pl.BlockSpec((tm, tn), lambda i, j: (i, j)),
        out_shape=jax.ShapeDtypeStruct((rows, d), BF16),
        scratch_shapes=[pltpu.VMEM((tm, width), BF16)],
        compiler_params=_cparams(2),
        name="mix",
    )(attn, p, p, p, p, p, p, p, p, p, w_conv, w_oa, w_ob)


def _mm_res_kernel(a_ref, w_ref, x_ref, gt_ref, o_ref, *, n_lat):
    tm = a_ref.shape[0]
    row = lax.broadcasted_iota(jnp.int32, (tm, 1), 0) + pl.program_id(1) * tm
    gate = jnp.where(row >= n_lat, gt_ref[1:2, :], gt_ref[0:1, :])
    o_ref[...] = x_ref[...] + gate * _dot(a_ref[...], w_ref[...])


def _matmul_residual(a, w, x, mods, gate_blk, rows, n_lat):
    k, n = w.shape
    tm = _tile(rows, 768, 640, 512, 256, 128)
    tn = _tile(n, 512, 256, LANES)
    nt = n // tn
    return pl.pallas_call(
        functools.partial(_mm_res_kernel, n_lat=n_lat),
        grid=(nt, rows // tm),
        in_specs=[pl.BlockSpec((tm, k), lambda j, i: (i, 0)),
                  pl.BlockSpec((k, tn), lambda j, i: (0, j)),
                  pl.BlockSpec((tm, tn), lambda j, i: (i, j)),
                  pl.BlockSpec((2, tn), lambda j, i: (0, gate_blk * nt + j))],
        out_specs=pl.BlockSpec((tm, tn), lambda j, i: (i, j)),
        out_shape=jax.ShapeDtypeStruct((rows, n), F32),
        compiler_params=_cparams(2),
        name="matmul_residual",
    )(a, w, x, mods)


def _block_in_use(eid_ref):
    return pl.program_id(1) < eid_ref[pl.num_programs(1)]


def _glu_kernel(eid_ref, a_ref, wg_ref, wu_ref, o_ref):
    @pl.when(_block_in_use(eid_ref))
    def _():
        a = a_ref[...].astype(BF16)
        gate = _dot(a, wg_ref[0])
        up = _dot(a, wu_ref[0])
        o_ref[...] = (gate * jax.nn.sigmoid(gate) * up).astype(o_ref.dtype)

    @pl.when(jnp.logical_not(_block_in_use(eid_ref)))
    def _():
        o_ref[...] = jnp.zeros_like(o_ref)


def _glu(a, w_gate, w_up, eid, tm):
    r, k = a.shape
    f = w_gate.shape[2]
    tn = _tile(f, 512, 1408, 256, LANES)
    grid_spec = pltpu.PrefetchScalarGridSpec(
        num_scalar_prefetch=1,
        grid=(f // tn, r // tm),
        in_specs=[pl.BlockSpec((tm, k), lambda j, i, e: (i, 0)),
                  pl.BlockSpec((1, k, tn), lambda j, i, e: (e[i], 0, j)),
                  pl.BlockSpec((1, k, tn), lambda j, i, e: (e[i], 0, j))],
        out_specs=pl.BlockSpec((tm, tn), lambda j, i, e: (i, j)),
    )
    return pl.pallas_call(
        _glu_kernel,
        grid_spec=grid_spec,
        out_shape=jax.ShapeDtypeStruct((r, f), BF16),
        compiler_params=_cparams(2),
        name="glu",
    )(eid, a, w_gate, w_up)


def _down_kernel(eid_ref, a_ref, w_ref, o_ref):
    @pl.when(_block_in_use(eid_ref))
    def _():
        o_ref[...] = _dot(a_ref[...], w_ref[0])

    @pl.when(jnp.logical_not(_block_in_use(eid_ref)))
    def _():
        o_ref[...] = jnp.zeros_like(o_ref)


def _down_grouped(a, w_down, eid, tm):
    r, f = a.shape
    d = w_down.shape[2]
    tn = _tile(d, 2048, 1024, 512, 256, LANES)
    grid_spec = pltpu.PrefetchScalarGridSpec(
        num_scalar_prefetch=1,
        grid=(d // tn, r // tm),
        in_specs=[pl.BlockSpec((tm, f), lambda j, i, e: (i, 0)),
                  pl.BlockSpec((1, f, tn), lambda j, i, e: (e[i], 0, j))],
        out_specs=pl.BlockSpec((tm, tn), lambda j, i, e: (i, j)),
    )
    return pl.pallas_call(
        _down_kernel,
        grid_spec=grid_spec,
        out_shape=jax.ShapeDtypeStruct((r, d), F32),
        compiler_params=_cparams(2),
        name="down_grouped",
    )(eid, a, w_down)


def _gather_block(idx_ref, idx_base, idx_stride, src_ref, dst_ref, sem, n_rows):
    def copy(r):
        row = idx_ref[idx_base + r * idx_stride]
        return pltpu.make_async_copy(src_ref.at[pl.ds(row, 1)], dst_ref.at[pl.ds(r, 1)], sem)

    def start(r, c):
        copy(r).start()
        return c

    def wait(r, c):
        copy(r).wait()
        return c

    lax.fori_loop(0, n_rows, start, 0, unroll=DMA_ISSUE_UNROLL)
    lax.fori_loop(0, n_rows, wait, 0, unroll=DMA_ISSUE_UNROLL)


def _gather_rows_kernel(idx_ref, src_ref, o_ref, sem):
    tm = o_ref.shape[0]
    _gather_block(idx_ref, pl.program_id(0) * tm, 1, src_ref, o_ref, sem, tm)


def _gather_rows(src, idx, tm):
    n = idx.shape[0]
    d = src.shape[1]
    grid_spec = pltpu.PrefetchScalarGridSpec(
        num_scalar_prefetch=1,
        grid=(n // tm,),
        in_specs=[pl.BlockSpec(memory_space=pl.ANY)],
        out_specs=pl.BlockSpec((tm, d), lambda i, idx_ref: (i, 0)),
        scratch_shapes=[pltpu.SemaphoreType.DMA(())],
    )
    return pl.pallas_call(
        _gather_rows_kernel,
        grid_spec=grid_spec,
        out_shape=jax.ShapeDtypeStruct((n, d), src.dtype),
        compiler_params=_cparams(1),
        name="gather_rows",
    )(idx, src)


def _final_kernel(dest_ref, x_ref, y_hbm_ref, gate_ref, gt_ref, g_ref, o_ref, y_ref, sem):
    tm = x_ref.shape[0]
    base = pl.program_id(0) * tm * TOP_K
    for kk in range(TOP_K):
        _gather_block(dest_ref, base + kk, TOP_K, y_hbm_ref, y_ref.at[kk], sem, tm)
    gates = gate_ref[...]
    y = gates[:, 0:1] * y_ref[0] + gates[:, 1:2] * y_ref[1]
    x = x_ref[...] + gt_ref[0:1, :] * y
    o_ref[...] = x * lax.rsqrt(jnp.mean(x * x, axis=-1, keepdims=True) + NORM_EPS) * g_ref[...]


def _combine_final(x, ybuf, dest, gates, mods, gate_blk, g_final):
    r, d = x.shape
    tm = _tile(r, 256, 128, 64, SUBLANES)
    grid_spec = pltpu.PrefetchScalarGridSpec(
        num_scalar_prefetch=1,
        grid=(r // tm,),
        in_specs=[pl.BlockSpec((tm, d), lambda i, dest_ref: (i, 0)),
                  pl.BlockSpec(memory_space=pl.ANY),
                  pl.BlockSpec((tm, LANES), lambda i, dest_ref: (i, 0)),
                  pl.BlockSpec((2, d), lambda i, dest_ref: (0, gate_blk)),
                  pl.BlockSpec((1, d), lambda i, dest_ref: (0, 0))],
        out_specs=pl.BlockSpec((tm, d), lambda i, dest_ref: (i, 0)),
        scratch_shapes=[pltpu.VMEM((TOP_K, tm, d), F32), pltpu.SemaphoreType.DMA(())],
    )
    return pl.pallas_call(
        _final_kernel,
        grid_spec=grid_spec,
        out_shape=jax.ShapeDtypeStruct((r, d), F32),
        compiler_params=_cparams(1),
        name="combine_final",
    )(dest, x, ybuf, gates, mods, g_final.reshape(1, d))


def _rope_tables(s_len, c_len):
    rows = s_len // GRID_W
    row = jnp.repeat(jnp.arange(rows), GRID_W).astype(F32)
    col = jnp.tile(jnp.arange(GRID_W), rows).astype(F32)
    half = QK_ROPE_DIM // 2
    inv_freq = jnp.power(ROPE_THETA, -jnp.arange(0, half, 2, dtype=F32) / half)
    ang_r = row[:, None] * inv_freq
    ang_c = col[:, None] * inv_freq
    pad = jnp.zeros((s_len, LANES - QK_ROPE_DIM), F32)
    cos = jnp.concatenate([jnp.cos(ang_r), jnp.cos(ang_r), jnp.cos(ang_c), jnp.cos(ang_c), pad], axis=1)
    sin = jnp.concatenate([-jnp.sin(ang_r), jnp.sin(ang_r), -jnp.sin(ang_c), jnp.sin(ang_c), pad], axis=1)
    cos_c = jnp.concatenate([jnp.ones((c_len, QK_ROPE_DIM), F32), jnp.zeros((c_len, LANES - QK_ROPE_DIM), F32)], 1)
    return jnp.concatenate([cos, cos_c], axis=0), jnp.concatenate([sin, jnp.zeros((c_len, LANES), F32)], axis=0)


def _rope_swap_perm():
    q = QK_ROPE_DIM // 4
    return jnp.concatenate([jnp.arange(q, 2 * q), jnp.arange(0, q), jnp.arange(3 * q, 4 * q), jnp.arange(2 * q, 3 * q)])


def _layer_weights(w_in, w_qb, w_kvb):
    d = w_in.shape[0]
    perm = _rope_swap_perm()
    kv_lo = Q_LORA_RANK
    kr_lo = Q_LORA_RANK + KV_LORA_RANK
    kr_hi = kr_lo + QK_ROPE_DIM
    w_kr = w_in[:, kr_lo:kr_hi]
    zpad = jnp.zeros((d, LANES - QK_ROPE_DIM), w_in.dtype)
    w_a = jnp.concatenate([w_in[:, :kv_lo], w_in[:, kv_lo:kr_lo], w_kr, zpad, w_kr[:, perm], zpad], axis=1)
    w_b = w_in[:, kr_hi:]
    qb = w_qb.reshape(Q_LORA_RANK, N_HEADS, QK_NOPE_DIM + QK_ROPE_DIM)
    q_rope = qb[:, :, QK_NOPE_DIM:]
    hpad = jnp.zeros((Q_LORA_RANK, N_HEADS, LANES - QK_ROPE_DIM), w_qb.dtype)
    w_q_main = jnp.concatenate([qb, hpad], axis=2).reshape(Q_LORA_RANK, N_HEADS * HEAD_W)
    w_q_swap = jnp.concatenate([q_rope[:, :, perm], hpad], axis=2).reshape(Q_LORA_RANK, N_HEADS * LANES)
    kvb = w_kvb.reshape(KV_LORA_RANK, N_HEADS, QK_NOPE_DIM + V_HEAD_DIM)
    w_k = kvb[:, :, :QK_NOPE_DIM].reshape(KV_LORA_RANK, N_HEADS * QK_NOPE_DIM)
    w_v = kvb[:, :, QK_NOPE_DIM:].reshape(KV_LORA_RANK, N_HEADS * V_HEAD_DIM)
    return tuple(w.astype(BF16) for w in (w_a, w_b, w_q_main, w_q_swap, w_k, w_v))


def _moe_slots(idx, n_blocks):
    e_flat = idx.reshape(-1)
    onehot = (e_flat[:, None] == jnp.arange(N_EXPERTS)[None, :]).astype(jnp.int32)
    csum = jnp.cumsum(onehot, axis=0)
    counts = csum[-1]
    rank = jnp.sum((csum - onehot) * onehot, axis=1)
    padded = (counts + MOE_ROWS - 1) // MOE_ROWS * MOE_ROWS
    p_end = jnp.cumsum(padded)
    p_start = p_end - padded
    dest = jnp.sum(onehot * p_start[None, :], axis=1) + rank
    blk_lo = jnp.arange(n_blocks) * MOE_ROWS
    block_expert = jnp.minimum(jnp.sum((blk_lo[:, None] >= p_end[None, :]).astype(jnp.int32), axis=1),
                               N_EXPERTS - 1)
    blocks_in_use = p_end[-1:] // MOE_ROWS
    return dest.astype(jnp.int32), jnp.concatenate([block_expert, blocks_in_use]).astype(jnp.int32)


def kernel(x, c, ctx, c_ctx, w_ada, b_ada, g_attn, w_in, g_qa, w_qb, g_kva, w_kvb, w_conv, w_oa, w_ob, w_o,
           g_ffn, w_gate_dense, w_up_dense, w_down_dense, w_router, w_gate_exp, w_up_exp, w_down_exp, g_final):
    _, s_len, d = x.shape
    c_len = ctx.shape[1]
    m_len = s_len + c_len
    depth = w_in.shape[0]
    width = w_conv.shape[2]

    mods_all = _ada_mod(c, c_ctx, w_ada, b_ada)
    cos_t, sin_t = _rope_tables(s_len, c_len)
    xa = jnp.concatenate([x[0], ctx[0]], axis=0)

    for li in range(depth):
        last = li == depth - 1
        mods = mods_all[li]
        w_a, w_b, w_q_main, w_q_swap, w_k, w_v = _layer_weights(w_in[li], w_qb[li], w_kvb[li])
        g_a = jnp.concatenate([g_qa[li], g_kva[li]]).reshape(1, -1)
        rows = s_len if last else m_len

        h = _norm_modulate(xa, g_attn[li], mods, 0, 1, s_len, BF16)
        qa_n, ckv_n, kr = _in_proj_a(h, w_a, g_a, cos_t, sin_t)
        p = _in_proj_b(h, w_b, 3 * width)
        q = _q_proj(qa_n, w_q_main, w_q_swap, cos_t, sin_t, rows)
        k, v = _kv_proj(ckv_n, kr, w_k, w_v)
        attn = _attention(q, k, v, 0, s_len, 0, m_len)
        if not last:
            attn = jnp.concatenate([attn, _attention(q, k, v, s_len, c_len, s_len, c_len)], axis=0)
        merged = _mix(attn, p, w_conv[li], w_oa[li].astype(BF16), w_ob[li].astype(BF16), rows, s_len, m_len)
        xa = _matmul_residual(merged, w_o[li].astype(BF16), xa, mods, 2, rows, s_len)

        j = li // 2
        if li % 2 == 0:
            h2 = _norm_modulate(xa, g_ffn[li], mods, 3, 4, s_len, BF16)
            tm = _tile(rows, 768, 640, 512, 256, 128)
            eid = jnp.zeros((rows // tm + 1,), jnp.int32).at[-1].set(rows // tm)
            hid = _glu(h2, w_gate_dense[j][None].astype(BF16), w_up_dense[j][None].astype(BF16), eid, tm)
            xa = _matmul_residual(hid, w_down_dense[j].astype(BF16), xa, mods, 5, rows, s_len)
        else:
            assert last, "expert layers are only supported as the final layer"
            h2, idx, gates = _norm_modulate_route(xa, g_ffn[li], mods, 3, 4, w_router[j])
            n_assign = rows * TOP_K
            n_blocks = -(-n_assign // MOE_ROWS) + N_EXPERTS
            dest, block_expert = _moe_slots(idx[:, :TOP_K], n_blocks)
            tok = jnp.arange(n_assign, dtype=jnp.int32) // TOP_K
            slot_tok = jnp.zeros((n_blocks * MOE_ROWS,), jnp.int32).at[dest].set(tok, unique_indices=True)
            buf = _gather_rows(h2, slot_tok, MOE_ROWS)
            hid = _glu(buf, w_gate_exp[j].astype(BF16), w_up_exp[j].astype(BF16), block_expert, MOE_ROWS)
            ybuf = _down_grouped(hid, w_down_exp[j].astype(BF16), block_expert, MOE_ROWS)
            return _combine_final(xa, ybuf, dest, gates, mods, 5, g_final)[None]
    raise NotImplementedError("a dense final layer is not needed for this problem")
```

```python
import functools

import jax
import jax.numpy as jnp
from jax import lax
from jax.experimental import pallas as pl
from jax.experimental.pallas import tpu as pltpu

F32 = jnp.float32
BF16 = jnp.bfloat16

N_HEADS = 16
QK_NOPE_DIM = 128
QK_ROPE_DIM = 64
V_HEAD_DIM = 128
Q_LORA_RANK = 512
KV_LORA_RANK = 512
GRID_W = 64
ROPE_THETA = 10000.0
ATTN_SCALE = (QK_NOPE_DIM + QK_ROPE_DIM) ** -0.5
Q_SCALE = ATTN_SCALE * 1.4426950408889634
CONV_K = 3
N_EXPERTS = 8
TOP_K = 2
NORM_EPS = 1e-6

LANES = 128
SUBLANES = 8
HEAD_W = 2 * LANES
VMEM_LIMIT_BYTES = 56 * 1024 * 1024

MOE_ROWS = 256
DMA_ISSUE_UNROLL = 8


def _tile(n, *cands):
    for c in cands:
        if n % c == 0:
            return c
    return n


def _cparams(n_axes):
    return pltpu.CompilerParams(dimension_semantics=("arbitrary",) * n_axes,
                                vmem_limit_bytes=VMEM_LIMIT_BYTES)


def _dot(a, b):
    return jnp.dot(a, b, preferred_element_type=F32)


def _bf16(w):
    return w if w.dtype == BF16 else w.astype(BF16)


def _ada_kernel(xt_ref, w_ref, b_ref, o_ref, *, k_chunk):
    d = xt_ref.shape[0]
    tn = o_ref.shape[-1]

    def body(k, acc):
        a0, a1 = acc
        ks = pl.multiple_of(k * k_chunk, k_chunk)
        xt = xt_ref[pl.ds(ks, k_chunk), :]
        s = xt * jax.nn.sigmoid(xt)
        w = w_ref[0, pl.ds(ks, k_chunk), :]
        a0 = a0 + jnp.sum(w * s[:, 0:1], axis=0, keepdims=True)
        a1 = a1 + jnp.sum(w * s[:, 1:2], axis=0, keepdims=True)
        return a0, a1

    z = jnp.zeros((1, tn), F32)
    a0, a1 = lax.fori_loop(0, d // k_chunk, body, (z, z))
    o_ref[0, 0:1, :] = a0 + b_ref[0]
    o_ref[0, 1:2, :] = a1 + b_ref[0]


def _ada_mod(c, c_ctx, w_ada, b_ada):
    depth, d, n = w_ada.shape
    xt = jnp.stack([c[0], c_ctx], axis=1)
    tn = _tile(n, 1024, 512, LANES)
    k_chunk = _tile(d, 256, SUBLANES)
    return pl.pallas_call(
        functools.partial(_ada_kernel, k_chunk=k_chunk),
        grid=(depth, n // tn),
        in_specs=[pl.BlockSpec((d, 2), lambda l, j: (0, 0)),
                  pl.BlockSpec((1, d, tn), lambda l, j: (l, 0, j)),
                  pl.BlockSpec((1, 1, tn), lambda l, j: (l, 0, j))],
        out_specs=pl.BlockSpec((1, 2, tn), lambda l, j: (l, 0, j)),
        out_shape=jax.ShapeDtypeStruct((depth, 2, n), F32),
        compiler_params=_cparams(2),
        name="ada_mod",
    )(xt, w_ada, b_ada.reshape(depth, 1, n))


def _norm_mod(x, g_ref, sh_ref, sc_ref, is_ctx):
    y = x * lax.rsqrt(jnp.mean(x * x, axis=-1, keepdims=True) + NORM_EPS) * g_ref[...]
    sh = jnp.where(is_ctx, sh_ref[1:2, :], sh_ref[0:1, :])
    sc = jnp.where(is_ctx, sc_ref[1:2, :], sc_ref[0:1, :])
    return y * (1.0 + sc) + sh


def _norm_kernel(x_ref, g_ref, sh_ref, sc_ref, o_ref, *, n_lat_tiles):
    is_ctx = pl.program_id(0) >= n_lat_tiles
    o_ref[...] = _norm_mod(x_ref[...], g_ref, sh_ref, sc_ref, is_ctx).astype(o_ref.dtype)


def _norm_router_kernel(x_ref, g_ref, sh_ref, sc_ref, wr_ref, h_ref, idx_ref, gate_ref):
    h = _norm_mod(x_ref[...], g_ref, sh_ref, sc_ref, False)
    h_ref[...] = h
    logits = jnp.dot(h, wr_ref[...], preferred_element_type=F32, precision=lax.Precision.HIGHEST)
    lane = lax.broadcasted_iota(jnp.int32, logits.shape, 1).astype(F32)
    neg = jnp.float32(-jnp.inf)
    l1 = jnp.where(lane < N_EXPERTS, logits, neg)
    v1 = jnp.max(l1, axis=-1, keepdims=True)
    i1 = jnp.min(jnp.where(l1 == v1, lane, float(LANES)), axis=-1, keepdims=True)
    l2 = jnp.where(lane == i1, neg, l1)
    v2 = jnp.max(l2, axis=-1, keepdims=True)
    i2 = jnp.min(jnp.where(l2 == v2, lane, float(LANES)), axis=-1, keepdims=True)
    e = jnp.exp(v2 - v1)
    g1 = 1.0 / (1.0 + e)
    g2 = e / (1.0 + e)
    idx_ref[...] = jnp.where(lane == 0, i1, jnp.where(lane == 1, i2, 0.0)).astype(jnp.int32)
    gate_ref[...] = jnp.where(lane == 0, g1, jnp.where(lane == 1, g2, 0.0))


def _norm_modulate(x, g, mods, sh_blk, sc_blk, n_lat, out_dtype):
    r, d = x.shape
    tm = _tile(n_lat, 256, 128, 64, SUBLANES) if r > n_lat else _tile(r, 256, 128, 64, SUBLANES)
    if r > n_lat:
        tm = _tile(r - n_lat, tm, 128, 64, SUBLANES)
    return pl.pallas_call(
        functools.partial(_norm_kernel, n_lat_tiles=n_lat // tm),
        grid=(r // tm,),
        in_specs=[pl.BlockSpec((tm, d), lambda i: (i, 0)),
                  pl.BlockSpec((1, d), lambda i: (0, 0)),
                  pl.BlockSpec((2, d), lambda i: (0, sh_blk)),
                  pl.BlockSpec((2, d), lambda i: (0, sc_blk))],
        out_specs=pl.BlockSpec((tm, d), lambda i: (i, 0)),
        out_shape=jax.ShapeDtypeStruct((r, d), out_dtype),
        compiler_params=_cparams(1),
        name="norm_modulate",
    )(x, g.reshape(1, d), mods, mods)


def _norm_modulate_route(x, g, mods, sh_blk, sc_blk, w_router):
    r, d = x.shape
    tm = _tile(r, 256, 128, 64, SUBLANES)
    wr = jnp.pad(w_router, ((0, 0), (0, LANES - N_EXPERTS)))
    return pl.pallas_call(
        _norm_router_kernel,
        grid=(r // tm,),
        in_specs=[pl.BlockSpec((tm, d), lambda i: (i, 0)),
                  pl.BlockSpec((1, d), lambda i: (0, 0)),
                  pl.BlockSpec((2, d), lambda i: (0, sh_blk)),
                  pl.BlockSpec((2, d), lambda i: (0, sc_blk)),
                  pl.BlockSpec((d, LANES), lambda i: (0, 0))],
        out_specs=[pl.BlockSpec((tm, d), lambda i: (i, 0)),
                   pl.BlockSpec((tm, LANES), lambda i: (i, 0)),
                   pl.BlockSpec((tm, LANES), lambda i: (i, 0))],
        out_shape=[jax.ShapeDtypeStruct((r, d), F32),
                   jax.ShapeDtypeStruct((r, LANES), jnp.int32),
                   jax.ShapeDtypeStruct((r, LANES), F32)],
        compiler_params=_cparams(1),
        name="norm_modulate_route",
    )(x, g.reshape(1, d), mods, mods, wr)


def _in_a_kernel(h_ref, w_ref, g_ref, cos_ref, sin_ref, qa_ref, ckv_ref, kr_ref):
    acc = _dot(h_ref[...], w_ref[...])

    def rms(v, g):
        return v * lax.rsqrt(jnp.mean(v * v, axis=-1, keepdims=True) + NORM_EPS) * g

    q0, q1 = 0, Q_LORA_RANK
    c1 = q1 + KV_LORA_RANK
    qa_ref[...] = rms(acc[:, q0:q1], g_ref[:, q0:q1]).astype(qa_ref.dtype)
    ckv_ref[...] = rms(acc[:, q1:c1], g_ref[:, q1:c1]).astype(ckv_ref.dtype)
    kr = acc[:, c1:c1 + LANES] * cos_ref[...] + acc[:, c1 + LANES:c1 + 2 * LANES] * sin_ref[...]
    kr_ref[...] = kr.astype(kr_ref.dtype)


def _in_proj_a(h, w_a, g_a, cos_t, sin_t):
    m, d = h.shape
    n = w_a.shape[1]
    tm = _tile(m, 768, 640, 512, 256, 128)
    return pl.pallas_call(
        _in_a_kernel,
        grid=(m // tm,),
        in_specs=[pl.BlockSpec((tm, d), lambda i: (i, 0)),
                  pl.BlockSpec((d, n), lambda i: (0, 0)),
                  pl.BlockSpec((1, Q_LORA_RANK + KV_LORA_RANK), lambda i: (0, 0)),
                  pl.BlockSpec((tm, LANES), lambda i: (i, 0)),
                  pl.BlockSpec((tm, LANES), lambda i: (i, 0))],
        out_specs=[pl.BlockSpec((tm, Q_LORA_RANK), lambda i: (i, 0)),
                   pl.BlockSpec((tm, KV_LORA_RANK), lambda i: (i, 0)),
                   pl.BlockSpec((tm, LANES), lambda i: (i, 0))],
        out_shape=[jax.ShapeDtypeStruct((m, Q_LORA_RANK), BF16),
                   jax.ShapeDtypeStruct((m, KV_LORA_RANK), BF16),
                   jax.ShapeDtypeStruct((m, LANES), BF16)],
        compiler_params=_cparams(1),
        name="in_proj_a",
    )(h, w_a, g_a, cos_t, sin_t)


def _in_b_kernel(h_ref, w_ref, o_ref, *, first_gate_tile):
    acc = _dot(h_ref[...], w_ref[...])
    is_gate = pl.program_id(0) >= first_gate_tile

    @pl.when(is_gate)
    def _():
        o_ref[...] = jax.nn.sigmoid(acc).astype(o_ref.dtype)

    @pl.when(jnp.logical_not(is_gate))
    def _():
        o_ref[...] = acc.astype(o_ref.dtype)


def _in_proj_b(h, w_b, first_gate_col):
    m, d = h.shape
    n = w_b.shape[1]
    tm = _tile(m, 1408, 768, 640, 512, 256, 128)
    tn = _tile(first_gate_col, 1024, 512, 256, LANES)
    return pl.pallas_call(
        functools.partial(_in_b_kernel, first_gate_tile=first_gate_col // tn),
        grid=(n // tn, m // tm),
        in_specs=[pl.BlockSpec((tm, d), lambda j, i: (i, 0)),
                  pl.BlockSpec((d, tn), lambda j, i: (0, j))],
        out_specs=pl.BlockSpec((tm, tn), lambda j, i: (i, j)),
        out_shape=jax.ShapeDtypeStruct((m, n), BF16),
        compiler_params=_cparams(2),
        name="in_proj_b",
    )(h, w_b)


def _q_kernel(a_ref, wm_ref, ws_ref, cos_ref, sin_ref, q_ref):
    a = a_ref[...]
    cos = cos_ref[...]
    sin = sin_ref[...]
    for hd in range(N_HEADS):
        main = _dot(a, wm_ref[:, hd * HEAD_W:(hd + 1) * HEAD_W])
        swap = _dot(a, ws_ref[:, hd * LANES:(hd + 1) * LANES])
        q_ref[:, hd * HEAD_W:hd * HEAD_W + LANES] = (main[:, :LANES] * Q_SCALE).astype(q_ref.dtype)
        rot = (main[:, LANES:] * cos + swap * sin) * Q_SCALE
        q_ref[:, hd * HEAD_W + LANES:(hd + 1) * HEAD_W] = rot.astype(q_ref.dtype)


def _q_proj(qa_n, w_main, w_swap, cos_t, sin_t, rows):
    k = qa_n.shape[1]
    tm = _tile(rows, 768, 640, 512, 256, 128)
    return pl.pallas_call(
        _q_kernel,
        grid=(rows // tm,),
        in_specs=[pl.BlockSpec((tm, k), lambda i: (i, 0)),
                  pl.BlockSpec(w_main.shape, lambda i: (0, 0)),
                  pl.BlockSpec(w_swap.shape, lambda i: (0, 0)),
                  pl.BlockSpec((tm, LANES), lambda i: (i, 0)),
                  pl.BlockSpec((tm, LANES), lambda i: (i, 0))],
        out_specs=pl.BlockSpec((tm, N_HEADS * HEAD_W), lambda i: (i, 0)),
        out_shape=jax.ShapeDtypeStruct((rows, N_HEADS * HEAD_W), BF16),
        compiler_params=_cparams(1),
        name="q_proj",
    )(qa_n, w_main, w_swap, cos_t, sin_t)


def _kv_kernel(c_ref, kr_ref, wk_ref, wv_ref, k_ref, v_ref):
    c = c_ref[...]
    kr = kr_ref[...]
    lane = lax.broadcasted_iota(jnp.int32, (c.shape[0], LANES), 1)
    ones_col = jnp.where(lane == 0, 1.0, 0.0).astype(v_ref.dtype)
    for hd in range(N_HEADS):
        kn = _dot(c, wk_ref[:, hd * LANES:(hd + 1) * LANES])
        k_ref[:, hd * HEAD_W:hd * HEAD_W + LANES] = kn.astype(k_ref.dtype)
        k_ref[:, hd * HEAD_W + LANES:(hd + 1) * HEAD_W] = kr
        vh = _dot(c, wv_ref[:, hd * LANES:(hd + 1) * LANES])
        v_ref[:, hd * HEAD_W:hd * HEAD_W + LANES] = vh.astype(v_ref.dtype)
        v_ref[:, hd * HEAD_W + LANES:(hd + 1) * HEAD_W] = ones_col


def _kv_proj(ckv_n, kr, w_k, w_v):
    m, k = ckv_n.shape
    tm = _tile(m, 768, 640, 512, 256, 128)
    return pl.pallas_call(
        _kv_kernel,
        grid=(m // tm,),
        in_specs=[pl.BlockSpec((tm, k), lambda i: (i, 0)),
                  pl.BlockSpec((tm, LANES), lambda i: (i, 0)),
                  pl.BlockSpec(w_k.shape, lambda i: (0, 0)),
                  pl.BlockSpec(w_v.shape, lambda i: (0, 0))],
        out_specs=[pl.BlockSpec((tm, N_HEADS * HEAD_W), lambda i: (i, 0)),
                   pl.BlockSpec((tm, N_HEADS * HEAD_W), lambda i: (i, 0))],
        out_shape=[jax.ShapeDtypeStruct((m, N_HEADS * HEAD_W), BF16),
                   jax.ShapeDtypeStruct((m, N_HEADS * HEAD_W), BF16)],
        compiler_params=_cparams(1),
        name="kv_proj",
    )(ckv_n, kr, w_k, w_v)


def _attn_kernel(q_ref, k_ref, v_ref, o_ref, sa_ref, sb_ref, *, tk, n_chunks):
    q = q_ref[...]
    tq = q.shape[0]

    def scores(c):
        return lax.dot_general(q, k_ref[c * tk:(c + 1) * tk, :], (((1,), (1,)), ((), ())),
                               preferred_element_type=F32)

    def absorb(s_ref, c, carry):
        m, acc = carry
        s = s_ref[...]
        m_new = jnp.maximum(m, jnp.max(s, axis=-1, keepdims=True))
        p = jnp.exp2(s - m_new).astype(v_ref.dtype)
        acc = jnp.exp2(m - m_new) * acc + _dot(p, v_ref[c * tk:(c + 1) * tk, :])
        return m_new, acc

    s_refs = (sa_ref, sb_ref)
    carry = (jnp.full((tq, 1), -jnp.inf, F32), jnp.zeros((tq, HEAD_W), F32))
    sa_ref[...] = scores(0)
    for c in range(n_chunks):
        if c + 1 < n_chunks:
            s_refs[(c + 1) % 2][...] = scores(c + 1)
        carry = absorb(s_refs[c % 2], c, carry)
    _, acc = carry
    o_ref[...] = (acc[:, :V_HEAD_DIM] / acc[:, V_HEAD_DIM:V_HEAD_DIM + 1]).astype(o_ref.dtype)


def _attention(q, k, v, q_row0, n_q, k_row0, n_k):
    tq = _tile(n_q, 512, 256, 128)
    tk = _tile(n_k, 768, 640, 512, 384, 256, 128)
    assert q_row0 % tq == 0 and k_row0 % n_k == 0
    q_blk0 = q_row0 // tq
    k_blk = k_row0 // n_k
    return pl.pallas_call(
        functools.partial(_attn_kernel, tk=tk, n_chunks=n_k // tk),
        grid=(N_HEADS, n_q // tq),
        in_specs=[pl.BlockSpec((tq, HEAD_W), lambda h, i: (q_blk0 + i, h)),
                  pl.BlockSpec((n_k, HEAD_W), lambda h, i: (k_blk, h)),
                  pl.BlockSpec((n_k, HEAD_W), lambda h, i: (k_blk, h))],
        out_specs=pl.BlockSpec((tq, V_HEAD_DIM), lambda h, i: (i, h)),
        out_shape=jax.ShapeDtypeStruct((n_q, N_HEADS * V_HEAD_DIM), BF16),
        scratch_shapes=[pltpu.VMEM((tq, tk), F32), pltpu.VMEM((tq, tk), F32)],
        compiler_params=_cparams(2),
        name="attention",
    )(q, k, v)


def _conv_gate_kernel(cx_ref, cb_ref, cc_ref, cxp_ref, ccp_ref, cxn_ref, ccn_ref, wc_ref, z_ref, *,
                      seg_starts, seg_ends, col_chunk):
    tm, width = z_ref.shape
    loc = lax.broadcasted_iota(jnp.int32, (tm, 1), 0)
    row = loc + pl.program_id(0) * tm
    first = functools.reduce(jnp.logical_or, [row == r for r in seg_starts])
    last = functools.reduce(jnp.logical_or, [row == r for r in seg_ends])
    for c0 in range(0, width, col_chunk):
        cs = slice(c0, c0 + col_chunk)
        u = cx_ref[:, cs].astype(F32) * cc_ref[:, cs].astype(F32)
        u_halo_prev = (cxp_ref[SUBLANES - 1:SUBLANES, cs].astype(F32)
                       * ccp_ref[SUBLANES - 1:SUBLANES, cs].astype(F32))
        u_halo_next = cxn_ref[0:1, cs].astype(F32) * ccn_ref[0:1, cs].astype(F32)
        u_prev = jnp.where(loc == 0, u_halo_prev, pltpu.roll(u, 1, 0))
        u_prev = jnp.where(first, 0.0, u_prev)
        u_next = jnp.where(loc == tm - 1, u_halo_next, pltpu.roll(u, tm - 1, 0))
        u_next = jnp.where(last, 0.0, u_next)
        conv = wc_ref[0:1, cs] * u_prev + wc_ref[1:2, cs] * u + wc_ref[2:3, cs] * u_next
        z_ref[:, cs] = (cb_ref[:, cs].astype(F32) * conv).astype(z_ref.dtype)


def _conv_gate(p, w_conv, rows, s_len, m_len):
    width = w_conv.shape[1]
    tm = _tile(rows, 768, 512, 384, 256, 128, 64)
    hb = tm // SUBLANES
    last_hb = p.shape[0] // SUBLANES - 1

    def prev_map(col):
        return lambda i: (jnp.maximum(i * hb - 1, 0), col)

    def next_map(col):
        return lambda i: (jnp.minimum((i + 1) * hb, last_hb), col)

    kern = functools.partial(_conv_gate_kernel, seg_starts=(0, s_len), seg_ends=(s_len - 1, m_len - 1),
                             col_chunk=_tile(width, 512, LANES))
    return pl.pallas_call(
        kern,
        grid=(rows // tm,),
        in_specs=[pl.BlockSpec((tm, width), lambda i: (i, 0)),
                  pl.BlockSpec((tm, width), lambda i: (i, 1)),
                  pl.BlockSpec((tm, width), lambda i: (i, 2)),
                  pl.BlockSpec((SUBLANES, width), prev_map(0)),
                  pl.BlockSpec((SUBLANES, width), prev_map(2)),
                  pl.BlockSpec((SUBLANES, width), next_map(0)),
                  pl.BlockSpec((SUBLANES, width), next_map(2)),
                  pl.BlockSpec((CONV_K, width), lambda i: (0, 0))],
        out_specs=pl.BlockSpec((tm, width), lambda i: (i, 0)),
        out_shape=jax.ShapeDtypeStruct((rows, width), BF16),
        compiler_params=_cparams(1),
        name="conv_gate",
    )(p, p, p, p, p, p, p, w_conv)


def _merge_kernel(attn_ref, z_ref, sga_ref, sgb_ref, woa_ref, wob_ref, o_ref):
    o_a = _dot(attn_ref[...], woa_ref[...])
    o_b = _dot(z_ref[...], wob_ref[...])
    o_ref[...] = (sga_ref[...].astype(F32) * o_a + sgb_ref[...].astype(F32) * o_b).astype(o_ref.dtype)


def _merge(attn, z, p, w_oa, w_ob, rows):
    attn_w = w_oa.shape[0]
    width, d = w_ob.shape
    tm = _tile(rows, 768, 512, 384, 256, 128, 64)
    tn = _tile(d, 1024, 512, 256, LANES)
    ga_blk = 3 * width // tn
    gb_blk = (3 * width + d) // tn
    return pl.pallas_call(
        _merge_kernel,
        grid=(d // tn, rows // tm),
        in_specs=[pl.BlockSpec((tm, attn_w), lambda j, i: (i, 0)),
                  pl.BlockSpec((tm, width), lambda j, i: (i, 0)),
                  pl.BlockSpec((tm, tn), lambda j, i: (i, ga_blk + j)),
                  pl.BlockSpec((tm, tn), lambda j, i: (i, gb_blk + j)),
                  pl.BlockSpec((attn_w, tn), lambda j, i: (0, j)),
                  pl.BlockSpec((width, tn), lambda j, i: (0, j))],
        out_specs=pl.BlockSpec((tm, tn), lambda j, i: (i, j)),
        out_shape=jax.ShapeDtypeStruct((rows, d), BF16),
        compiler_params=_cparams(2),
        name="merge",
    )(attn, z, p, p, w_oa, w_ob)


def _mm_res_kernel(a_ref, w_ref, x_ref, gt_ref, o_ref, *, n_lat):
    tm = a_ref.shape[0]
    row = lax.broadcasted_iota(jnp.int32, (tm, 1), 0) + pl.program_id(1) * tm
    gate = jnp.where(row >= n_lat, gt_ref[1:2, :], gt_ref[0:1, :])
    o_ref[...] = x_ref[...] + gate * _dot(a_ref[...], _bf16(w_ref[...]))


def _matmul_residual(a, w, x, mods, gate_blk, rows, n_lat):
    k, n = w.shape
    tm = _tile(rows, 768, 640, 512, 256, 128)
    tn = _tile(n, 512, 256, LANES)
    nt = n // tn
    return pl.pallas_call(
        functools.partial(_mm_res_kernel, n_lat=n_lat),
        grid=(nt, rows // tm),
        in_specs=[pl.BlockSpec((tm, k), lambda j, i: (i, 0)),
                  pl.BlockSpec((k, tn), lambda j, i: (0, j)),
                  pl.BlockSpec((tm, tn), lambda j, i: (i, j)),
                  pl.BlockSpec((2, tn), lambda j, i: (0, gate_blk * nt + j))],
        out_specs=pl.BlockSpec((tm, tn), lambda j, i: (i, j)),
        out_shape=jax.ShapeDtypeStruct((rows, n), F32),
        compiler_params=_cparams(2),
        name="matmul_residual",
    )(a, w, x, mods)


def _block_in_use(eid_ref):
    return pl.program_id(1) < eid_ref[pl.num_programs(1)]


def _glu_kernel(eid_ref, a_ref, wg_ref, wu_ref, o_ref):
    @pl.when(_block_in_use(eid_ref))
    def _():
        a = _bf16(a_ref[...])
        gate = _dot(a, _bf16(wg_ref[0]))
        up = _dot(a, _bf16(wu_ref[0]))
        o_ref[...] = (gate * jax.nn.sigmoid(gate) * up).astype(o_ref.dtype)

    @pl.when(jnp.logical_not(_block_in_use(eid_ref)))
    def _():
        o_ref[...] = jnp.zeros_like(o_ref)


def _glu(a, w_gate, w_up, eid, tm):
    r, k = a.shape
    f = w_gate.shape[2]
    tn = _tile(f, 512, 1408, 256, LANES)
    grid_spec = pltpu.PrefetchScalarGridSpec(
        num_scalar_prefetch=1,
        grid=(f // tn, r // tm),
        in_specs=[pl.BlockSpec((tm, k), lambda j, i, e: (i, 0)),
                  pl.BlockSpec((1, k, tn), lambda j, i, e: (e[i], 0, j)),
                  pl.BlockSpec((1, k, tn), lambda j, i, e: (e[i], 0, j))],
        out_specs=pl.BlockSpec((tm, tn), lambda j, i, e: (i, j)),
    )
    return pl.pallas_call(
        _glu_kernel,
        grid_spec=grid_spec,
        out_shape=jax.ShapeDtypeStruct((r, f), BF16),
        compiler_params=_cparams(2),
        name="glu",
    )(eid, a, w_gate, w_up)


def _down_kernel(eid_ref, a_ref, w_ref, o_ref):
    @pl.when(_block_in_use(eid_ref))
    def _():
        o_ref[...] = _dot(a_ref[...], w_ref[0])

    @pl.when(jnp.logical_not(_block_in_use(eid_ref)))
    def _():
        o_ref[...] = jnp.zeros_like(o_ref)


def _down_grouped(a, w_down, eid, tm):
    r, f = a.shape
    d = w_down.shape[2]
    grid_spec = pltpu.PrefetchScalarGridSpec(
        num_scalar_prefetch=1,
        grid=(1, r // tm),
        in_specs=[pl.BlockSpec((tm, f), lambda j, i, e: (i, 0)),
                  pl.BlockSpec((1, f, d), lambda j, i, e: (e[i], 0, 0))],
        out_specs=pl.BlockSpec((tm, d), lambda j, i, e: (i, 0)),
    )
    return pl.pallas_call(
        _down_kernel,
        grid_spec=grid_spec,
        out_shape=jax.ShapeDtypeStruct((r, d), F32),
        compiler_params=_cparams(2),
        name="down_grouped",
    )(eid, a, w_down)


def _row_copies(idx_ref, idx_base, idx_stride, src_ref, dst_ref, sem, n_rows, start):
    def body(r, c):
        row = idx_ref[idx_base + r * idx_stride]
        cp = pltpu.make_async_copy(src_ref.at[pl.ds(row, 1)], dst_ref.at[pl.ds(r, 1)], sem)
        if start:
            cp.start()
        else:
            cp.wait()
        return c

    lax.fori_loop(0, n_rows, body, 0, unroll=DMA_ISSUE_UNROLL)


def _prefetched_gather(copies):
    i = pl.program_id(0)
    slot = i % 2

    @pl.when(i == 0)
    def _():
        copies(0, 0, True)

    @pl.when(i + 1 < pl.num_programs(0))
    def _():
        copies(i + 1, 1 - slot, True)

    copies(i, slot, False)
    return slot


def _gather_rows_kernel(idx_ref, src_ref, o_ref, g_ref, sem):
    tm = o_ref.shape[0]

    def copies(step, slot, start):
        _row_copies(idx_ref, step * tm, 1, src_ref, g_ref.at[slot], sem.at[slot], tm, start)

    slot = _prefetched_gather(copies)
    o_ref[...] = g_ref[slot]


def _gather_rows(src, idx, tm):
    n = idx.shape[0]
    w = src.shape[1]
    grid_spec = pltpu.PrefetchScalarGridSpec(
        num_scalar_prefetch=1,
        grid=(n // tm,),
        in_specs=[pl.BlockSpec(memory_space=pl.ANY)],
        out_specs=pl.BlockSpec((tm, w), lambda i, idx_ref: (i, 0)),
        scratch_shapes=[pltpu.VMEM((2, tm, w), src.dtype), pltpu.SemaphoreType.DMA((2,))],
    )
    return pl.pallas_call(
        _gather_rows_kernel,
        grid_spec=grid_spec,
        out_shape=jax.ShapeDtypeStruct((n, w), src.dtype),
        compiler_params=_cparams(1),
        name="gather_rows",
    )(idx, src)


def _final_kernel(dest_ref, x_ref, y_hbm_ref, gate_ref, gt_ref, g_ref, o_ref, y_ref, sem):
    tm = x_ref.shape[0]

    def copies(step, slot, start):
        for kk in range(TOP_K):
            _row_copies(dest_ref, step * tm * TOP_K + kk, TOP_K, y_hbm_ref, y_ref.at[slot, kk], sem.at[slot],
                        tm, start)

    slot = _prefetched_gather(copies)
    gates = gate_ref[...]
    y = gates[:, 0:1] * y_ref[slot, 0] + gates[:, 1:2] * y_ref[slot, 1]
    x = x_ref[...] + gt_ref[0:1, :] * y
    o_ref[...] = x * lax.rsqrt(jnp.mean(x * x, axis=-1, keepdims=True) + NORM_EPS) * g_ref[...]


def _combine_final(x, ybuf, dest, gates, mods, gate_blk, g_final):
    r, d = x.shape
    tm = _tile(r, 256, 128, 64, SUBLANES)
    grid_spec = pltpu.PrefetchScalarGridSpec(
        num_scalar_prefetch=1,
        grid=(r // tm,),
        in_specs=[pl.BlockSpec((tm, d), lambda i, dest_ref: (i, 0)),
                  pl.BlockSpec(memory_space=pl.ANY),
                  pl.BlockSpec((tm, LANES), lambda i, dest_ref: (i, 0)),
                  pl.BlockSpec((2, d), lambda i, dest_ref: (0, gate_blk)),
                  pl.BlockSpec((1, d), lambda i, dest_ref: (0, 0))],
        out_specs=pl.BlockSpec((tm, d), lambda i, dest_ref: (i, 0)),
        scratch_shapes=[pltpu.VMEM((2, TOP_K, tm, d), F32), pltpu.SemaphoreType.DMA((2,))],
    )
    return pl.pallas_call(
        _final_kernel,
        grid_spec=grid_spec,
        out_shape=jax.ShapeDtypeStruct((r, d), F32),
        compiler_params=_cparams(1),
        name="combine_final",
    )(dest, x, ybuf, gates, mods, g_final.reshape(1, d))


def _shift_cast_kernel(a_ref, b_ref, o_ref, *, row_chunk):
    tn = o_ref.shape[-1]
    half = LANES // 2
    lane = lax.broadcasted_iota(jnp.int32, (row_chunk, LANES), 1)

    def body(r, c):
        rs = pl.ds(pl.multiple_of(r * row_chunk, row_chunk), row_chunk)
        a = a_ref[0, rs, :]
        shifted = pltpu.roll(a, tn - half, 1)
        tail = pltpu.roll(b_ref[0, rs, :], half, 1)
        o_ref[0, rs, :tn - LANES] = shifted[:, :tn - LANES].astype(o_ref.dtype)
        o_ref[0, rs, tn - LANES:] = jnp.where(lane < half, shifted[:, tn - LANES:], tail).astype(o_ref.dtype)
        return c

    lax.fori_loop(0, a_ref.shape[1] // row_chunk, body, 0)


def _shift_cast_columns(w, col0, n_out):
    depth, k, n = w.shape
    tn = _tile(n_out, 1024, 512, 256)
    base = col0 - LANES // 2
    assert base % tn == 0 and col0 + n_out <= n
    a_blk0 = base // tn
    b_per_tile = tn // LANES
    return pl.pallas_call(
        functools.partial(_shift_cast_kernel, row_chunk=_tile(k, 256, SUBLANES)),
        grid=(depth, n_out // tn),
        in_specs=[pl.BlockSpec((1, k, tn), lambda l, j: (l, 0, a_blk0 + j)),
                  pl.BlockSpec((1, k, LANES), lambda l, j: (l, 0, (a_blk0 + j + 1) * b_per_tile))],
        out_specs=pl.BlockSpec((1, k, tn), lambda l, j: (l, 0, j)),
        out_shape=jax.ShapeDtypeStruct((depth, k, n_out), BF16),
        compiler_params=_cparams(2),
        name="shift_cast_columns",
    )(w, w)


def _rope_tables(s_len, c_len):
    quarter = QK_ROPE_DIM // 4
    tok = jnp.arange(s_len + c_len, dtype=jnp.int32)[:, None]
    lane = jnp.arange(LANES, dtype=jnp.int32)[None, :]
    group = lane // quarter
    pos = jnp.where(group < 2, tok // GRID_W, tok % GRID_W).astype(F32)
    inv_freq = jnp.power(ROPE_THETA, -(2 * (lane % quarter)).astype(F32) / (QK_ROPE_DIM // 2))
    ang = jnp.where(tok < s_len, pos * inv_freq, 0.0)
    live = group < 4
    cos = jnp.where(live, jnp.cos(ang), 0.0)
    sin = jnp.where(live, jnp.where(group % 2 == 0, -jnp.sin(ang), jnp.sin(ang)), 0.0)
    return cos, sin


def _rope_swap_perm():
    q = QK_ROPE_DIM // 4
    return jnp.concatenate([jnp.arange(q, 2 * q), jnp.arange(0, q), jnp.arange(3 * q, 4 * q), jnp.arange(2 * q, 3 * q)])


def _layer_weights(w_in_all, w_qb_all, w_kvb_all, li):
    d = w_in_all.shape[1]
    perm = _rope_swap_perm()
    kr_lo = Q_LORA_RANK + KV_LORA_RANK
    kr_hi = kr_lo + QK_ROPE_DIM
    w_kr = w_in_all[li, :, kr_lo:kr_hi]
    zpad = jnp.zeros((d, LANES - QK_ROPE_DIM), w_in_all.dtype)
    w_a = jnp.concatenate([w_in_all[li, :, :kr_lo], w_kr, zpad, w_kr[:, perm], zpad], axis=1)
    w_qb, w_kvb = w_qb_all[li], w_kvb_all[li]
    qb = w_qb.reshape(Q_LORA_RANK, N_HEADS, QK_NOPE_DIM + QK_ROPE_DIM)
    q_rope = qb[:, :, QK_NOPE_DIM:]
    hpad = jnp.zeros((Q_LORA_RANK, N_HEADS, LANES - QK_ROPE_DIM), w_qb.dtype)
    w_q_main = jnp.concatenate([qb, hpad], axis=2).reshape(Q_LORA_RANK, N_HEADS * HEAD_W)
    w_q_swap = jnp.concatenate([q_rope[:, :, perm], hpad], axis=2).reshape(Q_LORA_RANK, N_HEADS * LANES)
    kvb = w_kvb.reshape(KV_LORA_RANK, N_HEADS, QK_NOPE_DIM + V_HEAD_DIM)
    w_k = kvb[:, :, :QK_NOPE_DIM].reshape(KV_LORA_RANK, N_HEADS * QK_NOPE_DIM)
    w_v = kvb[:, :, QK_NOPE_DIM:].reshape(KV_LORA_RANK, N_HEADS * V_HEAD_DIM)
    return tuple(w.astype(BF16) for w in (w_a, w_q_main, w_q_swap, w_k, w_v))


def _moe_slots(idx, n_blocks):
    e_flat = idx.reshape(-1)
    onehot = (e_flat[:, None] == jnp.arange(N_EXPERTS)[None, :]).astype(jnp.int32)
    csum = jnp.cumsum(onehot, axis=0)
    counts = csum[-1]
    rank = jnp.sum((csum - onehot) * onehot, axis=1)
    padded = (counts + MOE_ROWS - 1) // MOE_ROWS * MOE_ROWS
    p_end = jnp.cumsum(padded)
    p_start = p_end - padded
    dest = jnp.sum(onehot * p_start[None, :], axis=1) + rank
    blk_lo = jnp.arange(n_blocks) * MOE_ROWS
    block_expert = jnp.minimum(jnp.sum((blk_lo[:, None] >= p_end[None, :]).astype(jnp.int32), axis=1),
                               N_EXPERTS - 1)
    blocks_in_use = p_end[-1:] // MOE_ROWS
    return dest.astype(jnp.int32), jnp.concatenate([block_expert, blocks_in_use]).astype(jnp.int32)


def kernel(x, c, ctx, c_ctx, w_ada, b_ada, g_attn, w_in, g_qa, w_qb, g_kva, w_kvb, w_conv, w_oa, w_ob, w_o,
           g_ffn, w_gate_dense, w_up_dense, w_down_dense, w_router, w_gate_exp, w_up_exp, w_down_exp, g_final):
    _, s_len, d = x.shape
    c_len = ctx.shape[1]
    m_len = s_len + c_len
    depth = w_in.shape[0]
    width = w_conv.shape[2]
    assert depth == 2, "supported stack: dense-FFN layer with context updates, then a final expert-FFN layer"

    mods_all = _ada_mod(c, c_ctx, w_ada, b_ada)
    cos_t, sin_t = _rope_tables(s_len, c_len)
    kr_hi = Q_LORA_RANK + KV_LORA_RANK + QK_ROPE_DIM
    w_b_all = _shift_cast_columns(w_in, kr_hi, w_in.shape[2] - kr_hi)
    xa = jnp.concatenate([x[0], ctx[0]], axis=0)

    for li in range(depth):
        last = li == depth - 1
        mods = mods_all[li]
        w_a, w_q_main, w_q_swap, w_k, w_v = _layer_weights(w_in, w_qb, w_kvb, li)
        g_a = jnp.concatenate([g_qa[li], g_kva[li]]).reshape(1, -1)
        rows = s_len if last else m_len

        h = _norm_modulate(xa, g_attn[li], mods, 0, 1, s_len, BF16)
        qa_n, ckv_n, kr = _in_proj_a(h, w_a, g_a, cos_t, sin_t)
        p = _in_proj_b(h, w_b_all[li], 3 * width)
        q = _q_proj(qa_n, w_q_main, w_q_swap, cos_t, sin_t, rows)
        k, v = _kv_proj(ckv_n, kr, w_k, w_v)
        attn = _attention(q, k, v, 0, s_len, 0, m_len)
        if not last:
            attn = jnp.concatenate([attn, _attention(q, k, v, s_len, c_len, s_len, c_len)], axis=0)
        z = _conv_gate(p, w_conv[li], rows, s_len, m_len)
        merged = _merge(attn, z, p, w_oa[li].astype(BF16), w_ob[li].astype(BF16), rows)
        xa = _matmul_residual(merged, w_o[li], xa, mods, 2, rows, s_len)

        j = li // 2
        if li % 2 == 0:
            h2 = _norm_modulate(xa, g_ffn[li], mods, 3, 4, s_len, BF16)
            tm = _tile(rows, 768, 640, 512, 256, 128)
            eid = jnp.zeros((rows // tm + 1,), jnp.int32).at[-1].set(rows // tm)
            hid = _glu(h2, w_gate_dense[j][None], w_up_dense[j][None], eid, tm)
            xa = _matmul_residual(hid, w_down_dense[j].astype(BF16), xa, mods, 5, rows, s_len)
        else:
            h2, idx, gates = _norm_modulate_route(xa, g_ffn[li], mods, 3, 4, w_router[j])
            n_assign = rows * TOP_K
            n_blocks = -(-n_assign // MOE_ROWS) + N_EXPERTS
            dest, block_expert = _moe_slots(idx[:, :TOP_K], n_blocks)
            tok = jnp.arange(n_assign, dtype=jnp.int32) // TOP_K
            slot_tok = jnp.zeros((n_blocks * MOE_ROWS,), jnp.int32).at[dest].set(tok, unique_indices=True)
            buf = _gather_rows(h2, slot_tok, MOE_ROWS)
            hid = _glu(buf, w_gate_exp[j].astype(BF16), w_up_exp[j].astype(BF16), block_expert, MOE_ROWS)
            ybuf = _down_grouped(hid, w_down_exp[j].astype(BF16), block_expert, MOE_ROWS)
            return _combine_final(xa, ybuf, dest, gates, mods, 5, g_final)[None]
    raise AssertionError("unreachable: the final layer returns")
```

```python
import functools

import jax
import jax.numpy as jnp
from jax import lax
from jax.experimental import pallas as pl
from jax.experimental.pallas import tpu as pltpu

F32 = jnp.float32
BF16 = jnp.bfloat16

N_HEADS = 16
QK_NOPE_DIM = 128
QK_ROPE_DIM = 64
V_HEAD_DIM = 128
Q_LORA_RANK = 512
KV_LORA_RANK = 512
GRID_W = 64
ROPE_THETA = 10000.0
ATTN_SCALE = (QK_NOPE_DIM + QK_ROPE_DIM) ** -0.5
Q_SCALE = ATTN_SCALE * 1.4426950408889634
CONV_K = 3
N_EXPERTS = 8
TOP_K = 2
NORM_EPS = 1e-6

LANES = 128
SUBLANES = 8
HEAD_W = 2 * LANES
VMEM_LIMIT_BYTES = 56 * 1024 * 1024

MOE_ROWS = 256
DMA_ISSUE_UNROLL = 8


def _tile(n, *cands):
    for c in cands:
        if n % c == 0:
            return c
    return n


def _cparams(n_axes):
    return pltpu.CompilerParams(dimension_semantics=("arbitrary",) * n_axes,
                                vmem_limit_bytes=VMEM_LIMIT_BYTES)


def _dot(a, b):
    return jnp.dot(a, b, preferred_element_type=F32)


def _dot_t(a, bt):
    return lax.dot_general(a, bt, (((1,), (1,)), ((), ())), preferred_element_type=F32)


def _bf16(w):
    return w if w.dtype == BF16 else w.astype(BF16)


def _ada_kernel(xt_ref, w_ref, b_ref, o_ref, *, k_chunk):
    d = xt_ref.shape[0]
    tn = o_ref.shape[-1]

    def body(k, acc):
        a0, a1 = acc
        ks = pl.multiple_of(k * k_chunk, k_chunk)
        xt = xt_ref[pl.ds(ks, k_chunk), :]
        s = xt * jax.nn.sigmoid(xt)
        w = w_ref[0, pl.ds(ks, k_chunk), :]
        a0 = a0 + jnp.sum(w * s[:, 0:1], axis=0, keepdims=True)
        a1 = a1 + jnp.sum(w * s[:, 1:2], axis=0, keepdims=True)
        return a0, a1

    z = jnp.zeros((1, tn), F32)
    a0, a1 = lax.fori_loop(0, d // k_chunk, body, (z, z))
    o_ref[0, 0:1, :] = a0 + b_ref[0]
    o_ref[0, 1:2, :] = a1 + b_ref[0]


def _ada_mod(c, c_ctx, w_ada, b_ada):
    depth, d, n = w_ada.shape
    xt = jnp.stack([c[0], c_ctx], axis=1)
    tn = _tile(n, 1024, 512, LANES)
    k_chunk = _tile(d, 256, SUBLANES)
    return pl.pallas_call(
        functools.partial(_ada_kernel, k_chunk=k_chunk),
        grid=(depth, n // tn),
        in_specs=[pl.BlockSpec((d, 2), lambda l, j: (0, 0)),
                  pl.BlockSpec((1, d, tn), lambda l, j: (l, 0, j)),
                  pl.BlockSpec((1, 1, tn), lambda l, j: (l, 0, j))],
        out_specs=pl.BlockSpec((1, 2, tn), lambda l, j: (l, 0, j)),
        out_shape=jax.ShapeDtypeStruct((depth, 2, n), F32),
        compiler_params=_cparams(2),
        name="ada_mod",
    )(xt, w_ada, b_ada.reshape(depth, 1, n))


def _norm_mod(x, g_ref, sh_ref, sc_ref, is_ctx):
    y = x * lax.rsqrt(jnp.mean(x * x, axis=-1, keepdims=True) + NORM_EPS) * g_ref[...]
    sh = jnp.where(is_ctx, sh_ref[1:2, :], sh_ref[0:1, :])
    sc = jnp.where(is_ctx, sc_ref[1:2, :], sc_ref[0:1, :])
    return y * (1.0 + sc) + sh


def _norm_kernel(x_ref, g_ref, sh_ref, sc_ref, o_ref, *, n_lat_tiles):
    is_ctx = pl.program_id(0) >= n_lat_tiles
    o_ref[...] = _norm_mod(x_ref[...], g_ref, sh_ref, sc_ref, is_ctx).astype(o_ref.dtype)


def _norm_router_kernel(x_ref, g_ref, sh_ref, sc_ref, wr_ref, h_ref, idx_ref, gate_ref):
    h = _norm_mod(x_ref[...], g_ref, sh_ref, sc_ref, False)
    h_ref[...] = h
    logits = jnp.dot(h, wr_ref[...], preferred_element_type=F32, precision=lax.Precision.HIGHEST)
    lane = lax.broadcasted_iota(jnp.int32, logits.shape, 1).astype(F32)
    neg = jnp.float32(-jnp.inf)
    l1 = jnp.where(lane < N_EXPERTS, logits, neg)
    v1 = jnp.max(l1, axis=-1, keepdims=True)
    i1 = jnp.min(jnp.where(l1 == v1, lane, float(LANES)), axis=-1, keepdims=True)
    l2 = jnp.where(lane == i1, neg, l1)
    v2 = jnp.max(l2, axis=-1, keepdims=True)
    i2 = jnp.min(jnp.where(l2 == v2, lane, float(LANES)), axis=-1, keepdims=True)
    e = jnp.exp(v2 - v1)
    g1 = 1.0 / (1.0 + e)
    g2 = e / (1.0 + e)
    idx_ref[...] = jnp.where(lane == 0, i1, jnp.where(lane == 1, i2, 0.0)).astype(jnp.int32)
    gate_ref[...] = jnp.where(lane == 0, g1, jnp.where(lane == 1, g2, 0.0))


def _norm_modulate(x, g, mods, sh_blk, sc_blk, n_lat, out_dtype):
    r, d = x.shape
    tm = _tile(n_lat, 256, 128, 64, SUBLANES) if r > n_lat else _tile(r, 256, 128, 64, SUBLANES)
    if r > n_lat:
        tm = _tile(r - n_lat, tm, 128, 64, SUBLANES)
    return pl.pallas_call(
        functools.partial(_norm_kernel, n_lat_tiles=n_lat // tm),
        grid=(r // tm,),
        in_specs=[pl.BlockSpec((tm, d), lambda i: (i, 0)),
                  pl.BlockSpec((1, d), lambda i: (0, 0)),
                  pl.BlockSpec((2, d), lambda i: (0, sh_blk)),
                  pl.BlockSpec((2, d), lambda i: (0, sc_blk))],
        out_specs=pl.BlockSpec((tm, d), lambda i: (i, 0)),
        out_shape=jax.ShapeDtypeStruct((r, d), out_dtype),
        compiler_params=_cparams(1),
        name="norm_modulate",
    )(x, g.reshape(1, d), mods, mods)


def _norm_modulate_route(x, g, mods, sh_blk, sc_blk, w_router):
    r, d = x.shape
    tm = _tile(r, 256, 128, 64, SUBLANES)
    wr = jnp.pad(w_router, ((0, 0), (0, LANES - N_EXPERTS)))
    return pl.pallas_call(
        _norm_router_kernel,
        grid=(r // tm,),
        in_specs=[pl.BlockSpec((tm, d), lambda i: (i, 0)),
                  pl.BlockSpec((1, d), lambda i: (0, 0)),
                  pl.BlockSpec((2, d), lambda i: (0, sh_blk)),
                  pl.BlockSpec((2, d), lambda i: (0, sc_blk)),
                  pl.BlockSpec((d, LANES), lambda i: (0, 0))],
        out_specs=[pl.BlockSpec((tm, d), lambda i: (i, 0)),
                   pl.BlockSpec((tm, LANES), lambda i: (i, 0)),
                   pl.BlockSpec((tm, LANES), lambda i: (i, 0))],
        out_shape=[jax.ShapeDtypeStruct((r, d), F32),
                   jax.ShapeDtypeStruct((r, LANES), jnp.int32),
                   jax.ShapeDtypeStruct((r, LANES), F32)],
        compiler_params=_cparams(1),
        name="norm_modulate_route",
    )(x, g.reshape(1, d), mods, mods, wr)


def _in_a_kernel(h_ref, wt_ref, g_ref, cos_ref, sin_ref, qa_ref, ckv_ref, kr_ref):
    acc = _dot_t(h_ref[...], wt_ref[...])

    def rms(v, g):
        return v * lax.rsqrt(jnp.mean(v * v, axis=-1, keepdims=True) + NORM_EPS) * g

    q0, q1 = 0, Q_LORA_RANK
    c1 = q1 + KV_LORA_RANK
    qa_ref[...] = rms(acc[:, q0:q1], g_ref[:, q0:q1]).astype(qa_ref.dtype)
    ckv_ref[...] = rms(acc[:, q1:c1], g_ref[:, q1:c1]).astype(ckv_ref.dtype)
    kr = acc[:, c1:c1 + LANES] * cos_ref[...] + acc[:, c1 + LANES:c1 + 2 * LANES] * sin_ref[...]
    kr_ref[...] = kr.astype(kr_ref.dtype)


def _in_proj_a(h, w_at, g_a, cos_t, sin_t):
    m, d = h.shape
    n = w_at.shape[0]
    tm = _tile(m, 768, 640, 512, 256, 128)
    return pl.pallas_call(
        _in_a_kernel,
        grid=(m // tm,),
        in_specs=[pl.BlockSpec((tm, d), lambda i: (i, 0)),
                  pl.BlockSpec((n, d), lambda i: (0, 0)),
                  pl.BlockSpec((1, Q_LORA_RANK + KV_LORA_RANK), lambda i: (0, 0)),
                  pl.BlockSpec((tm, LANES), lambda i: (i, 0)),
                  pl.BlockSpec((tm, LANES), lambda i: (i, 0))],
        out_specs=[pl.BlockSpec((tm, Q_LORA_RANK), lambda i: (i, 0)),
                   pl.BlockSpec((tm, KV_LORA_RANK), lambda i: (i, 0)),
                   pl.BlockSpec((tm, LANES), lambda i: (i, 0))],
        out_shape=[jax.ShapeDtypeStruct((m, Q_LORA_RANK), BF16),
                   jax.ShapeDtypeStruct((m, KV_LORA_RANK), BF16),
                   jax.ShapeDtypeStruct((m, LANES), BF16)],
        compiler_params=_cparams(1),
        name="in_proj_a",
    )(h, w_at, g_a, cos_t, sin_t)


def _in_b_kernel(h_ref, wt_ref, o_ref, *, first_gate_tile):
    acc = _dot_t(h_ref[...], wt_ref[...])
    is_gate = pl.program_id(0) >= first_gate_tile

    @pl.when(is_gate)
    def _():
        o_ref[...] = jax.nn.sigmoid(acc).astype(o_ref.dtype)

    @pl.when(jnp.logical_not(is_gate))
    def _():
        o_ref[...] = acc.astype(o_ref.dtype)


def _in_proj_b(h, w_bt, first_gate_col):
    m, d = h.shape
    n = w_bt.shape[0]
    tm = _tile(m, 1408, 768, 640, 512, 256, 128)
    tn = _tile(first_gate_col, 1024, 512, 256, LANES)
    return pl.pallas_call(
        functools.partial(_in_b_kernel, first_gate_tile=first_gate_col // tn),
        grid=(n // tn, m // tm),
        in_specs=[pl.BlockSpec((tm, d), lambda j, i: (i, 0)),
                  pl.BlockSpec((tn, d), lambda j, i: (j, 0))],
        out_specs=pl.BlockSpec((tm, tn), lambda j, i: (i, j)),
        out_shape=jax.ShapeDtypeStruct((m, n), BF16),
        compiler_params=_cparams(2),
        name="in_proj_b",
    )(h, w_bt)


def _q_kernel(a_ref, wm_ref, ws_ref, cos_ref, sin_ref, q_ref):
    a = a_ref[...]
    cos = cos_ref[...]
    sin = sin_ref[...]
    for hd in range(N_HEADS):
        main = _dot(a, wm_ref[:, hd * HEAD_W:(hd + 1) * HEAD_W])
        swap = _dot(a, ws_ref[:, hd * LANES:(hd + 1) * LANES])
        q_ref[:, hd * HEAD_W:hd * HEAD_W + LANES] = (main[:, :LANES] * Q_SCALE).astype(q_ref.dtype)
        rot = (main[:, LANES:] * cos + swap * sin) * Q_SCALE
        q_ref[:, hd * HEAD_W + LANES:(hd + 1) * HEAD_W] = rot.astype(q_ref.dtype)


def _q_proj(qa_n, w_main, w_swap, cos_t, sin_t, rows):
    k = qa_n.shape[1]
    tm = _tile(rows, 768, 640, 512, 256, 128)
    return pl.pallas_call(
        _q_kernel,
        grid=(rows // tm,),
        in_specs=[pl.BlockSpec((tm, k), lambda i: (i, 0)),
                  pl.BlockSpec(w_main.shape, lambda i: (0, 0)),
                  pl.BlockSpec(w_swap.shape, lambda i: (0, 0)),
                  pl.BlockSpec((tm, LANES), lambda i: (i, 0)),
                  pl.BlockSpec((tm, LANES), lambda i: (i, 0))],
        out_specs=pl.BlockSpec((tm, N_HEADS * HEAD_W), lambda i: (i, 0)),
        out_shape=jax.ShapeDtypeStruct((rows, N_HEADS * HEAD_W), BF16),
        compiler_params=_cparams(1),
        name="q_proj",
    )(qa_n, w_main, w_swap, cos_t, sin_t)


def _kv_kernel(c_ref, kr_ref, wk_ref, wv_ref, k_ref, v_ref):
    c = c_ref[...]
    kr = kr_ref[...]
    lane = lax.broadcasted_iota(jnp.int32, (c.shape[0], LANES), 1)
    ones_col = jnp.where(lane == 0, 1.0, 0.0).astype(v_ref.dtype)
    for hd in range(N_HEADS):
        kn = _dot(c, wk_ref[:, hd * LANES:(hd + 1) * LANES])
        k_ref[:, hd * HEAD_W:hd * HEAD_W + LANES] = kn.astype(k_ref.dtype)
        k_ref[:, hd * HEAD_W + LANES:(hd + 1) * HEAD_W] = kr
        vh = _dot(c, wv_ref[:, hd * LANES:(hd + 1) * LANES])
        v_ref[:, hd * HEAD_W:hd * HEAD_W + LANES] = vh.astype(v_ref.dtype)
        v_ref[:, hd * HEAD_W + LANES:(hd + 1) * HEAD_W] = ones_col


def _kv_proj(ckv_n, kr, w_k, w_v):
    m, k = ckv_n.shape
    tm = _tile(m, 768, 640, 512, 256, 128)
    return pl.pallas_call(
        _kv_kernel,
        grid=(m // tm,),
        in_specs=[pl.BlockSpec((tm, k), lambda i: (i, 0)),
                  pl.BlockSpec((tm, LANES), lambda i: (i, 0)),
                  pl.BlockSpec(w_k.shape, lambda i: (0, 0)),
                  pl.BlockSpec(w_v.shape, lambda i: (0, 0))],
        out_specs=[pl.BlockSpec((tm, N_HEADS * HEAD_W), lambda i: (i, 0)),
                   pl.BlockSpec((tm, N_HEADS * HEAD_W), lambda i: (i, 0))],
        out_shape=[jax.ShapeDtypeStruct((m, N_HEADS * HEAD_W), BF16),
                   jax.ShapeDtypeStruct((m, N_HEADS * HEAD_W), BF16)],
        compiler_params=_cparams(1),
        name="kv_proj",
    )(ckv_n, kr, w_k, w_v)


def _attn_kernel(q_ref, k_ref, v_ref, o_ref, sa_ref, sb_ref, *, tk, n_chunks):
    q = q_ref[...]
    tq = q.shape[0]

    def scores(c):
        return _dot_t(q, k_ref[c * tk:(c + 1) * tk, :])

    def absorb(s_ref, c, carry):
        m, acc = carry
        s = s_ref[...]
        m_new = jnp.maximum(m, jnp.max(s, axis=-1, keepdims=True))
        p = jnp.exp2(s - m_new).astype(v_ref.dtype)
        acc = jnp.exp2(m - m_new) * acc + _dot(p, v_ref[c * tk:(c + 1) * tk, :])
        return m_new, acc

    s_refs = (sa_ref, sb_ref)
    carry = (jnp.full((tq, 1), -jnp.inf, F32), jnp.zeros((tq, HEAD_W), F32))
    sa_ref[...] = scores(0)
    for c in range(n_chunks):
        if c + 1 < n_chunks:
            s_refs[(c + 1) % 2][...] = scores(c + 1)
        carry = absorb(s_refs[c % 2], c, carry)
    _, acc = carry
    o_ref[...] = (acc[:, :V_HEAD_DIM] / acc[:, V_HEAD_DIM:V_HEAD_DIM + 1]).astype(o_ref.dtype)


def _attention(q, k, v, q_row0, n_q, k_row0, n_k):
    tq = _tile(n_q, 512, 256, 128)
    tk = _tile(n_k, 768, 640, 512, 384, 256, 128)
    assert q_row0 % tq == 0 and k_row0 % n_k == 0
    q_blk0 = q_row0 // tq
    k_blk = k_row0 // n_k
    return pl.pallas_call(
        functools.partial(_attn_kernel, tk=tk, n_chunks=n_k // tk),
        grid=(N_HEADS, n_q // tq),
        in_specs=[pl.BlockSpec((tq, HEAD_W), lambda h, i: (q_blk0 + i, h)),
                  pl.BlockSpec((n_k, HEAD_W), lambda h, i: (k_blk, h)),
                  pl.BlockSpec((n_k, HEAD_W), lambda h, i: (k_blk, h))],
        out_specs=pl.BlockSpec((tq, V_HEAD_DIM), lambda h, i: (i, h)),
        out_shape=jax.ShapeDtypeStruct((n_q, N_HEADS * V_HEAD_DIM), BF16),
        scratch_shapes=[pltpu.VMEM((tq, tk), F32), pltpu.VMEM((tq, tk), F32)],
        compiler_params=_cparams(2),
        name="attention",
    )(q, k, v)


def _conv_gate_kernel(cx_ref, cb_ref, cc_ref, cxp_ref, ccp_ref, cxn_ref, ccn_ref, wc_ref, z_ref, *,
                      seg_starts, seg_ends, col_chunk):
    tm, width = z_ref.shape
    loc = lax.broadcasted_iota(jnp.int32, (tm, 1), 0)
    row = loc + pl.program_id(0) * tm
    first = functools.reduce(jnp.logical_or, [row == r for r in seg_starts])
    last = functools.reduce(jnp.logical_or, [row == r for r in seg_ends])
    for c0 in range(0, width, col_chunk):
        cs = slice(c0, c0 + col_chunk)
        u = cx_ref[:, cs].astype(F32) * cc_ref[:, cs].astype(F32)
        u_halo_prev = (cxp_ref[SUBLANES - 1:SUBLANES, cs].astype(F32)
                       * ccp_ref[SUBLANES - 1:SUBLANES, cs].astype(F32))
        u_halo_next = cxn_ref[0:1, cs].astype(F32) * ccn_ref[0:1, cs].astype(F32)
        u_prev = jnp.where(loc == 0, u_halo_prev, pltpu.roll(u, 1, 0))
        u_prev = jnp.where(first, 0.0, u_prev)
        u_next = jnp.where(loc == tm - 1, u_halo_next, pltpu.roll(u, tm - 1, 0))
        u_next = jnp.where(last, 0.0, u_next)
        conv = wc_ref[0:1, cs] * u_prev + wc_ref[1:2, cs] * u + wc_ref[2:3, cs] * u_next
        z_ref[:, cs] = (cb_ref[:, cs].astype(F32) * conv).astype(z_ref.dtype)


def _conv_gate(p, w_conv, rows, s_len, m_len):
    width = w_conv.shape[1]
    tm = _tile(rows, 768, 512, 384, 256, 128, 64)
    hb = tm // SUBLANES
    last_hb = p.shape[0] // SUBLANES - 1

    def prev_map(col):
        return lambda i: (jnp.maximum(i * hb - 1, 0), col)

    def next_map(col):
        return lambda i: (jnp.minimum((i + 1) * hb, last_hb), col)

    kern = functools.partial(_conv_gate_kernel, seg_starts=(0, s_len), seg_ends=(s_len - 1, m_len - 1),
                             col_chunk=_tile(width, 512, LANES))
    return pl.pallas_call(
        kern,
        grid=(rows // tm,),
        in_specs=[pl.BlockSpec((tm, width), lambda i: (i, 0)),
                  pl.BlockSpec((tm, width), lambda i: (i, 1)),
                  pl.BlockSpec((tm, width), lambda i: (i, 2)),
                  pl.BlockSpec((SUBLANES, width), prev_map(0)),
                  pl.BlockSpec((SUBLANES, width), prev_map(2)),
                  pl.BlockSpec((SUBLANES, width), next_map(0)),
                  pl.BlockSpec((SUBLANES, width), next_map(2)),
                  pl.BlockSpec((CONV_K, width), lambda i: (0, 0))],
        out_specs=pl.BlockSpec((tm, width), lambda i: (i, 0)),
        out_shape=jax.ShapeDtypeStruct((rows, width), BF16),
        compiler_params=_cparams(1),
        name="conv_gate",
    )(p, p, p, p, p, p, p, w_conv)


def _merge_kernel(attn_ref, z_ref, sga_ref, sgb_ref, woa_ref, wob_ref, o_ref):
    o_a = _dot(attn_ref[...], woa_ref[...])
    o_b = _dot(z_ref[...], wob_ref[...])
    o_ref[...] = (sga_ref[...].astype(F32) * o_a + sgb_ref[...].astype(F32) * o_b).astype(o_ref.dtype)


def _merge(attn, z, p, w_oa, w_ob, rows):
    attn_w = w_oa.shape[0]
    width, d = w_ob.shape
    tm = _tile(rows, 768, 512, 384, 256, 128, 64)
    tn = _tile(d, 1024, 512, 256, LANES)
    ga_blk = 3 * width // tn
    gb_blk = (3 * width + d) // tn
    return pl.pallas_call(
        _merge_kernel,
        grid=(d // tn, rows // tm),
        in_specs=[pl.BlockSpec((tm, attn_w), lambda j, i: (i, 0)),
                  pl.BlockSpec((tm, width), lambda j, i: (i, 0)),
                  pl.BlockSpec((tm, tn), lambda j, i: (i, ga_blk + j)),
                  pl.BlockSpec((tm, tn), lambda j, i: (i, gb_blk + j)),
                  pl.BlockSpec((attn_w, tn), lambda j, i: (0, j)),
                  pl.BlockSpec((width, tn), lambda j, i: (0, j))],
        out_specs=pl.BlockSpec((tm, tn), lambda j, i: (i, j)),
        out_shape=jax.ShapeDtypeStruct((rows, d), BF16),
        compiler_params=_cparams(2),
        name="merge",
    )(attn, z, p, p, w_oa, w_ob)


def _mm_res_kernel(a_ref, w_ref, x_ref, gt_ref, o_ref, *, n_lat):
    tm = a_ref.shape[0]
    row = lax.broadcasted_iota(jnp.int32, (tm, 1), 0) + pl.program_id(1) * tm
    gate = jnp.where(row >= n_lat, gt_ref[1:2, :], gt_ref[0:1, :])
    o_ref[...] = x_ref[...] + gate * _dot(a_ref[...], _bf16(w_ref[...]))


def _matmul_residual(a, w, x, mods, gate_blk, rows, n_lat):
    k, n = w.shape
    tm = _tile(rows, 768, 640, 512, 256, 128)
    tn = _tile(n, 512, 256, LANES)
    nt = n // tn
    return pl.pallas_call(
        functools.partial(_mm_res_kernel, n_lat=n_lat),
        grid=(nt, rows // tm),
        in_specs=[pl.BlockSpec((tm, k), lambda j, i: (i, 0)),
                  pl.BlockSpec((k, tn), lambda j, i: (0, j)),
                  pl.BlockSpec((tm, tn), lambda j, i: (i, j)),
                  pl.BlockSpec((2, tn), lambda j, i: (0, gate_blk * nt + j))],
        out_specs=pl.BlockSpec((tm, tn), lambda j, i: (i, j)),
        out_shape=jax.ShapeDtypeStruct((rows, n), F32),
        compiler_params=_cparams(2),
        name="matmul_residual",
    )(a, w, x, mods)


def _block_in_use(eid_ref):
    return pl.program_id(1) < eid_ref[pl.num_programs(1)]


def _glu_kernel(eid_ref, a_ref, wg_ref, wu_ref, o_ref):
    @pl.when(_block_in_use(eid_ref))
    def _():
        a = _bf16(a_ref[...])
        gate = _dot(a, _bf16(wg_ref[0]))
        up = _dot(a, _bf16(wu_ref[0]))
        o_ref[...] = (gate * jax.nn.sigmoid(gate) * up).astype(o_ref.dtype)

    @pl.when(jnp.logical_not(_block_in_use(eid_ref)))
    def _():
        o_ref[...] = jnp.zeros_like(o_ref)


def _glu(a, w_gate, w_up, eid, tm):
    r, k = a.shape
    f = w_gate.shape[2]
    tn = _tile(f, 512, 1408, 256, LANES)
    grid_spec = pltpu.PrefetchScalarGridSpec(
        num_scalar_prefetch=1,
        grid=(f // tn, r // tm),
        in_specs=[pl.BlockSpec((tm, k), lambda j, i, e: (i, 0)),
                  pl.BlockSpec((1, k, tn), lambda j, i, e: (e[i], 0, j)),
                  pl.BlockSpec((1, k, tn), lambda j, i, e: (e[i], 0, j))],
        out_specs=pl.BlockSpec((tm, tn), lambda j, i, e: (i, j)),
    )
    return pl.pallas_call(
        _glu_kernel,
        grid_spec=grid_spec,
        out_shape=jax.ShapeDtypeStruct((r, f), BF16),
        compiler_params=_cparams(2),
        name="glu",
    )(eid, a, w_gate, w_up)


def _down_kernel(eid_ref, a_ref, w_ref, o_ref):
    @pl.when(_block_in_use(eid_ref))
    def _():
        o_ref[...] = _dot(a_ref[...], w_ref[0])

    @pl.when(jnp.logical_not(_block_in_use(eid_ref)))
    def _():
        o_ref[...] = jnp.zeros_like(o_ref)


def _down_grouped(a, w_down, eid, tm):
    r, f = a.shape
    d = w_down.shape[2]
    grid_spec = pltpu.PrefetchScalarGridSpec(
        num_scalar_prefetch=1,
        grid=(1, r // tm),
        in_specs=[pl.BlockSpec((tm, f), lambda j, i, e: (i, 0)),
                  pl.BlockSpec((1, f, d), lambda j, i, e: (e[i], 0, 0))],
        out_specs=pl.BlockSpec((tm, d), lambda j, i, e: (i, 0)),
    )
    return pl.pallas_call(
        _down_kernel,
        grid_spec=grid_spec,
        out_shape=jax.ShapeDtypeStruct((r, d), F32),
        compiler_params=_cparams(2),
        name="down_grouped",
    )(eid, a, w_down)


def _row_copies(idx_ref, idx_base, idx_stride, src_ref, dst_ref, sem, n_rows, start):
    def body(r, c):
        row = idx_ref[idx_base + r * idx_stride]
        cp = pltpu.make_async_copy(src_ref.at[pl.ds(row, 1)], dst_ref.at[pl.ds(r, 1)], sem)
        if start:
            cp.start()
        else:
            cp.wait()
        return c

    lax.fori_loop(0, n_rows, body, 0, unroll=DMA_ISSUE_UNROLL)


def _prefetched_gather(copies):
    i = pl.program_id(0)
    slot = i % 2

    @pl.when(i == 0)
    def _():
        copies(0, 0, True)

    @pl.when(i + 1 < pl.num_programs(0))
    def _():
        copies(i + 1, 1 - slot, True)

    copies(i, slot, False)
    return slot


def _gather_rows_kernel(idx_ref, src_ref, o_ref, g_ref, sem):
    tm = o_ref.shape[0]

    def copies(step, slot, start):
        _row_copies(idx_ref, step * tm, 1, src_ref, g_ref.at[slot], sem.at[slot], tm, start)

    slot = _prefetched_gather(copies)
    o_ref[...] = g_ref[slot]


def _gather_rows(src, idx, tm):
    n = idx.shape[0]
    w = src.shape[1]
    grid_spec = pltpu.PrefetchScalarGridSpec(
        num_scalar_prefetch=1,
        grid=(n // tm,),
        in_specs=[pl.BlockSpec(memory_space=pl.ANY)],
        out_specs=pl.BlockSpec((tm, w), lambda i, idx_ref: (i, 0)),
        scratch_shapes=[pltpu.VMEM((2, tm, w), src.dtype), pltpu.SemaphoreType.DMA((2,))],
    )
    return pl.pallas_call(
        _gather_rows_kernel,
        grid_spec=grid_spec,
        out_shape=jax.ShapeDtypeStruct((n, w), src.dtype),
        compiler_params=_cparams(1),
        name="gather_rows",
    )(idx, src)


def _final_kernel(dest_ref, x_ref, y_hbm_ref, gate_ref, gt_ref, g_ref, o_ref, y_ref, sem):
    tm = x_ref.shape[0]

    def copies(step, slot, start):
        for kk in range(TOP_K):
            _row_copies(dest_ref, step * tm * TOP_K + kk, TOP_K, y_hbm_ref, y_ref.at[slot, kk], sem.at[slot],
                        tm, start)

    slot = _prefetched_gather(copies)
    gates = gate_ref[...]
    y = gates[:, 0:1] * y_ref[slot, 0] + gates[:, 1:2] * y_ref[slot, 1]
    x = x_ref[...] + gt_ref[0:1, :] * y
    o_ref[...] = x * lax.rsqrt(jnp.mean(x * x, axis=-1, keepdims=True) + NORM_EPS) * g_ref[...]


def _combine_final(x, ybuf, dest, gates, mods, gate_blk, g_final):
    r, d = x.shape
    tm = _tile(r, 256, 128, 64, SUBLANES)
    grid_spec = pltpu.PrefetchScalarGridSpec(
        num_scalar_prefetch=1,
        grid=(r // tm,),
        in_specs=[pl.BlockSpec((tm, d), lambda i, dest_ref: (i, 0)),
                  pl.BlockSpec(memory_space=pl.ANY),
                  pl.BlockSpec((tm, LANES), lambda i, dest_ref: (i, 0)),
                  pl.BlockSpec((2, d), lambda i, dest_ref: (0, gate_blk)),
                  pl.BlockSpec((1, d), lambda i, dest_ref: (0, 0))],
        out_specs=pl.BlockSpec((tm, d), lambda i, dest_ref: (i, 0)),
        scratch_shapes=[pltpu.VMEM((2, TOP_K, tm, d), F32), pltpu.SemaphoreType.DMA((2,))],
    )
    return pl.pallas_call(
        _final_kernel,
        grid_spec=grid_spec,
        out_shape=jax.ShapeDtypeStruct((r, d), F32),
        compiler_params=_cparams(1),
        name="combine_final",
    )(dest, x, ybuf, gates, mods, g_final.reshape(1, d))


def _rope_tables(s_len, c_len):
    quarter = QK_ROPE_DIM // 4
    tok = jnp.arange(s_len + c_len, dtype=jnp.int32)[:, None]
    lane = jnp.arange(LANES, dtype=jnp.int32)[None, :]
    group = lane // quarter
    pos = jnp.where(group < 2, tok // GRID_W, tok % GRID_W).astype(F32)
    inv_freq = jnp.power(ROPE_THETA, -(2 * (lane % quarter)).astype(F32) / (QK_ROPE_DIM // 2))
    ang = jnp.where(tok < s_len, pos * inv_freq, 0.0)
    live = group < 4
    cos = jnp.where(live, jnp.cos(ang), 0.0)
    sin = jnp.where(live, jnp.where(group % 2 == 0, -jnp.sin(ang), jnp.sin(ang)), 0.0)
    return cos, sin


def _rope_swap_perm():
    q = QK_ROPE_DIM // 4
    return jnp.concatenate([jnp.arange(q, 2 * q), jnp.arange(0, q), jnp.arange(3 * q, 4 * q), jnp.arange(2 * q, 3 * q)])


def _layer_weights(w_in_t, w_qb_all, w_kvb_all, li):
    d = w_in_t.shape[2]
    perm = _rope_swap_perm()
    kr_lo = Q_LORA_RANK + KV_LORA_RANK
    kr_hi = kr_lo + QK_ROPE_DIM
    w_kr = w_in_t[li, kr_lo:kr_hi]
    zpad = jnp.zeros((LANES - QK_ROPE_DIM, d), w_in_t.dtype)
    w_a = jnp.concatenate([w_in_t[li, :kr_lo], w_kr, zpad, w_kr[perm], zpad], axis=0)
    w_b = w_in_t[li, kr_hi:]
    w_qb, w_kvb = w_qb_all[li], w_kvb_all[li]
    qb = w_qb.reshape(Q_LORA_RANK, N_HEADS, QK_NOPE_DIM + QK_ROPE_DIM)
    q_rope = qb[:, :, QK_NOPE_DIM:]
    hpad = jnp.zeros((Q_LORA_RANK, N_HEADS, LANES - QK_ROPE_DIM), w_qb.dtype)
    w_q_main = jnp.concatenate([qb, hpad], axis=2).reshape(Q_LORA_RANK, N_HEADS * HEAD_W)
    w_q_swap = jnp.concatenate([q_rope[:, :, perm], hpad], axis=2).reshape(Q_LORA_RANK, N_HEADS * LANES)
    kvb = w_kvb.reshape(KV_LORA_RANK, N_HEADS, QK_NOPE_DIM + V_HEAD_DIM)
    w_k = kvb[:, :, :QK_NOPE_DIM].reshape(KV_LORA_RANK, N_HEADS * QK_NOPE_DIM)
    w_v = kvb[:, :, QK_NOPE_DIM:].reshape(KV_LORA_RANK, N_HEADS * V_HEAD_DIM)
    return tuple(w.astype(BF16) for w in (w_a, w_b, w_q_main, w_q_swap, w_k, w_v))


def _moe_slots(idx, n_blocks):
    e_flat = idx.reshape(-1)
    onehot = (e_flat[:, None] == jnp.arange(N_EXPERTS)[None, :]).astype(jnp.int32)
    csum = jnp.cumsum(onehot, axis=0)
    counts = csum[-1]
    rank = jnp.sum((csum - onehot) * onehot, axis=1)
    padded = (counts + MOE_ROWS - 1) // MOE_ROWS * MOE_ROWS
    p_end = jnp.cumsum(padded)
    p_start = p_end - padded
    dest = jnp.sum(onehot * p_start[None, :], axis=1) + rank
    blk_lo = jnp.arange(n_blocks) * MOE_ROWS
    block_expert = jnp.minimum(jnp.sum((blk_lo[:, None] >= p_end[None, :]).astype(jnp.int32), axis=1),
                               N_EXPERTS - 1)
    blocks_in_use = p_end[-1:] // MOE_ROWS
    return dest.astype(jnp.int32), jnp.concatenate([block_expert, blocks_in_use]).astype(jnp.int32)


def kernel(x, c, ctx, c_ctx, w_ada, b_ada, g_attn, w_in, g_qa, w_qb, g_kva, w_kvb, w_conv, w_oa, w_ob, w_o,
           g_ffn, w_gate_dense, w_up_dense, w_down_dense, w_router, w_gate_exp, w_up_exp, w_down_exp, g_final):
    _, s_len, d = x.shape
    c_len = ctx.shape[1]
    m_len = s_len + c_len
    depth = w_in.shape[0]
    width = w_conv.shape[2]
    assert depth == 2, "supported stack: dense-FFN layer with context updates, then a final expert-FFN layer"

    mods_all = _ada_mod(c, c_ctx, w_ada, b_ada)
    cos_t, sin_t = _rope_tables(s_len, c_len)
    w_in_t = jnp.swapaxes(w_in, 1, 2)
    xa = jnp.concatenate([x[0], ctx[0]], axis=0)

    for li in range(depth):
        last = li == depth - 1
        mods = mods_all[li]
        w_a, w_b, w_q_main, w_q_swap, w_k, w_v = _layer_weights(w_in_t, w_qb, w_kvb, li)
        g_a = jnp.concatenate([g_qa[li], g_kva[li]]).reshape(1, -1)
        rows = s_len if last else m_len

        h = _norm_modulate(xa, g_attn[li], mods, 0, 1, s_len, BF16)
        qa_n, ckv_n, kr = _in_proj_a(h, w_a, g_a, cos_t, sin_t)
        p = _in_proj_b(h, w_b, 3 * width)
        q = _q_proj(qa_n, w_q_main, w_q_swap, cos_t, sin_t, rows)
        k, v = _kv_proj(ckv_n, kr, w_k, w_v)
        attn = _attention(q, k, v, 0, s_len, 0, m_len)
        if not last:
            attn = jnp.concatenate([attn, _attention(q, k, v, s_len, c_len, s_len, c_len)], axis=0)
        z = _conv_gate(p, w_conv[li], rows, s_len, m_len)
        merged = _merge(attn, z, p, w_oa[li].astype(BF16), w_ob[li].astype(BF16), rows)
        xa = _matmul_residual(merged, w_o[li], xa, mods, 2, rows, s_len)

        j = li // 2
        if li % 2 == 0:
            h2 = _norm_modulate(xa, g_ffn[li], mods, 3, 4, s_len, BF16)
            tm = _tile(rows, 768, 640, 512, 256, 128)
            eid = jnp.zeros((rows // tm + 1,), jnp.int32).at[-1].set(rows // tm)
            hid = _glu(h2, w_gate_dense[j][None], w_up_dense[j][None], eid, tm)
            xa = _matmul_residual(hid, w_down_dense[j].astype(BF16), xa, mods, 5, rows, s_len)
        else:
            h2, idx, gates = _norm_modulate_route(xa, g_ffn[li], mods, 3, 4, w_router[j])
            n_assign = rows * TOP_K
            n_blocks = -(-n_assign // MOE_ROWS) + N_EXPERTS
            dest, block_expert = _moe_slots(idx[:, :TOP_K], n_blocks)
            tok = jnp.arange(n_assign, dtype=jnp.int32) // TOP_K
            slot_tok = jnp.zeros((n_blocks * MOE_ROWS,), jnp.int32).at[dest].set(tok, unique_indices=True)
            buf = _gather_rows(h2, slot_tok, MOE_ROWS)
            hid = _glu(buf, w_gate_exp[j].astype(BF16), w_up_exp[j].astype(BF16), block_expert, MOE_ROWS)
            ybuf = _down_grouped(hid, w_down_exp[j].astype(BF16), block_expert, MOE_ROWS)
            return _combine_final(xa, ybuf, dest, gates, mods, 5, g_final)[None]
    raise AssertionError("unreachable: the final layer returns")
```

```python
import functools

import jax
import jax.numpy as jnp
from jax import lax
from jax.experimental import pallas as pl
from jax.experimental.pallas import tpu as pltpu

F32 = jnp.float32
BF16 = jnp.bfloat16

N_HEADS = 16
QK_NOPE_DIM = 128
QK_ROPE_DIM = 64
V_HEAD_DIM = 128
Q_LORA_RANK = 512
KV_LORA_RANK = 512
GRID_W = 64
ROPE_THETA = 10000.0
ATTN_SCALE = (QK_NOPE_DIM + QK_ROPE_DIM) ** -0.5
Q_SCALE = ATTN_SCALE * 1.4426950408889634
CONV_K = 3
N_EXPERTS = 8
TOP_K = 2
NORM_EPS = 1e-6

LANES = 128
SUBLANES = 8
HEAD_W = 2 * LANES
VMEM_LIMIT_BYTES = 56 * 1024 * 1024

MOE_ROWS = 256
DMA_ISSUE_UNROLL = 8


def _tile(n, *cands):
    for c in cands:
        if n % c == 0:
            return c
    return n


def _cparams(n_axes):
    return pltpu.CompilerParams(dimension_semantics=("arbitrary",) * n_axes,
                                vmem_limit_bytes=VMEM_LIMIT_BYTES)


def _dot(a, b):
    return jnp.dot(a, b, preferred_element_type=F32)


def _dot_t(a, bt):
    return lax.dot_general(a, bt, (((1,), (1,)), ((), ())), preferred_element_type=F32)


def _bf16(w):
    return w if w.dtype == BF16 else w.astype(BF16)


def _ada_kernel(xt_ref, w_ref, b_ref, o_ref, *, k_chunk):
    d = xt_ref.shape[0]
    tn = o_ref.shape[-1]

    def body(k, acc):
        a0, a1 = acc
        ks = pl.multiple_of(k * k_chunk, k_chunk)
        xt = xt_ref[pl.ds(ks, k_chunk), :]
        s = xt * jax.nn.sigmoid(xt)
        w = w_ref[0, pl.ds(ks, k_chunk), :]
        a0 = a0 + jnp.sum(w * s[:, 0:1], axis=0, keepdims=True)
        a1 = a1 + jnp.sum(w * s[:, 1:2], axis=0, keepdims=True)
        return a0, a1

    z = jnp.zeros((1, tn), F32)
    a0, a1 = lax.fori_loop(0, d // k_chunk, body, (z, z))
    o_ref[0, 0:1, :] = a0 + b_ref[0]
    o_ref[0, 1:2, :] = a1 + b_ref[0]


def _ada_mod(c, c_ctx, w_ada, b_ada):
    depth, d, n = w_ada.shape
    xt = jnp.stack([c[0], c_ctx], axis=1)
    tn = _tile(n, 1024, 512, LANES)
    k_chunk = _tile(d, 256, SUBLANES)
    return pl.pallas_call(
        functools.partial(_ada_kernel, k_chunk=k_chunk),
        grid=(depth, n // tn),
        in_specs=[pl.BlockSpec((d, 2), lambda l, j: (0, 0)),
                  pl.BlockSpec((1, d, tn), lambda l, j: (l, 0, j)),
                  pl.BlockSpec((1, 1, tn), lambda l, j: (l, 0, j))],
        out_specs=pl.BlockSpec((1, 2, tn), lambda l, j: (l, 0, j)),
        out_shape=jax.ShapeDtypeStruct((depth, 2, n), F32),
        compiler_params=_cparams(2),
        name="ada_mod",
    )(xt, w_ada, b_ada.reshape(depth, 1, n))


def _norm_mod(x, g_ref, sh_ref, sc_ref, is_ctx):
    y = x * lax.rsqrt(jnp.mean(x * x, axis=-1, keepdims=True) + NORM_EPS) * g_ref[...]
    sh = jnp.where(is_ctx, sh_ref[1:2, :], sh_ref[0:1, :])
    sc = jnp.where(is_ctx, sc_ref[1:2, :], sc_ref[0:1, :])
    return y * (1.0 + sc) + sh


def _norm_kernel(x_ref, g_ref, sh_ref, sc_ref, o_ref, *, n_lat):
    tm = x_ref.shape[0]
    is_ctx = lax.broadcasted_iota(jnp.int32, (tm, 1), 0) + pl.program_id(0) * tm >= n_lat
    o_ref[...] = _norm_mod(x_ref[...], g_ref, sh_ref, sc_ref, is_ctx).astype(o_ref.dtype)


def _norm_router_kernel(x_ref, g_ref, sh_ref, sc_ref, wr_ref, h_ref, idx_ref, gate_ref):
    h = _norm_mod(x_ref[...], g_ref, sh_ref, sc_ref, False)
    h_ref[...] = h
    logits = jnp.dot(h, wr_ref[...], preferred_element_type=F32, precision=lax.Precision.HIGHEST)
    lane = lax.broadcasted_iota(jnp.int32, logits.shape, 1).astype(F32)
    neg = jnp.float32(-jnp.inf)
    l1 = jnp.where(lane < N_EXPERTS, logits, neg)
    v1 = jnp.max(l1, axis=-1, keepdims=True)
    i1 = jnp.min(jnp.where(l1 == v1, lane, float(LANES)), axis=-1, keepdims=True)
    l2 = jnp.where(lane == i1, neg, l1)
    v2 = jnp.max(l2, axis=-1, keepdims=True)
    i2 = jnp.min(jnp.where(l2 == v2, lane, float(LANES)), axis=-1, keepdims=True)
    e = jnp.exp(v2 - v1)
    g1 = 1.0 / (1.0 + e)
    g2 = e / (1.0 + e)
    idx_ref[...] = jnp.where(lane == 0, i1, jnp.where(lane == 1, i2, 0.0)).astype(jnp.int32)
    gate_ref[...] = jnp.where(lane == 0, g1, jnp.where(lane == 1, g2, 0.0))


def _norm_modulate(x, g, mods, sh_blk, sc_blk, n_lat, out_dtype):
    r, d = x.shape
    tm = _tile(r, 768, 512, 256, 128, 64, SUBLANES)
    return pl.pallas_call(
        functools.partial(_norm_kernel, n_lat=n_lat),
        grid=(r // tm,),
        in_specs=[pl.BlockSpec((tm, d), lambda i: (i, 0)),
                  pl.BlockSpec((1, d), lambda i: (0, 0)),
                  pl.BlockSpec((2, d), lambda i: (0, sh_blk)),
                  pl.BlockSpec((2, d), lambda i: (0, sc_blk))],
        out_specs=pl.BlockSpec((tm, d), lambda i: (i, 0)),
        out_shape=jax.ShapeDtypeStruct((r, d), out_dtype),
        compiler_params=_cparams(1),
        name="norm_modulate",
    )(x, g.reshape(1, d), mods, mods)


def _norm_modulate_route(x, g, mods, sh_blk, sc_blk, w_router):
    r, d = x.shape
    tm = _tile(r, 512, 256, 128, 64, SUBLANES)
    wr = jnp.pad(w_router, ((0, 0), (0, LANES - N_EXPERTS)))
    return pl.pallas_call(
        _norm_router_kernel,
        grid=(r // tm,),
        in_specs=[pl.BlockSpec((tm, d), lambda i: (i, 0)),
                  pl.BlockSpec((1, d), lambda i: (0, 0)),
                  pl.BlockSpec((2, d), lambda i: (0, sh_blk)),
                  pl.BlockSpec((2, d), lambda i: (0, sc_blk)),
                  pl.BlockSpec((d, LANES), lambda i: (0, 0))],
        out_specs=[pl.BlockSpec((tm, d), lambda i: (i, 0)),
                   pl.BlockSpec((tm, LANES), lambda i: (i, 0)),
                   pl.BlockSpec((tm, LANES), lambda i: (i, 0))],
        out_shape=[jax.ShapeDtypeStruct((r, d), F32),
                   jax.ShapeDtypeStruct((r, LANES), jnp.int32),
                   jax.ShapeDtypeStruct((r, LANES), F32)],
        compiler_params=_cparams(1),
        name="norm_modulate_route",
    )(x, g.reshape(1, d), mods, mods, wr)


def _in_a_kernel(h_ref, wt_ref, g_ref, cos_ref, sin_ref, qa_ref, ckv_ref, kr_ref):
    acc = _dot_t(h_ref[...], wt_ref[...])

    def rms(v, g):
        return v * lax.rsqrt(jnp.mean(v * v, axis=-1, keepdims=True) + NORM_EPS) * g

    q0, q1 = 0, Q_LORA_RANK
    c1 = q1 + KV_LORA_RANK
    qa_ref[...] = rms(acc[:, q0:q1], g_ref[:, q0:q1]).astype(qa_ref.dtype)
    ckv_ref[...] = rms(acc[:, q1:c1], g_ref[:, q1:c1]).astype(ckv_ref.dtype)
    kr = acc[:, c1:c1 + LANES] * cos_ref[...] + acc[:, c1 + LANES:c1 + 2 * LANES] * sin_ref[...]
    kr_ref[...] = kr.astype(kr_ref.dtype)


def _in_proj_a(h, w_at, g_a, cos_t, sin_t):
    m, d = h.shape
    n = w_at.shape[0]
    tm = _tile(m, 768, 640, 512, 256, 128)
    return pl.pallas_call(
        _in_a_kernel,
        grid=(m // tm,),
        in_specs=[pl.BlockSpec((tm, d), lambda i: (i, 0)),
                  pl.BlockSpec((n, d), lambda i: (0, 0)),
                  pl.BlockSpec((1, Q_LORA_RANK + KV_LORA_RANK), lambda i: (0, 0)),
                  pl.BlockSpec((tm, LANES), lambda i: (i, 0)),
                  pl.BlockSpec((tm, LANES), lambda i: (i, 0))],
        out_specs=[pl.BlockSpec((tm, Q_LORA_RANK), lambda i: (i, 0)),
                   pl.BlockSpec((tm, KV_LORA_RANK), lambda i: (i, 0)),
                   pl.BlockSpec((tm, LANES), lambda i: (i, 0))],
        out_shape=[jax.ShapeDtypeStruct((m, Q_LORA_RANK), BF16),
                   jax.ShapeDtypeStruct((m, KV_LORA_RANK), BF16),
                   jax.ShapeDtypeStruct((m, LANES), BF16)],
        compiler_params=_cparams(1),
        name="in_proj_a",
    )(h, w_at, g_a, cos_t, sin_t)


def _in_b_kernel(h_ref, wt_ref, o_ref, *, first_gate_tile):
    acc = _dot_t(h_ref[...], wt_ref[...])
    is_gate = pl.program_id(0) >= first_gate_tile

    @pl.when(is_gate)
    def _():
        o_ref[...] = jax.nn.sigmoid(acc).astype(o_ref.dtype)

    @pl.when(jnp.logical_not(is_gate))
    def _():
        o_ref[...] = acc.astype(o_ref.dtype)


def _in_proj_b(h, w_bt, first_gate_col):
    m, d = h.shape
    n = w_bt.shape[0]
    tm = _tile(m, 1408, 768, 640, 512, 256, 128)
    tn = _tile(first_gate_col, 1024, 512, 256, LANES)
    return pl.pallas_call(
        functools.partial(_in_b_kernel, first_gate_tile=first_gate_col // tn),
        grid=(n // tn, m // tm),
        in_specs=[pl.BlockSpec((tm, d), lambda j, i: (i, 0)),
                  pl.BlockSpec((tn, d), lambda j, i: (j, 0))],
        out_specs=pl.BlockSpec((tm, tn), lambda j, i: (i, j)),
        out_shape=jax.ShapeDtypeStruct((m, n), BF16),
        compiler_params=_cparams(2),
        name="in_proj_b",
    )(h, w_bt)


def _q_kernel(a_ref, wm_ref, ws_ref, cos_ref, sin_ref, q_ref):
    a = a_ref[...]
    cos = cos_ref[...]
    sin = sin_ref[...]
    for hd in range(N_HEADS):
        main = _dot(a, wm_ref[:, hd * HEAD_W:(hd + 1) * HEAD_W])
        swap = _dot(a, ws_ref[:, hd * LANES:(hd + 1) * LANES])
        q_ref[:, hd * HEAD_W:hd * HEAD_W + LANES] = (main[:, :LANES] * Q_SCALE).astype(q_ref.dtype)
        rot = (main[:, LANES:] * cos + swap * sin) * Q_SCALE
        q_ref[:, hd * HEAD_W + LANES:(hd + 1) * HEAD_W] = rot.astype(q_ref.dtype)


def _q_proj(qa_n, w_main, w_swap, cos_t, sin_t, rows):
    k = qa_n.shape[1]
    tm = _tile(rows, 768, 640, 512, 256, 128)
    return pl.pallas_call(
        _q_kernel,
        grid=(rows // tm,),
        in_specs=[pl.BlockSpec((tm, k), lambda i: (i, 0)),
                  pl.BlockSpec(w_main.shape, lambda i: (0, 0)),
                  pl.BlockSpec(w_swap.shape, lambda i: (0, 0)),
                  pl.BlockSpec((tm, LANES), lambda i: (i, 0)),
                  pl.BlockSpec((tm, LANES), lambda i: (i, 0))],
        out_specs=pl.BlockSpec((tm, N_HEADS * HEAD_W), lambda i: (i, 0)),
        out_shape=jax.ShapeDtypeStruct((rows, N_HEADS * HEAD_W), BF16),
        compiler_params=_cparams(1),
        name="q_proj",
    )(qa_n, w_main, w_swap, cos_t, sin_t)


def _kv_kernel(c_ref, kr_ref, wk_ref, wv_ref, k_ref, v_ref):
    c = c_ref[...]
    kr = kr_ref[...]
    lane = lax.broadcasted_iota(jnp.int32, (c.shape[0], LANES), 1)
    ones_col = jnp.where(lane == 0, 1.0, 0.0).astype(v_ref.dtype)
    for hd in range(N_HEADS):
        kn = _dot(c, wk_ref[:, hd * LANES:(hd + 1) * LANES])
        k_ref[:, hd * HEAD_W:hd * HEAD_W + LANES] = kn.astype(k_ref.dtype)
        k_ref[:, hd * HEAD_W + LANES:(hd + 1) * HEAD_W] = kr
        vh = _dot(c, wv_ref[:, hd * LANES:(hd + 1) * LANES])
        v_ref[:, hd * HEAD_W:hd * HEAD_W + LANES] = vh.astype(v_ref.dtype)
        v_ref[:, hd * HEAD_W + LANES:(hd + 1) * HEAD_W] = ones_col


def _kv_proj(ckv_n, kr, w_k, w_v):
    m, k = ckv_n.shape
    tm = _tile(m, 768, 640, 512, 256, 128)
    return pl.pallas_call(
        _kv_kernel,
        grid=(m // tm,),
        in_specs=[pl.BlockSpec((tm, k), lambda i: (i, 0)),
                  pl.BlockSpec((tm, LANES), lambda i: (i, 0)),
                  pl.BlockSpec(w_k.shape, lambda i: (0, 0)),
                  pl.BlockSpec(w_v.shape, lambda i: (0, 0))],
        out_specs=[pl.BlockSpec((tm, N_HEADS * HEAD_W), lambda i: (i, 0)),
                   pl.BlockSpec((tm, N_HEADS * HEAD_W), lambda i: (i, 0))],
        out_shape=[jax.ShapeDtypeStruct((m, N_HEADS * HEAD_W), BF16),
                   jax.ShapeDtypeStruct((m, N_HEADS * HEAD_W), BF16)],
        compiler_params=_cparams(1),
        name="kv_proj",
    )(ckv_n, kr, w_k, w_v)


def _attn_kernel(q_ref, k_ref, v_ref, o_ref, sa_ref, sb_ref, *, tk, n_chunks):
    q = q_ref[...]
    tq = q.shape[0]

    def scores(c):
        return _dot_t(q, k_ref[c * tk:(c + 1) * tk, :])

    def absorb(s_ref, c, carry):
        m, acc = carry
        s = s_ref[...]
        m_new = jnp.maximum(m, jnp.max(s, axis=-1, keepdims=True))
        p = jnp.exp2((s - m_new).astype(v_ref.dtype))
        acc = jnp.exp2(m - m_new) * acc + _dot(p, v_ref[c * tk:(c + 1) * tk, :])
        return m_new, acc

    s_refs = (sa_ref, sb_ref)
    carry = (jnp.full((tq, 1), -jnp.inf, F32), jnp.zeros((tq, HEAD_W), F32))
    sa_ref[...] = scores(0)
    for c in range(n_chunks):
        if c + 1 < n_chunks:
            s_refs[(c + 1) % 2][...] = scores(c + 1)
        carry = absorb(s_refs[c % 2], c, carry)
    _, acc = carry
    o_ref[...] = (acc[:, :V_HEAD_DIM] / acc[:, V_HEAD_DIM:V_HEAD_DIM + 1]).astype(o_ref.dtype)


def _attention(q, k, v, q_row0, n_q, k_row0, n_k):
    tq = _tile(n_q, 512, 256, 128)
    tk = _tile(n_k, 768, 640, 512, 384, 256, 128)
    assert q_row0 % tq == 0 and k_row0 % n_k == 0
    q_blk0 = q_row0 // tq
    k_blk = k_row0 // n_k
    return pl.pallas_call(
        functools.partial(_attn_kernel, tk=tk, n_chunks=n_k // tk),
        grid=(N_HEADS, n_q // tq),
        in_specs=[pl.BlockSpec((tq, HEAD_W), lambda h, i: (q_blk0 + i, h)),
                  pl.BlockSpec((n_k, HEAD_W), lambda h, i: (k_blk, h)),
                  pl.BlockSpec((n_k, HEAD_W), lambda h, i: (k_blk, h))],
        out_specs=pl.BlockSpec((tq, V_HEAD_DIM), lambda h, i: (i, h)),
        out_shape=jax.ShapeDtypeStruct((n_q, N_HEADS * V_HEAD_DIM), BF16),
        scratch_shapes=[pltpu.VMEM((tq, tk), F32), pltpu.VMEM((tq, tk), F32)],
        compiler_params=_cparams(2),
        name="attention",
    )(q, k, v)


def _conv_gate_kernel(cx_ref, cb_ref, cc_ref, cxp_ref, ccp_ref, cxn_ref, ccn_ref, wc_ref, z_ref, *,
                      seg_starts, seg_ends, col_chunk):
    tm, width = z_ref.shape
    loc = lax.broadcasted_iota(jnp.int32, (tm, 1), 0)
    row = loc + pl.program_id(0) * tm
    first = functools.reduce(jnp.logical_or, [row == r for r in seg_starts])
    last = functools.reduce(jnp.logical_or, [row == r for r in seg_ends])
    for c0 in range(0, width, col_chunk):
        cs = slice(c0, c0 + col_chunk)
        u = cx_ref[:, cs].astype(F32) * cc_ref[:, cs].astype(F32)
        u_halo_prev = (cxp_ref[SUBLANES - 1:SUBLANES, cs].astype(F32)
                       * ccp_ref[SUBLANES - 1:SUBLANES, cs].astype(F32))
        u_halo_next = cxn_ref[0:1, cs].astype(F32) * ccn_ref[0:1, cs].astype(F32)
        u_prev = jnp.where(loc == 0, u_halo_prev, pltpu.roll(u, 1, 0))
        u_prev = jnp.where(first, 0.0, u_prev)
        u_next = jnp.where(loc == tm - 1, u_halo_next, pltpu.roll(u, tm - 1, 0))
        u_next = jnp.where(last, 0.0, u_next)
        conv = wc_ref[0:1, cs] * u_prev + wc_ref[1:2, cs] * u + wc_ref[2:3, cs] * u_next
        z_ref[:, cs] = (cb_ref[:, cs].astype(F32) * conv).astype(z_ref.dtype)


def _conv_gate(p, w_conv, rows, s_len, m_len):
    width = w_conv.shape[1]
    tm = _tile(rows, 768, 512, 384, 256, 128, 64)
    hb = tm // SUBLANES
    last_hb = p.shape[0] // SUBLANES - 1

    def prev_map(col):
        return lambda i: (jnp.maximum(i * hb - 1, 0), col)

    def next_map(col):
        return lambda i: (jnp.minimum((i + 1) * hb, last_hb), col)

    kern = functools.partial(_conv_gate_kernel, seg_starts=(0, s_len), seg_ends=(s_len - 1, m_len - 1),
                             col_chunk=_tile(width, 512, LANES))
    return pl.pallas_call(
        kern,
        grid=(rows // tm,),
        in_specs=[pl.BlockSpec((tm, width), lambda i: (i, 0)),
                  pl.BlockSpec((tm, width), lambda i: (i, 1)),
                  pl.BlockSpec((tm, width), lambda i: (i, 2)),
                  pl.BlockSpec((SUBLANES, width), prev_map(0)),
                  pl.BlockSpec((SUBLANES, width), prev_map(2)),
                  pl.BlockSpec((SUBLANES, width), next_map(0)),
                  pl.BlockSpec((SUBLANES, width), next_map(2)),
                  pl.BlockSpec((CONV_K, width), lambda i: (0, 0))],
        out_specs=pl.BlockSpec((tm, width), lambda i: (i, 0)),
        out_shape=jax.ShapeDtypeStruct((rows, width), BF16),
        compiler_params=_cparams(1),
        name="conv_gate",
    )(p, p, p, p, p, p, p, w_conv)


def _merge_kernel(attn_ref, z_ref, sga_ref, sgb_ref, woa_ref, wob_ref, o_ref):
    o_a = _dot(attn_ref[...], woa_ref[...])
    o_b = _dot(z_ref[...], wob_ref[...])
    o_ref[...] = (sga_ref[...].astype(F32) * o_a + sgb_ref[...].astype(F32) * o_b).astype(o_ref.dtype)


def _merge(attn, z, p, w_oa, w_ob, rows):
    attn_w = w_oa.shape[0]
    width, d = w_ob.shape
    tm = _tile(rows, 768, 512, 384, 256, 128, 64)
    tn = _tile(d, 1024, 512, 256, LANES)
    ga_blk = 3 * width // tn
    gb_blk = (3 * width + d) // tn
    return pl.pallas_call(
        _merge_kernel,
        grid=(d // tn, rows // tm),
        in_specs=[pl.BlockSpec((tm, attn_w), lambda j, i: (i, 0)),
                  pl.BlockSpec((tm, width), lambda j, i: (i, 0)),
                  pl.BlockSpec((tm, tn), lambda j, i: (i, ga_blk + j)),
                  pl.BlockSpec((tm, tn), lambda j, i: (i, gb_blk + j)),
                  pl.BlockSpec((attn_w, tn), lambda j, i: (0, j)),
                  pl.BlockSpec((width, tn), lambda j, i: (0, j))],
        out_specs=pl.BlockSpec((tm, tn), lambda j, i: (i, j)),
        out_shape=jax.ShapeDtypeStruct((rows, d), BF16),
        compiler_params=_cparams(2),
        name="merge",
    )(attn, z, p, p, w_oa, w_ob)


def _mm_res_kernel(a_ref, w_ref, x_ref, gt_ref, o_ref, *, n_lat):
    tm = a_ref.shape[0]
    row = lax.broadcasted_iota(jnp.int32, (tm, 1), 0) + pl.program_id(1) * tm
    gate = jnp.where(row >= n_lat, gt_ref[1:2, :], gt_ref[0:1, :])
    o_ref[...] = x_ref[...] + gate * _dot(a_ref[...], _bf16(w_ref[...]))


def _matmul_residual(a, w, x, mods, gate_blk, rows, n_lat):
    k, n = w.shape
    tm = _tile(rows, 768, 640, 512, 256, 128)
    tn = _tile(n, 512, 256, LANES)
    nt = n // tn
    return pl.pallas_call(
        functools.partial(_mm_res_kernel, n_lat=n_lat),
        grid=(nt, rows // tm),
        in_specs=[pl.BlockSpec((tm, k), lambda j, i: (i, 0)),
                  pl.BlockSpec((k, tn), lambda j, i: (0, j)),
                  pl.BlockSpec((tm, tn), lambda j, i: (i, j)),
                  pl.BlockSpec((2, tn), lambda j, i: (0, gate_blk * nt + j))],
        out_specs=pl.BlockSpec((tm, tn), lambda j, i: (i, j)),
        out_shape=jax.ShapeDtypeStruct((rows, n), F32),
        compiler_params=_cparams(2),
        name="matmul_residual",
    )(a, w, x, mods)


def _block_in_use(eid_ref):
    return pl.program_id(1) < eid_ref[pl.num_programs(1)]


def _glu_kernel(eid_ref, a_ref, wg_ref, wu_ref, o_ref):
    @pl.when(_block_in_use(eid_ref))
    def _():
        a = _bf16(a_ref[...])
        gate = _dot(a, _bf16(wg_ref[0]))
        up = _dot(a, _bf16(wu_ref[0]))
        o_ref[...] = (gate * jax.nn.sigmoid(gate) * up).astype(o_ref.dtype)

    @pl.when(jnp.logical_not(_block_in_use(eid_ref)))
    def _():
        o_ref[...] = jnp.zeros_like(o_ref)


def _glu(a, w_gate, w_up, eid, tm):
    r, k = a.shape
    f = w_gate.shape[2]
    tn = _tile(f, 512, 1408, 256, LANES)
    grid_spec = pltpu.PrefetchScalarGridSpec(
        num_scalar_prefetch=1,
        grid=(f // tn, r // tm),
        in_specs=[pl.BlockSpec((tm, k), lambda j, i, e: (i, 0)),
                  pl.BlockSpec((1, k, tn), lambda j, i, e: (e[i], 0, j)),
                  pl.BlockSpec((1, k, tn), lambda j, i, e: (e[i], 0, j))],
        out_specs=pl.BlockSpec((tm, tn), lambda j, i, e: (i, j)),
    )
    return pl.pallas_call(
        _glu_kernel,
        grid_spec=grid_spec,
        out_shape=jax.ShapeDtypeStruct((r, f), BF16),
        compiler_params=_cparams(2),
        name="glu",
    )(eid, a, w_gate, w_up)


def _down_kernel(eid_ref, a_ref, w_ref, o_ref):
    @pl.when(_block_in_use(eid_ref))
    def _():
        o_ref[...] = _dot(a_ref[...], w_ref[0])

    @pl.when(jnp.logical_not(_block_in_use(eid_ref)))
    def _():
        o_ref[...] = jnp.zeros_like(o_ref)


def _down_grouped(a, w_down, eid, tm):
    r, f = a.shape
    d = w_down.shape[2]
    grid_spec = pltpu.PrefetchScalarGridSpec(
        num_scalar_prefetch=1,
        grid=(1, r // tm),
        in_specs=[pl.BlockSpec((tm, f), lambda j, i, e: (i, 0)),
                  pl.BlockSpec((1, f, d), lambda j, i, e: (e[i], 0, 0))],
        out_specs=pl.BlockSpec((tm, d), lambda j, i, e: (i, 0)),
    )
    return pl.pallas_call(
        _down_kernel,
        grid_spec=grid_spec,
        out_shape=jax.ShapeDtypeStruct((r, d), F32),
        compiler_params=_cparams(2),
        name="down_grouped",
    )(eid, a, w_down)


def _row_copies(idx_ref, idx_base, idx_stride, src_ref, dst_ref, sem, n_rows, start):
    def body(r2, c):
        for prio in range(2):
            r = 2 * r2 + prio
            row = idx_ref[idx_base + r * idx_stride]
            cp = pltpu.make_async_copy(src_ref.at[pl.ds(row, 1)], dst_ref.at[pl.ds(r, 1)], sem)
            if start:
                cp.start(priority=prio)
            else:
                cp.wait()
        return c

    assert n_rows % 2 == 0
    lax.fori_loop(0, n_rows // 2, body, 0, unroll=DMA_ISSUE_UNROLL // 2)


def _prefetched_gather(copies):
    i = pl.program_id(0)
    slot = i % 2

    @pl.when(i == 0)
    def _():
        copies(0, 0, True)

    @pl.when(i + 1 < pl.num_programs(0))
    def _():
        copies(i + 1, 1 - slot, True)

    copies(i, slot, False)
    return slot


def _gather_rows_kernel(idx_ref, src_ref, o_ref, g_ref, sem):
    tm = o_ref.shape[0]

    def copies(step, slot, start):
        _row_copies(idx_ref, step * tm, 1, src_ref, g_ref.at[slot], sem.at[slot], tm, start)

    slot = _prefetched_gather(copies)
    o_ref[...] = g_ref[slot]


def _gather_rows(src, idx, tm):
    n = idx.shape[0]
    w = src.shape[1]
    grid_spec = pltpu.PrefetchScalarGridSpec(
        num_scalar_prefetch=1,
        grid=(n // tm,),
        in_specs=[pl.BlockSpec(memory_space=pl.ANY)],
        out_specs=pl.BlockSpec((tm, w), lambda i, idx_ref: (i, 0)),
        scratch_shapes=[pltpu.VMEM((2, tm, w), src.dtype), pltpu.SemaphoreType.DMA((2,))],
    )
    return pl.pallas_call(
        _gather_rows_kernel,
        grid_spec=grid_spec,
        out_shape=jax.ShapeDtypeStruct((n, w), src.dtype),
        compiler_params=_cparams(1),
        name="gather_rows",
    )(idx, src)


def _final_kernel(dest_ref, x_ref, y_hbm_ref, gate_ref, gt_ref, g_ref, o_ref, y_ref, sem):
    tm = x_ref.shape[0]

    def copies(step, slot, start):
        for kk in range(TOP_K):
            _row_copies(dest_ref, step * tm * TOP_K + kk, TOP_K, y_hbm_ref, y_ref.at[slot, kk], sem.at[slot],
                        tm, start)

    slot = _prefetched_gather(copies)
    gates = gate_ref[...]
    y = gates[:, 0:1] * y_ref[slot, 0] + gates[:, 1:2] * y_ref[slot, 1]
    x = x_ref[...] + gt_ref[0:1, :] * y
    o_ref[...] = x * lax.rsqrt(jnp.mean(x * x, axis=-1, keepdims=True) + NORM_EPS) * g_ref[...]


def _combine_final(x, ybuf, dest, gates, mods, gate_blk, g_final):
    r, d = x.shape
    tm = _tile(r, 256, 128, 64, SUBLANES)
    grid_spec = pltpu.PrefetchScalarGridSpec(
        num_scalar_prefetch=1,
        grid=(r // tm,),
        in_specs=[pl.BlockSpec((tm, d), lambda i, dest_ref: (i, 0)),
                  pl.BlockSpec(memory_space=pl.ANY),
                  pl.BlockSpec((tm, LANES), lambda i, dest_ref: (i, 0)),
                  pl.BlockSpec((2, d), lambda i, dest_ref: (0, gate_blk)),
                  pl.BlockSpec((1, d), lambda i, dest_ref: (0, 0))],
        out_specs=pl.BlockSpec((tm, d), lambda i, dest_ref: (i, 0)),
        scratch_shapes=[pltpu.VMEM((2, TOP_K, tm, d), F32), pltpu.SemaphoreType.DMA((2,))],
    )
    return pl.pallas_call(
        _final_kernel,
        grid_spec=grid_spec,
        out_shape=jax.ShapeDtypeStruct((r, d), F32),
        compiler_params=_cparams(1),
        name="combine_final",
    )(dest, x, ybuf, gates, mods, g_final.reshape(1, d))


def _rope_tables(s_len, c_len):
    quarter = QK_ROPE_DIM // 4
    tok = jnp.arange(s_len + c_len, dtype=jnp.int32)[:, None]
    lane = jnp.arange(LANES, dtype=jnp.int32)[None, :]
    group = lane // quarter
    pos = jnp.where(group < 2, tok // GRID_W, tok % GRID_W).astype(F32)
    inv_freq = jnp.power(ROPE_THETA, -(2 * (lane % quarter)).astype(F32) / (QK_ROPE_DIM // 2))
    ang = jnp.where(tok < s_len, pos * inv_freq, 0.0)
    live = group < 4
    cos = jnp.where(live, jnp.cos(ang), 0.0)
    sin = jnp.where(live, jnp.where(group % 2 == 0, -jnp.sin(ang), jnp.sin(ang)), 0.0)
    return cos, sin


def _rope_swap_perm():
    q = QK_ROPE_DIM // 4
    return jnp.concatenate([jnp.arange(q, 2 * q), jnp.arange(0, q), jnp.arange(3 * q, 4 * q), jnp.arange(2 * q, 3 * q)])


def _layer_weights(w_in_t, w_qb_all, w_kvb_all, li):
    d = w_in_t.shape[2]
    perm = _rope_swap_perm()
    kr_lo = Q_LORA_RANK + KV_LORA_RANK
    kr_hi = kr_lo + QK_ROPE_DIM
    w_kr = w_in_t[li, kr_lo:kr_hi]
    zpad = jnp.zeros((LANES - QK_ROPE_DIM, d), w_in_t.dtype)
    w_a = jnp.concatenate([w_in_t[li, :kr_lo], w_kr, zpad, w_kr[perm], zpad], axis=0)
    w_b = w_in_t[li, kr_hi:]
    w_qb, w_kvb = w_qb_all[li], w_kvb_all[li]
    qb = w_qb.reshape(Q_LORA_RANK, N_HEADS, QK_NOPE_DIM + QK_ROPE_DIM)
    q_rope = qb[:, :, QK_NOPE_DIM:]
    hpad = jnp.zeros((Q_LORA_RANK, N_HEADS, LANES - QK_ROPE_DIM), w_qb.dtype)
    w_q_main = jnp.concatenate([qb, hpad], axis=2).reshape(Q_LORA_RANK, N_HEADS * HEAD_W)
    w_q_swap = jnp.concatenate([q_rope[:, :, perm], hpad], axis=2).reshape(Q_LORA_RANK, N_HEADS * LANES)
    kvb = w_kvb.reshape(KV_LORA_RANK, N_HEADS, QK_NOPE_DIM + V_HEAD_DIM)
    w_k = kvb[:, :, :QK_NOPE_DIM].reshape(KV_LORA_RANK, N_HEADS * QK_NOPE_DIM)
    w_v = kvb[:, :, QK_NOPE_DIM:].reshape(KV_LORA_RANK, N_HEADS * V_HEAD_DIM)
    return tuple(w.astype(BF16) for w in (w_a, w_b, w_q_main, w_q_swap, w_k, w_v))


def _moe_slots(idx, n_blocks):
    e_flat = idx.reshape(-1)
    onehot = (e_flat[:, None] == jnp.arange(N_EXPERTS)[None, :]).astype(jnp.int32)
    csum = jnp.cumsum(onehot, axis=0)
    counts = csum[-1]
    rank = jnp.sum((csum - onehot) * onehot, axis=1)
    padded = (counts + MOE_ROWS - 1) // MOE_ROWS * MOE_ROWS
    p_end = jnp.cumsum(padded)
    p_start = p_end - padded
    dest = jnp.sum(onehot * p_start[None, :], axis=1) + rank
    blk_lo = jnp.arange(n_blocks) * MOE_ROWS
    block_expert = jnp.minimum(jnp.sum((blk_lo[:, None] >= p_end[None, :]).astype(jnp.int32), axis=1),
                               N_EXPERTS - 1)
    blocks_in_use = p_end[-1:] // MOE_ROWS
    return dest.astype(jnp.int32), jnp.concatenate([block_expert, blocks_in_use]).astype(jnp.int32)


def kernel(x, c, ctx, c_ctx, w_ada, b_ada, g_attn, w_in, g_qa, w_qb, g_kva, w_kvb, w_conv, w_oa, w_ob, w_o,
           g_ffn, w_gate_dense, w_up_dense, w_down_dense, w_router, w_gate_exp, w_up_exp, w_down_exp, g_final):
    _, s_len, d = x.shape
    c_len = ctx.shape[1]
    m_len = s_len + c_len
    depth = w_in.shape[0]
    width = w_conv.shape[2]
    assert depth == 2, "supported stack: dense-FFN layer with context updates, then a final expert-FFN layer"

    mods_all = _ada_mod(c, c_ctx, w_ada, b_ada)
    cos_t, sin_t = _rope_tables(s_len, c_len)
    w_in_t = jnp.swapaxes(w_in, 1, 2)
    xa = jnp.concatenate([x[0], ctx[0]], axis=0)

    for li in range(depth):
        last = li == depth - 1
        mods = mods_all[li]
        w_a, w_b, w_q_main, w_q_swap, w_k, w_v = _layer_weights(w_in_t, w_qb, w_kvb, li)
        g_a = jnp.concatenate([g_qa[li], g_kva[li]]).reshape(1, -1)
        rows = s_len if last else m_len

        h = _norm_modulate(xa, g_attn[li], mods, 0, 1, s_len, BF16)
        qa_n, ckv_n, kr = _in_proj_a(h, w_a, g_a, cos_t, sin_t)
        p = _in_proj_b(h, w_b, 3 * width)
        q = _q_proj(qa_n, w_q_main, w_q_swap, cos_t, sin_t, rows)
        k, v = _kv_proj(ckv_n, kr, w_k, w_v)
        attn = _attention(q, k, v, 0, s_len, 0, m_len)
        if not last:
            attn = jnp.concatenate([attn, _attention(q, k, v, s_len, c_len, s_len, c_len)], axis=0)
        z = _conv_gate(p, w_conv[li], rows, s_len, m_len)
        merged = _merge(attn, z, p, w_oa[li].astype(BF16), w_ob[li].astype(BF16), rows)
        xa = _matmul_residual(merged, w_o[li], xa, mods, 2, rows, s_len)

        j = li // 2
        if li % 2 == 0:
            h2 = _norm_modulate(xa, g_ffn[li], mods, 3, 4, s_len, BF16)
            tm = _tile(rows, 768, 640, 512, 256, 128)
            eid = jnp.zeros((rows // tm + 1,), jnp.int32).at[-1].set(rows // tm)
            hid = _glu(h2, w_gate_dense[j][None], w_up_dense[j][None], eid, tm)
            xa = _matmul_residual(hid, w_down_dense[j].astype(BF16), xa, mods, 5, rows, s_len)
        else:
            h2, idx, gates = _norm_modulate_route(xa, g_ffn[li], mods, 3, 4, w_router[j])
            n_assign = rows * TOP_K
            n_blocks = -(-n_assign // MOE_ROWS) + N_EXPERTS
            dest, block_expert = _moe_slots(idx[:, :TOP_K], n_blocks)
            tok = jnp.arange(n_assign, dtype=jnp.int32) // TOP_K
            slot_tok = jnp.zeros((n_blocks * MOE_ROWS,), jnp.int32).at[dest].set(tok, unique_indices=True)
            buf = _gather_rows(h2, slot_tok, MOE_ROWS)
            hid = _glu(buf, w_gate_exp[j].astype(BF16), w_up_exp[j].astype(BF16), block_expert, MOE_ROWS)
            ybuf = _down_grouped(hid, w_down_exp[j].astype(BF16), block_expert, MOE_ROWS)
            return _combine_final(xa, ybuf, dest, gates, mods, 5, g_final)[None]
    raise AssertionError("unreachable: the final layer returns")
```

```python
import functools

import jax
import jax.numpy as jnp
from jax import lax
from jax.experimental import pallas as pl
from jax.experimental.pallas import tpu as pltpu

F32 = jnp.float32
BF16 = jnp.bfloat16

N_HEADS = 16
QK_NOPE_DIM = 128
QK_ROPE_DIM = 64
V_HEAD_DIM = 128
Q_LORA_RANK = 512
KV_LORA_RANK = 512
GRID_W = 64
ROPE_THETA = 10000.0
ATTN_SCALE = (QK_NOPE_DIM + QK_ROPE_DIM) ** -0.5
Q_SCALE = ATTN_SCALE * 1.4426950408889634
CONV_K = 3
N_EXPERTS = 8
TOP_K = 2
NORM_EPS = 1e-6

LANES = 128
SUBLANES = 8
HEAD_W = 2 * LANES
VMEM_LIMIT_BYTES = 56 * 1024 * 1024

MOE_ROWS = 256
DMA_ISSUE_UNROLL = 8


def _tile(n, *cands):
    for c in cands:
        if n % c == 0:
            return c
    return n


def _cparams(n_axes):
    return pltpu.CompilerParams(dimension_semantics=("arbitrary",) * n_axes,
                                vmem_limit_bytes=VMEM_LIMIT_BYTES)


def _dot(a, b):
    return jnp.dot(a, b, preferred_element_type=F32)


def _dot_t(a, bt):
    return lax.dot_general(a, bt, (((1,), (1,)), ((), ())), preferred_element_type=F32)


def _bf16(w):
    return w if w.dtype == BF16 else w.astype(BF16)


def _ada_kernel(xt_ref, w_ref, b_ref, o_ref, *, k_chunk):
    d = xt_ref.shape[0]
    tn = o_ref.shape[-1]

    def body(k, acc):
        a0, a1 = acc
        ks = pl.multiple_of(k * k_chunk, k_chunk)
        xt = xt_ref[pl.ds(ks, k_chunk), :]
        s = xt * jax.nn.sigmoid(xt)
        w = w_ref[0, pl.ds(ks, k_chunk), :]
        a0 = a0 + jnp.sum(w * s[:, 0:1], axis=0, keepdims=True)
        a1 = a1 + jnp.sum(w * s[:, 1:2], axis=0, keepdims=True)
        return a0, a1

    z = jnp.zeros((1, tn), F32)
    a0, a1 = lax.fori_loop(0, d // k_chunk, body, (z, z))
    o_ref[0, 0:1, :] = a0 + b_ref[0]
    o_ref[0, 1:2, :] = a1 + b_ref[0]


def _ada_mod(c, c_ctx, w_ada, b_ada):
    depth, d, n = w_ada.shape
    xt = jnp.stack([c[0], c_ctx], axis=1)
    tn = _tile(n, 1024, 512, LANES)
    k_chunk = _tile(d, 256, SUBLANES)
    return pl.pallas_call(
        functools.partial(_ada_kernel, k_chunk=k_chunk),
        grid=(depth, n // tn),
        in_specs=[pl.BlockSpec((d, 2), lambda l, j: (0, 0)),
                  pl.BlockSpec((1, d, tn), lambda l, j: (l, 0, j)),
                  pl.BlockSpec((1, 1, tn), lambda l, j: (l, 0, j))],
        out_specs=pl.BlockSpec((1, 2, tn), lambda l, j: (l, 0, j)),
        out_shape=jax.ShapeDtypeStruct((depth, 2, n), F32),
        compiler_params=_cparams(2),
        name="ada_mod",
    )(xt, w_ada, b_ada.reshape(depth, 1, n))


def _norm_mod(x, g_ref, sh_ref, sc_ref, is_ctx):
    y = x * lax.rsqrt(jnp.mean(x * x, axis=-1, keepdims=True) + NORM_EPS) * g_ref[...]
    sh = jnp.where(is_ctx, sh_ref[1:2, :], sh_ref[0:1, :])
    sc = jnp.where(is_ctx, sc_ref[1:2, :], sc_ref[0:1, :])
    return y * (1.0 + sc) + sh


def _norm_kernel(x_ref, g_ref, sh_ref, sc_ref, o_ref, *, n_lat_tiles):
    is_ctx = pl.program_id(0) >= n_lat_tiles
    o_ref[...] = _norm_mod(x_ref[...], g_ref, sh_ref, sc_ref, is_ctx).astype(o_ref.dtype)


def _norm_router_kernel(x_ref, g_ref, sh_ref, sc_ref, wr_ref, h_ref, idx_ref, gate_ref):
    h = _norm_mod(x_ref[...], g_ref, sh_ref, sc_ref, False)
    h_ref[...] = h
    logits = jnp.dot(h, wr_ref[...], preferred_element_type=F32, precision=lax.Precision.HIGHEST)
    lane = lax.broadcasted_iota(jnp.int32, logits.shape, 1).astype(F32)
    neg = jnp.float32(-jnp.inf)
    l1 = jnp.where(lane < N_EXPERTS, logits, neg)
    v1 = jnp.max(l1, axis=-1, keepdims=True)
    i1 = jnp.min(jnp.where(l1 == v1, lane, float(LANES)), axis=-1, keepdims=True)
    l2 = jnp.where(lane == i1, neg, l1)
    v2 = jnp.max(l2, axis=-1, keepdims=True)
    i2 = jnp.min(jnp.where(l2 == v2, lane, float(LANES)), axis=-1, keepdims=True)
    e = jnp.exp(v2 - v1)
    g1 = 1.0 / (1.0 + e)
    g2 = e / (1.0 + e)
    idx_ref[...] = jnp.where(lane == 0, i1, jnp.where(lane == 1, i2, 0.0)).astype(jnp.int32)
    gate_ref[...] = jnp.where(lane == 0, g1, jnp.where(lane == 1, g2, 0.0))


def _norm_modulate(x, g, mods, sh_blk, sc_blk, n_lat, out_dtype):
    r, d = x.shape
    tm = _tile(n_lat, 256, 128, 64, SUBLANES) if r > n_lat else _tile(r, 256, 128, 64, SUBLANES)
    if r > n_lat:
        tm = _tile(r - n_lat, tm, 128, 64, SUBLANES)
    return pl.pallas_call(
        functools.partial(_norm_kernel, n_lat_tiles=n_lat // tm),
        grid=(r // tm,),
        in_specs=[pl.BlockSpec((tm, d), lambda i: (i, 0)),
                  pl.BlockSpec((1, d), lambda i: (0, 0)),
                  pl.BlockSpec((2, d), lambda i: (0, sh_blk)),
                  pl.BlockSpec((2, d), lambda i: (0, sc_blk))],
        out_specs=pl.BlockSpec((tm, d), lambda i: (i, 0)),
        out_shape=jax.ShapeDtypeStruct((r, d), out_dtype),
        compiler_params=_cparams(1),
        name="norm_modulate",
    )(x, g.reshape(1, d), mods, mods)


def _norm_modulate_route(x, g, mods, sh_blk, sc_blk, w_router):
    r, d = x.shape
    tm = _tile(r, 512, 256, 128, 64, SUBLANES)
    wr = jnp.pad(w_router, ((0, 0), (0, LANES - N_EXPERTS)))
    return pl.pallas_call(
        _norm_router_kernel,
        grid=(r // tm,),
        in_specs=[pl.BlockSpec((tm, d), lambda i: (i, 0)),
                  pl.BlockSpec((1, d), lambda i: (0, 0)),
                  pl.BlockSpec((2, d), lambda i: (0, sh_blk)),
                  pl.BlockSpec((2, d), lambda i: (0, sc_blk)),
                  pl.BlockSpec((d, LANES), lambda i: (0, 0))],
        out_specs=[pl.BlockSpec((tm, d), lambda i: (i, 0)),
                   pl.BlockSpec((tm, LANES), lambda i: (i, 0)),
                   pl.BlockSpec((tm, LANES), lambda i: (i, 0))],
        out_shape=[jax.ShapeDtypeStruct((r, d), F32),
                   jax.ShapeDtypeStruct((r, LANES), jnp.int32),
                   jax.ShapeDtypeStruct((r, LANES), F32)],
        compiler_params=_cparams(1),
        name="norm_modulate_route",
    )(x, g.reshape(1, d), mods, mods, wr)


def _in_a_kernel(h_ref, wt_ref, g_ref, cos_ref, sin_ref, qa_ref, ckv_ref, kr_ref):
    acc = _dot_t(h_ref[...], wt_ref[...])

    def rms(v, g):
        return v * lax.rsqrt(jnp.mean(v * v, axis=-1, keepdims=True) + NORM_EPS) * g

    q0, q1 = 0, Q_LORA_RANK
    c1 = q1 + KV_LORA_RANK
    qa_ref[...] = rms(acc[:, q0:q1], g_ref[:, q0:q1]).astype(qa_ref.dtype)
    ckv_ref[...] = rms(acc[:, q1:c1], g_ref[:, q1:c1]).astype(ckv_ref.dtype)
    kr = acc[:, c1:c1 + LANES] * cos_ref[...] + acc[:, c1 + LANES:c1 + 2 * LANES] * sin_ref[...]
    kr_ref[...] = kr.astype(kr_ref.dtype)


def _in_proj_a(h, w_at, g_a, cos_t, sin_t):
    m, d = h.shape
    n = w_at.shape[0]
    tm = _tile(m, 768, 640, 512, 256, 128)
    return pl.pallas_call(
        _in_a_kernel,
        grid=(m // tm,),
        in_specs=[pl.BlockSpec((tm, d), lambda i: (i, 0)),
                  pl.BlockSpec((n, d), lambda i: (0, 0)),
                  pl.BlockSpec((1, Q_LORA_RANK + KV_LORA_RANK), lambda i: (0, 0)),
                  pl.BlockSpec((tm, LANES), lambda i: (i, 0)),
                  pl.BlockSpec((tm, LANES), lambda i: (i, 0))],
        out_specs=[pl.BlockSpec((tm, Q_LORA_RANK), lambda i: (i, 0)),
                   pl.BlockSpec((tm, KV_LORA_RANK), lambda i: (i, 0)),
                   pl.BlockSpec((tm, LANES), lambda i: (i, 0))],
        out_shape=[jax.ShapeDtypeStruct((m, Q_LORA_RANK), BF16),
                   jax.ShapeDtypeStruct((m, KV_LORA_RANK), BF16),
                   jax.ShapeDtypeStruct((m, LANES), BF16)],
        compiler_params=_cparams(1),
        name="in_proj_a",
    )(h, w_at, g_a, cos_t, sin_t)


def _in_b_kernel(h_ref, wt_ref, o_ref, *, first_gate_tile):
    acc = _dot_t(h_ref[...], wt_ref[...])
    is_gate = pl.program_id(0) >= first_gate_tile

    @pl.when(is_gate)
    def _():
        o_ref[...] = jax.nn.sigmoid(acc).astype(o_ref.dtype)

    @pl.when(jnp.logical_not(is_gate))
    def _():
        o_ref[...] = acc.astype(o_ref.dtype)


def _in_proj_b(h, w_bt, first_gate_col):
    m, d = h.shape
    n = w_bt.shape[0]
    tm = _tile(m, 1408, 768, 640, 512, 256, 128)
    tn = _tile(first_gate_col, 1024, 512, 256, LANES)
    return pl.pallas_call(
        functools.partial(_in_b_kernel, first_gate_tile=first_gate_col // tn),
        grid=(n // tn, m // tm),
        in_specs=[pl.BlockSpec((tm, d), lambda j, i: (i, 0)),
                  pl.BlockSpec((tn, d), lambda j, i: (j, 0))],
        out_specs=pl.BlockSpec((tm, tn), lambda j, i: (i, j)),
        out_shape=jax.ShapeDtypeStruct((m, n), BF16),
        compiler_params=_cparams(2),
        name="in_proj_b",
    )(h, w_bt)


def _q_kernel(a_ref, wm_ref, ws_ref, cos_ref, sin_ref, q_ref):
    a = a_ref[...]
    cos = cos_ref[...]
    sin = sin_ref[...]
    for hd in range(N_HEADS):
        main = _dot(a, wm_ref[:, hd * HEAD_W:(hd + 1) * HEAD_W])
        swap = _dot(a, ws_ref[:, hd * LANES:(hd + 1) * LANES])
        q_ref[:, hd * HEAD_W:hd * HEAD_W + LANES] = (main[:, :LANES] * Q_SCALE).astype(q_ref.dtype)
        rot = (main[:, LANES:] * cos + swap * sin) * Q_SCALE
        q_ref[:, hd * HEAD_W + LANES:(hd + 1) * HEAD_W] = rot.astype(q_ref.dtype)


def _q_proj(qa_n, w_main, w_swap, cos_t, sin_t, rows):
    k = qa_n.shape[1]
    tm = _tile(rows, 768, 640, 512, 256, 128)
    return pl.pallas_call(
        _q_kernel,
        grid=(rows // tm,),
        in_specs=[pl.BlockSpec((tm, k), lambda i: (i, 0)),
                  pl.BlockSpec(w_main.shape, lambda i: (0, 0)),
                  pl.BlockSpec(w_swap.shape, lambda i: (0, 0)),
                  pl.BlockSpec((tm, LANES), lambda i: (i, 0)),
                  pl.BlockSpec((tm, LANES), lambda i: (i, 0))],
        out_specs=pl.BlockSpec((tm, N_HEADS * HEAD_W), lambda i: (i, 0)),
        out_shape=jax.ShapeDtypeStruct((rows, N_HEADS * HEAD_W), BF16),
        compiler_params=_cparams(1),
        name="q_proj",
    )(qa_n, w_main, w_swap, cos_t, sin_t)


def _kv_kernel(c_ref, kr_ref, wk_ref, wv_ref, k_ref, v_ref):
    c = c_ref[...]
    kr = kr_ref[...]
    lane = lax.broadcasted_iota(jnp.int32, (c.shape[0], LANES), 1)
    ones_col = jnp.where(lane == 0, 1.0, 0.0).astype(v_ref.dtype)
    for hd in range(N_HEADS):
        kn = _dot(c, wk_ref[:, hd * LANES:(hd + 1) * LANES])
        k_ref[:, hd * HEAD_W:hd * HEAD_W + LANES] = kn.astype(k_ref.dtype)
        k_ref[:, hd * HEAD_W + LANES:(hd + 1) * HEAD_W] = kr
        vh = _dot(c, wv_ref[:, hd * LANES:(hd + 1) * LANES])
        v_ref[:, hd * HEAD_W:hd * HEAD_W + LANES] = vh.astype(v_ref.dtype)
        v_ref[:, hd * HEAD_W + LANES:(hd + 1) * HEAD_W] = ones_col


def _kv_proj(ckv_n, kr, w_k, w_v):
    m, k = ckv_n.shape
    tm = _tile(m, 768, 640, 512, 256, 128)
    return pl.pallas_call(
        _kv_kernel,
        grid=(m // tm,),
        in_specs=[pl.BlockSpec((tm, k), lambda i: (i, 0)),
                  pl.BlockSpec((tm, LANES), lambda i: (i, 0)),
                  pl.BlockSpec(w_k.shape, lambda i: (0, 0)),
                  pl.BlockSpec(w_v.shape, lambda i: (0, 0))],
        out_specs=[pl.BlockSpec((tm, N_HEADS * HEAD_W), lambda i: (i, 0)),
                   pl.BlockSpec((tm, N_HEADS * HEAD_W), lambda i: (i, 0))],
        out_shape=[jax.ShapeDtypeStruct((m, N_HEADS * HEAD_W), BF16),
                   jax.ShapeDtypeStruct((m, N_HEADS * HEAD_W), BF16)],
        compiler_params=_cparams(1),
        name="kv_proj",
    )(ckv_n, kr, w_k, w_v)


def _attn_kernel(q_ref, k_ref, v_ref, o_ref, sa_ref, sb_ref, *, tk, n_chunks):
    q = q_ref[...]
    tq = q.shape[0]

    def scores(c):
        return _dot_t(q, k_ref[c * tk:(c + 1) * tk, :])

    def absorb(s_ref, c, carry):
        m, acc = carry
        s = s_ref[...]
        m_new = jnp.maximum(m, jnp.max(s, axis=-1, keepdims=True))
        p = jnp.exp2(s - m_new).astype(v_ref.dtype)
        acc = jnp.exp2(m - m_new) * acc + _dot(p, v_ref[c * tk:(c + 1) * tk, :])
        return m_new, acc

    s_refs = (sa_ref, sb_ref)
    carry = (jnp.full((tq, 1), -jnp.inf, F32), jnp.zeros((tq, HEAD_W), F32))
    sa_ref[...] = scores(0)
    for c in range(n_chunks):
        if c + 1 < n_chunks:
            s_refs[(c + 1) % 2][...] = scores(c + 1)
        carry = absorb(s_refs[c % 2], c, carry)
    _, acc = carry
    o_ref[...] = (acc[:, :V_HEAD_DIM] / acc[:, V_HEAD_DIM:V_HEAD_DIM + 1]).astype(o_ref.dtype)


def _attention(q, k, v, q_row0, n_q, k_row0, n_k):
    tq = _tile(n_q, 512, 256, 128)
    tk = _tile(n_k, 768, 640, 512, 384, 256, 128)
    assert q_row0 % tq == 0 and k_row0 % n_k == 0
    q_blk0 = q_row0 // tq
    k_blk = k_row0 // n_k
    return pl.pallas_call(
        functools.partial(_attn_kernel, tk=tk, n_chunks=n_k // tk),
        grid=(N_HEADS, n_q // tq),
        in_specs=[pl.BlockSpec((tq, HEAD_W), lambda h, i: (q_blk0 + i, h)),
                  pl.BlockSpec((n_k, HEAD_W), lambda h, i: (k_blk, h)),
                  pl.BlockSpec((n_k, HEAD_W), lambda h, i: (k_blk, h))],
        out_specs=pl.BlockSpec((tq, V_HEAD_DIM), lambda h, i: (i, h)),
        out_shape=jax.ShapeDtypeStruct((n_q, N_HEADS * V_HEAD_DIM), BF16),
        scratch_shapes=[pltpu.VMEM((tq, tk), F32), pltpu.VMEM((tq, tk), F32)],
        compiler_params=_cparams(2),
        name="attention",
    )(q, k, v)


def _conv_gate_kernel(cx_ref, cb_ref, cc_ref, cxp_ref, ccp_ref, cxn_ref, ccn_ref, wc_ref, z_ref, *,
                      seg_starts, seg_ends, col_chunk):
    tm, width = z_ref.shape
    loc = lax.broadcasted_iota(jnp.int32, (tm, 1), 0)
    row = loc + pl.program_id(0) * tm
    first = functools.reduce(jnp.logical_or, [row == r for r in seg_starts])
    last = functools.reduce(jnp.logical_or, [row == r for r in seg_ends])
    for c0 in range(0, width, col_chunk):
        cs = slice(c0, c0 + col_chunk)
        u = cx_ref[:, cs].astype(F32) * cc_ref[:, cs].astype(F32)
        u_halo_prev = (cxp_ref[SUBLANES - 1:SUBLANES, cs].astype(F32)
                       * ccp_ref[SUBLANES - 1:SUBLANES, cs].astype(F32))
        u_halo_next = cxn_ref[0:1, cs].astype(F32) * ccn_ref[0:1, cs].astype(F32)
        u_prev = jnp.where(loc == 0, u_halo_prev, pltpu.roll(u, 1, 0))
        u_prev = jnp.where(first, 0.0, u_prev)
        u_next = jnp.where(loc == tm - 1, u_halo_next, pltpu.roll(u, tm - 1, 0))
        u_next = jnp.where(last, 0.0, u_next)
        conv = wc_ref[0:1, cs] * u_prev + wc_ref[1:2, cs] * u + wc_ref[2:3, cs] * u_next
        z_ref[:, cs] = (cb_ref[:, cs].astype(F32) * conv).astype(z_ref.dtype)


def _conv_gate(p, w_conv, rows, s_len, m_len):
    width = w_conv.shape[1]
    tm = _tile(rows, 768, 512, 384, 256, 128, 64)
    hb = tm // SUBLANES
    last_hb = p.shape[0] // SUBLANES - 1

    def prev_map(col):
        return lambda i: (jnp.maximum(i * hb - 1, 0), col)

    def next_map(col):
        return lambda i: (jnp.minimum((i + 1) * hb, last_hb), col)

    kern = functools.partial(_conv_gate_kernel, seg_starts=(0, s_len), seg_ends=(s_len - 1, m_len - 1),
                             col_chunk=_tile(width, 512, LANES))
    return pl.pallas_call(
        kern,
        grid=(rows // tm,),
        in_specs=[pl.BlockSpec((tm, width), lambda i: (i, 0)),
                  pl.BlockSpec((tm, width), lambda i: (i, 1)),
                  pl.BlockSpec((tm, width), lambda i: (i, 2)),
                  pl.BlockSpec((SUBLANES, width), prev_map(0)),
                  pl.BlockSpec((SUBLANES, width), prev_map(2)),
                  pl.BlockSpec((SUBLANES, width), next_map(0)),
                  pl.BlockSpec((SUBLANES, width), next_map(2)),
                  pl.BlockSpec((CONV_K, width), lambda i: (0, 0))],
        out_specs=pl.BlockSpec((tm, width), lambda i: (i, 0)),
        out_shape=jax.ShapeDtypeStruct((rows, width), BF16),
        compiler_params=_cparams(1),
        name="conv_gate",
    )(p, p, p, p, p, p, p, w_conv)


def _merge_kernel(attn_ref, z_ref, sga_ref, sgb_ref, woa_ref, wob_ref, o_ref):
    o_a = _dot(attn_ref[...], woa_ref[...])
    o_b = _dot(z_ref[...], wob_ref[...])
    o_ref[...] = (sga_ref[...].astype(F32) * o_a + sgb_ref[...].astype(F32) * o_b).astype(o_ref.dtype)


def _merge(attn, z, p, w_oa, w_ob, rows):
    attn_w = w_oa.shape[0]
    width, d = w_ob.shape
    tm = _tile(rows, 768, 512, 384, 256, 128, 64)
    tn = _tile(d, 1024, 512, 256, LANES)
    ga_blk = 3 * width // tn
    gb_blk = (3 * width + d) // tn
    return pl.pallas_call(
        _merge_kernel,
        grid=(d // tn, rows // tm),
        in_specs=[pl.BlockSpec((tm, attn_w), lambda j, i: (i, 0)),
                  pl.BlockSpec((tm, width), lambda j, i: (i, 0)),
                  pl.BlockSpec((tm, tn), lambda j, i: (i, ga_blk + j)),
                  pl.BlockSpec((tm, tn), lambda j, i: (i, gb_blk + j)),
                  pl.BlockSpec((attn_w, tn), lambda j, i: (0, j)),
                  pl.BlockSpec((width, tn), lambda j, i: (0, j))],
        out_specs=pl.BlockSpec((tm, tn), lambda j, i: (i, j)),
        out_shape=jax.ShapeDtypeStruct((rows, d), BF16),
        compiler_params=_cparams(2),
        name="merge",
    )(attn, z, p, p, w_oa, w_ob)


def _mm_res_kernel(a_ref, w_ref, x_ref, gt_ref, o_ref, *, n_lat):
    tm = a_ref.shape[0]
    row = lax.broadcasted_iota(jnp.int32, (tm, 1), 0) + pl.program_id(1) * tm
    gate = jnp.where(row >= n_lat, gt_ref[1:2, :], gt_ref[0:1, :])
    o_ref[...] = x_ref[...] + gate * _dot(a_ref[...], _bf16(w_ref[...]))


def _matmul_residual(a, w, x, mods, gate_blk, rows, n_lat):
    k, n = w.shape
    tm = _tile(rows, 768, 640, 512, 256, 128)
    tn = _tile(n, 512, 256, LANES)
    nt = n // tn
    return pl.pallas_call(
        functools.partial(_mm_res_kernel, n_lat=n_lat),
        grid=(nt, rows // tm),
        in_specs=[pl.BlockSpec((tm, k), lambda j, i: (i, 0)),
                  pl.BlockSpec((k, tn), lambda j, i: (0, j)),
                  pl.BlockSpec((tm, tn), lambda j, i: (i, j)),
                  pl.BlockSpec((2, tn), lambda j, i: (0, gate_blk * nt + j))],
        out_specs=pl.BlockSpec((tm, tn), lambda j, i: (i, j)),
        out_shape=jax.ShapeDtypeStruct((rows, n), F32),
        compiler_params=_cparams(2),
        name="matmul_residual",
    )(a, w, x, mods)


def _block_in_use(eid_ref):
    return pl.program_id(1) < eid_ref[pl.num_programs(1)]


def _glu_kernel(eid_ref, a_ref, wg_ref, wu_ref, o_ref):
    @pl.when(_block_in_use(eid_ref))
    def _():
        a = _bf16(a_ref[...])
        gate = _dot(a, _bf16(wg_ref[0]))
        up = _dot(a, _bf16(wu_ref[0]))
        o_ref[...] = (gate * jax.nn.sigmoid(gate) * up).astype(o_ref.dtype)

    @pl.when(jnp.logical_not(_block_in_use(eid_ref)))
    def _():
        o_ref[...] = jnp.zeros_like(o_ref)


def _glu(a, w_gate, w_up, eid, tm):
    r, k = a.shape
    f = w_gate.shape[2]
    tn = _tile(f, 512, 1408, 256, LANES)
    grid_spec = pltpu.PrefetchScalarGridSpec(
        num_scalar_prefetch=1,
        grid=(f // tn, r // tm),
        in_specs=[pl.BlockSpec((tm, k), lambda j, i, e: (i, 0)),
                  pl.BlockSpec((1, k, tn), lambda j, i, e: (e[i], 0, j)),
                  pl.BlockSpec((1, k, tn), lambda j, i, e: (e[i], 0, j))],
        out_specs=pl.BlockSpec((tm, tn), lambda j, i, e: (i, j)),
    )
    return pl.pallas_call(
        _glu_kernel,
        grid_spec=grid_spec,
        out_shape=jax.ShapeDtypeStruct((r, f), BF16),
        compiler_params=_cparams(2),
        name="glu",
    )(eid, a, w_gate, w_up)


def _down_kernel(eid_ref, a_ref, w_ref, o_ref):
    @pl.when(_block_in_use(eid_ref))
    def _():
        o_ref[...] = _dot(a_ref[...], w_ref[0])

    @pl.when(jnp.logical_not(_block_in_use(eid_ref)))
    def _():
        o_ref[...] = jnp.zeros_like(o_ref)


def _down_grouped(a, w_down, eid, tm):
    r, f = a.shape
    d = w_down.shape[2]
    grid_spec = pltpu.PrefetchScalarGridSpec(
        num_scalar_prefetch=1,
        grid=(1, r // tm),
        in_specs=[pl.BlockSpec((tm, f), lambda j, i, e: (i, 0)),
                  pl.BlockSpec((1, f, d), lambda j, i, e: (e[i], 0, 0))],
        out_specs=pl.BlockSpec((tm, d), lambda j, i, e: (i, 0)),
    )
    return pl.pallas_call(
        _down_kernel,
        grid_spec=grid_spec,
        out_shape=jax.ShapeDtypeStruct((r, d), F32),
        compiler_params=_cparams(2),
        name="down_grouped",
    )(eid, a, w_down)


def _row_copies(idx_ref, idx_base, idx_stride, src_ref, dst_ref, sem, n_rows, start):
    def body(r2, c):
        for prio in range(2):
            r = 2 * r2 + prio
            row = idx_ref[idx_base + r * idx_stride]
            cp = pltpu.make_async_copy(src_ref.at[pl.ds(row, 1)], dst_ref.at[pl.ds(r, 1)], sem)
            if start:
                cp.start(priority=prio)
            else:
                cp.wait()
        return c

    assert n_rows % 2 == 0
    lax.fori_loop(0, n_rows // 2, body, 0, unroll=DMA_ISSUE_UNROLL // 2)


def _prefetched_gather(copies):
    i = pl.program_id(0)
    slot = i % 2

    @pl.when(i == 0)
    def _():
        copies(0, 0, True)

    @pl.when(i + 1 < pl.num_programs(0))
    def _():
        copies(i + 1, 1 - slot, True)

    copies(i, slot, False)
    return slot


def _gather_rows_kernel(idx_ref, src_ref, o_ref, g_ref, sem):
    tm = o_ref.shape[0]

    def copies(step, slot, start):
        _row_copies(idx_ref, step * tm, 1, src_ref, g_ref.at[slot], sem.at[slot], tm, start)

    slot = _prefetched_gather(copies)
    o_ref[...] = g_ref[slot]


def _gather_rows(src, idx, tm):
    n = idx.shape[0]
    w = src.shape[1]
    grid_spec = pltpu.PrefetchScalarGridSpec(
        num_scalar_prefetch=1,
        grid=(n // tm,),
        in_specs=[pl.BlockSpec(memory_space=pl.ANY)],
        out_specs=pl.BlockSpec((tm, w), lambda i, idx_ref: (i, 0)),
        scratch_shapes=[pltpu.VMEM((2, tm, w), src.dtype), pltpu.SemaphoreType.DMA((2,))],
    )
    return pl.pallas_call(
        _gather_rows_kernel,
        grid_spec=grid_spec,
        out_shape=jax.ShapeDtypeStruct((n, w), src.dtype),
        compiler_params=_cparams(1),
        name="gather_rows",
    )(idx, src)


def _final_kernel(dest_ref, x_ref, y_hbm_ref, gate_ref, gt_ref, g_ref, o_ref, y_ref, sem):
    tm = x_ref.shape[0]

    def copies(step, slot, start):
        for kk in range(TOP_K):
            _row_copies(dest_ref, step * tm * TOP_K + kk, TOP_K, y_hbm_ref, y_ref.at[slot, kk], sem.at[slot],
                        tm, start)

    slot = _prefetched_gather(copies)
    gates = gate_ref[...]
    y = gates[:, 0:1] * y_ref[slot, 0] + gates[:, 1:2] * y_ref[slot, 1]
    x = x_ref[...] + gt_ref[0:1, :] * y
    o_ref[...] = x * lax.rsqrt(jnp.mean(x * x, axis=-1, keepdims=True) + NORM_EPS) * g_ref[...]


def _combine_final(x, ybuf, dest, gates, mods, gate_blk, g_final):
    r, d = x.shape
    tm = _tile(r, 256, 128, 64, SUBLANES)
    grid_spec = pltpu.PrefetchScalarGridSpec(
        num_scalar_prefetch=1,
        grid=(r // tm,),
        in_specs=[pl.BlockSpec((tm, d), lambda i, dest_ref: (i, 0)),
                  pl.BlockSpec(memory_space=pl.ANY),
                  pl.BlockSpec((tm, LANES), lambda i, dest_ref: (i, 0)),
                  pl.BlockSpec((2, d), lambda i, dest_ref: (0, gate_blk)),
                  pl.BlockSpec((1, d), lambda i, dest_ref: (0, 0))],
        out_specs=pl.BlockSpec((tm, d), lambda i, dest_ref: (i, 0)),
        scratch_shapes=[pltpu.VMEM((2, TOP_K, tm, d), F32), pltpu.SemaphoreType.DMA((2,))],
    )
    return pl.pallas_call(
        _final_kernel,
        grid_spec=grid_spec,
        out_shape=jax.ShapeDtypeStruct((r, d), F32),
        compiler_params=_cparams(1),
        name="combine_final",
    )(dest, x, ybuf, gates, mods, g_final.reshape(1, d))


def _rope_tables(s_len, c_len):
    quarter = QK_ROPE_DIM // 4
    tok = jnp.arange(s_len + c_len, dtype=jnp.int32)[:, None]
    lane = jnp.arange(LANES, dtype=jnp.int32)[None, :]
    group = lane // quarter
    pos = jnp.where(group < 2, tok // GRID_W, tok % GRID_W).astype(F32)
    inv_freq = jnp.power(ROPE_THETA, -(2 * (lane % quarter)).astype(F32) / (QK_ROPE_DIM // 2))
    ang = jnp.where(tok < s_len, pos * inv_freq, 0.0)
    live = group < 4
    cos = jnp.where(live, jnp.cos(ang), 0.0)
    sin = jnp.where(live, jnp.where(group % 2 == 0, -jnp.sin(ang), jnp.sin(ang)), 0.0)
    return cos, sin


def _rope_swap_perm():
    q = QK_ROPE_DIM // 4
    return jnp.concatenate([jnp.arange(q, 2 * q), jnp.arange(0, q), jnp.arange(3 * q, 4 * q), jnp.arange(2 * q, 3 * q)])


def _layer_weights(w_in_t, w_qb_all, w_kvb_all, li):
    d = w_in_t.shape[2]
    perm = _rope_swap_perm()
    kr_lo = Q_LORA_RANK + KV_LORA_RANK
    kr_hi = kr_lo + QK_ROPE_DIM
    w_kr = w_in_t[li, kr_lo:kr_hi]
    zpad = jnp.zeros((LANES - QK_ROPE_DIM, d), w_in_t.dtype)
    w_a = jnp.concatenate([w_in_t[li, :kr_lo], w_kr, zpad, w_kr[perm], zpad], axis=0)
    w_b = w_in_t[li, kr_hi:]
    w_qb, w_kvb = w_qb_all[li], w_kvb_all[li]
    qb = w_qb.reshape(Q_LORA_RANK, N_HEADS, QK_NOPE_DIM + QK_ROPE_DIM)
    q_rope = qb[:, :, QK_NOPE_DIM:]
    hpad = jnp.zeros((Q_LORA_RANK, N_HEADS, LANES - QK_ROPE_DIM), w_qb.dtype)
    w_q_main = jnp.concatenate([qb, hpad], axis=2).reshape(Q_LORA_RANK, N_HEADS * HEAD_W)
    w_q_swap = jnp.concatenate([q_rope[:, :, perm], hpad], axis=2).reshape(Q_LORA_RANK, N_HEADS * LANES)
    kvb = w_kvb.reshape(KV_LORA_RANK, N_HEADS, QK_NOPE_DIM + V_HEAD_DIM)
    w_k = kvb[:, :, :QK_NOPE_DIM].reshape(KV_LORA_RANK, N_HEADS * QK_NOPE_DIM)
    w_v = kvb[:, :, QK_NOPE_DIM:].reshape(KV_LORA_RANK, N_HEADS * V_HEAD_DIM)
    return tuple(w.astype(BF16) for w in (w_a, w_b, w_q_main, w_q_swap, w_k, w_v))


def _moe_slots(idx, n_blocks):
    e_flat = idx.reshape(-1)
    onehot = (e_flat[:, None] == jnp.arange(N_EXPERTS)[None, :]).astype(jnp.int32)
    csum = jnp.cumsum(onehot, axis=0)
    counts = csum[-1]
    rank = jnp.sum((csum - onehot) * onehot, axis=1)
    padded = (counts + MOE_ROWS - 1) // MOE_ROWS * MOE_ROWS
    p_end = jnp.cumsum(padded)
    p_start = p_end - padded
    dest = jnp.sum(onehot * p_start[None, :], axis=1) + rank
    blk_lo = jnp.arange(n_blocks) * MOE_ROWS
    block_expert = jnp.minimum(jnp.sum((blk_lo[:, None] >= p_end[None, :]).astype(jnp.int32), axis=1),
                               N_EXPERTS - 1)
    blocks_in_use = p_end[-1:] // MOE_ROWS
    return dest.astype(jnp.int32), jnp.concatenate([block_expert, blocks_in_use]).astype(jnp.int32)


def kernel(x, c, ctx, c_ctx, w_ada, b_ada, g_attn, w_in, g_qa, w_qb, g_kva, w_kvb, w_conv, w_oa, w_ob, w_o,
           g_ffn, w_gate_dense, w_up_dense, w_down_dense, w_router, w_gate_exp, w_up_exp, w_down_exp, g_final):
    _, s_len, d = x.shape
    c_len = ctx.shape[1]
    m_len = s_len + c_len
    depth = w_in.shape[0]
    width = w_conv.shape[2]
    assert depth == 2, "supported stack: dense-FFN layer with context updates, then a final expert-FFN layer"

    mods_all = _ada_mod(c, c_ctx, w_ada, b_ada)
    cos_t, sin_t = _rope_tables(s_len, c_len)
    w_in_t = jnp.swapaxes(w_in, 1, 2)
    xa = jnp.concatenate([x[0], ctx[0]], axis=0)

    for li in range(depth):
        last = li == depth - 1
        mods = mods_all[li]
        w_a, w_b, w_q_main, w_q_swap, w_k, w_v = _layer_weights(w_in_t, w_qb, w_kvb, li)
        g_a = jnp.concatenate([g_qa[li], g_kva[li]]).reshape(1, -1)
        rows = s_len if last else m_len

        h = _norm_modulate(xa, g_attn[li], mods, 0, 1, s_len, BF16)
        qa_n, ckv_n, kr = _in_proj_a(h, w_a, g_a, cos_t, sin_t)
        p = _in_proj_b(h, w_b, 3 * width)
        q = _q_proj(qa_n, w_q_main, w_q_swap, cos_t, sin_t, rows)
        k, v = _kv_proj(ckv_n, kr, w_k, w_v)
        attn = _attention(q, k, v, 0, s_len, 0, m_len)
        if not last:
            attn = jnp.concatenate([attn, _attention(q, k, v, s_len, c_len, s_len, c_len)], axis=0)
        z = _conv_gate(p, w_conv[li], rows, s_len, m_len)
        merged = _merge(attn, z, p, w_oa[li].astype(BF16), w_ob[li].astype(BF16), rows)
        xa = _matmul_residual(merged, w_o[li], xa, mods, 2, rows, s_len)

        j = li // 2
        if li % 2 == 0:
            h2 = _norm_modulate(xa, g_ffn[li], mods, 3, 4, s_len, BF16)
            tm = _tile(rows, 768, 640, 512, 256, 128)
            eid = jnp.zeros((rows // tm + 1,), jnp.int32).at[-1].set(rows // tm)
            hid = _glu(h2, w_gate_dense[j][None], w_up_dense[j][None], eid, tm)
            xa = _matmul_residual(hid, w_down_dense[j].astype(BF16), xa, mods, 5, rows, s_len)
        else:
            h2, idx, gates = _norm_modulate_route(xa, g_ffn[li], mods, 3, 4, w_router[j])
            n_assign = rows * TOP_K
            n_blocks = -(-n_assign // MOE_ROWS) + N_EXPERTS
            dest, block_expert = _moe_slots(idx[:, :TOP_K], n_blocks)
            tok = jnp.arange(n_assign, dtype=jnp.int32) // TOP_K
            slot_tok = jnp.zeros((n_blocks * MOE_ROWS,), jnp.int32).at[dest].set(tok, unique_indices=True)
            buf = _gather_rows(h2, slot_tok, MOE_ROWS)
            hid = _glu(buf, w_gate_exp[j], w_up_exp[j], block_expert, MOE_ROWS)
            ybuf = _down_grouped(hid, w_down_exp[j].astype(BF16), block_expert, MOE_ROWS)
            return _combine_final(xa, ybuf, dest, gates, mods, 5, g_final)[None]
    raise AssertionError("unreachable: the final layer returns")
```

```python
import functools

import jax
import jax.numpy as jnp
from jax import lax
from jax.experimental import pallas as pl
from jax.experimental.pallas import tpu as pltpu

F32 = jnp.float32
BF16 = jnp.bfloat16

N_HEADS = 16
QK_NOPE_DIM = 128
QK_ROPE_DIM = 64
V_HEAD_DIM = 128
Q_LORA_RANK = 512
KV_LORA_RANK = 512
GRID_W = 64
ROPE_THETA = 10000.0
ATTN_SCALE = (QK_NOPE_DIM + QK_ROPE_DIM) ** -0.5
Q_SCALE = ATTN_SCALE * 1.4426950408889634
CONV_K = 3
N_EXPERTS = 8
TOP_K = 2
NORM_EPS = 1e-6

LANES = 128
SUBLANES = 8
HEAD_W = 2 * LANES
VMEM_LIMIT_BYTES = 56 * 1024 * 1024

MOE_ROWS = 256
DMA_ISSUE_UNROLL = 8


def _tile(n, *cands):
    for c in cands:
        if n % c == 0:
            return c
    return n


def _cparams(n_axes):
    return pltpu.CompilerParams(dimension_semantics=("arbitrary",) * n_axes,
                                vmem_limit_bytes=VMEM_LIMIT_BYTES)


def _dot(a, b):
    return jnp.dot(a, b, preferred_element_type=F32)


def _dot_t(a, bt):
    return lax.dot_general(a, bt, (((1,), (1,)), ((), ())), preferred_element_type=F32)


def _bf16(w):
    return w if w.dtype == BF16 else w.astype(BF16)


def _ada_kernel(xt_ref, w_ref, b_ref, o_ref, *, k_chunk):
    d = xt_ref.shape[0]
    tn = o_ref.shape[-1]

    def body(k, acc):
        a0, a1 = acc
        ks = pl.multiple_of(k * k_chunk, k_chunk)
        xt = xt_ref[pl.ds(ks, k_chunk), :]
        s = xt * jax.nn.sigmoid(xt)
        w = w_ref[0, pl.ds(ks, k_chunk), :]
        a0 = a0 + jnp.sum(w * s[:, 0:1], axis=0, keepdims=True)
        a1 = a1 + jnp.sum(w * s[:, 1:2], axis=0, keepdims=True)
        return a0, a1

    z = jnp.zeros((1, tn), F32)
    a0, a1 = lax.fori_loop(0, d // k_chunk, body, (z, z))
    o_ref[0, 0:1, :] = a0 + b_ref[0]
    o_ref[0, 1:2, :] = a1 + b_ref[0]


def _ada_mod(c, c_ctx, w_ada, b_ada):
    depth, d, n = w_ada.shape
    xt = jnp.stack([c[0], c_ctx], axis=1)
    tn = _tile(n, 1024, 512, LANES)
    k_chunk = _tile(d, 256, SUBLANES)
    return pl.pallas_call(
        functools.partial(_ada_kernel, k_chunk=k_chunk),
        grid=(depth, n // tn),
        in_specs=[pl.BlockSpec((d, 2), lambda l, j: (0, 0)),
                  pl.BlockSpec((1, d, tn), lambda l, j: (l, 0, j)),
                  pl.BlockSpec((1, 1, tn), lambda l, j: (l, 0, j))],
        out_specs=pl.BlockSpec((1, 2, tn), lambda l, j: (l, 0, j)),
        out_shape=jax.ShapeDtypeStruct((depth, 2, n), F32),
        compiler_params=_cparams(2),
        name="ada_mod",
    )(xt, w_ada, b_ada.reshape(depth, 1, n))


def _norm_mod(x, g_ref, sh_ref, sc_ref, is_ctx):
    y = x * lax.rsqrt(jnp.mean(x * x, axis=-1, keepdims=True) + NORM_EPS) * g_ref[...]
    sh = jnp.where(is_ctx, sh_ref[1:2, :], sh_ref[0:1, :])
    sc = jnp.where(is_ctx, sc_ref[1:2, :], sc_ref[0:1, :])
    return y * (1.0 + sc) + sh


def _norm_kernel(x_ref, g_ref, sh_ref, sc_ref, o_ref, *, n_lat_tiles):
    is_ctx = pl.program_id(0) >= n_lat_tiles
    o_ref[...] = _norm_mod(x_ref[...], g_ref, sh_ref, sc_ref, is_ctx).astype(o_ref.dtype)


def _norm_router_kernel(x_ref, g_ref, sh_ref, sc_ref, wr_ref, h_ref, idx_ref, gate_ref):
    h = _norm_mod(x_ref[...], g_ref, sh_ref, sc_ref, False)
    h_ref[...] = h
    logits = jnp.dot(h, wr_ref[...], preferred_element_type=F32, precision=lax.Precision.HIGHEST)
    lane = lax.broadcasted_iota(jnp.int32, logits.shape, 1).astype(F32)
    neg = jnp.float32(-jnp.inf)
    l1 = jnp.where(lane < N_EXPERTS, logits, neg)
    v1 = jnp.max(l1, axis=-1, keepdims=True)
    i1 = jnp.min(jnp.where(l1 == v1, lane, float(LANES)), axis=-1, keepdims=True)
    l2 = jnp.where(lane == i1, neg, l1)
    v2 = jnp.max(l2, axis=-1, keepdims=True)
    i2 = jnp.min(jnp.where(l2 == v2, lane, float(LANES)), axis=-1, keepdims=True)
    e = jnp.exp(v2 - v1)
    g1 = 1.0 / (1.0 + e)
    g2 = e / (1.0 + e)
    idx_ref[...] = jnp.where(lane == 0, i1, jnp.where(lane == 1, i2, 0.0)).astype(jnp.int32)
    gate_ref[...] = jnp.where(lane == 0, g1, jnp.where(lane == 1, g2, 0.0))


def _norm_modulate(x, g, mods, sh_blk, sc_blk, n_lat, out_dtype):
    r, d = x.shape
    tm = _tile(n_lat, 256, 128, 64, SUBLANES) if r > n_lat else _tile(r, 256, 128, 64, SUBLANES)
    if r > n_lat:
        tm = _tile(r - n_lat, tm, 128, 64, SUBLANES)
    return pl.pallas_call(
        functools.partial(_norm_kernel, n_lat_tiles=n_lat // tm),
        grid=(r // tm,),
        in_specs=[pl.BlockSpec((tm, d), lambda i: (i, 0)),
                  pl.BlockSpec((1, d), lambda i: (0, 0)),
                  pl.BlockSpec((2, d), lambda i: (0, sh_blk)),
                  pl.BlockSpec((2, d), lambda i: (0, sc_blk))],
        out_specs=pl.BlockSpec((tm, d), lambda i: (i, 0)),
        out_shape=jax.ShapeDtypeStruct((r, d), out_dtype),
        compiler_params=_cparams(1),
        name="norm_modulate",
    )(x, g.reshape(1, d), mods, mods)


def _norm_modulate_route(x, g, mods, sh_blk, sc_blk, w_router):
    r, d = x.shape
    tm = _tile(r, 512, 256, 128, 64, SUBLANES)
    wr = jnp.pad(w_router, ((0, 0), (0, LANES - N_EXPERTS)))
    return pl.pallas_call(
        _norm_router_kernel,
        grid=(r // tm,),
        in_specs=[pl.BlockSpec((tm, d), lambda i: (i, 0)),
                  pl.BlockSpec((1, d), lambda i: (0, 0)),
                  pl.BlockSpec((2, d), lambda i: (0, sh_blk)),
                  pl.BlockSpec((2, d), lambda i: (0, sc_blk)),
                  pl.BlockSpec((d, LANES), lambda i: (0, 0))],
        out_specs=[pl.BlockSpec((tm, d), lambda i: (i, 0)),
                   pl.BlockSpec((tm, LANES), lambda i: (i, 0)),
                   pl.BlockSpec((tm, LANES), lambda i: (i, 0))],
        out_shape=[jax.ShapeDtypeStruct((r, d), F32),
                   jax.ShapeDtypeStruct((r, LANES), jnp.int32),
                   jax.ShapeDtypeStruct((r, LANES), F32)],
        compiler_params=_cparams(1),
        name="norm_modulate_route",
    )(x, g.reshape(1, d), mods, mods, wr)


def _in_a_kernel(h_ref, wt_ref, g_ref, cos_ref, sin_ref, qa_ref, ckv_ref, kr_ref):
    acc = _dot_t(h_ref[...], wt_ref[...])

    def rms(v, g):
        return v * lax.rsqrt(jnp.mean(v * v, axis=-1, keepdims=True) + NORM_EPS) * g

    q0, q1 = 0, Q_LORA_RANK
    c1 = q1 + KV_LORA_RANK
    qa_ref[...] = rms(acc[:, q0:q1], g_ref[:, q0:q1]).astype(qa_ref.dtype)
    ckv_ref[...] = rms(acc[:, q1:c1], g_ref[:, q1:c1]).astype(ckv_ref.dtype)
    kr = acc[:, c1:c1 + LANES] * cos_ref[...] + acc[:, c1 + LANES:c1 + 2 * LANES] * sin_ref[...]
    kr_ref[...] = kr.astype(kr_ref.dtype)


def _in_proj_a(h, w_at, g_a, cos_t, sin_t):
    m, d = h.shape
    n = w_at.shape[0]
    tm = _tile(m, 768, 640, 512, 256, 128)
    return pl.pallas_call(
        _in_a_kernel,
        grid=(m // tm,),
        in_specs=[pl.BlockSpec((tm, d), lambda i: (i, 0)),
                  pl.BlockSpec((n, d), lambda i: (0, 0)),
                  pl.BlockSpec((1, Q_LORA_RANK + KV_LORA_RANK), lambda i: (0, 0)),
                  pl.BlockSpec((tm, LANES), lambda i: (i, 0)),
                  pl.BlockSpec((tm, LANES), lambda i: (i, 0))],
        out_specs=[pl.BlockSpec((tm, Q_LORA_RANK), lambda i: (i, 0)),
                   pl.BlockSpec((tm, KV_LORA_RANK), lambda i: (i, 0)),
                   pl.BlockSpec((tm, LANES), lambda i: (i, 0))],
        out_shape=[jax.ShapeDtypeStruct((m, Q_LORA_RANK), BF16),
                   jax.ShapeDtypeStruct((m, KV_LORA_RANK), BF16),
                   jax.ShapeDtypeStruct((m, LANES), BF16)],
        compiler_params=_cparams(1),
        name="in_proj_a",
    )(h, w_at, g_a, cos_t, sin_t)


def _in_b_kernel(h_ref, wt_ref, o_ref, *, first_gate_tile):
    acc = _dot_t(h_ref[...], _bf16(wt_ref[...]))
    is_gate = pl.program_id(0) >= first_gate_tile

    @pl.when(is_gate)
    def _():
        o_ref[...] = jax.nn.sigmoid(acc).astype(o_ref.dtype)

    @pl.when(jnp.logical_not(is_gate))
    def _():
        o_ref[...] = acc.astype(o_ref.dtype)


def _in_proj_b(h, w_in_t, li, row0, first_gate_col):
    m, d = h.shape
    n = w_in_t.shape[1] - row0
    tm = _tile(m, 1408, 768, 640, 512, 256, 128)
    tn = _tile(first_gate_col, 1024, 512, 256, LANES)
    base = li * w_in_t.shape[1] + row0
    assert base % SUBLANES == 0 and n % tn == 0
    return pl.pallas_call(
        functools.partial(_in_b_kernel, first_gate_tile=first_gate_col // tn),
        grid=(n // tn, m // tm),
        in_specs=[pl.BlockSpec((tm, d), lambda j, i: (i, 0)),
                  pl.BlockSpec((pl.Element(tn), pl.Element(d)),
                               lambda j, i: (pl.multiple_of(base + j * tn, SUBLANES), 0))],
        out_specs=pl.BlockSpec((tm, tn), lambda j, i: (i, j)),
        out_shape=jax.ShapeDtypeStruct((m, n), BF16),
        compiler_params=_cparams(2),
        name="in_proj_b",
    )(h, w_in_t.reshape(-1, d))


def _q_kernel(a_ref, wm_ref, ws_ref, cos_ref, sin_ref, q_ref):
    a = a_ref[...]
    cos = cos_ref[...]
    sin = sin_ref[...]
    for hd in range(N_HEADS):
        main = _dot(a, wm_ref[:, hd * HEAD_W:(hd + 1) * HEAD_W])
        swap = _dot(a, ws_ref[:, hd * LANES:(hd + 1) * LANES])
        q_ref[:, hd * HEAD_W:hd * HEAD_W + LANES] = (main[:, :LANES] * Q_SCALE).astype(q_ref.dtype)
        rot = (main[:, LANES:] * cos + swap * sin) * Q_SCALE
        q_ref[:, hd * HEAD_W + LANES:(hd + 1) * HEAD_W] = rot.astype(q_ref.dtype)


def _q_proj(qa_n, w_main, w_swap, cos_t, sin_t, rows):
    k = qa_n.shape[1]
    tm = _tile(rows, 768, 640, 512, 256, 128)
    return pl.pallas_call(
        _q_kernel,
        grid=(rows // tm,),
        in_specs=[pl.BlockSpec((tm, k), lambda i: (i, 0)),
                  pl.BlockSpec(w_main.shape, lambda i: (0, 0)),
                  pl.BlockSpec(w_swap.shape, lambda i: (0, 0)),
                  pl.BlockSpec((tm, LANES), lambda i: (i, 0)),
                  pl.BlockSpec((tm, LANES), lambda i: (i, 0))],
        out_specs=pl.BlockSpec((tm, N_HEADS * HEAD_W), lambda i: (i, 0)),
        out_shape=jax.ShapeDtypeStruct((rows, N_HEADS * HEAD_W), BF16),
        compiler_params=_cparams(1),
        name="q_proj",
    )(qa_n, w_main, w_swap, cos_t, sin_t)


def _kv_kernel(c_ref, kr_ref, wk_ref, wv_ref, k_ref, v_ref):
    c = c_ref[...]
    kr = kr_ref[...]
    lane = lax.broadcasted_iota(jnp.int32, (c.shape[0], LANES), 1)
    ones_col = jnp.where(lane == 0, 1.0, 0.0).astype(v_ref.dtype)
    for hd in range(N_HEADS):
        kn = _dot(c, wk_ref[:, hd * LANES:(hd + 1) * LANES])
        k_ref[:, hd * HEAD_W:hd * HEAD_W + LANES] = kn.astype(k_ref.dtype)
        k_ref[:, hd * HEAD_W + LANES:(hd + 1) * HEAD_W] = kr
        vh = _dot(c, wv_ref[:, hd * LANES:(hd + 1) * LANES])
        v_ref[:, hd * HEAD_W:hd * HEAD_W + LANES] = vh.astype(v_ref.dtype)
        v_ref[:, hd * HEAD_W + LANES:(hd + 1) * HEAD_W] = ones_col


def _kv_proj(ckv_n, kr, w_k, w_v):
    m, k = ckv_n.shape
    tm = _tile(m, 768, 640, 512, 256, 128)
    return pl.pallas_call(
        _kv_kernel,
        grid=(m // tm,),
        in_specs=[pl.BlockSpec((tm, k), lambda i: (i, 0)),
                  pl.BlockSpec((tm, LANES), lambda i: (i, 0)),
                  pl.BlockSpec(w_k.shape, lambda i: (0, 0)),
                  pl.BlockSpec(w_v.shape, lambda i: (0, 0))],
        out_specs=[pl.BlockSpec((tm, N_HEADS * HEAD_W), lambda i: (i, 0)),
                   pl.BlockSpec((tm, N_HEADS * HEAD_W), lambda i: (i, 0))],
        out_shape=[jax.ShapeDtypeStruct((m, N_HEADS * HEAD_W), BF16),
                   jax.ShapeDtypeStruct((m, N_HEADS * HEAD_W), BF16)],
        compiler_params=_cparams(1),
        name="kv_proj",
    )(ckv_n, kr, w_k, w_v)


def _attn_kernel(q_ref, k_ref, v_ref, o_ref, sa_ref, sb_ref, *, tk, n_chunks):
    q = q_ref[...]
    tq = q.shape[0]

    def scores(c):
        return _dot_t(q, k_ref[c * tk:(c + 1) * tk, :])

    def absorb(s_ref, c, carry):
        m, acc = carry
        s = s_ref[...]
        m_new = jnp.maximum(m, jnp.max(s, axis=-1, keepdims=True))
        p = jnp.exp2(s - m_new).astype(v_ref.dtype)
        acc = jnp.exp2(m - m_new) * acc + _dot(p, v_ref[c * tk:(c + 1) * tk, :])
        return m_new, acc

    s_refs = (sa_ref, sb_ref)
    carry = (jnp.full((tq, 1), -jnp.inf, F32), jnp.zeros((tq, HEAD_W), F32))
    sa_ref[...] = scores(0)
    for c in range(n_chunks):
        if c + 1 < n_chunks:
            s_refs[(c + 1) % 2][...] = scores(c + 1)
        carry = absorb(s_refs[c % 2], c, carry)
    _, acc = carry
    o_ref[...] = (acc[:, :V_HEAD_DIM] / acc[:, V_HEAD_DIM:V_HEAD_DIM + 1]).astype(o_ref.dtype)


def _attention(q, k, v, q_row0, n_q, k_row0, n_k):
    tq = _tile(n_q, 512, 256, 128)
    tk = _tile(n_k, 768, 640, 512, 384, 256, 128)
    assert q_row0 % tq == 0 and k_row0 % n_k == 0
    q_blk0 = q_row0 // tq
    k_blk = k_row0 // n_k
    return pl.pallas_call(
        functools.partial(_attn_kernel, tk=tk, n_chunks=n_k // tk),
        grid=(N_HEADS, n_q // tq),
        in_specs=[pl.BlockSpec((tq, HEAD_W), lambda h, i: (q_blk0 + i, h)),
                  pl.BlockSpec((n_k, HEAD_W), lambda h, i: (k_blk, h)),
                  pl.BlockSpec((n_k, HEAD_W), lambda h, i: (k_blk, h))],
        out_specs=pl.BlockSpec((tq, V_HEAD_DIM), lambda h, i: (i, h)),
        out_shape=jax.ShapeDtypeStruct((n_q, N_HEADS * V_HEAD_DIM), BF16),
        scratch_shapes=[pltpu.VMEM((tq, tk), F32), pltpu.VMEM((tq, tk), F32)],
        compiler_params=_cparams(2),
        name="attention",
    )(q, k, v)


def _conv_gate_kernel(cx_ref, cb_ref, cc_ref, cxp_ref, ccp_ref, cxn_ref, ccn_ref, wc_ref, z_ref, *,
                      seg_starts, seg_ends, col_chunk):
    tm, width = z_ref.shape
    loc = lax.broadcasted_iota(jnp.int32, (tm, 1), 0)
    row = loc + pl.program_id(0) * tm
    first = functools.reduce(jnp.logical_or, [row == r for r in seg_starts])
    last = functools.reduce(jnp.logical_or, [row == r for r in seg_ends])
    for c0 in range(0, width, col_chunk):
        cs = slice(c0, c0 + col_chunk)
        u = cx_ref[:, cs].astype(F32) * cc_ref[:, cs].astype(F32)
        u_halo_prev = (cxp_ref[SUBLANES - 1:SUBLANES, cs].astype(F32)
                       * ccp_ref[SUBLANES - 1:SUBLANES, cs].astype(F32))
        u_halo_next = cxn_ref[0:1, cs].astype(F32) * ccn_ref[0:1, cs].astype(F32)
        u_prev = jnp.where(loc == 0, u_halo_prev, pltpu.roll(u, 1, 0))
        u_prev = jnp.where(first, 0.0, u_prev)
        u_next = jnp.where(loc == tm - 1, u_halo_next, pltpu.roll(u, tm - 1, 0))
        u_next = jnp.where(last, 0.0, u_next)
        conv = wc_ref[0:1, cs] * u_prev + wc_ref[1:2, cs] * u + wc_ref[2:3, cs] * u_next
        z_ref[:, cs] = (cb_ref[:, cs].astype(F32) * conv).astype(z_ref.dtype)


def _conv_gate(p, w_conv, rows, s_len, m_len):
    width = w_conv.shape[1]
    tm = _tile(rows, 768, 512, 384, 256, 128, 64)
    hb = tm // SUBLANES
    last_hb = p.shape[0] // SUBLANES - 1

    def prev_map(col):
        return lambda i: (jnp.maximum(i * hb - 1, 0), col)

    def next_map(col):
        return lambda i: (jnp.minimum((i + 1) * hb, last_hb), col)

    kern = functools.partial(_conv_gate_kernel, seg_starts=(0, s_len), seg_ends=(s_len - 1, m_len - 1),
                             col_chunk=_tile(width, 512, LANES))
    return pl.pallas_call(
        kern,
        grid=(rows // tm,),
        in_specs=[pl.BlockSpec((tm, width), lambda i: (i, 0)),
                  pl.BlockSpec((tm, width), lambda i: (i, 1)),
                  pl.BlockSpec((tm, width), lambda i: (i, 2)),
                  pl.BlockSpec((SUBLANES, width), prev_map(0)),
                  pl.BlockSpec((SUBLANES, width), prev_map(2)),
                  pl.BlockSpec((SUBLANES, width), next_map(0)),
                  pl.BlockSpec((SUBLANES, width), next_map(2)),
                  pl.BlockSpec((CONV_K, width), lambda i: (0, 0))],
        out_specs=pl.BlockSpec((tm, width), lambda i: (i, 0)),
        out_shape=jax.ShapeDtypeStruct((rows, width), BF16),
        compiler_params=_cparams(1),
        name="conv_gate",
    )(p, p, p, p, p, p, p, w_conv)


def _merge_kernel(attn_ref, z_ref, sga_ref, sgb_ref, woa_ref, wob_ref, o_ref):
    o_a = _dot(attn_ref[...], woa_ref[...])
    o_b = _dot(z_ref[...], wob_ref[...])
    o_ref[...] = (sga_ref[...].astype(F32) * o_a + sgb_ref[...].astype(F32) * o_b).astype(o_ref.dtype)


def _merge(attn, z, p, w_oa, w_ob, rows):
    attn_w = w_oa.shape[0]
    width, d = w_ob.shape
    tm = _tile(rows, 768, 512, 384, 256, 128, 64)
    tn = _tile(d, 1024, 512, 256, LANES)
    ga_blk = 3 * width // tn
    gb_blk = (3 * width + d) // tn
    return pl.pallas_call(
        _merge_kernel,
        grid=(d // tn, rows // tm),
        in_specs=[pl.BlockSpec((tm, attn_w), lambda j, i: (i, 0)),
                  pl.BlockSpec((tm, width), lambda j, i: (i, 0)),
                  pl.BlockSpec((tm, tn), lambda j, i: (i, ga_blk + j)),
                  pl.BlockSpec((tm, tn), lambda j, i: (i, gb_blk + j)),
                  pl.BlockSpec((attn_w, tn), lambda j, i: (0, j)),
                  pl.BlockSpec((width, tn), lambda j, i: (0, j))],
        out_specs=pl.BlockSpec((tm, tn), lambda j, i: (i, j)),
        out_shape=jax.ShapeDtypeStruct((rows, d), BF16),
        compiler_params=_cparams(2),
        name="merge",
    )(attn, z, p, p, w_oa, w_ob)


def _mm_res_kernel(a_ref, w_ref, x_ref, gt_ref, o_ref, *, n_lat):
    tm = a_ref.shape[0]
    row = lax.broadcasted_iota(jnp.int32, (tm, 1), 0) + pl.program_id(1) * tm
    gate = jnp.where(row >= n_lat, gt_ref[1:2, :], gt_ref[0:1, :])
    o_ref[...] = x_ref[...] + gate * _dot(a_ref[...], _bf16(w_ref[...]))


def _matmul_residual(a, w, layer, x, mods, gate_blk, rows, n_lat):
    _, k, n = w.shape
    tm = _tile(rows, 768, 640, 512, 256, 128)
    tn = _tile(n, 512, 256, LANES)
    nt = n // tn
    return pl.pallas_call(
        functools.partial(_mm_res_kernel, n_lat=n_lat),
        grid=(nt, rows // tm),
        in_specs=[pl.BlockSpec((tm, k), lambda j, i: (i, 0)),
                  pl.BlockSpec((None, k, tn), lambda j, i: (layer, 0, j)),
                  pl.BlockSpec((tm, tn), lambda j, i: (i, j)),
                  pl.BlockSpec((2, tn), lambda j, i: (0, gate_blk * nt + j))],
        out_specs=pl.BlockSpec((tm, tn), lambda j, i: (i, j)),
        out_shape=jax.ShapeDtypeStruct((rows, n), F32),
        compiler_params=_cparams(2),
        name="matmul_residual",
    )(a, w, x, mods)


def _block_in_use(eid_ref):
    return pl.program_id(1) < eid_ref[pl.num_programs(1)]


def _glu_kernel(eid_ref, a_ref, wg_ref, wu_ref, o_ref):
    @pl.when(_block_in_use(eid_ref))
    def _():
        a = _bf16(a_ref[...])
        gate = _dot(a, _bf16(wg_ref[0]))
        up = _dot(a, _bf16(wu_ref[0]))
        o_ref[...] = (gate * jax.nn.sigmoid(gate) * up).astype(o_ref.dtype)

    @pl.when(jnp.logical_not(_block_in_use(eid_ref)))
    def _():
        o_ref[...] = jnp.zeros_like(o_ref)


def _glu(a, w_gate, w_up, eid, tm):
    r, k = a.shape
    f = w_gate.shape[2]
    tn = _tile(f, 512, 1408, 256, LANES)
    grid_spec = pltpu.PrefetchScalarGridSpec(
        num_scalar_prefetch=1,
        grid=(f // tn, r // tm),
        in_specs=[pl.BlockSpec((tm, k), lambda j, i, e: (i, 0)),
                  pl.BlockSpec((1, k, tn), lambda j, i, e: (e[i], 0, j)),
                  pl.BlockSpec((1, k, tn), lambda j, i, e: (e[i], 0, j))],
        out_specs=pl.BlockSpec((tm, tn), lambda j, i, e: (i, j)),
    )
    return pl.pallas_call(
        _glu_kernel,
        grid_spec=grid_spec,
        out_shape=jax.ShapeDtypeStruct((r, f), BF16),
        compiler_params=_cparams(2),
        name="glu",
    )(eid, a, w_gate, w_up)


def _down_kernel(eid_ref, a_ref, w_ref, o_ref):
    @pl.when(_block_in_use(eid_ref))
    def _():
        o_ref[...] = _dot(a_ref[...], w_ref[0])

    @pl.when(jnp.logical_not(_block_in_use(eid_ref)))
    def _():
        o_ref[...] = jnp.zeros_like(o_ref)


def _down_grouped(a, w_down, eid, tm):
    r, f = a.shape
    d = w_down.shape[2]
    grid_spec = pltpu.PrefetchScalarGridSpec(
        num_scalar_prefetch=1,
        grid=(1, r // tm),
        in_specs=[pl.BlockSpec((tm, f), lambda j, i, e: (i, 0)),
                  pl.BlockSpec((1, f, d), lambda j, i, e: (e[i], 0, 0))],
        out_specs=pl.BlockSpec((tm, d), lambda j, i, e: (i, 0)),
    )
    return pl.pallas_call(
        _down_kernel,
        grid_spec=grid_spec,
        out_shape=jax.ShapeDtypeStruct((r, d), F32),
        compiler_params=_cparams(2),
        name="down_grouped",
    )(eid, a, w_down)


def _row_copies(idx_ref, idx_base, idx_stride, src_ref, dst_ref, sem, n_rows, start):
    def body(r2, c):
        for prio in range(2):
            r = 2 * r2 + prio
            row = idx_ref[idx_base + r * idx_stride]
            cp = pltpu.make_async_copy(src_ref.at[pl.ds(row, 1)], dst_ref.at[pl.ds(r, 1)], sem)
            if start:
                cp.start(priority=prio)
            else:
                cp.wait()
        return c

    assert n_rows % 2 == 0
    lax.fori_loop(0, n_rows // 2, body, 0, unroll=DMA_ISSUE_UNROLL // 2)


def _prefetched_gather(copies):
    i = pl.program_id(0)
    slot = i % 2

    @pl.when(i == 0)
    def _():
        copies(0, 0, True)

    @pl.when(i + 1 < pl.num_programs(0))
    def _():
        copies(i + 1, 1 - slot, True)

    copies(i, slot, False)
    return slot


def _gather_rows_kernel(idx_ref, src_ref, o_ref, g_ref, sem):
    tm = o_ref.shape[0]

    def copies(step, slot, start):
        _row_copies(idx_ref, step * tm, 1, src_ref, g_ref.at[slot], sem.at[slot], tm, start)

    slot = _prefetched_gather(copies)
    o_ref[...] = g_ref[slot]


def _gather_rows(src, idx, tm):
    n = idx.shape[0]
    w = src.shape[1]
    grid_spec = pltpu.PrefetchScalarGridSpec(
        num_scalar_prefetch=1,
        grid=(n // tm,),
        in_specs=[pl.BlockSpec(memory_space=pl.ANY)],
        out_specs=pl.BlockSpec((tm, w), lambda i, idx_ref: (i, 0)),
        scratch_shapes=[pltpu.VMEM((2, tm, w), src.dtype), pltpu.SemaphoreType.DMA((2,))],
    )
    return pl.pallas_call(
        _gather_rows_kernel,
        grid_spec=grid_spec,
        out_shape=jax.ShapeDtypeStruct((n, w), src.dtype),
        compiler_params=_cparams(1),
        name="gather_rows",
    )(idx, src)


def _final_kernel(dest_ref, x_ref, y_hbm_ref, gate_ref, gt_ref, g_ref, o_ref, y_ref, sem):
    tm = x_ref.shape[0]

    def copies(step, slot, start):
        for kk in range(TOP_K):
            _row_copies(dest_ref, step * tm * TOP_K + kk, TOP_K, y_hbm_ref, y_ref.at[slot, kk], sem.at[slot],
                        tm, start)

    slot = _prefetched_gather(copies)
    gates = gate_ref[...]
    y = gates[:, 0:1] * y_ref[slot, 0] + gates[:, 1:2] * y_ref[slot, 1]
    x = x_ref[...] + gt_ref[0:1, :] * y
    o_ref[...] = x * lax.rsqrt(jnp.mean(x * x, axis=-1, keepdims=True) + NORM_EPS) * g_ref[...]


def _combine_final(x, ybuf, dest, gates, mods, gate_blk, g_final):
    r, d = x.shape
    tm = _tile(r, 256, 128, 64, SUBLANES)
    grid_spec = pltpu.PrefetchScalarGridSpec(
        num_scalar_prefetch=1,
        grid=(r // tm,),
        in_specs=[pl.BlockSpec((tm, d), lambda i, dest_ref: (i, 0)),
                  pl.BlockSpec(memory_space=pl.ANY),
                  pl.BlockSpec((tm, LANES), lambda i, dest_ref: (i, 0)),
                  pl.BlockSpec((2, d), lambda i, dest_ref: (0, gate_blk)),
                  pl.BlockSpec((1, d), lambda i, dest_ref: (0, 0))],
        out_specs=pl.BlockSpec((tm, d), lambda i, dest_ref: (i, 0)),
        scratch_shapes=[pltpu.VMEM((2, TOP_K, tm, d), F32), pltpu.SemaphoreType.DMA((2,))],
    )
    return pl.pallas_call(
        _final_kernel,
        grid_spec=grid_spec,
        out_shape=jax.ShapeDtypeStruct((r, d), F32),
        compiler_params=_cparams(1),
        name="combine_final",
    )(dest, x, ybuf, gates, mods, g_final.reshape(1, d))


def _rope_tables(s_len, c_len):
    quarter = QK_ROPE_DIM // 4
    tok = jnp.arange(s_len + c_len, dtype=jnp.int32)[:, None]
    lane = jnp.arange(LANES, dtype=jnp.int32)[None, :]
    group = lane // quarter
    pos = jnp.where(group < 2, tok // GRID_W, tok % GRID_W).astype(F32)
    inv_freq = jnp.power(ROPE_THETA, -(2 * (lane % quarter)).astype(F32) / (QK_ROPE_DIM // 2))
    ang = jnp.where(tok < s_len, pos * inv_freq, 0.0)
    live = group < 4
    cos = jnp.where(live, jnp.cos(ang), 0.0)
    sin = jnp.where(live, jnp.where(group % 2 == 0, -jnp.sin(ang), jnp.sin(ang)), 0.0)
    return cos, sin


def _rope_swap_perm():
    q = QK_ROPE_DIM // 4
    return jnp.concatenate([jnp.arange(q, 2 * q), jnp.arange(0, q), jnp.arange(3 * q, 4 * q), jnp.arange(2 * q, 3 * q)])


def _layer_weights(w_in_t, w_qb_all, w_kvb_all, li):
    d = w_in_t.shape[2]
    perm = _rope_swap_perm()
    kr_lo = Q_LORA_RANK + KV_LORA_RANK
    kr_hi = kr_lo + QK_ROPE_DIM
    w_kr = w_in_t[li, kr_lo:kr_hi]
    zpad = jnp.zeros((LANES - QK_ROPE_DIM, d), w_in_t.dtype)
    w_a = jnp.concatenate([w_in_t[li, :kr_lo], w_kr, zpad, w_kr[perm], zpad], axis=0)
    w_qb, w_kvb = w_qb_all[li], w_kvb_all[li]
    qb = w_qb.reshape(Q_LORA_RANK, N_HEADS, QK_NOPE_DIM + QK_ROPE_DIM)
    q_rope = qb[:, :, QK_NOPE_DIM:]
    hpad = jnp.zeros((Q_LORA_RANK, N_HEADS, LANES - QK_ROPE_DIM), w_qb.dtype)
    w_q_main = jnp.concatenate([qb, hpad], axis=2).reshape(Q_LORA_RANK, N_HEADS * HEAD_W)
    w_q_swap = jnp.concatenate([q_rope[:, :, perm], hpad], axis=2).reshape(Q_LORA_RANK, N_HEADS * LANES)
    kvb = w_kvb.reshape(KV_LORA_RANK, N_HEADS, QK_NOPE_DIM + V_HEAD_DIM)
    w_k = kvb[:, :, :QK_NOPE_DIM].reshape(KV_LORA_RANK, N_HEADS * QK_NOPE_DIM)
    w_v = kvb[:, :, QK_NOPE_DIM:].reshape(KV_LORA_RANK, N_HEADS * V_HEAD_DIM)
    return tuple(w.astype(BF16) for w in (w_a, w_q_main, w_q_swap, w_k, w_v))


def _moe_slots(idx, n_blocks):
    e_flat = idx.reshape(-1)
    onehot = (e_flat[:, None] == jnp.arange(N_EXPERTS)[None, :]).astype(jnp.int32)
    csum = jnp.cumsum(onehot, axis=0)
    counts = csum[-1]
    rank = jnp.sum((csum - onehot) * onehot, axis=1)
    padded = (counts + MOE_ROWS - 1) // MOE_ROWS * MOE_ROWS
    p_end = jnp.cumsum(padded)
    p_start = p_end - padded
    dest = jnp.sum(onehot * p_start[None, :], axis=1) + rank
    blk_lo = jnp.arange(n_blocks) * MOE_ROWS
    block_expert = jnp.minimum(jnp.sum((blk_lo[:, None] >= p_end[None, :]).astype(jnp.int32), axis=1),
                               N_EXPERTS - 1)
    blocks_in_use = p_end[-1:] // MOE_ROWS
    return dest.astype(jnp.int32), jnp.concatenate([block_expert, blocks_in_use]).astype(jnp.int32)


def kernel(x, c, ctx, c_ctx, w_ada, b_ada, g_attn, w_in, g_qa, w_qb, g_kva, w_kvb, w_conv, w_oa, w_ob, w_o,
           g_ffn, w_gate_dense, w_up_dense, w_down_dense, w_router, w_gate_exp, w_up_exp, w_down_exp, g_final):
    _, s_len, d = x.shape
    c_len = ctx.shape[1]
    m_len = s_len + c_len
    depth = w_in.shape[0]
    width = w_conv.shape[2]
    assert depth == 2, "supported stack: dense-FFN layer with context updates, then a final expert-FFN layer"

    mods_all = _ada_mod(c, c_ctx, w_ada, b_ada)
    cos_t, sin_t = _rope_tables(s_len, c_len)
    w_in_t = jnp.swapaxes(w_in, 1, 2)
    xa = jnp.concatenate([x[0], ctx[0]], axis=0)

    for li in range(depth):
        last = li == depth - 1
        mods = mods_all[li]
        w_a, w_q_main, w_q_swap, w_k, w_v = _layer_weights(w_in_t, w_qb, w_kvb, li)
        g_a = jnp.concatenate([g_qa[li], g_kva[li]]).reshape(1, -1)
        rows = s_len if last else m_len

        h = _norm_modulate(xa, g_attn[li], mods, 0, 1, s_len, BF16)
        qa_n, ckv_n, kr = _in_proj_a(h, w_a, g_a, cos_t, sin_t)
        p = _in_proj_b(h, w_in_t, li, Q_LORA_RANK + KV_LORA_RANK + QK_ROPE_DIM, 3 * width)
        q = _q_proj(qa_n, w_q_main, w_q_swap, cos_t, sin_t, rows)
        k, v = _kv_proj(ckv_n, kr, w_k, w_v)
        attn = _attention(q, k, v, 0, s_len, 0, m_len)
        if not last:
            attn = jnp.concatenate([attn, _attention(q, k, v, s_len, c_len, s_len, c_len)], axis=0)
        z = _conv_gate(p, w_conv[li], rows, s_len, m_len)
        merged = _merge(attn, z, p, w_oa[li].astype(BF16), w_ob[li].astype(BF16), rows)
        xa = _matmul_residual(merged, w_o, li, xa, mods, 2, rows, s_len)

        j = li // 2
        if li % 2 == 0:
            h2 = _norm_modulate(xa, g_ffn[li], mods, 3, 4, s_len, BF16)
            tm = _tile(rows, 768, 640, 512, 256, 128)
            eid = jnp.zeros((rows // tm + 1,), jnp.int32).at[-1].set(rows // tm)
            hid = _glu(h2, w_gate_dense[j][None], w_up_dense[j][None], eid, tm)
            xa = _matmul_residual(hid, w_down_dense.astype(BF16), j, xa, mods, 5, rows, s_len)
        else:
            h2, idx, gates = _norm_modulate_route(xa, g_ffn[li], mods, 3, 4, w_router[j])
            n_assign = rows * TOP_K
            n_blocks = -(-n_assign // MOE_ROWS) + N_EXPERTS
            dest, block_expert = _moe_slots(idx[:, :TOP_K], n_blocks)
            tok = jnp.arange(n_assign, dtype=jnp.int32) // TOP_K
            slot_tok = jnp.zeros((n_blocks * MOE_ROWS,), jnp.int32).at[dest].set(tok, unique_indices=True)
            buf = _gather_rows(h2, slot_tok, MOE_ROWS)
            hid = _glu(buf, w_gate_exp[j], w_up_exp[j], block_expert, MOE_ROWS)
            ybuf = _down_grouped(hid, w_down_exp[j].astype(BF16), block_expert, MOE_ROWS)
            return _combine_final(xa, ybuf, dest, gates, mods, 5, g_final)[None]
    raise AssertionError("unreachable: the final layer returns")
```

```python
import functools

import jax
import jax.numpy as jnp
from jax import lax
from jax.experimental import pallas as pl
from jax.experimental.pallas import tpu as pltpu

F32 = jnp.float32
BF16 = jnp.bfloat16

N_HEADS = 16
QK_NOPE_DIM = 128
QK_ROPE_DIM = 64
V_HEAD_DIM = 128
Q_LORA_RANK = 512
KV_LORA_RANK = 512
GRID_W = 64
ROPE_THETA = 10000.0
ATTN_SCALE = (QK_NOPE_DIM + QK_ROPE_DIM) ** -0.5
Q_SCALE = ATTN_SCALE * 1.4426950408889634
CONV_K = 3
N_EXPERTS = 8
TOP_K = 2
NORM_EPS = 1e-6

LANES = 128
SUBLANES = 8
HEAD_W = 2 * LANES
VMEM_LIMIT_BYTES = 56 * 1024 * 1024

MOE_ROWS = 256
DMA_ISSUE_UNROLL = 8


def _tile(n, *cands):
    for c in cands:
        if n % c == 0:
            return c
    return n


def _cparams(n_axes):
    return pltpu.CompilerParams(dimension_semantics=("arbitrary",) * n_axes,
                                vmem_limit_bytes=VMEM_LIMIT_BYTES)


def _dot(a, b):
    return jnp.dot(a, b, preferred_element_type=F32)


def _dot_t(a, bt):
    return lax.dot_general(a, bt, (((1,), (1,)), ((), ())), preferred_element_type=F32)


def _bf16(w):
    return w if w.dtype == BF16 else w.astype(BF16)


def _ada_kernel(xt_ref, w_ref, b_ref, o_ref, *, k_chunk):
    d = xt_ref.shape[0]
    tn = o_ref.shape[-1]

    def body(k, acc):
        a0, a1 = acc
        ks = pl.multiple_of(k * k_chunk, k_chunk)
        xt = xt_ref[pl.ds(ks, k_chunk), :]
        s = xt * jax.nn.sigmoid(xt)
        w = w_ref[0, pl.ds(ks, k_chunk), :]
        a0 = a0 + jnp.sum(w * s[:, 0:1], axis=0, keepdims=True)
        a1 = a1 + jnp.sum(w * s[:, 1:2], axis=0, keepdims=True)
        return a0, a1

    z = jnp.zeros((1, tn), F32)
    a0, a1 = lax.fori_loop(0, d // k_chunk, body, (z, z))
    o_ref[0, 0:1, :] = a0 + b_ref[0]
    o_ref[0, 1:2, :] = a1 + b_ref[0]


def _ada_mod(c, c_ctx, w_ada, b_ada):
    depth, d, n = w_ada.shape
    xt = jnp.stack([c[0], c_ctx], axis=1)
    tn = _tile(n, 1024, 512, LANES)
    k_chunk = _tile(d, 256, SUBLANES)
    return pl.pallas_call(
        functools.partial(_ada_kernel, k_chunk=k_chunk),
        grid=(depth, n // tn),
        in_specs=[pl.BlockSpec((d, 2), lambda l, j: (0, 0)),
                  pl.BlockSpec((1, d, tn), lambda l, j: (l, 0, j)),
                  pl.BlockSpec((1, 1, tn), lambda l, j: (l, 0, j))],
        out_specs=pl.BlockSpec((1, 2, tn), lambda l, j: (l, 0, j)),
        out_shape=jax.ShapeDtypeStruct((depth, 2, n), F32),
        compiler_params=_cparams(2),
        name="ada_mod",
    )(xt, w_ada, b_ada.reshape(depth, 1, n))


def _norm_mod(x, g_ref, sh_ref, sc_ref, is_ctx):
    y = x * lax.rsqrt(jnp.mean(x * x, axis=-1, keepdims=True) + NORM_EPS) * g_ref[...]
    sh = jnp.where(is_ctx, sh_ref[1:2, :], sh_ref[0:1, :])
    sc = jnp.where(is_ctx, sc_ref[1:2, :], sc_ref[0:1, :])
    return y * (1.0 + sc) + sh


def _norm_kernel(x_ref, g_ref, sh_ref, sc_ref, o_ref, *, n_lat_tiles):
    is_ctx = pl.program_id(0) >= n_lat_tiles
    o_ref[...] = _norm_mod(x_ref[...], g_ref, sh_ref, sc_ref, is_ctx).astype(o_ref.dtype)


def _norm_router_kernel(x_ref, g_ref, sh_ref, sc_ref, wr_ref, h_ref, idx_ref, gate_ref):
    h = _norm_mod(x_ref[...], g_ref, sh_ref, sc_ref, False)
    h_ref[...] = h
    logits = jnp.dot(h, wr_ref[...], preferred_element_type=F32, precision=lax.Precision.HIGHEST)
    lane = lax.broadcasted_iota(jnp.int32, logits.shape, 1).astype(F32)
    neg = jnp.float32(-jnp.inf)
    l1 = jnp.where(lane < N_EXPERTS, logits, neg)
    v1 = jnp.max(l1, axis=-1, keepdims=True)
    i1 = jnp.min(jnp.where(l1 == v1, lane, float(LANES)), axis=-1, keepdims=True)
    l2 = jnp.where(lane == i1, neg, l1)
    v2 = jnp.max(l2, axis=-1, keepdims=True)
    i2 = jnp.min(jnp.where(l2 == v2, lane, float(LANES)), axis=-1, keepdims=True)
    e = jnp.exp(v2 - v1)
    g1 = 1.0 / (1.0 + e)
    g2 = e / (1.0 + e)
    idx_ref[...] = jnp.where(lane == 0, i1, jnp.where(lane == 1, i2, 0.0)).astype(jnp.int32)
    gate_ref[...] = jnp.where(lane == 0, g1, jnp.where(lane == 1, g2, 0.0))


def _norm_modulate(x, g, mods, sh_blk, sc_blk, n_lat, out_dtype):
    r, d = x.shape
    tm = _tile(n_lat, 256, 128, 64, SUBLANES) if r > n_lat else _tile(r, 256, 128, 64, SUBLANES)
    if r > n_lat:
        tm = _tile(r - n_lat, tm, 128, 64, SUBLANES)
    return pl.pallas_call(
        functools.partial(_norm_kernel, n_lat_tiles=n_lat // tm),
        grid=(r // tm,),
        in_specs=[pl.BlockSpec((tm, d), lambda i: (i, 0)),
                  pl.BlockSpec((1, d), lambda i: (0, 0)),
                  pl.BlockSpec((2, d), lambda i: (0, sh_blk)),
                  pl.BlockSpec((2, d), lambda i: (0, sc_blk))],
        out_specs=pl.BlockSpec((tm, d), lambda i: (i, 0)),
        out_shape=jax.ShapeDtypeStruct((r, d), out_dtype),
        compiler_params=_cparams(1),
        name="norm_modulate",
    )(x, g.reshape(1, d), mods, mods)


def _norm_modulate_route(x, g, mods, sh_blk, sc_blk, w_router):
    r, d = x.shape
    tm = _tile(r, 512, 256, 128, 64, SUBLANES)
    wr = jnp.pad(w_router, ((0, 0), (0, LANES - N_EXPERTS)))
    return pl.pallas_call(
        _norm_router_kernel,
        grid=(r // tm,),
        in_specs=[pl.BlockSpec((tm, d), lambda i: (i, 0)),
                  pl.BlockSpec((1, d), lambda i: (0, 0)),
                  pl.BlockSpec((2, d), lambda i: (0, sh_blk)),
                  pl.BlockSpec((2, d), lambda i: (0, sc_blk)),
                  pl.BlockSpec((d, LANES), lambda i: (0, 0))],
        out_specs=[pl.BlockSpec((tm, d), lambda i: (i, 0)),
                   pl.BlockSpec((tm, LANES), lambda i: (i, 0)),
                   pl.BlockSpec((tm, LANES), lambda i: (i, 0))],
        out_shape=[jax.ShapeDtypeStruct((r, d), F32),
                   jax.ShapeDtypeStruct((r, LANES), jnp.int32),
                   jax.ShapeDtypeStruct((r, LANES), F32)],
        compiler_params=_cparams(1),
        name="norm_modulate_route",
    )(x, g.reshape(1, d), mods, mods, wr)


def _in_a_kernel(h_ref, wt_ref, wkr_ref, g_ref, cos_ref, sin_ref, qa_ref, ckv_ref, kr_ref):
    h = h_ref[...]
    acc = _dot_t(h, _bf16(wt_ref[...]))
    acc_kr = _dot_t(h, wkr_ref[...])

    def rms(v, g):
        return v * lax.rsqrt(jnp.mean(v * v, axis=-1, keepdims=True) + NORM_EPS) * g

    q1 = Q_LORA_RANK
    c1 = q1 + KV_LORA_RANK
    qa_ref[...] = rms(acc[:, :q1], g_ref[:, :q1]).astype(qa_ref.dtype)
    ckv_ref[...] = rms(acc[:, q1:c1], g_ref[:, q1:c1]).astype(ckv_ref.dtype)
    kr = acc_kr[:, :LANES] * cos_ref[...] + acc_kr[:, LANES:] * sin_ref[...]
    kr_ref[...] = kr.astype(kr_ref.dtype)


def _in_proj_a(h, w_in_t, li, w_kr, g_a, cos_t, sin_t):
    m, d = h.shape
    n = Q_LORA_RANK + KV_LORA_RANK
    base = li * w_in_t.shape[1]
    assert base % SUBLANES == 0
    tm = _tile(m, 768, 640, 512, 256, 128)
    return pl.pallas_call(
        _in_a_kernel,
        grid=(m // tm,),
        in_specs=[pl.BlockSpec((tm, d), lambda i: (i, 0)),
                  pl.BlockSpec((pl.Element(n), pl.Element(d)), lambda i: (base, 0)),
                  pl.BlockSpec(w_kr.shape, lambda i: (0, 0)),
                  pl.BlockSpec((1, n), lambda i: (0, 0)),
                  pl.BlockSpec((tm, LANES), lambda i: (i, 0)),
                  pl.BlockSpec((tm, LANES), lambda i: (i, 0))],
        out_specs=[pl.BlockSpec((tm, Q_LORA_RANK), lambda i: (i, 0)),
                   pl.BlockSpec((tm, KV_LORA_RANK), lambda i: (i, 0)),
                   pl.BlockSpec((tm, LANES), lambda i: (i, 0))],
        out_shape=[jax.ShapeDtypeStruct((m, Q_LORA_RANK), BF16),
                   jax.ShapeDtypeStruct((m, KV_LORA_RANK), BF16),
                   jax.ShapeDtypeStruct((m, LANES), BF16)],
        compiler_params=_cparams(1),
        name="in_proj_a",
    )(h, w_in_t.reshape(-1, d), w_kr, g_a, cos_t, sin_t)


def _in_b_kernel(h_ref, wt_ref, o_ref, *, first_gate_tile):
    acc = _dot_t(h_ref[...], _bf16(wt_ref[...]))
    is_gate = pl.program_id(0) >= first_gate_tile

    @pl.when(is_gate)
    def _():
        o_ref[...] = jax.nn.sigmoid(acc).astype(o_ref.dtype)

    @pl.when(jnp.logical_not(is_gate))
    def _():
        o_ref[...] = acc.astype(o_ref.dtype)


def _in_proj_b(h, w_in_t, li, row0, first_gate_col):
    m, d = h.shape
    n = w_in_t.shape[1] - row0
    tm = _tile(m, 1408, 768, 640, 512, 256, 128)
    tn = _tile(first_gate_col, 1024, 512, 256, LANES)
    base = li * w_in_t.shape[1] + row0
    assert base % SUBLANES == 0 and n % tn == 0
    return pl.pallas_call(
        functools.partial(_in_b_kernel, first_gate_tile=first_gate_col // tn),
        grid=(n // tn, m // tm),
        in_specs=[pl.BlockSpec((tm, d), lambda j, i: (i, 0)),
                  pl.BlockSpec((pl.Element(tn), pl.Element(d)),
                               lambda j, i: (pl.multiple_of(base + j * tn, SUBLANES), 0))],
        out_specs=pl.BlockSpec((tm, tn), lambda j, i: (i, j)),
        out_shape=jax.ShapeDtypeStruct((m, n), BF16),
        compiler_params=_cparams(2),
        name="in_proj_b",
    )(h, w_in_t.reshape(-1, d))


def _q_kernel(a_ref, wm_ref, ws_ref, cos_ref, sin_ref, q_ref):
    a = a_ref[...]
    cos = cos_ref[...]
    sin = sin_ref[...]
    for hd in range(N_HEADS):
        main = _dot(a, wm_ref[:, hd * HEAD_W:(hd + 1) * HEAD_W])
        swap = _dot(a, ws_ref[:, hd * LANES:(hd + 1) * LANES])
        q_ref[:, hd * HEAD_W:hd * HEAD_W + LANES] = (main[:, :LANES] * Q_SCALE).astype(q_ref.dtype)
        rot = (main[:, LANES:] * cos + swap * sin) * Q_SCALE
        q_ref[:, hd * HEAD_W + LANES:(hd + 1) * HEAD_W] = rot.astype(q_ref.dtype)


def _q_proj(qa_n, w_main, w_swap, cos_t, sin_t, rows):
    k = qa_n.shape[1]
    tm = _tile(rows, 768, 640, 512, 256, 128)
    return pl.pallas_call(
        _q_kernel,
        grid=(rows // tm,),
        in_specs=[pl.BlockSpec((tm, k), lambda i: (i, 0)),
                  pl.BlockSpec(w_main.shape, lambda i: (0, 0)),
                  pl.BlockSpec(w_swap.shape, lambda i: (0, 0)),
                  pl.BlockSpec((tm, LANES), lambda i: (i, 0)),
                  pl.BlockSpec((tm, LANES), lambda i: (i, 0))],
        out_specs=pl.BlockSpec((tm, N_HEADS * HEAD_W), lambda i: (i, 0)),
        out_shape=jax.ShapeDtypeStruct((rows, N_HEADS * HEAD_W), BF16),
        compiler_params=_cparams(1),
        name="q_proj",
    )(qa_n, w_main, w_swap, cos_t, sin_t)


def _kv_kernel(c_ref, kr_ref, wk_ref, wv_ref, k_ref, v_ref):
    c = c_ref[...]
    kr = kr_ref[...]
    lane = lax.broadcasted_iota(jnp.int32, (c.shape[0], LANES), 1)
    ones_col = jnp.where(lane == 0, 1.0, 0.0).astype(v_ref.dtype)
    for hd in range(N_HEADS):
        kn = _dot(c, wk_ref[:, hd * LANES:(hd + 1) * LANES])
        k_ref[:, hd * HEAD_W:hd * HEAD_W + LANES] = kn.astype(k_ref.dtype)
        k_ref[:, hd * HEAD_W + LANES:(hd + 1) * HEAD_W] = kr
        vh = _dot(c, wv_ref[:, hd * LANES:(hd + 1) * LANES])
        v_ref[:, hd * HEAD_W:hd * HEAD_W + LANES] = vh.astype(v_ref.dtype)
        v_ref[:, hd * HEAD_W + LANES:(hd + 1) * HEAD_W] = ones_col


def _kv_proj(ckv_n, kr, w_k, w_v):
    m, k = ckv_n.shape
    tm = _tile(m, 768, 640, 512, 256, 128)
    return pl.pallas_call(
        _kv_kernel,
        grid=(m // tm,),
        in_specs=[pl.BlockSpec((tm, k), lambda i: (i, 0)),
                  pl.BlockSpec((tm, LANES), lambda i: (i, 0)),
                  pl.BlockSpec(w_k.shape, lambda i: (0, 0)),
                  pl.BlockSpec(w_v.shape, lambda i: (0, 0))],
        out_specs=[pl.BlockSpec((tm, N_HEADS * HEAD_W), lambda i: (i, 0)),
                   pl.BlockSpec((tm, N_HEADS * HEAD_W), lambda i: (i, 0))],
        out_shape=[jax.ShapeDtypeStruct((m, N_HEADS * HEAD_W), BF16),
                   jax.ShapeDtypeStruct((m, N_HEADS * HEAD_W), BF16)],
        compiler_params=_cparams(1),
        name="kv_proj",
    )(ckv_n, kr, w_k, w_v)


def _attn_kernel(q_ref, k_ref, v_ref, o_ref, sa_ref, sb_ref, *, tk, n_chunks):
    q = q_ref[...]
    tq = q.shape[0]

    def scores(c):
        return _dot_t(q, k_ref[c * tk:(c + 1) * tk, :])

    def absorb(s_ref, c, carry):
        m, acc = carry
        s = s_ref[...]
        m_new = jnp.maximum(m, jnp.max(s, axis=-1, keepdims=True))
        p = jnp.exp2(s - m_new).astype(v_ref.dtype)
        acc = jnp.exp2(m - m_new) * acc + _dot(p, v_ref[c * tk:(c + 1) * tk, :])
        return m_new, acc

    s_refs = (sa_ref, sb_ref)
    carry = (jnp.full((tq, 1), -jnp.inf, F32), jnp.zeros((tq, HEAD_W), F32))
    sa_ref[...] = scores(0)
    for c in range(n_chunks):
        if c + 1 < n_chunks:
            s_refs[(c + 1) % 2][...] = scores(c + 1)
        carry = absorb(s_refs[c % 2], c, carry)
    _, acc = carry
    o_ref[...] = (acc[:, :V_HEAD_DIM] / acc[:, V_HEAD_DIM:V_HEAD_DIM + 1]).astype(o_ref.dtype)


def _attention(q, k, v, q_row0, n_q, k_row0, n_k):
    tq = _tile(n_q, 512, 256, 128)
    tk = _tile(n_k, 768, 640, 512, 384, 256, 128)
    assert q_row0 % tq == 0 and k_row0 % n_k == 0
    q_blk0 = q_row0 // tq
    k_blk = k_row0 // n_k
    return pl.pallas_call(
        functools.partial(_attn_kernel, tk=tk, n_chunks=n_k // tk),
        grid=(N_HEADS, n_q // tq),
        in_specs=[pl.BlockSpec((tq, HEAD_W), lambda h, i: (q_blk0 + i, h)),
                  pl.BlockSpec((n_k, HEAD_W), lambda h, i: (k_blk, h)),
                  pl.BlockSpec((n_k, HEAD_W), lambda h, i: (k_blk, h))],
        out_specs=pl.BlockSpec((tq, V_HEAD_DIM), lambda h, i: (i, h)),
        out_shape=jax.ShapeDtypeStruct((n_q, N_HEADS * V_HEAD_DIM), BF16),
        scratch_shapes=[pltpu.VMEM((tq, tk), F32), pltpu.VMEM((tq, tk), F32)],
        compiler_params=_cparams(2),
        name="attention",
    )(q, k, v)


def _conv_gate_kernel(cx_ref, cb_ref, cc_ref, cxp_ref, ccp_ref, cxn_ref, ccn_ref, wc_ref, z_ref, *,
                      seg_starts, seg_ends, col_chunk):
    tm, width = z_ref.shape
    loc = lax.broadcasted_iota(jnp.int32, (tm, 1), 0)
    row = loc + pl.program_id(0) * tm
    first = functools.reduce(jnp.logical_or, [row == r for r in seg_starts])
    last = functools.reduce(jnp.logical_or, [row == r for r in seg_ends])
    for c0 in range(0, width, col_chunk):
        cs = slice(c0, c0 + col_chunk)
        u = cx_ref[:, cs].astype(F32) * cc_ref[:, cs].astype(F32)
        u_halo_prev = (cxp_ref[SUBLANES - 1:SUBLANES, cs].astype(F32)
                       * ccp_ref[SUBLANES - 1:SUBLANES, cs].astype(F32))
        u_halo_next = cxn_ref[0:1, cs].astype(F32) * ccn_ref[0:1, cs].astype(F32)
        u_prev = jnp.where(loc == 0, u_halo_prev, pltpu.roll(u, 1, 0))
        u_prev = jnp.where(first, 0.0, u_prev)
        u_next = jnp.where(loc == tm - 1, u_halo_next, pltpu.roll(u, tm - 1, 0))
        u_next = jnp.where(last, 0.0, u_next)
        conv = wc_ref[0:1, cs] * u_prev + wc_ref[1:2, cs] * u + wc_ref[2:3, cs] * u_next
        z_ref[:, cs] = (cb_ref[:, cs].astype(F32) * conv).astype(z_ref.dtype)


def _conv_gate(p, w_conv, rows, s_len, m_len):
    width = w_conv.shape[1]
    tm = _tile(rows, 768, 512, 384, 256, 128, 64)
    hb = tm // SUBLANES
    last_hb = p.shape[0] // SUBLANES - 1

    def prev_map(col):
        return lambda i: (jnp.maximum(i * hb - 1, 0), col)

    def next_map(col):
        return lambda i: (jnp.minimum((i + 1) * hb, last_hb), col)

    kern = functools.partial(_conv_gate_kernel, seg_starts=(0, s_len), seg_ends=(s_len - 1, m_len - 1),
                             col_chunk=_tile(width, 512, LANES))
    return pl.pallas_call(
        kern,
        grid=(rows // tm,),
        in_specs=[pl.BlockSpec((tm, width), lambda i: (i, 0)),
                  pl.BlockSpec((tm, width), lambda i: (i, 1)),
                  pl.BlockSpec((tm, width), lambda i: (i, 2)),
                  pl.BlockSpec((SUBLANES, width), prev_map(0)),
                  pl.BlockSpec((SUBLANES, width), prev_map(2)),
                  pl.BlockSpec((SUBLANES, width), next_map(0)),
                  pl.BlockSpec((SUBLANES, width), next_map(2)),
                  pl.BlockSpec((CONV_K, width), lambda i: (0, 0))],
        out_specs=pl.BlockSpec((tm, width), lambda i: (i, 0)),
        out_shape=jax.ShapeDtypeStruct((rows, width), BF16),
        compiler_params=_cparams(1),
        name="conv_gate",
    )(p, p, p, p, p, p, p, w_conv)


def _merge_kernel(attn_ref, z_ref, sga_ref, sgb_ref, woa_ref, wob_ref, o_ref):
    o_a = _dot(attn_ref[...], woa_ref[...])
    o_b = _dot(z_ref[...], wob_ref[...])
    o_ref[...] = (sga_ref[...].astype(F32) * o_a + sgb_ref[...].astype(F32) * o_b).astype(o_ref.dtype)


def _merge(attn, z, p, w_oa, w_ob, layer, rows):
    attn_w = w_oa.shape[1]
    _, width, d = w_ob.shape
    tm = _tile(rows, 768, 512, 384, 256, 128, 64)
    tn = _tile(d, 1024, 512, 256, LANES)
    ga_blk = 3 * width // tn
    gb_blk = (3 * width + d) // tn
    return pl.pallas_call(
        _merge_kernel,
        grid=(d // tn, rows // tm),
        in_specs=[pl.BlockSpec((tm, attn_w), lambda j, i: (i, 0)),
                  pl.BlockSpec((tm, width), lambda j, i: (i, 0)),
                  pl.BlockSpec((tm, tn), lambda j, i: (i, ga_blk + j)),
                  pl.BlockSpec((tm, tn), lambda j, i: (i, gb_blk + j)),
                  pl.BlockSpec((None, attn_w, tn), lambda j, i: (layer, 0, j)),
                  pl.BlockSpec((None, width, tn), lambda j, i: (layer, 0, j))],
        out_specs=pl.BlockSpec((tm, tn), lambda j, i: (i, j)),
        out_shape=jax.ShapeDtypeStruct((rows, d), BF16),
        compiler_params=_cparams(2),
        name="merge",
    )(attn, z, p, p, w_oa, w_ob)


def _mm_res_kernel(a_ref, w_ref, x_ref, gt_ref, o_ref, *, n_lat):
    tm = a_ref.shape[0]
    row = lax.broadcasted_iota(jnp.int32, (tm, 1), 0) + pl.program_id(1) * tm
    gate = jnp.where(row >= n_lat, gt_ref[1:2, :], gt_ref[0:1, :])
    o_ref[...] = x_ref[...] + gate * _dot(a_ref[...], _bf16(w_ref[...]))


def _matmul_residual(a, w, layer, x, mods, gate_blk, rows, n_lat):
    _, k, n = w.shape
    tm = _tile(rows, 768, 640, 512, 256, 128)
    tn = _tile(n, 512, 256, LANES)
    nt = n // tn
    return pl.pallas_call(
        functools.partial(_mm_res_kernel, n_lat=n_lat),
        grid=(nt, rows // tm),
        in_specs=[pl.BlockSpec((tm, k), lambda j, i: (i, 0)),
                  pl.BlockSpec((None, k, tn), lambda j, i: (layer, 0, j)),
                  pl.BlockSpec((tm, tn), lambda j, i: (i, j)),
                  pl.BlockSpec((2, tn), lambda j, i: (0, gate_blk * nt + j))],
        out_specs=pl.BlockSpec((tm, tn), lambda j, i: (i, j)),
        out_shape=jax.ShapeDtypeStruct((rows, n), F32),
        compiler_params=_cparams(2),
        name="matmul_residual",
    )(a, w, x, mods)


def _block_in_use(eid_ref):
    return pl.program_id(1) < eid_ref[pl.num_programs(1)]


def _glu_kernel(eid_ref, a_ref, wg_ref, wu_ref, o_ref):
    @pl.when(_block_in_use(eid_ref))
    def _():
        a = _bf16(a_ref[...])
        gate = _dot(a, _bf16(wg_ref[0]))
        up = _dot(a, _bf16(wu_ref[0]))
        o_ref[...] = (gate * jax.nn.sigmoid(gate) * up).astype(o_ref.dtype)

    @pl.when(jnp.logical_not(_block_in_use(eid_ref)))
    def _():
        o_ref[...] = jnp.zeros_like(o_ref)


def _glu(a, w_gate, w_up, eid, tm):
    r, k = a.shape
    f = w_gate.shape[2]
    tn = _tile(f, 512, 1408, 256, LANES)
    grid_spec = pltpu.PrefetchScalarGridSpec(
        num_scalar_prefetch=1,
        grid=(f // tn, r // tm),
        in_specs=[pl.BlockSpec((tm, k), lambda j, i, e: (i, 0)),
                  pl.BlockSpec((1, k, tn), lambda j, i, e: (e[i], 0, j)),
                  pl.BlockSpec((1, k, tn), lambda j, i, e: (e[i], 0, j))],
        out_specs=pl.BlockSpec((tm, tn), lambda j, i, e: (i, j)),
    )
    return pl.pallas_call(
        _glu_kernel,
        grid_spec=grid_spec,
        out_shape=jax.ShapeDtypeStruct((r, f), BF16),
        compiler_params=_cparams(2),
        name="glu",
    )(eid, a, w_gate, w_up)


def _down_kernel(eid_ref, a_ref, w_ref, o_ref):
    @pl.when(_block_in_use(eid_ref))
    def _():
        o_ref[...] = _dot(a_ref[...], w_ref[0])

    @pl.when(jnp.logical_not(_block_in_use(eid_ref)))
    def _():
        o_ref[...] = jnp.zeros_like(o_ref)


def _down_grouped(a, w_down, eid, tm):
    r, f = a.shape
    d = w_down.shape[2]
    grid_spec = pltpu.PrefetchScalarGridSpec(
        num_scalar_prefetch=1,
        grid=(1, r // tm),
        in_specs=[pl.BlockSpec((tm, f), lambda j, i, e: (i, 0)),
                  pl.BlockSpec((1, f, d), lambda j, i, e: (e[i], 0, 0))],
        out_specs=pl.BlockSpec((tm, d), lambda j, i, e: (i, 0)),
    )
    return pl.pallas_call(
        _down_kernel,
        grid_spec=grid_spec,
        out_shape=jax.ShapeDtypeStruct((r, d), F32),
        compiler_params=_cparams(2),
        name="down_grouped",
    )(eid, a, w_down)


def _row_copies(idx_ref, idx_base, idx_stride, src_ref, dst_ref, sem, n_rows, start):
    def body(r2, c):
        for prio in range(2):
            r = 2 * r2 + prio
            row = idx_ref[idx_base + r * idx_stride]
            cp = pltpu.make_async_copy(src_ref.at[pl.ds(row, 1)], dst_ref.at[pl.ds(r, 1)], sem)
            if start:
                cp.start(priority=prio)
            else:
                cp.wait()
        return c

    assert n_rows % 2 == 0
    lax.fori_loop(0, n_rows // 2, body, 0, unroll=DMA_ISSUE_UNROLL // 2)


def _prefetched_gather(copies):
    i = pl.program_id(0)
    slot = i % 2

    @pl.when(i == 0)
    def _():
        copies(0, 0, True)

    @pl.when(i + 1 < pl.num_programs(0))
    def _():
        copies(i + 1, 1 - slot, True)

    copies(i, slot, False)
    return slot


def _gather_rows_kernel(idx_ref, src_ref, o_ref, g_ref, sem):
    tm = o_ref.shape[0]

    def copies(step, slot, start):
        _row_copies(idx_ref, step * tm, 1, src_ref, g_ref.at[slot], sem.at[slot], tm, start)

    slot = _prefetched_gather(copies)
    o_ref[...] = g_ref[slot]


def _gather_rows(src, idx, tm):
    n = idx.shape[0]
    w = src.shape[1]
    grid_spec = pltpu.PrefetchScalarGridSpec(
        num_scalar_prefetch=1,
        grid=(n // tm,),
        in_specs=[pl.BlockSpec(memory_space=pl.ANY)],
        out_specs=pl.BlockSpec((tm, w), lambda i, idx_ref: (i, 0)),
        scratch_shapes=[pltpu.VMEM((2, tm, w), src.dtype), pltpu.SemaphoreType.DMA((2,))],
    )
    return pl.pallas_call(
        _gather_rows_kernel,
        grid_spec=grid_spec,
        out_shape=jax.ShapeDtypeStruct((n, w), src.dtype),
        compiler_params=_cparams(1),
        name="gather_rows",
    )(idx, src)


def _final_kernel(dest_ref, x_ref, y_hbm_ref, gate_ref, gt_ref, g_ref, o_ref, y_ref, sem):
    tm = x_ref.shape[0]

    def copies(step, slot, start):
        for kk in range(TOP_K):
            _row_copies(dest_ref, step * tm * TOP_K + kk, TOP_K, y_hbm_ref, y_ref.at[slot, kk], sem.at[slot],
                        tm, start)

    slot = _prefetched_gather(copies)
    gates = gate_ref[...]
    y = gates[:, 0:1] * y_ref[slot, 0] + gates[:, 1:2] * y_ref[slot, 1]
    x = x_ref[...] + gt_ref[0:1, :] * y
    o_ref[...] = x * lax.rsqrt(jnp.mean(x * x, axis=-1, keepdims=True) + NORM_EPS) * g_ref[...]


def _combine_final(x, ybuf, dest, gates, mods, gate_blk, g_final):
    r, d = x.shape
    tm = _tile(r, 256, 128, 64, SUBLANES)
    grid_spec = pltpu.PrefetchScalarGridSpec(
        num_scalar_prefetch=1,
        grid=(r // tm,),
        in_specs=[pl.BlockSpec((tm, d), lambda i, dest_ref: (i, 0)),
                  pl.BlockSpec(memory_space=pl.ANY),
                  pl.BlockSpec((tm, LANES), lambda i, dest_ref: (i, 0)),
                  pl.BlockSpec((2, d), lambda i, dest_ref: (0, gate_blk)),
                  pl.BlockSpec((1, d), lambda i, dest_ref: (0, 0))],
        out_specs=pl.BlockSpec((tm, d), lambda i, dest_ref: (i, 0)),
        scratch_shapes=[pltpu.VMEM((2, TOP_K, tm, d), F32), pltpu.SemaphoreType.DMA((2,))],
    )
    return pl.pallas_call(
        _final_kernel,
        grid_spec=grid_spec,
        out_shape=jax.ShapeDtypeStruct((r, d), F32),
        compiler_params=_cparams(1),
        name="combine_final",
    )(dest, x, ybuf, gates, mods, g_final.reshape(1, d))


def _rope_tables(s_len, c_len):
    quarter = QK_ROPE_DIM // 4
    tok = jnp.arange(s_len + c_len, dtype=jnp.int32)[:, None]
    lane = jnp.arange(LANES, dtype=jnp.int32)[None, :]
    group = lane // quarter
    pos = jnp.where(group < 2, tok // GRID_W, tok % GRID_W).astype(F32)
    inv_freq = jnp.power(ROPE_THETA, -(2 * (lane % quarter)).astype(F32) / (QK_ROPE_DIM // 2))
    ang = jnp.where(tok < s_len, pos * inv_freq, 0.0)
    live = group < 4
    cos = jnp.where(live, jnp.cos(ang), 0.0)
    sin = jnp.where(live, jnp.where(group % 2 == 0, -jnp.sin(ang), jnp.sin(ang)), 0.0)
    return cos, sin


def _rope_swap_perm():
    q = QK_ROPE_DIM // 4
    return jnp.concatenate([jnp.arange(q, 2 * q), jnp.arange(0, q), jnp.arange(3 * q, 4 * q), jnp.arange(2 * q, 3 * q)])


def _layer_weights(w_in_t, w_qb_all, w_kvb_all, li):
    d = w_in_t.shape[2]
    perm = _rope_swap_perm()
    kr_lo = Q_LORA_RANK + KV_LORA_RANK
    kr_hi = kr_lo + QK_ROPE_DIM
    w_kr = w_in_t[li, kr_lo:kr_hi]
    zpad = jnp.zeros((LANES - QK_ROPE_DIM, d), w_in_t.dtype)
    w_a = jnp.concatenate([w_kr, zpad, w_kr[perm], zpad], axis=0)
    w_qb, w_kvb = w_qb_all[li], w_kvb_all[li]
    qb = w_qb.reshape(Q_LORA_RANK, N_HEADS, QK_NOPE_DIM + QK_ROPE_DIM)
    q_rope = qb[:, :, QK_NOPE_DIM:]
    hpad = jnp.zeros((Q_LORA_RANK, N_HEADS, LANES - QK_ROPE_DIM), w_qb.dtype)
    w_q_main = jnp.concatenate([qb, hpad], axis=2).reshape(Q_LORA_RANK, N_HEADS * HEAD_W)
    w_q_swap = jnp.concatenate([q_rope[:, :, perm], hpad], axis=2).reshape(Q_LORA_RANK, N_HEADS * LANES)
    kvb = w_kvb.reshape(KV_LORA_RANK, N_HEADS, QK_NOPE_DIM + V_HEAD_DIM)
    w_k = kvb[:, :, :QK_NOPE_DIM].reshape(KV_LORA_RANK, N_HEADS * QK_NOPE_DIM)
    w_v = kvb[:, :, QK_NOPE_DIM:].reshape(KV_LORA_RANK, N_HEADS * V_HEAD_DIM)
    return tuple(w.astype(BF16) for w in (w_a, w_q_main, w_q_swap, w_k, w_v))


def _moe_slots(idx, n_blocks):
    e_flat = idx.reshape(-1)
    onehot = (e_flat[:, None] == jnp.arange(N_EXPERTS)[None, :]).astype(jnp.int32)
    csum = jnp.cumsum(onehot, axis=0)
    counts = csum[-1]
    rank = jnp.sum((csum - onehot) * onehot, axis=1)
    padded = (counts + MOE_ROWS - 1) // MOE_ROWS * MOE_ROWS
    p_end = jnp.cumsum(padded)
    p_start = p_end - padded
    dest = jnp.sum(onehot * p_start[None, :], axis=1) + rank
    blk_lo = jnp.arange(n_blocks) * MOE_ROWS
    block_expert = jnp.minimum(jnp.sum((blk_lo[:, None] >= p_end[None, :]).astype(jnp.int32), axis=1),
                               N_EXPERTS - 1)
    blocks_in_use = p_end[-1:] // MOE_ROWS
    return dest.astype(jnp.int32), jnp.concatenate([block_expert, blocks_in_use]).astype(jnp.int32)


def kernel(x, c, ctx, c_ctx, w_ada, b_ada, g_attn, w_in, g_qa, w_qb, g_kva, w_kvb, w_conv, w_oa, w_ob, w_o,
           g_ffn, w_gate_dense, w_up_dense, w_down_dense, w_router, w_gate_exp, w_up_exp, w_down_exp, g_final):
    _, s_len, d = x.shape
    c_len = ctx.shape[1]
    m_len = s_len + c_len
    depth = w_in.shape[0]
    width = w_conv.shape[2]
    assert depth == 2, "supported stack: dense-FFN layer with context updates, then a final expert-FFN layer"

    mods_all = _ada_mod(c, c_ctx, w_ada, b_ada)
    cos_t, sin_t = _rope_tables(s_len, c_len)
    w_in_t = jnp.swapaxes(w_in, 1, 2)
    w_oa_bf, w_ob_bf = w_oa.astype(BF16), w_ob.astype(BF16)
    xa = jnp.concatenate([x[0], ctx[0]], axis=0)

    for li in range(depth):
        last = li == depth - 1
        mods = mods_all[li]
        w_a, w_q_main, w_q_swap, w_k, w_v = _layer_weights(w_in_t, w_qb, w_kvb, li)
        g_a = jnp.concatenate([g_qa[li], g_kva[li]]).reshape(1, -1)
        rows = s_len if last else m_len

        h = _norm_modulate(xa, g_attn[li], mods, 0, 1, s_len, BF16)
        qa_n, ckv_n, kr = _in_proj_a(h, w_in_t, li, w_a, g_a, cos_t, sin_t)
        p = _in_proj_b(h, w_in_t, li, Q_LORA_RANK + KV_LORA_RANK + QK_ROPE_DIM, 3 * width)
        q = _q_proj(qa_n, w_q_main, w_q_swap, cos_t, sin_t, rows)
        k, v = _kv_proj(ckv_n, kr, w_k, w_v)
        attn = _attention(q, k, v, 0, s_len, 0, m_len)
        if not last:
            attn = jnp.concatenate([attn, _attention(q, k, v, s_len, c_len, s_len, c_len)], axis=0)
        z = _conv_gate(p, w_conv[li], rows, s_len, m_len)
        merged = _merge(attn, z, p, w_oa_bf, w_ob_bf, li, rows)
        xa = _matmul_residual(merged, w_o, li, xa, mods, 2, rows, s_len)

        j = li // 2
        if li % 2 == 0:
            h2 = _norm_modulate(xa, g_ffn[li], mods, 3, 4, s_len, BF16)
            tm = _tile(rows, 768, 640, 512, 256, 128)
            eid = jnp.zeros((rows // tm + 1,), jnp.int32).at[-1].set(rows // tm)
            hid = _glu(h2, w_gate_dense[j][None], w_up_dense[j][None], eid, tm)
            xa = _matmul_residual(hid, w_down_dense.astype(BF16), j, xa, mods, 5, rows, s_len)
        else:
            h2, idx, gates = _norm_modulate_route(xa, g_ffn[li], mods, 3, 4, w_router[j])
            n_assign = rows * TOP_K
            n_blocks = -(-n_assign // MOE_ROWS) + N_EXPERTS
            dest, block_expert = _moe_slots(idx[:, :TOP_K], n_blocks)
            tok = jnp.arange(n_assign, dtype=jnp.int32) // TOP_K
            slot_tok = jnp.zeros((n_blocks * MOE_ROWS,), jnp.int32).at[dest].set(tok, unique_indices=True)
            buf = _gather_rows(h2, slot_tok, MOE_ROWS)
            hid = _glu(buf, w_gate_exp[j], w_up_exp[j], block_expert, MOE_ROWS)
            ybuf = _down_grouped(hid, w_down_exp[j].astype(BF16), block_expert, MOE_ROWS)
            return _combine_final(xa, ybuf, dest, gates, mods, 5, g_final)[None]
    raise AssertionError("unreachable: the final layer returns")
```

```python
import functools

import jax
import jax.numpy as jnp
from jax import lax
from jax.experimental import pallas as pl
from jax.experimental.pallas import tpu as pltpu

F32 = jnp.float32
BF16 = jnp.bfloat16

N_HEADS = 16
QK_NOPE_DIM = 128
QK_ROPE_DIM = 64
V_HEAD_DIM = 128
Q_LORA_RANK = 512
KV_LORA_RANK = 512
GRID_W = 64
ROPE_THETA = 10000.0
ATTN_SCALE = (QK_NOPE_DIM + QK_ROPE_DIM) ** -0.5
Q_SCALE = ATTN_SCALE * 1.4426950408889634
CONV_K = 3
N_EXPERTS = 8
TOP_K = 2
NORM_EPS = 1e-6

LANES = 128
SUBLANES = 8
HEAD_W = 2 * LANES
VMEM_LIMIT_BYTES = 56 * 1024 * 1024

MOE_ROWS = 256
DMA_ISSUE_UNROLL = 8


def _tile(n, *cands):
    for c in cands:
        if n % c == 0:
            return c
    return n


def _cparams(n_axes):
    return pltpu.CompilerParams(dimension_semantics=("arbitrary",) * n_axes,
                                vmem_limit_bytes=VMEM_LIMIT_BYTES)


def _dot(a, b):
    return jnp.dot(a, b, preferred_element_type=F32)


def _dot_t(a, bt):
    return lax.dot_general(a, bt, (((1,), (1,)), ((), ())), preferred_element_type=F32)


def _bf16(w):
    return w if w.dtype == BF16 else w.astype(BF16)


def _ada_kernel(xt_ref, w_ref, b_ref, o_ref, *, k_chunk):
    d = xt_ref.shape[0]
    tn = o_ref.shape[-1]

    def body(k, acc):
        a0, a1 = acc
        ks = pl.multiple_of(k * k_chunk, k_chunk)
        xt = xt_ref[pl.ds(ks, k_chunk), :]
        s = xt * jax.nn.sigmoid(xt)
        w = w_ref[0, pl.ds(ks, k_chunk), :]
        a0 = a0 + jnp.sum(w * s[:, 0:1], axis=0, keepdims=True)
        a1 = a1 + jnp.sum(w * s[:, 1:2], axis=0, keepdims=True)
        return a0, a1

    z = jnp.zeros((1, tn), F32)
    a0, a1 = lax.fori_loop(0, d // k_chunk, body, (z, z))
    o_ref[0, 0:1, :] = a0 + b_ref[0]
    o_ref[0, 1:2, :] = a1 + b_ref[0]


def _ada_mod(c, c_ctx, w_ada, b_ada):
    depth, d, n = w_ada.shape
    xt = jnp.stack([c[0], c_ctx], axis=1)
    tn = _tile(n, 1024, 512, LANES)
    k_chunk = _tile(d, 256, SUBLANES)
    return pl.pallas_call(
        functools.partial(_ada_kernel, k_chunk=k_chunk),
        grid=(depth, n // tn),
        in_specs=[pl.BlockSpec((d, 2), lambda l, j: (0, 0)),
                  pl.BlockSpec((1, d, tn), lambda l, j: (l, 0, j)),
                  pl.BlockSpec((1, 1, tn), lambda l, j: (l, 0, j))],
        out_specs=pl.BlockSpec((1, 2, tn), lambda l, j: (l, 0, j)),
        out_shape=jax.ShapeDtypeStruct((depth, 2, n), F32),
        compiler_params=_cparams(2),
        name="ada_mod",
    )(xt, w_ada, b_ada.reshape(depth, 1, n))


def _norm_mod(x, g_ref, sh_ref, sc_ref, is_ctx):
    y = x * lax.rsqrt(jnp.mean(x * x, axis=-1, keepdims=True) + NORM_EPS) * g_ref[...]
    sh = jnp.where(is_ctx, sh_ref[1:2, :], sh_ref[0:1, :])
    sc = jnp.where(is_ctx, sc_ref[1:2, :], sc_ref[0:1, :])
    return y * (1.0 + sc) + sh


def _norm_kernel(x_ref, g_ref, sh_ref, sc_ref, o_ref, *, n_lat_tiles):
    is_ctx = pl.program_id(0) >= n_lat_tiles
    o_ref[...] = _norm_mod(x_ref[...], g_ref, sh_ref, sc_ref, is_ctx).astype(o_ref.dtype)


def _norm_router_kernel(x_ref, g_ref, sh_ref, sc_ref, wr_ref, h_ref, idx_ref, gate_ref):
    h = _norm_mod(x_ref[...], g_ref, sh_ref, sc_ref, False)
    h_ref[...] = h
    logits = jnp.dot(h, wr_ref[...], preferred_element_type=F32, precision=lax.Precision.HIGHEST)
    lane = lax.broadcasted_iota(jnp.int32, logits.shape, 1).astype(F32)
    neg = jnp.float32(-jnp.inf)
    l1 = jnp.where(lane < N_EXPERTS, logits, neg)
    v1 = jnp.max(l1, axis=-1, keepdims=True)
    i1 = jnp.min(jnp.where(l1 == v1, lane, float(LANES)), axis=-1, keepdims=True)
    l2 = jnp.where(lane == i1, neg, l1)
    v2 = jnp.max(l2, axis=-1, keepdims=True)
    i2 = jnp.min(jnp.where(l2 == v2, lane, float(LANES)), axis=-1, keepdims=True)
    e = jnp.exp(v2 - v1)
    g1 = 1.0 / (1.0 + e)
    g2 = e / (1.0 + e)
    idx_ref[...] = jnp.where(lane == 0, i1, jnp.where(lane == 1, i2, 0.0)).astype(jnp.int32)
    gate_ref[...] = jnp.where(lane == 0, g1, jnp.where(lane == 1, g2, 0.0))


def _norm_modulate(x, g, mods, sh_blk, sc_blk, n_lat, out_dtype):
    r, d = x.shape
    tm = _tile(n_lat, 256, 128, 64, SUBLANES) if r > n_lat else _tile(r, 256, 128, 64, SUBLANES)
    if r > n_lat:
        tm = _tile(r - n_lat, tm, 128, 64, SUBLANES)
    return pl.pallas_call(
        functools.partial(_norm_kernel, n_lat_tiles=n_lat // tm),
        grid=(r // tm,),
        in_specs=[pl.BlockSpec((tm, d), lambda i: (i, 0)),
                  pl.BlockSpec((1, d), lambda i: (0, 0)),
                  pl.BlockSpec((2, d), lambda i: (0, sh_blk)),
                  pl.BlockSpec((2, d), lambda i: (0, sc_blk))],
        out_specs=pl.BlockSpec((tm, d), lambda i: (i, 0)),
        out_shape=jax.ShapeDtypeStruct((r, d), out_dtype),
        compiler_params=_cparams(1),
        name="norm_modulate",
    )(x, g.reshape(1, d), mods, mods)


def _norm_modulate_route(x, g, mods, sh_blk, sc_blk, w_router):
    r, d = x.shape
    tm = _tile(r, 512, 256, 128, 64, SUBLANES)
    wr = jnp.pad(w_router, ((0, 0), (0, LANES - N_EXPERTS)))
    return pl.pallas_call(
        _norm_router_kernel,
        grid=(r // tm,),
        in_specs=[pl.BlockSpec((tm, d), lambda i: (i, 0)),
                  pl.BlockSpec((1, d), lambda i: (0, 0)),
                  pl.BlockSpec((2, d), lambda i: (0, sh_blk)),
                  pl.BlockSpec((2, d), lambda i: (0, sc_blk)),
                  pl.BlockSpec((d, LANES), lambda i: (0, 0))],
        out_specs=[pl.BlockSpec((tm, d), lambda i: (i, 0)),
                   pl.BlockSpec((tm, LANES), lambda i: (i, 0)),
                   pl.BlockSpec((tm, LANES), lambda i: (i, 0))],
        out_shape=[jax.ShapeDtypeStruct((r, d), F32),
                   jax.ShapeDtypeStruct((r, LANES), jnp.int32),
                   jax.ShapeDtypeStruct((r, LANES), F32)],
        compiler_params=_cparams(1),
        name="norm_modulate_route",
    )(x, g.reshape(1, d), mods, mods, wr)


def _in_a_kernel(h_ref, wt_ref, wkr_ref, g_ref, cos_ref, sin_ref, qa_ref, ckv_ref, kr_ref):
    h = h_ref[...]
    acc = _dot_t(h, _bf16(wt_ref[...]))
    acc_kr = _dot_t(h, wkr_ref[...])

    def rms(v, g):
        return v * lax.rsqrt(jnp.mean(v * v, axis=-1, keepdims=True) + NORM_EPS) * g

    q1 = Q_LORA_RANK
    c1 = q1 + KV_LORA_RANK
    qa_ref[...] = rms(acc[:, :q1], g_ref[:, :q1]).astype(qa_ref.dtype)
    ckv_ref[...] = rms(acc[:, q1:c1], g_ref[:, q1:c1]).astype(ckv_ref.dtype)
    kr = acc_kr[:, :LANES] * cos_ref[...] + acc_kr[:, LANES:] * sin_ref[...]
    kr_ref[...] = kr.astype(kr_ref.dtype)


def _in_proj_a(h, w_in_t, li, w_kr, g_a, cos_t, sin_t):
    m, d = h.shape
    n = Q_LORA_RANK + KV_LORA_RANK
    base = li * w_in_t.shape[1]
    assert base % SUBLANES == 0
    tm = _tile(m, 768, 640, 512, 256, 128)
    return pl.pallas_call(
        _in_a_kernel,
        grid=(m // tm,),
        in_specs=[pl.BlockSpec((tm, d), lambda i: (i, 0)),
                  pl.BlockSpec((pl.Element(n), pl.Element(d)), lambda i: (base, 0)),
                  pl.BlockSpec(w_kr.shape, lambda i: (0, 0)),
                  pl.BlockSpec((1, n), lambda i: (0, 0)),
                  pl.BlockSpec((tm, LANES), lambda i: (i, 0)),
                  pl.BlockSpec((tm, LANES), lambda i: (i, 0))],
        out_specs=[pl.BlockSpec((tm, Q_LORA_RANK), lambda i: (i, 0)),
                   pl.BlockSpec((tm, KV_LORA_RANK), lambda i: (i, 0)),
                   pl.BlockSpec((tm, LANES), lambda i: (i, 0))],
        out_shape=[jax.ShapeDtypeStruct((m, Q_LORA_RANK), BF16),
                   jax.ShapeDtypeStruct((m, KV_LORA_RANK), BF16),
                   jax.ShapeDtypeStruct((m, LANES), BF16)],
        compiler_params=_cparams(1),
        name="in_proj_a",
    )(h, w_in_t.reshape(-1, d), w_kr, g_a, cos_t, sin_t)


def _in_b_kernel(h_ref, wt_ref, o_ref, *, first_gate_tile):
    acc = _dot_t(h_ref[...], _bf16(wt_ref[...]))
    is_gate = pl.program_id(0) >= first_gate_tile

    @pl.when(is_gate)
    def _():
        o_ref[...] = jax.nn.sigmoid(acc).astype(o_ref.dtype)

    @pl.when(jnp.logical_not(is_gate))
    def _():
        o_ref[...] = acc.astype(o_ref.dtype)


def _in_proj_b(h, w_in_t, li, row0, first_gate_col):
    m, d = h.shape
    n = w_in_t.shape[1] - row0
    tm = _tile(m, 1408, 768, 640, 512, 256, 128)
    tn = _tile(first_gate_col, 1024, 512, 256, LANES)
    base = li * w_in_t.shape[1] + row0
    assert base % SUBLANES == 0 and n % tn == 0
    return pl.pallas_call(
        functools.partial(_in_b_kernel, first_gate_tile=first_gate_col // tn),
        grid=(n // tn, m // tm),
        in_specs=[pl.BlockSpec((tm, d), lambda j, i: (i, 0)),
                  pl.BlockSpec((pl.Element(tn), pl.Element(d)),
                               lambda j, i: (pl.multiple_of(base + j * tn, SUBLANES), 0))],
        out_specs=pl.BlockSpec((tm, tn), lambda j, i: (i, j)),
        out_shape=jax.ShapeDtypeStruct((m, n), BF16),
        compiler_params=_cparams(2),
        name="in_proj_b",
    )(h, w_in_t.reshape(-1, d))


def _q_kernel(a_ref, wm_ref, ws_ref, cos_ref, sin_ref, q_ref):
    a = a_ref[...]
    cos = cos_ref[...]
    sin = sin_ref[...]
    for hd in range(N_HEADS):
        main = _dot(a, wm_ref[:, hd * HEAD_W:(hd + 1) * HEAD_W])
        swap = _dot(a, ws_ref[:, hd * LANES:(hd + 1) * LANES])
        q_ref[:, hd * HEAD_W:hd * HEAD_W + LANES] = (main[:, :LANES] * Q_SCALE).astype(q_ref.dtype)
        rot = (main[:, LANES:] * cos + swap * sin) * Q_SCALE
        q_ref[:, hd * HEAD_W + LANES:(hd + 1) * HEAD_W] = rot.astype(q_ref.dtype)


def _q_proj(qa_n, w_main, w_swap, cos_t, sin_t, rows):
    k = qa_n.shape[1]
    tm = _tile(rows, 768, 640, 512, 256, 128)
    return pl.pallas_call(
        _q_kernel,
        grid=(rows // tm,),
        in_specs=[pl.BlockSpec((tm, k), lambda i: (i, 0)),
                  pl.BlockSpec(w_main.shape, lambda i: (0, 0)),
                  pl.BlockSpec(w_swap.shape, lambda i: (0, 0)),
                  pl.BlockSpec((tm, LANES), lambda i: (i, 0)),
                  pl.BlockSpec((tm, LANES), lambda i: (i, 0))],
        out_specs=pl.BlockSpec((tm, N_HEADS * HEAD_W), lambda i: (i, 0)),
        out_shape=jax.ShapeDtypeStruct((rows, N_HEADS * HEAD_W), BF16),
        compiler_params=_cparams(1),
        name="q_proj",
    )(qa_n, w_main, w_swap, cos_t, sin_t)


def _kv_kernel(c_ref, kr_ref, wkv_ref, k_ref, v_ref):
    c = c_ref[...]
    kr = kr_ref[...]
    lane = lax.broadcasted_iota(jnp.int32, (c.shape[0], LANES), 1)
    ones_col = jnp.where(lane == 0, 1.0, 0.0).astype(v_ref.dtype)
    for hd in range(N_HEADS):
        kv = _dot(c, wkv_ref[:, hd * HEAD_W:(hd + 1) * HEAD_W])
        k_ref[:, hd * HEAD_W:hd * HEAD_W + LANES] = kv[:, :QK_NOPE_DIM].astype(k_ref.dtype)
        k_ref[:, hd * HEAD_W + LANES:(hd + 1) * HEAD_W] = kr
        v_ref[:, hd * HEAD_W:hd * HEAD_W + LANES] = kv[:, QK_NOPE_DIM:].astype(v_ref.dtype)
        v_ref[:, hd * HEAD_W + LANES:(hd + 1) * HEAD_W] = ones_col


def _kv_proj(ckv_n, kr, w_kvb, layer):
    m, k = ckv_n.shape
    tm = _tile(m, 768, 640, 512, 256, 128)
    assert QK_NOPE_DIM == LANES and V_HEAD_DIM == LANES
    return pl.pallas_call(
        _kv_kernel,
        grid=(m // tm,),
        in_specs=[pl.BlockSpec((tm, k), lambda i: (i, 0)),
                  pl.BlockSpec((tm, LANES), lambda i: (i, 0)),
                  pl.BlockSpec((None,) + w_kvb.shape[1:], lambda i: (layer, 0, 0))],
        out_specs=[pl.BlockSpec((tm, N_HEADS * HEAD_W), lambda i: (i, 0)),
                   pl.BlockSpec((tm, N_HEADS * HEAD_W), lambda i: (i, 0))],
        out_shape=[jax.ShapeDtypeStruct((m, N_HEADS * HEAD_W), BF16),
                   jax.ShapeDtypeStruct((m, N_HEADS * HEAD_W), BF16)],
        compiler_params=_cparams(1),
        name="kv_proj",
    )(ckv_n, kr, w_kvb)


def _attn_kernel(q_ref, k_ref, v_ref, o_ref, sa_ref, sb_ref, *, tk, n_chunks):
    q = q_ref[...]
    tq = q.shape[0]

    def scores(c):
        return _dot_t(q, k_ref[c * tk:(c + 1) * tk, :])

    def absorb(s_ref, c, carry):
        m, acc = carry
        s = s_ref[...]
        m_new = jnp.maximum(m, jnp.max(s, axis=-1, keepdims=True))
        p = jnp.exp2(s - m_new).astype(v_ref.dtype)
        acc = jnp.exp2(m - m_new) * acc + _dot(p, v_ref[c * tk:(c + 1) * tk, :])
        return m_new, acc

    s_refs = (sa_ref, sb_ref)
    carry = (jnp.full((tq, 1), -jnp.inf, F32), jnp.zeros((tq, HEAD_W), F32))
    sa_ref[...] = scores(0)
    for c in range(n_chunks):
        if c + 1 < n_chunks:
            s_refs[(c + 1) % 2][...] = scores(c + 1)
        carry = absorb(s_refs[c % 2], c, carry)
    _, acc = carry
    o_ref[...] = (acc[:, :V_HEAD_DIM] / acc[:, V_HEAD_DIM:V_HEAD_DIM + 1]).astype(o_ref.dtype)


def _attention(q, k, v, q_row0, n_q, k_row0, n_k):
    tq = _tile(n_q, 1024, 512, 256, 128)
    tk = _tile(n_k, 768, 640, 512, 384, 256, 128)
    assert q_row0 % tq == 0 and k_row0 % n_k == 0
    q_blk0 = q_row0 // tq
    k_blk = k_row0 // n_k
    return pl.pallas_call(
        functools.partial(_attn_kernel, tk=tk, n_chunks=n_k // tk),
        grid=(N_HEADS, n_q // tq),
        in_specs=[pl.BlockSpec((tq, HEAD_W), lambda h, i: (q_blk0 + i, h)),
                  pl.BlockSpec((n_k, HEAD_W), lambda h, i: (k_blk, h)),
                  pl.BlockSpec((n_k, HEAD_W), lambda h, i: (k_blk, h))],
        out_specs=pl.BlockSpec((tq, V_HEAD_DIM), lambda h, i: (i, h)),
        out_shape=jax.ShapeDtypeStruct((n_q, N_HEADS * V_HEAD_DIM), BF16),
        scratch_shapes=[pltpu.VMEM((tq, tk), F32), pltpu.VMEM((tq, tk), F32)],
        compiler_params=_cparams(2),
        name="attention",
    )(q, k, v)


def _conv_gate_kernel(cx_ref, cb_ref, cc_ref, cxp_ref, ccp_ref, cxn_ref, ccn_ref, wc_ref, z_ref, *,
                      seg_starts, seg_ends, col_chunk):
    tm, width = z_ref.shape
    loc = lax.broadcasted_iota(jnp.int32, (tm, 1), 0)
    row = loc + pl.program_id(0) * tm
    first = functools.reduce(jnp.logical_or, [row == r for r in seg_starts])
    last = functools.reduce(jnp.logical_or, [row == r for r in seg_ends])
    for c0 in range(0, width, col_chunk):
        cs = slice(c0, c0 + col_chunk)
        u = cx_ref[:, cs].astype(F32) * cc_ref[:, cs].astype(F32)
        u_halo_prev = (cxp_ref[SUBLANES - 1:SUBLANES, cs].astype(F32)
                       * ccp_ref[SUBLANES - 1:SUBLANES, cs].astype(F32))
        u_halo_next = cxn_ref[0:1, cs].astype(F32) * ccn_ref[0:1, cs].astype(F32)
        u_prev = jnp.where(loc == 0, u_halo_prev, pltpu.roll(u, 1, 0))
        u_prev = jnp.where(first, 0.0, u_prev)
        u_next = jnp.where(loc == tm - 1, u_halo_next, pltpu.roll(u, tm - 1, 0))
        u_next = jnp.where(last, 0.0, u_next)
        conv = wc_ref[0:1, cs] * u_prev + wc_ref[1:2, cs] * u + wc_ref[2:3, cs] * u_next
        z_ref[:, cs] = (cb_ref[:, cs].astype(F32) * conv).astype(z_ref.dtype)


def _conv_gate(p, w_conv, rows, s_len, m_len):
    width = w_conv.shape[1]
    tm = _tile(rows, 768, 512, 384, 256, 128, 64)
    hb = tm // SUBLANES
    last_hb = p.shape[0] // SUBLANES - 1

    def prev_map(col):
        return lambda i: (jnp.maximum(i * hb - 1, 0), col)

    def next_map(col):
        return lambda i: (jnp.minimum((i + 1) * hb, last_hb), col)

    kern = functools.partial(_conv_gate_kernel, seg_starts=(0, s_len), seg_ends=(s_len - 1, m_len - 1),
                             col_chunk=_tile(width, 512, LANES))
    return pl.pallas_call(
        kern,
        grid=(rows // tm,),
        in_specs=[pl.BlockSpec((tm, width), lambda i: (i, 0)),
                  pl.BlockSpec((tm, width), lambda i: (i, 1)),
                  pl.BlockSpec((tm, width), lambda i: (i, 2)),
                  pl.BlockSpec((SUBLANES, width), prev_map(0)),
                  pl.BlockSpec((SUBLANES, width), prev_map(2)),
                  pl.BlockSpec((SUBLANES, width), next_map(0)),
                  pl.BlockSpec((SUBLANES, width), next_map(2)),
                  pl.BlockSpec((CONV_K, width), lambda i: (0, 0))],
        out_specs=pl.BlockSpec((tm, width), lambda i: (i, 0)),
        out_shape=jax.ShapeDtypeStruct((rows, width), BF16),
        compiler_params=_cparams(1),
        name="conv_gate",
    )(p, p, p, p, p, p, p, w_conv)


def _merge_kernel(attn_ref, z_ref, sga_ref, sgb_ref, woa_ref, wob_ref, o_ref):
    o_a = _dot(attn_ref[...], woa_ref[...])
    o_b = _dot(z_ref[...], wob_ref[...])
    o_ref[...] = (sga_ref[...].astype(F32) * o_a + sgb_ref[...].astype(F32) * o_b).astype(o_ref.dtype)


def _merge(attn, z, p, w_oa, w_ob, layer, rows):
    attn_w = w_oa.shape[1]
    _, width, d = w_ob.shape
    tm = _tile(rows, 768, 512, 384, 256, 128, 64)
    tn = _tile(d, 1024, 512, 256, LANES)
    ga_blk = 3 * width // tn
    gb_blk = (3 * width + d) // tn
    return pl.pallas_call(
        _merge_kernel,
        grid=(d // tn, rows // tm),
        in_specs=[pl.BlockSpec((tm, attn_w), lambda j, i: (i, 0)),
                  pl.BlockSpec((tm, width), lambda j, i: (i, 0)),
                  pl.BlockSpec((tm, tn), lambda j, i: (i, ga_blk + j)),
                  pl.BlockSpec((tm, tn), lambda j, i: (i, gb_blk + j)),
                  pl.BlockSpec((None, attn_w, tn), lambda j, i: (layer, 0, j)),
                  pl.BlockSpec((None, width, tn), lambda j, i: (layer, 0, j))],
        out_specs=pl.BlockSpec((tm, tn), lambda j, i: (i, j)),
        out_shape=jax.ShapeDtypeStruct((rows, d), BF16),
        compiler_params=_cparams(2),
        name="merge",
    )(attn, z, p, p, w_oa, w_ob)


def _mm_res_kernel(a_ref, w_ref, x_ref, gt_ref, o_ref, *, n_lat):
    tm = a_ref.shape[0]
    row = lax.broadcasted_iota(jnp.int32, (tm, 1), 0) + pl.program_id(1) * tm
    gate = jnp.where(row >= n_lat, gt_ref[1:2, :], gt_ref[0:1, :])
    o_ref[...] = x_ref[...] + gate * _dot(a_ref[...], _bf16(w_ref[...]))


def _matmul_residual(a, w, layer, x, mods, gate_blk, rows, n_lat):
    _, k, n = w.shape
    tm = _tile(rows, 768, 640, 512, 256, 128)
    tn = _tile(n, 1024 if k <= n else 512, 512, 256, LANES)
    nt = n // tn
    return pl.pallas_call(
        functools.partial(_mm_res_kernel, n_lat=n_lat),
        grid=(nt, rows // tm),
        in_specs=[pl.BlockSpec((tm, k), lambda j, i: (i, 0)),
                  pl.BlockSpec((None, k, tn), lambda j, i: (layer, 0, j)),
                  pl.BlockSpec((tm, tn), lambda j, i: (i, j)),
                  pl.BlockSpec((2, tn), lambda j, i: (0, gate_blk * nt + j))],
        out_specs=pl.BlockSpec((tm, tn), lambda j, i: (i, j)),
        out_shape=jax.ShapeDtypeStruct((rows, n), F32),
        compiler_params=_cparams(2),
        name="matmul_residual",
    )(a, w, x, mods)


def _block_in_use(eid_ref):
    return pl.program_id(1) < eid_ref[pl.num_programs(1)]


def _glu_kernel(eid_ref, a_ref, wg_ref, wu_ref, o_ref):
    @pl.when(_block_in_use(eid_ref))
    def _():
        a = _bf16(a_ref[...])
        gate = _dot(a, _bf16(wg_ref[0]))
        up = _dot(a, _bf16(wu_ref[0]))
        o_ref[...] = (gate * jax.nn.sigmoid(gate) * up).astype(o_ref.dtype)

    @pl.when(jnp.logical_not(_block_in_use(eid_ref)))
    def _():
        o_ref[...] = jnp.zeros_like(o_ref)


def _glu(a, w_gate, w_up, eid, tm):
    r, k = a.shape
    f = w_gate.shape[2]
    tn = _tile(f, 512, 1408, 256, LANES)
    grid_spec = pltpu.PrefetchScalarGridSpec(
        num_scalar_prefetch=1,
        grid=(f // tn, r // tm),
        in_specs=[pl.BlockSpec((tm, k), lambda j, i, e: (i, 0)),
                  pl.BlockSpec((1, k, tn), lambda j, i, e: (e[i], 0, j)),
                  pl.BlockSpec((1, k, tn), lambda j, i, e: (e[i], 0, j))],
        out_specs=pl.BlockSpec((tm, tn), lambda j, i, e: (i, j)),
    )
    return pl.pallas_call(
        _glu_kernel,
        grid_spec=grid_spec,
        out_shape=jax.ShapeDtypeStruct((r, f), BF16),
        compiler_params=_cparams(2),
        name="glu",
    )(eid, a, w_gate, w_up)


def _down_kernel(eid_ref, a_ref, w_ref, o_ref):
    @pl.when(_block_in_use(eid_ref))
    def _():
        o_ref[...] = _dot(a_ref[...], w_ref[0])

    @pl.when(jnp.logical_not(_block_in_use(eid_ref)))
    def _():
        o_ref[...] = jnp.zeros_like(o_ref)


def _down_grouped(a, w_down, eid, tm):
    r, f = a.shape
    d = w_down.shape[2]
    grid_spec = pltpu.PrefetchScalarGridSpec(
        num_scalar_prefetch=1,
        grid=(1, r // tm),
        in_specs=[pl.BlockSpec((tm, f), lambda j, i, e: (i, 0)),
                  pl.BlockSpec((1, f, d), lambda j, i, e: (e[i], 0, 0))],
        out_specs=pl.BlockSpec((tm, d), lambda j, i, e: (i, 0)),
    )
    return pl.pallas_call(
        _down_kernel,
        grid_spec=grid_spec,
        out_shape=jax.ShapeDtypeStruct((r, d), F32),
        compiler_params=_cparams(2),
        name="down_grouped",
    )(eid, a, w_down)


def _row_copies(idx_ref, idx_base, idx_stride, src_ref, dst_ref, sem, n_rows, start):
    def body(r2, c):
        for prio in range(2):
            r = 2 * r2 + prio
            row = idx_ref[idx_base + r * idx_stride]
            cp = pltpu.make_async_copy(src_ref.at[pl.ds(row, 1)], dst_ref.at[pl.ds(r, 1)], sem)
            if start:
                cp.start(priority=prio)
            else:
                cp.wait()
        return c

    assert n_rows % 2 == 0
    lax.fori_loop(0, n_rows // 2, body, 0, unroll=DMA_ISSUE_UNROLL // 2)


def _prefetched_gather(copies):
    i = pl.program_id(0)
    slot = i % 2

    @pl.when(i == 0)
    def _():
        copies(0, 0, True)

    @pl.when(i + 1 < pl.num_programs(0))
    def _():
        copies(i + 1, 1 - slot, True)

    copies(i, slot, False)
    return slot


def _gather_rows_kernel(idx_ref, src_ref, o_ref, g_ref, sem):
    tm = o_ref.shape[0]

    def copies(step, slot, start):
        _row_copies(idx_ref, step * tm, 1, src_ref, g_ref.at[slot], sem.at[slot], tm, start)

    slot = _prefetched_gather(copies)
    o_ref[...] = g_ref[slot]


def _gather_rows(src, idx, tm):
    n = idx.shape[0]
    w = src.shape[1]
    grid_spec = pltpu.PrefetchScalarGridSpec(
        num_scalar_prefetch=1,
        grid=(n // tm,),
        in_specs=[pl.BlockSpec(memory_space=pl.ANY)],
        out_specs=pl.BlockSpec((tm, w), lambda i, idx_ref: (i, 0)),
        scratch_shapes=[pltpu.VMEM((2, tm, w), src.dtype), pltpu.SemaphoreType.DMA((2,))],
    )
    return pl.pallas_call(
        _gather_rows_kernel,
        grid_spec=grid_spec,
        out_shape=jax.ShapeDtypeStruct((n, w), src.dtype),
        compiler_params=_cparams(1),
        name="gather_rows",
    )(idx, src)


def _final_kernel(dest_ref, x_ref, y_hbm_ref, gate_ref, gt_ref, g_ref, o_ref, y_ref, sem):
    tm = x_ref.shape[0]

    def copies(step, slot, start):
        for kk in range(TOP_K):
            _row_copies(dest_ref, step * tm * TOP_K + kk, TOP_K, y_hbm_ref, y_ref.at[slot, kk], sem.at[slot],
                        tm, start)

    slot = _prefetched_gather(copies)
    gates = gate_ref[...]
    y = gates[:, 0:1] * y_ref[slot, 0] + gates[:, 1:2] * y_ref[slot, 1]
    x = x_ref[...] + gt_ref[0:1, :] * y
    o_ref[...] = x * lax.rsqrt(jnp.mean(x * x, axis=-1, keepdims=True) + NORM_EPS) * g_ref[...]


def _combine_final(x, ybuf, dest, gates, mods, gate_blk, g_final):
    r, d = x.shape
    tm = _tile(r, 256, 128, 64, SUBLANES)
    grid_spec = pltpu.PrefetchScalarGridSpec(
        num_scalar_prefetch=1,
        grid=(r // tm,),
        in_specs=[pl.BlockSpec((tm, d), lambda i, dest_ref: (i, 0)),
                  pl.BlockSpec(memory_space=pl.ANY),
                  pl.BlockSpec((tm, LANES), lambda i, dest_ref: (i, 0)),
                  pl.BlockSpec((2, d), lambda i, dest_ref: (0, gate_blk)),
                  pl.BlockSpec((1, d), lambda i, dest_ref: (0, 0))],
        out_specs=pl.BlockSpec((tm, d), lambda i, dest_ref: (i, 0)),
        scratch_shapes=[pltpu.VMEM((2, TOP_K, tm, d), F32), pltpu.SemaphoreType.DMA((2,))],
    )
    return pl.pallas_call(
        _final_kernel,
        grid_spec=grid_spec,
        out_shape=jax.ShapeDtypeStruct((r, d), F32),
        compiler_params=_cparams(1),
        name="combine_final",
    )(dest, x, ybuf, gates, mods, g_final.reshape(1, d))


def _rope_tables(s_len, c_len):
    quarter = QK_ROPE_DIM // 4
    tok = jnp.arange(s_len + c_len, dtype=jnp.int32)[:, None]
    lane = jnp.arange(LANES, dtype=jnp.int32)[None, :]
    group = lane // quarter
    pos = jnp.where(group < 2, tok // GRID_W, tok % GRID_W).astype(F32)
    inv_freq = jnp.power(ROPE_THETA, -(2 * (lane % quarter)).astype(F32) / (QK_ROPE_DIM // 2))
    ang = jnp.where(tok < s_len, pos * inv_freq, 0.0)
    live = group < 4
    cos = jnp.where(live, jnp.cos(ang), 0.0)
    sin = jnp.where(live, jnp.where(group % 2 == 0, -jnp.sin(ang), jnp.sin(ang)), 0.0)
    return cos, sin


def _rope_swap_perm():
    q = QK_ROPE_DIM // 4
    return jnp.concatenate([jnp.arange(q, 2 * q), jnp.arange(0, q), jnp.arange(3 * q, 4 * q), jnp.arange(2 * q, 3 * q)])


def _layer_weights(w_in_t, w_qb_all, li):
    d = w_in_t.shape[2]
    perm = _rope_swap_perm()
    kr_lo = Q_LORA_RANK + KV_LORA_RANK
    kr_hi = kr_lo + QK_ROPE_DIM
    w_kr = w_in_t[li, kr_lo:kr_hi]
    zpad = jnp.zeros((LANES - QK_ROPE_DIM, d), w_in_t.dtype)
    w_a = jnp.concatenate([w_kr, zpad, w_kr[perm], zpad], axis=0)
    w_qb = w_qb_all[li]
    qb = w_qb.reshape(Q_LORA_RANK, N_HEADS, QK_NOPE_DIM + QK_ROPE_DIM)
    q_rope = qb[:, :, QK_NOPE_DIM:]
    hpad = jnp.zeros((Q_LORA_RANK, N_HEADS, LANES - QK_ROPE_DIM), w_qb.dtype)
    w_q_main = jnp.concatenate([qb, hpad], axis=2).reshape(Q_LORA_RANK, N_HEADS * HEAD_W)
    w_q_swap = jnp.concatenate([q_rope[:, :, perm], hpad], axis=2).reshape(Q_LORA_RANK, N_HEADS * LANES)
    return tuple(w.astype(BF16) for w in (w_a, w_q_main, w_q_swap))


def _moe_slots(idx, n_blocks):
    e_flat = idx.reshape(-1)
    onehot = (e_flat[:, None] == jnp.arange(N_EXPERTS)[None, :]).astype(jnp.int32)
    csum = jnp.cumsum(onehot, axis=0)
    counts = csum[-1]
    rank = jnp.sum((csum - onehot) * onehot, axis=1)
    padded = (counts + MOE_ROWS - 1) // MOE_ROWS * MOE_ROWS
    p_end = jnp.cumsum(padded)
    p_start = p_end - padded
    dest = jnp.sum(onehot * p_start[None, :], axis=1) + rank
    blk_lo = jnp.arange(n_blocks) * MOE_ROWS
    block_expert = jnp.minimum(jnp.sum((blk_lo[:, None] >= p_end[None, :]).astype(jnp.int32), axis=1),
                               N_EXPERTS - 1)
    blocks_in_use = p_end[-1:] // MOE_ROWS
    return dest.astype(jnp.int32), jnp.concatenate([block_expert, blocks_in_use]).astype(jnp.int32)


def kernel(x, c, ctx, c_ctx, w_ada, b_ada, g_attn, w_in, g_qa, w_qb, g_kva, w_kvb, w_conv, w_oa, w_ob, w_o,
           g_ffn, w_gate_dense, w_up_dense, w_down_dense, w_router, w_gate_exp, w_up_exp, w_down_exp, g_final):
    _, s_len, d = x.shape
    c_len = ctx.shape[1]
    m_len = s_len + c_len
    depth = w_in.shape[0]
    width = w_conv.shape[2]
    assert depth == 2, "supported stack: dense-FFN layer with context updates, then a final expert-FFN layer"

    mods_all = _ada_mod(c, c_ctx, w_ada, b_ada)
    cos_t, sin_t = _rope_tables(s_len, c_len)
    w_in_t = jnp.swapaxes(w_in, 1, 2)
    w_oa_bf, w_ob_bf, w_kvb_bf = w_oa.astype(BF16), w_ob.astype(BF16), w_kvb.astype(BF16)
    xa = jnp.concatenate([x[0], ctx[0]], axis=0)

    for li in range(depth):
        last = li == depth - 1
        mods = mods_all[li]
        w_a, w_q_main, w_q_swap = _layer_weights(w_in_t, w_qb, li)
        g_a = jnp.concatenate([g_qa[li], g_kva[li]]).reshape(1, -1)
        rows = s_len if last else m_len

        h = _norm_modulate(xa, g_attn[li], mods, 0, 1, s_len, BF16)
        qa_n, ckv_n, kr = _in_proj_a(h, w_in_t, li, w_a, g_a, cos_t, sin_t)
        p = _in_proj_b(h, w_in_t, li, Q_LORA_RANK + KV_LORA_RANK + QK_ROPE_DIM, 3 * width)
        q = _q_proj(qa_n, w_q_main, w_q_swap, cos_t, sin_t, rows)
        k, v = _kv_proj(ckv_n, kr, w_kvb_bf, li)
        attn = _attention(q, k, v, 0, s_len, 0, m_len)
        if not last:
            attn = jnp.concatenate([attn, _attention(q, k, v, s_len, c_len, s_len, c_len)], axis=0)
        z = _conv_gate(p, w_conv[li], rows, s_len, m_len)
        merged = _merge(attn, z, p, w_oa_bf, w_ob_bf, li, rows)
        xa = _matmul_residual(merged, w_o, li, xa, mods, 2, rows, s_len)

        j = li // 2
        if li % 2 == 0:
            h2 = _norm_modulate(xa, g_ffn[li], mods, 3, 4, s_len, BF16)
            tm = _tile(rows, 768, 640, 512, 256, 128)
            eid = jnp.zeros((rows // tm + 1,), jnp.int32).at[-1].set(rows // tm)
            hid = _glu(h2, w_gate_dense[j][None], w_up_dense[j][None], eid, tm)
            xa = _matmul_residual(hid, w_down_dense.astype(BF16), j, xa, mods, 5, rows, s_len)
        else:
            h2, idx, gates = _norm_modulate_route(xa, g_ffn[li], mods, 3, 4, w_router[j])
            n_assign = rows * TOP_K
            n_blocks = -(-n_assign // MOE_ROWS) + N_EXPERTS
            dest, block_expert = _moe_slots(idx[:, :TOP_K], n_blocks)
            tok = jnp.arange(n_assign, dtype=jnp.int32) // TOP_K
            slot_tok = jnp.zeros((n_blocks * MOE_ROWS,), jnp.int32).at[dest].set(tok, unique_indices=True)
            buf = _gather_rows(h2, slot_tok, MOE_ROWS)
            hid = _glu(buf, w_gate_exp[j], w_up_exp[j], block_expert, MOE_ROWS)
            ybuf = _down_grouped(hid, w_down_exp[j].astype(BF16), block_expert, MOE_ROWS)
            return _combine_final(xa, ybuf, dest, gates, mods, 5, g_final)[None]
    raise AssertionError("unreachable: the final layer returns")
```

```python
import functools

import jax
import jax.numpy as jnp
from jax import lax
from jax.experimental import pallas as pl
from jax.experimental.pallas import tpu as pltpu

F32 = jnp.float32
BF16 = jnp.bfloat16

N_HEADS = 16
QK_NOPE_DIM = 128
QK_ROPE_DIM = 64
V_HEAD_DIM = 128
Q_LORA_RANK = 512
KV_LORA_RANK = 512
GRID_W = 64
ROPE_THETA = 10000.0
ATTN_SCALE = (QK_NOPE_DIM + QK_ROPE_DIM) ** -0.5
Q_SCALE = ATTN_SCALE * 1.4426950408889634
CONV_K = 3
N_EXPERTS = 8
TOP_K = 2
NORM_EPS = 1e-6

LANES = 128
SUBLANES = 8
HEAD_W = 2 * LANES
VMEM_LIMIT_BYTES = 56 * 1024 * 1024

MOE_ROWS = 256
DMA_ISSUE_UNROLL = 8


def _tile(n, *cands):
    for c in cands:
        if n % c == 0:
            return c
    return n


def _cparams(n_axes):
    return pltpu.CompilerParams(dimension_semantics=("arbitrary",) * n_axes,
                                vmem_limit_bytes=VMEM_LIMIT_BYTES)


def _dot(a, b):
    return jnp.dot(a, b, preferred_element_type=F32)


def _dot_t(a, bt):
    return lax.dot_general(a, bt, (((1,), (1,)), ((), ())), preferred_element_type=F32)


def _bf16(w):
    return w if w.dtype == BF16 else w.astype(BF16)


def _ada_kernel(xt_ref, w_ref, b_ref, o_ref, *, k_chunk):
    d = xt_ref.shape[0]
    tn = o_ref.shape[-1]

    def body(k, acc):
        a0, a1 = acc
        ks = pl.multiple_of(k * k_chunk, k_chunk)
        xt = xt_ref[pl.ds(ks, k_chunk), :]
        s = xt * jax.nn.sigmoid(xt)
        w = w_ref[0, pl.ds(ks, k_chunk), :]
        a0 = a0 + jnp.sum(w * s[:, 0:1], axis=0, keepdims=True)
        a1 = a1 + jnp.sum(w * s[:, 1:2], axis=0, keepdims=True)
        return a0, a1

    z = jnp.zeros((1, tn), F32)
    a0, a1 = lax.fori_loop(0, d // k_chunk, body, (z, z))
    o_ref[0, 0:1, :] = a0 + b_ref[0]
    o_ref[0, 1:2, :] = a1 + b_ref[0]


def _ada_mod(c, c_ctx, w_ada, b_ada):
    depth, d, n = w_ada.shape
    xt = jnp.stack([c[0], c_ctx], axis=1)
    tn = _tile(n, 1024, 512, LANES)
    k_chunk = _tile(d, 256, SUBLANES)
    return pl.pallas_call(
        functools.partial(_ada_kernel, k_chunk=k_chunk),
        grid=(depth, n // tn),
        in_specs=[pl.BlockSpec((d, 2), lambda l, j: (0, 0)),
                  pl.BlockSpec((1, d, tn), lambda l, j: (l, 0, j)),
                  pl.BlockSpec((1, 1, tn), lambda l, j: (l, 0, j))],
        out_specs=pl.BlockSpec((1, 2, tn), lambda l, j: (l, 0, j)),
        out_shape=jax.ShapeDtypeStruct((depth, 2, n), F32),
        compiler_params=_cparams(2),
        name="ada_mod",
    )(xt, w_ada, b_ada.reshape(depth, 1, n))


def _norm_mod(x, g_ref, sh_ref, sc_ref, is_ctx):
    y = x * lax.rsqrt(jnp.mean(x * x, axis=-1, keepdims=True) + NORM_EPS) * g_ref[...]
    sh = jnp.where(is_ctx, sh_ref[1:2, :], sh_ref[0:1, :])
    sc = jnp.where(is_ctx, sc_ref[1:2, :], sc_ref[0:1, :])
    return y * (1.0 + sc) + sh


def _norm_kernel(x_ref, g_ref, sh_ref, sc_ref, o_ref, *, n_lat_tiles):
    is_ctx = pl.program_id(0) >= n_lat_tiles
    o_ref[...] = _norm_mod(x_ref[...], g_ref, sh_ref, sc_ref, is_ctx).astype(o_ref.dtype)


def _norm_router_kernel(x_ref, g_ref, sh_ref, sc_ref, wr_ref, h_ref, idx_ref, gate_ref):
    h = _norm_mod(x_ref[...], g_ref, sh_ref, sc_ref, False)
    h_ref[...] = h
    logits = jnp.dot(h, wr_ref[...], preferred_element_type=F32, precision=lax.Precision.HIGHEST)
    lane = lax.broadcasted_iota(jnp.int32, logits.shape, 1).astype(F32)
    neg = jnp.float32(-jnp.inf)
    l1 = jnp.where(lane < N_EXPERTS, logits, neg)
    v1 = jnp.max(l1, axis=-1, keepdims=True)
    i1 = jnp.min(jnp.where(l1 == v1, lane, float(LANES)), axis=-1, keepdims=True)
    l2 = jnp.where(lane == i1, neg, l1)
    v2 = jnp.max(l2, axis=-1, keepdims=True)
    i2 = jnp.min(jnp.where(l2 == v2, lane, float(LANES)), axis=-1, keepdims=True)
    e = jnp.exp(v2 - v1)
    g1 = 1.0 / (1.0 + e)
    g2 = e / (1.0 + e)
    idx_ref[...] = jnp.where(lane == 0, i1, jnp.where(lane == 1, i2, 0.0)).astype(jnp.int32)
    gate_ref[...] = jnp.where(lane == 0, g1, jnp.where(lane == 1, g2, 0.0))


def _norm_modulate(x, g, mods, sh_blk, sc_blk, n_lat, out_dtype):
    r, d = x.shape
    tm = _tile(n_lat, 256, 128, 64, SUBLANES) if r > n_lat else _tile(r, 256, 128, 64, SUBLANES)
    if r > n_lat:
        tm = _tile(r - n_lat, tm, 128, 64, SUBLANES)
    return pl.pallas_call(
        functools.partial(_norm_kernel, n_lat_tiles=n_lat // tm),
        grid=(r // tm,),
        in_specs=[pl.BlockSpec((tm, d), lambda i: (i, 0)),
                  pl.BlockSpec((1, d), lambda i: (0, 0)),
                  pl.BlockSpec((2, d), lambda i: (0, sh_blk)),
                  pl.BlockSpec((2, d), lambda i: (0, sc_blk))],
        out_specs=pl.BlockSpec((tm, d), lambda i: (i, 0)),
        out_shape=jax.ShapeDtypeStruct((r, d), out_dtype),
        compiler_params=_cparams(1),
        name="norm_modulate",
    )(x, g.reshape(1, d), mods, mods)


def _norm_modulate_route(x, g, mods, sh_blk, sc_blk, w_router):
    r, d = x.shape
    tm = _tile(r, 512, 256, 128, 64, SUBLANES)
    wr = jnp.pad(w_router, ((0, 0), (0, LANES - N_EXPERTS)))
    return pl.pallas_call(
        _norm_router_kernel,
        grid=(r // tm,),
        in_specs=[pl.BlockSpec((tm, d), lambda i: (i, 0)),
                  pl.BlockSpec((1, d), lambda i: (0, 0)),
                  pl.BlockSpec((2, d), lambda i: (0, sh_blk)),
                  pl.BlockSpec((2, d), lambda i: (0, sc_blk)),
                  pl.BlockSpec((d, LANES), lambda i: (0, 0))],
        out_specs=[pl.BlockSpec((tm, d), lambda i: (i, 0)),
                   pl.BlockSpec((tm, LANES), lambda i: (i, 0)),
                   pl.BlockSpec((tm, LANES), lambda i: (i, 0))],
        out_shape=[jax.ShapeDtypeStruct((r, d), F32),
                   jax.ShapeDtypeStruct((r, LANES), jnp.int32),
                   jax.ShapeDtypeStruct((r, LANES), F32)],
        compiler_params=_cparams(1),
        name="norm_modulate_route",
    )(x, g.reshape(1, d), mods, mods, wr)


def _in_a_kernel(h_ref, wt_ref, wkr_ref, g_ref, cos_ref, sin_ref, qa_ref, ckv_ref, kr_ref):
    h = h_ref[...]
    acc = _dot_t(h, _bf16(wt_ref[...]))
    acc_kr = _dot_t(h, wkr_ref[...])

    def rms(v, g):
        return v * lax.rsqrt(jnp.mean(v * v, axis=-1, keepdims=True) + NORM_EPS) * g

    q1 = Q_LORA_RANK
    c1 = q1 + KV_LORA_RANK
    qa_ref[...] = rms(acc[:, :q1], g_ref[:, :q1]).astype(qa_ref.dtype)
    ckv_ref[...] = rms(acc[:, q1:c1], g_ref[:, q1:c1]).astype(ckv_ref.dtype)
    kr = acc_kr[:, :LANES] * cos_ref[...] + acc_kr[:, LANES:] * sin_ref[...]
    kr_ref[...] = kr.astype(kr_ref.dtype)


def _in_proj_a(h, w_in_t, li, w_kr, g_a, cos_t, sin_t):
    m, d = h.shape
    n = Q_LORA_RANK + KV_LORA_RANK
    base = li * w_in_t.shape[1]
    assert base % SUBLANES == 0
    tm = _tile(m, 768, 640, 512, 256, 128)
    return pl.pallas_call(
        _in_a_kernel,
        grid=(m // tm,),
        in_specs=[pl.BlockSpec((tm, d), lambda i: (i, 0)),
                  pl.BlockSpec((pl.Element(n), pl.Element(d)), lambda i: (base, 0)),
                  pl.BlockSpec(w_kr.shape, lambda i: (0, 0)),
                  pl.BlockSpec((1, n), lambda i: (0, 0)),
                  pl.BlockSpec((tm, LANES), lambda i: (i, 0)),
                  pl.BlockSpec((tm, LANES), lambda i: (i, 0))],
        out_specs=[pl.BlockSpec((tm, Q_LORA_RANK), lambda i: (i, 0)),
                   pl.BlockSpec((tm, KV_LORA_RANK), lambda i: (i, 0)),
                   pl.BlockSpec((tm, LANES), lambda i: (i, 0))],
        out_shape=[jax.ShapeDtypeStruct((m, Q_LORA_RANK), BF16),
                   jax.ShapeDtypeStruct((m, KV_LORA_RANK), BF16),
                   jax.ShapeDtypeStruct((m, LANES), BF16)],
        compiler_params=_cparams(1),
        name="in_proj_a",
    )(h, w_in_t.reshape(-1, d), w_kr, g_a, cos_t, sin_t)


def _in_b_kernel(h_ref, wt_ref, o_ref, *, first_gate_tile):
    acc = _dot_t(h_ref[...], _bf16(wt_ref[...]))
    is_gate = pl.program_id(0) >= first_gate_tile

    @pl.when(is_gate)
    def _():
        o_ref[...] = jax.nn.sigmoid(acc).astype(o_ref.dtype)

    @pl.when(jnp.logical_not(is_gate))
    def _():
        o_ref[...] = acc.astype(o_ref.dtype)


def _in_proj_b(h, w_in_t, li, row0, first_gate_col):
    m, d = h.shape
    n = w_in_t.shape[1] - row0
    tm = _tile(m, 1408, 768, 640, 512, 256, 128)
    tn = _tile(first_gate_col, 1024, 512, 256, LANES)
    base = li * w_in_t.shape[1] + row0
    assert base % SUBLANES == 0 and n % tn == 0
    return pl.pallas_call(
        functools.partial(_in_b_kernel, first_gate_tile=first_gate_col // tn),
        grid=(n // tn, m // tm),
        in_specs=[pl.BlockSpec((tm, d), lambda j, i: (i, 0)),
                  pl.BlockSpec((pl.Element(tn), pl.Element(d)),
                               lambda j, i: (pl.multiple_of(base + j * tn, SUBLANES), 0))],
        out_specs=pl.BlockSpec((tm, tn), lambda j, i: (i, j)),
        out_shape=jax.ShapeDtypeStruct((m, n), BF16),
        compiler_params=_cparams(2),
        name="in_proj_b",
    )(h, w_in_t.reshape(-1, d))


def _q_kernel(a_ref, wm_ref, ws_ref, cos_ref, sin_ref, q_ref):
    a = a_ref[...]
    cos = cos_ref[...]
    sin = sin_ref[...]
    for hd in range(N_HEADS):
        main = _dot(a, wm_ref[:, hd * HEAD_W:(hd + 1) * HEAD_W])
        swap = _dot(a, ws_ref[:, hd * LANES:(hd + 1) * LANES])
        q_ref[:, hd * HEAD_W:hd * HEAD_W + LANES] = (main[:, :LANES] * Q_SCALE).astype(q_ref.dtype)
        rot = (main[:, LANES:] * cos + swap * sin) * Q_SCALE
        q_ref[:, hd * HEAD_W + LANES:(hd + 1) * HEAD_W] = rot.astype(q_ref.dtype)


def _q_proj(qa_n, w_main, w_swap, cos_t, sin_t, rows):
    k = qa_n.shape[1]
    tm = _tile(rows, 768, 640, 512, 256, 128)
    return pl.pallas_call(
        _q_kernel,
        grid=(rows // tm,),
        in_specs=[pl.BlockSpec((tm, k), lambda i: (i, 0)),
                  pl.BlockSpec(w_main.shape, lambda i: (0, 0)),
                  pl.BlockSpec(w_swap.shape, lambda i: (0, 0)),
                  pl.BlockSpec((tm, LANES), lambda i: (i, 0)),
                  pl.BlockSpec((tm, LANES), lambda i: (i, 0))],
        out_specs=pl.BlockSpec((tm, N_HEADS * HEAD_W), lambda i: (i, 0)),
        out_shape=jax.ShapeDtypeStruct((rows, N_HEADS * HEAD_W), BF16),
        compiler_params=_cparams(1),
        name="q_proj",
    )(qa_n, w_main, w_swap, cos_t, sin_t)


def _kv_kernel(c_ref, kr_ref, wkv_ref, k_ref, v_ref):
    c = c_ref[...]
    kr = kr_ref[...]
    lane = lax.broadcasted_iota(jnp.int32, (c.shape[0], LANES), 1)
    ones_col = jnp.where(lane == 0, 1.0, 0.0).astype(v_ref.dtype)
    for hd in range(N_HEADS):
        kv = _dot(c, wkv_ref[:, hd * HEAD_W:(hd + 1) * HEAD_W])
        k_ref[:, hd * HEAD_W:hd * HEAD_W + LANES] = kv[:, :QK_NOPE_DIM].astype(k_ref.dtype)
        k_ref[:, hd * HEAD_W + LANES:(hd + 1) * HEAD_W] = kr
        v_ref[:, hd * HEAD_W:hd * HEAD_W + LANES] = kv[:, QK_NOPE_DIM:].astype(v_ref.dtype)
        v_ref[:, hd * HEAD_W + LANES:(hd + 1) * HEAD_W] = ones_col


def _kv_proj(ckv_n, kr, w_kvb, layer):
    m, k = ckv_n.shape
    tm = _tile(m, 768, 640, 512, 256, 128)
    assert QK_NOPE_DIM == LANES and V_HEAD_DIM == LANES
    return pl.pallas_call(
        _kv_kernel,
        grid=(m // tm,),
        in_specs=[pl.BlockSpec((tm, k), lambda i: (i, 0)),
                  pl.BlockSpec((tm, LANES), lambda i: (i, 0)),
                  pl.BlockSpec((None,) + w_kvb.shape[1:], lambda i: (layer, 0, 0))],
        out_specs=[pl.BlockSpec((tm, N_HEADS * HEAD_W), lambda i: (i, 0)),
                   pl.BlockSpec((tm, N_HEADS * HEAD_W), lambda i: (i, 0))],
        out_shape=[jax.ShapeDtypeStruct((m, N_HEADS * HEAD_W), BF16),
                   jax.ShapeDtypeStruct((m, N_HEADS * HEAD_W), BF16)],
        compiler_params=_cparams(1),
        name="kv_proj",
    )(ckv_n, kr, w_kvb)


def _attn_kernel(q_ref, k_ref, v_ref, o_ref, sa_ref, sb_ref, *, tk, n_chunks):
    q = q_ref[...]
    tq = q.shape[0]

    def scores(c):
        return _dot_t(q, k_ref[c * tk:(c + 1) * tk, :])

    def absorb(s_ref, c, carry):
        m, acc = carry
        s = s_ref[...]
        m_new = jnp.maximum(m, jnp.max(s, axis=-1, keepdims=True))
        p = jnp.exp2(s - m_new).astype(v_ref.dtype)
        acc = jnp.exp2(m - m_new) * acc + _dot(p, v_ref[c * tk:(c + 1) * tk, :])
        return m_new, acc

    s_refs = (sa_ref, sb_ref)
    carry = (jnp.full((tq, 1), -jnp.inf, F32), jnp.zeros((tq, HEAD_W), F32))
    sa_ref[...] = scores(0)
    for c in range(n_chunks):
        if c + 1 < n_chunks:
            s_refs[(c + 1) % 2][...] = scores(c + 1)
        carry = absorb(s_refs[c % 2], c, carry)
    _, acc = carry
    o_ref[...] = (acc[:, :V_HEAD_DIM] / acc[:, V_HEAD_DIM:V_HEAD_DIM + 1]).astype(o_ref.dtype)


def _attention(q, k, v, q_row0, n_q, k_row0, n_k):
    tq = _tile(n_q, 1024, 512, 256, 128)
    tk = _tile(n_k, 768, 640, 512, 384, 256, 128)
    assert q_row0 % tq == 0 and k_row0 % n_k == 0
    q_blk0 = q_row0 // tq
    k_blk = k_row0 // n_k
    return pl.pallas_call(
        functools.partial(_attn_kernel, tk=tk, n_chunks=n_k // tk),
        grid=(N_HEADS, n_q // tq),
        in_specs=[pl.BlockSpec((tq, HEAD_W), lambda h, i: (q_blk0 + i, h)),
                  pl.BlockSpec((n_k, HEAD_W), lambda h, i: (k_blk, h)),
                  pl.BlockSpec((n_k, HEAD_W), lambda h, i: (k_blk, h))],
        out_specs=pl.BlockSpec((tq, V_HEAD_DIM), lambda h, i: (i, h)),
        out_shape=jax.ShapeDtypeStruct((n_q, N_HEADS * V_HEAD_DIM), BF16),
        scratch_shapes=[pltpu.VMEM((tq, tk), F32), pltpu.VMEM((tq, tk), F32)],
        compiler_params=_cparams(2),
        name="attention",
    )(q, k, v)


def _conv_gate_kernel(cx_ref, cb_ref, cc_ref, cxp_ref, ccp_ref, cxn_ref, ccn_ref, wc_ref, z_ref, *,
                      seg_starts, seg_ends, col_chunk):
    tm, width = z_ref.shape
    loc = lax.broadcasted_iota(jnp.int32, (tm, 1), 0)
    row = loc + pl.program_id(0) * tm
    first = functools.reduce(jnp.logical_or, [row == r for r in seg_starts])
    last = functools.reduce(jnp.logical_or, [row == r for r in seg_ends])
    for c0 in range(0, width, col_chunk):
        cs = slice(c0, c0 + col_chunk)
        u = cx_ref[:, cs].astype(F32) * cc_ref[:, cs].astype(F32)
        u_halo_prev = (cxp_ref[SUBLANES - 1:SUBLANES, cs].astype(F32)
                       * ccp_ref[SUBLANES - 1:SUBLANES, cs].astype(F32))
        u_halo_next = cxn_ref[0:1, cs].astype(F32) * ccn_ref[0:1, cs].astype(F32)
        u_prev = jnp.where(loc == 0, u_halo_prev, pltpu.roll(u, 1, 0))
        u_prev = jnp.where(first, 0.0, u_prev)
        u_next = jnp.where(loc == tm - 1, u_halo_next, pltpu.roll(u, tm - 1, 0))
        u_next = jnp.where(last, 0.0, u_next)
        conv = wc_ref[0:1, cs] * u_prev + wc_ref[1:2, cs] * u + wc_ref[2:3, cs] * u_next
        z_ref[:, cs] = (cb_ref[:, cs].astype(F32) * conv).astype(z_ref.dtype)


def _conv_gate(p, w_conv, rows, s_len, m_len):
    width = w_conv.shape[1]
    tm = _tile(rows, 768, 512, 384, 256, 128, 64)
    hb = tm // SUBLANES
    last_hb = p.shape[0] // SUBLANES - 1

    def prev_map(col):
        return lambda i: (jnp.maximum(i * hb - 1, 0), col)

    def next_map(col):
        return lambda i: (jnp.minimum((i + 1) * hb, last_hb), col)

    kern = functools.partial(_conv_gate_kernel, seg_starts=(0, s_len), seg_ends=(s_len - 1, m_len - 1),
                             col_chunk=_tile(width, 512, LANES))
    return pl.pallas_call(
        kern,
        grid=(rows // tm,),
        in_specs=[pl.BlockSpec((tm, width), lambda i: (i, 0)),
                  pl.BlockSpec((tm, width), lambda i: (i, 1)),
                  pl.BlockSpec((tm, width), lambda i: (i, 2)),
                  pl.BlockSpec((SUBLANES, width), prev_map(0)),
                  pl.BlockSpec((SUBLANES, width), prev_map(2)),
                  pl.BlockSpec((SUBLANES, width), next_map(0)),
                  pl.BlockSpec((SUBLANES, width), next_map(2)),
                  pl.BlockSpec((CONV_K, width), lambda i: (0, 0))],
        out_specs=pl.BlockSpec((tm, width), lambda i: (i, 0)),
        out_shape=jax.ShapeDtypeStruct((rows, width), BF16),
        compiler_params=_cparams(1),
        name="conv_gate",
    )(p, p, p, p, p, p, p, w_conv)


def _merge_kernel(attn_ref, z_ref, sga_ref, sgb_ref, woa_ref, wob_ref, o_ref):
    o_a = _dot(attn_ref[...], woa_ref[...])
    o_b = _dot(z_ref[...], wob_ref[...])
    o_ref[...] = (sga_ref[...].astype(F32) * o_a + sgb_ref[...].astype(F32) * o_b).astype(o_ref.dtype)


def _merge(attn, z, p, w_oa, w_ob, layer, rows):
    attn_w = w_oa.shape[1]
    _, width, d = w_ob.shape
    tm = _tile(rows, 768, 512, 384, 256, 128, 64)
    tn = _tile(d, 1024, 512, 256, LANES)
    ga_blk = 3 * width // tn
    gb_blk = (3 * width + d) // tn
    return pl.pallas_call(
        _merge_kernel,
        grid=(d // tn, rows // tm),
        in_specs=[pl.BlockSpec((tm, attn_w), lambda j, i: (i, 0)),
                  pl.BlockSpec((tm, width), lambda j, i: (i, 0)),
                  pl.BlockSpec((tm, tn), lambda j, i: (i, ga_blk + j)),
                  pl.BlockSpec((tm, tn), lambda j, i: (i, gb_blk + j)),
                  pl.BlockSpec((None, attn_w, tn), lambda j, i: (layer, 0, j)),
                  pl.BlockSpec((None, width, tn), lambda j, i: (layer, 0, j))],
        out_specs=pl.BlockSpec((tm, tn), lambda j, i: (i, j)),
        out_shape=jax.ShapeDtypeStruct((rows, d), BF16),
        compiler_params=_cparams(2),
        name="merge",
    )(attn, z, p, p, w_oa, w_ob)


def _mm_res_kernel(a_ref, w_ref, x_ref, gt_ref, o_ref, *, n_lat):
    tm = a_ref.shape[0]
    row = lax.broadcasted_iota(jnp.int32, (tm, 1), 0) + pl.program_id(1) * tm
    gate = jnp.where(row >= n_lat, gt_ref[1:2, :], gt_ref[0:1, :])
    o_ref[...] = x_ref[...] + gate * _dot(a_ref[...], _bf16(w_ref[...]))


def _matmul_residual(a, w, layer, x, mods, gate_blk, rows, n_lat):
    _, k, n = w.shape
    tm = _tile(rows, 768, 640, 512, 256, 128)
    tn = _tile(n, 1024 if k <= n else 512, 512, 256, LANES)
    nt = n // tn
    return pl.pallas_call(
        functools.partial(_mm_res_kernel, n_lat=n_lat),
        grid=(nt, rows // tm),
        in_specs=[pl.BlockSpec((tm, k), lambda j, i: (i, 0)),
                  pl.BlockSpec((None, k, tn), lambda j, i: (layer, 0, j)),
                  pl.BlockSpec((tm, tn), lambda j, i: (i, j)),
                  pl.BlockSpec((2, tn), lambda j, i: (0, gate_blk * nt + j))],
        out_specs=pl.BlockSpec((tm, tn), lambda j, i: (i, j)),
        out_shape=jax.ShapeDtypeStruct((rows, n), F32),
        compiler_params=_cparams(2),
        name="matmul_residual",
    )(a, w, x, mods)


def _block_in_use(eid_ref):
    return pl.program_id(1) < eid_ref[pl.num_programs(1)]


def _glu_kernel(eid_ref, a_ref, wg_ref, wu_ref, o_ref):
    @pl.when(_block_in_use(eid_ref))
    def _():
        a = _bf16(a_ref[...])
        gate = _dot(a, _bf16(wg_ref[0]))
        up = _dot(a, _bf16(wu_ref[0]))
        o_ref[...] = (gate * jax.nn.sigmoid(gate) * up).astype(o_ref.dtype)

    @pl.when(jnp.logical_not(_block_in_use(eid_ref)))
    def _():
        o_ref[...] = jnp.zeros_like(o_ref)


def _glu(a, w_gate, w_up, eid, tm):
    r, k = a.shape
    f = w_gate.shape[2]
    tn = _tile(f, 512, 1408, 256, LANES)
    grid_spec = pltpu.PrefetchScalarGridSpec(
        num_scalar_prefetch=1,
        grid=(f // tn, r // tm),
        in_specs=[pl.BlockSpec((tm, k), lambda j, i, e: (i, 0)),
                  pl.BlockSpec((1, k, tn), lambda j, i, e: (e[i], 0, j)),
                  pl.BlockSpec((1, k, tn), lambda j, i, e: (e[i], 0, j))],
        out_specs=pl.BlockSpec((tm, tn), lambda j, i, e: (i, j)),
    )
    return pl.pallas_call(
        _glu_kernel,
        grid_spec=grid_spec,
        out_shape=jax.ShapeDtypeStruct((r, f), BF16),
        compiler_params=_cparams(2),
        name="glu",
    )(eid, a, w_gate, w_up)


def _down_kernel(eid_ref, a_ref, w_ref, o_ref):
    @pl.when(_block_in_use(eid_ref))
    def _():
        o_ref[...] = _dot(a_ref[...], _bf16(w_ref[0]))

    @pl.when(jnp.logical_not(_block_in_use(eid_ref)))
    def _():
        o_ref[...] = jnp.zeros_like(o_ref)


def _down_grouped(a, w_down, eid, tm):
    r, f = a.shape
    d = w_down.shape[2]
    tn = _tile(d, 1024, 512, 256, LANES)
    grid_spec = pltpu.PrefetchScalarGridSpec(
        num_scalar_prefetch=1,
        grid=(d // tn, r // tm),
        in_specs=[pl.BlockSpec((tm, f), lambda j, i, e: (i, 0)),
                  pl.BlockSpec((1, f, tn), lambda j, i, e: (e[i], 0, j))],
        out_specs=pl.BlockSpec((tm, tn), lambda j, i, e: (i, j)),
    )
    return pl.pallas_call(
        _down_kernel,
        grid_spec=grid_spec,
        out_shape=jax.ShapeDtypeStruct((r, d), F32),
        compiler_params=_cparams(2),
        name="down_grouped",
    )(eid, a, w_down)


def _row_copies(idx_ref, idx_base, idx_stride, src_ref, dst_ref, sem, n_rows, start):
    def body(r2, c):
        for prio in range(2):
            r = 2 * r2 + prio
            row = idx_ref[idx_base + r * idx_stride]
            cp = pltpu.make_async_copy(src_ref.at[pl.ds(row, 1)], dst_ref.at[pl.ds(r, 1)], sem)
            if start:
                cp.start(priority=prio)
            else:
                cp.wait()
        return c

    assert n_rows % 2 == 0
    lax.fori_loop(0, n_rows // 2, body, 0, unroll=DMA_ISSUE_UNROLL // 2)


def _prefetched_gather(copies):
    i = pl.program_id(0)
    slot = i % 2

    @pl.when(i == 0)
    def _():
        copies(0, 0, True)

    @pl.when(i + 1 < pl.num_programs(0))
    def _():
        copies(i + 1, 1 - slot, True)

    copies(i, slot, False)
    return slot


def _gather_rows_kernel(idx_ref, src_ref, o_ref, g_ref, sem):
    tm = o_ref.shape[0]

    def copies(step, slot, start):
        _row_copies(idx_ref, step * tm, 1, src_ref, g_ref.at[slot], sem.at[slot], tm, start)

    slot = _prefetched_gather(copies)
    o_ref[...] = g_ref[slot]


def _gather_rows(src, idx, tm):
    n = idx.shape[0]
    w = src.shape[1]
    grid_spec = pltpu.PrefetchScalarGridSpec(
        num_scalar_prefetch=1,
        grid=(n // tm,),
        in_specs=[pl.BlockSpec(memory_space=pl.ANY)],
        out_specs=pl.BlockSpec((tm, w), lambda i, idx_ref: (i, 0)),
        scratch_shapes=[pltpu.VMEM((2, tm, w), src.dtype), pltpu.SemaphoreType.DMA((2,))],
    )
    return pl.pallas_call(
        _gather_rows_kernel,
        grid_spec=grid_spec,
        out_shape=jax.ShapeDtypeStruct((n, w), src.dtype),
        compiler_params=_cparams(1),
        name="gather_rows",
    )(idx, src)


def _final_kernel(dest_ref, x_ref, y_hbm_ref, gate_ref, gt_ref, g_ref, o_ref, y_ref, sem):
    tm = x_ref.shape[0]

    def copies(step, slot, start):
        for kk in range(TOP_K):
            _row_copies(dest_ref, step * tm * TOP_K + kk, TOP_K, y_hbm_ref, y_ref.at[slot, kk], sem.at[slot],
                        tm, start)

    slot = _prefetched_gather(copies)
    gates = gate_ref[...]
    y = gates[:, 0:1] * y_ref[slot, 0] + gates[:, 1:2] * y_ref[slot, 1]
    x = x_ref[...] + gt_ref[0:1, :] * y
    o_ref[...] = x * lax.rsqrt(jnp.mean(x * x, axis=-1, keepdims=True) + NORM_EPS) * g_ref[...]


def _combine_final(x, ybuf, dest, gates, mods, gate_blk, g_final):
    r, d = x.shape
    tm = _tile(r, 256, 128, 64, SUBLANES)
    grid_spec = pltpu.PrefetchScalarGridSpec(
        num_scalar_prefetch=1,
        grid=(r // tm,),
        in_specs=[pl.BlockSpec((tm, d), lambda i, dest_ref: (i, 0)),
                  pl.BlockSpec(memory_space=pl.ANY),
                  pl.BlockSpec((tm, LANES), lambda i, dest_ref: (i, 0)),
                  pl.BlockSpec((2, d), lambda i, dest_ref: (0, gate_blk)),
                  pl.BlockSpec((1, d), lambda i, dest_ref: (0, 0))],
        out_specs=pl.BlockSpec((tm, d), lambda i, dest_ref: (i, 0)),
        scratch_shapes=[pltpu.VMEM((2, TOP_K, tm, d), F32), pltpu.SemaphoreType.DMA((2,))],
    )
    return pl.pallas_call(
        _final_kernel,
        grid_spec=grid_spec,
        out_shape=jax.ShapeDtypeStruct((r, d), F32),
        compiler_params=_cparams(1),
        name="combine_final",
    )(dest, x, ybuf, gates, mods, g_final.reshape(1, d))


def _rope_tables(s_len, c_len):
    quarter = QK_ROPE_DIM // 4
    tok = jnp.arange(s_len + c_len, dtype=jnp.int32)[:, None]
    lane = jnp.arange(LANES, dtype=jnp.int32)[None, :]
    group = lane // quarter
    pos = jnp.where(group < 2, tok // GRID_W, tok % GRID_W).astype(F32)
    inv_freq = jnp.power(ROPE_THETA, -(2 * (lane % quarter)).astype(F32) / (QK_ROPE_DIM // 2))
    ang = jnp.where(tok < s_len, pos * inv_freq, 0.0)
    live = group < 4
    cos = jnp.where(live, jnp.cos(ang), 0.0)
    sin = jnp.where(live, jnp.where(group % 2 == 0, -jnp.sin(ang), jnp.sin(ang)), 0.0)
    return cos, sin


def _rope_swap_perm():
    q = QK_ROPE_DIM // 4
    return jnp.concatenate([jnp.arange(q, 2 * q), jnp.arange(0, q), jnp.arange(3 * q, 4 * q), jnp.arange(2 * q, 3 * q)])


def _layer_weights(w_in_t, w_qb_all, li):
    d = w_in_t.shape[2]
    perm = _rope_swap_perm()
    kr_lo = Q_LORA_RANK + KV_LORA_RANK
    kr_hi = kr_lo + QK_ROPE_DIM
    w_kr = w_in_t[li, kr_lo:kr_hi]
    zpad = jnp.zeros((LANES - QK_ROPE_DIM, d), w_in_t.dtype)
    w_a = jnp.concatenate([w_kr, zpad, w_kr[perm], zpad], axis=0)
    w_qb = w_qb_all[li]
    qb = w_qb.reshape(Q_LORA_RANK, N_HEADS, QK_NOPE_DIM + QK_ROPE_DIM)
    q_rope = qb[:, :, QK_NOPE_DIM:]
    hpad = jnp.zeros((Q_LORA_RANK, N_HEADS, LANES - QK_ROPE_DIM), w_qb.dtype)
    w_q_main = jnp.concatenate([qb, hpad], axis=2).reshape(Q_LORA_RANK, N_HEADS * HEAD_W)
    w_q_swap = jnp.concatenate([q_rope[:, :, perm], hpad], axis=2).reshape(Q_LORA_RANK, N_HEADS * LANES)
    return tuple(w.astype(BF16) for w in (w_a, w_q_main, w_q_swap))


def _moe_slots(idx, n_blocks):
    e_flat = idx.reshape(-1)
    onehot = (e_flat[:, None] == jnp.arange(N_EXPERTS)[None, :]).astype(jnp.int32)
    csum = jnp.cumsum(onehot, axis=0)
    counts = csum[-1]
    rank = jnp.sum((csum - onehot) * onehot, axis=1)
    padded = (counts + MOE_ROWS - 1) // MOE_ROWS * MOE_ROWS
    p_end = jnp.cumsum(padded)
    p_start = p_end - padded
    dest = jnp.sum(onehot * p_start[None, :], axis=1) + rank
    blk_lo = jnp.arange(n_blocks) * MOE_ROWS
    block_expert = jnp.minimum(jnp.sum((blk_lo[:, None] >= p_end[None, :]).astype(jnp.int32), axis=1),
                               N_EXPERTS - 1)
    blocks_in_use = p_end[-1:] // MOE_ROWS
    return dest.astype(jnp.int32), jnp.concatenate([block_expert, blocks_in_use]).astype(jnp.int32)


def kernel(x, c, ctx, c_ctx, w_ada, b_ada, g_attn, w_in, g_qa, w_qb, g_kva, w_kvb, w_conv, w_oa, w_ob, w_o,
           g_ffn, w_gate_dense, w_up_dense, w_down_dense, w_router, w_gate_exp, w_up_exp, w_down_exp, g_final):
    _, s_len, d = x.shape
    c_len = ctx.shape[1]
    m_len = s_len + c_len
    depth = w_in.shape[0]
    width = w_conv.shape[2]
    assert depth == 2, "supported stack: dense-FFN layer with context updates, then a final expert-FFN layer"

    mods_all = _ada_mod(c, c_ctx, w_ada, b_ada)
    cos_t, sin_t = _rope_tables(s_len, c_len)
    w_in_t = jnp.swapaxes(w_in, 1, 2)
    w_oa_bf, w_ob_bf, w_kvb_bf = w_oa.astype(BF16), w_ob.astype(BF16), w_kvb.astype(BF16)
    xa = jnp.concatenate([x[0], ctx[0]], axis=0)

    for li in range(depth):
        last = li == depth - 1
        mods = mods_all[li]
        w_a, w_q_main, w_q_swap = _layer_weights(w_in_t, w_qb, li)
        g_a = jnp.concatenate([g_qa[li], g_kva[li]]).reshape(1, -1)
        rows = s_len if last else m_len

        h = _norm_modulate(xa, g_attn[li], mods, 0, 1, s_len, BF16)
        qa_n, ckv_n, kr = _in_proj_a(h, w_in_t, li, w_a, g_a, cos_t, sin_t)
        p = _in_proj_b(h, w_in_t, li, Q_LORA_RANK + KV_LORA_RANK + QK_ROPE_DIM, 3 * width)
        q = _q_proj(qa_n, w_q_main, w_q_swap, cos_t, sin_t, rows)
        k, v = _kv_proj(ckv_n, kr, w_kvb_bf, li)
        attn = _attention(q, k, v, 0, s_len, 0, m_len)
        if not last:
            attn = jnp.concatenate([attn, _attention(q, k, v, s_len, c_len, s_len, c_len)], axis=0)
        z = _conv_gate(p, w_conv[li], rows, s_len, m_len)
        merged = _merge(attn, z, p, w_oa_bf, w_ob_bf, li, rows)
        xa = _matmul_residual(merged, w_o, li, xa, mods, 2, rows, s_len)

        j = li // 2
        if li % 2 == 0:
            h2 = _norm_modulate(xa, g_ffn[li], mods, 3, 4, s_len, BF16)
            tm = _tile(rows, 1408, 768, 640, 512, 256, 128)
            eid = jnp.zeros((rows // tm + 1,), jnp.int32).at[-1].set(rows // tm)
            hid = _glu(h2, w_gate_dense[j][None], w_up_dense[j][None], eid, tm)
            xa = _matmul_residual(hid, w_down_dense.astype(BF16), j, xa, mods, 5, rows, s_len)
        else:
            h2, idx, gates = _norm_modulate_route(xa, g_ffn[li], mods, 3, 4, w_router[j])
            n_assign = rows * TOP_K
            n_blocks = -(-n_assign // MOE_ROWS) + N_EXPERTS
            dest, block_expert = _moe_slots(idx[:, :TOP_K], n_blocks)
            tok = jnp.arange(n_assign, dtype=jnp.int32) // TOP_K
            slot_tok = jnp.zeros((n_blocks * MOE_ROWS,), jnp.int32).at[dest].set(tok, unique_indices=True)
            buf = _gather_rows(h2, slot_tok, MOE_ROWS)
            hid = _glu(buf, w_gate_exp[j], w_up_exp[j], block_expert, MOE_ROWS)
            ybuf = _down_grouped(hid, w_down_exp[j], block_expert, MOE_ROWS)
            return _combine_final(xa, ybuf, dest, gates, mods, 5, g_final)[None]
    raise AssertionError("unreachable: the final layer returns")
```

```python
import functools

import jax
import jax.numpy as jnp
from jax import lax
from jax.experimental import pallas as pl
from jax.experimental.pallas import tpu as pltpu

F32 = jnp.float32
BF16 = jnp.bfloat16

N_HEADS = 16
QK_NOPE_DIM = 128
QK_ROPE_DIM = 64
V_HEAD_DIM = 128
Q_LORA_RANK = 512
KV_LORA_RANK = 512
GRID_W = 64
ROPE_THETA = 10000.0
ATTN_SCALE = (QK_NOPE_DIM + QK_ROPE_DIM) ** -0.5
Q_SCALE = ATTN_SCALE * 1.4426950408889634
CONV_K = 3
N_EXPERTS = 8
TOP_K = 2
NORM_EPS = 1e-6

LANES = 128
SUBLANES = 8
HEAD_W = 2 * LANES
VMEM_LIMIT_BYTES = 56 * 1024 * 1024

MOE_ROWS = 256
DMA_ISSUE_UNROLL = 8


def _tile(n, *cands):
    for c in cands:
        if n % c == 0:
            return c
    return n


def _cparams(n_axes):
    return pltpu.CompilerParams(dimension_semantics=("arbitrary",) * n_axes,
                                vmem_limit_bytes=VMEM_LIMIT_BYTES)


def _dot(a, b):
    return jnp.dot(a, b, preferred_element_type=F32)


def _dot_t(a, bt):
    return lax.dot_general(a, bt, (((1,), (1,)), ((), ())), preferred_element_type=F32)


def _bf16(w):
    return w if w.dtype == BF16 else w.astype(BF16)


def _ada_kernel(xt_ref, w_ref, b_ref, o_ref, *, k_chunk):
    d = xt_ref.shape[0]
    tn = o_ref.shape[-1]

    def body(k, acc):
        a0, a1 = acc
        ks = pl.multiple_of(k * k_chunk, k_chunk)
        xt = xt_ref[pl.ds(ks, k_chunk), :]
        s = xt * jax.nn.sigmoid(xt)
        w = w_ref[0, pl.ds(ks, k_chunk), :]
        a0 = a0 + jnp.sum(w * s[:, 0:1], axis=0, keepdims=True)
        a1 = a1 + jnp.sum(w * s[:, 1:2], axis=0, keepdims=True)
        return a0, a1

    z = jnp.zeros((1, tn), F32)
    a0, a1 = lax.fori_loop(0, d // k_chunk, body, (z, z))
    o_ref[0, 0:1, :] = a0 + b_ref[0]
    o_ref[0, 1:2, :] = a1 + b_ref[0]


def _ada_mod(c, c_ctx, w_ada, b_ada):
    depth, d, n = w_ada.shape
    xt = jnp.stack([c[0], c_ctx], axis=1)
    tn = _tile(n, 1024, 512, LANES)
    k_chunk = _tile(d, 256, SUBLANES)
    return pl.pallas_call(
        functools.partial(_ada_kernel, k_chunk=k_chunk),
        grid=(depth, n // tn),
        in_specs=[pl.BlockSpec((d, 2), lambda l, j: (0, 0)),
                  pl.BlockSpec((1, d, tn), lambda l, j: (l, 0, j)),
                  pl.BlockSpec((1, 1, tn), lambda l, j: (l, 0, j))],
        out_specs=pl.BlockSpec((1, 2, tn), lambda l, j: (l, 0, j)),
        out_shape=jax.ShapeDtypeStruct((depth, 2, n), F32),
        compiler_params=_cparams(2),
        name="ada_mod",
    )(xt, w_ada, b_ada.reshape(depth, 1, n))


def _norm_mod(x, g_ref, sh_ref, sc_ref, is_ctx):
    y = x * lax.rsqrt(jnp.mean(x * x, axis=-1, keepdims=True) + NORM_EPS) * g_ref[...]
    sh = jnp.where(is_ctx, sh_ref[1:2, :], sh_ref[0:1, :])
    sc = jnp.where(is_ctx, sc_ref[1:2, :], sc_ref[0:1, :])
    return y * (1.0 + sc) + sh


def _norm_kernel(x_ref, g_ref, sh_ref, sc_ref, o_ref, *, n_lat_tiles):
    is_ctx = pl.program_id(0) >= n_lat_tiles
    o_ref[...] = _norm_mod(x_ref[...], g_ref, sh_ref, sc_ref, is_ctx).astype(o_ref.dtype)


def _norm_router_kernel(x_ref, g_ref, sh_ref, sc_ref, wr_ref, h_ref, idx_ref, gate_ref):
    h = _norm_mod(x_ref[...], g_ref, sh_ref, sc_ref, False)
    h_ref[...] = h
    logits = jnp.dot(h, wr_ref[...], preferred_element_type=F32, precision=lax.Precision.HIGHEST)
    lane = lax.broadcasted_iota(jnp.int32, logits.shape, 1).astype(F32)
    neg = jnp.float32(-jnp.inf)
    l1 = jnp.where(lane < N_EXPERTS, logits, neg)
    v1 = jnp.max(l1, axis=-1, keepdims=True)
    i1 = jnp.min(jnp.where(l1 == v1, lane, float(LANES)), axis=-1, keepdims=True)
    l2 = jnp.where(lane == i1, neg, l1)
    v2 = jnp.max(l2, axis=-1, keepdims=True)
    i2 = jnp.min(jnp.where(l2 == v2, lane, float(LANES)), axis=-1, keepdims=True)
    e = jnp.exp(v2 - v1)
    g1 = 1.0 / (1.0 + e)
    g2 = e / (1.0 + e)
    idx_ref[...] = jnp.where(lane == 0, i1, jnp.where(lane == 1, i2, 0.0)).astype(jnp.int32)
    gate_ref[...] = jnp.where(lane == 0, g1, jnp.where(lane == 1, g2, 0.0))


def _norm_modulate(x, g, mods, sh_blk, sc_blk, n_lat, out_dtype):
    r, d = x.shape
    tm = _tile(n_lat, 256, 128, 64, SUBLANES) if r > n_lat else _tile(r, 256, 128, 64, SUBLANES)
    if r > n_lat:
        tm = _tile(r - n_lat, tm, 128, 64, SUBLANES)
    return pl.pallas_call(
        functools.partial(_norm_kernel, n_lat_tiles=n_lat // tm),
        grid=(r // tm,),
        in_specs=[pl.BlockSpec((tm, d), lambda i: (i, 0)),
                  pl.BlockSpec((1, d), lambda i: (0, 0)),
                  pl.BlockSpec((2, d), lambda i: (0, sh_blk)),
                  pl.BlockSpec((2, d), lambda i: (0, sc_blk))],
        out_specs=pl.BlockSpec((tm, d), lambda i: (i, 0)),
        out_shape=jax.ShapeDtypeStruct((r, d), out_dtype),
        compiler_params=_cparams(1),
        name="norm_modulate",
    )(x, g.reshape(1, d), mods, mods)


def _norm_modulate_route(x, g, mods, sh_blk, sc_blk, w_router):
    r, d = x.shape
    tm = _tile(r, 512, 256, 128, 64, SUBLANES)
    wr = jnp.pad(w_router, ((0, 0), (0, LANES - N_EXPERTS)))
    return pl.pallas_call(
        _norm_router_kernel,
        grid=(r // tm,),
        in_specs=[pl.BlockSpec((tm, d), lambda i: (i, 0)),
                  pl.BlockSpec((1, d), lambda i: (0, 0)),
                  pl.BlockSpec((2, d), lambda i: (0, sh_blk)),
                  pl.BlockSpec((2, d), lambda i: (0, sc_blk)),
                  pl.BlockSpec((d, LANES), lambda i: (0, 0))],
        out_specs=[pl.BlockSpec((tm, d), lambda i: (i, 0)),
                   pl.BlockSpec((tm, LANES), lambda i: (i, 0)),
                   pl.BlockSpec((tm, LANES), lambda i: (i, 0))],
        out_shape=[jax.ShapeDtypeStruct((r, d), F32),
                   jax.ShapeDtypeStruct((r, LANES), jnp.int32),
                   jax.ShapeDtypeStruct((r, LANES), F32)],
        compiler_params=_cparams(1),
        name="norm_modulate_route",
    )(x, g.reshape(1, d), mods, mods, wr)


def _in_a_kernel(h_ref, wt_ref, wkr_ref, g_ref, cos_ref, sin_ref, qa_ref, ckv_ref, kr_ref):
    h = h_ref[...]
    acc = _dot_t(h, _bf16(wt_ref[...]))
    acc_kr = _dot_t(h, wkr_ref[...])

    def rms(v, g):
        return v * lax.rsqrt(jnp.mean(v * v, axis=-1, keepdims=True) + NORM_EPS) * g

    q1 = Q_LORA_RANK
    c1 = q1 + KV_LORA_RANK
    qa_ref[...] = rms(acc[:, :q1], g_ref[:, :q1]).astype(qa_ref.dtype)
    ckv_ref[...] = rms(acc[:, q1:c1], g_ref[:, q1:c1]).astype(ckv_ref.dtype)
    kr = acc_kr[:, :LANES] * cos_ref[...] + acc_kr[:, LANES:] * sin_ref[...]
    kr_ref[...] = kr.astype(kr_ref.dtype)


def _in_proj_a(h, w_in_t, li, w_kr, g_a, cos_t, sin_t):
    m, d = h.shape
    n = Q_LORA_RANK + KV_LORA_RANK
    base = li * w_in_t.shape[1]
    assert base % SUBLANES == 0
    tm = _tile(m, 768, 640, 512, 256, 128)
    return pl.pallas_call(
        _in_a_kernel,
        grid=(m // tm,),
        in_specs=[pl.BlockSpec((tm, d), lambda i: (i, 0)),
                  pl.BlockSpec((pl.Element(n), pl.Element(d)), lambda i: (base, 0)),
                  pl.BlockSpec(w_kr.shape, lambda i: (0, 0)),
                  pl.BlockSpec((1, n), lambda i: (0, 0)),
                  pl.BlockSpec((tm, LANES), lambda i: (i, 0)),
                  pl.BlockSpec((tm, LANES), lambda i: (i, 0))],
        out_specs=[pl.BlockSpec((tm, Q_LORA_RANK), lambda i: (i, 0)),
                   pl.BlockSpec((tm, KV_LORA_RANK), lambda i: (i, 0)),
                   pl.BlockSpec((tm, LANES), lambda i: (i, 0))],
        out_shape=[jax.ShapeDtypeStruct((m, Q_LORA_RANK), BF16),
                   jax.ShapeDtypeStruct((m, KV_LORA_RANK), BF16),
                   jax.ShapeDtypeStruct((m, LANES), BF16)],
        compiler_params=_cparams(1),
        name="in_proj_a",
    )(h, w_in_t.reshape(-1, d), w_kr, g_a, cos_t, sin_t)


def _in_b_kernel(h_ref, wt_ref, o_ref, *, first_gate_tile):
    acc = _dot_t(h_ref[...], _bf16(wt_ref[...]))
    is_gate = pl.program_id(0) >= first_gate_tile

    @pl.when(is_gate)
    def _():
        o_ref[...] = jax.nn.sigmoid(acc).astype(o_ref.dtype)

    @pl.when(jnp.logical_not(is_gate))
    def _():
        o_ref[...] = acc.astype(o_ref.dtype)


def _in_proj_b(h, w_in_t, li, row0, first_gate_col):
    m, d = h.shape
    n = w_in_t.shape[1] - row0
    tm = _tile(m, 1408, 768, 640, 512, 256, 128)
    tn = _tile(first_gate_col, 1024, 512, 256, LANES)
    base = li * w_in_t.shape[1] + row0
    assert base % SUBLANES == 0 and n % tn == 0
    return pl.pallas_call(
        functools.partial(_in_b_kernel, first_gate_tile=first_gate_col // tn),
        grid=(n // tn, m // tm),
        in_specs=[pl.BlockSpec((tm, d), lambda j, i: (i, 0)),
                  pl.BlockSpec((pl.Element(tn), pl.Element(d)),
                               lambda j, i: (pl.multiple_of(base + j * tn, SUBLANES), 0))],
        out_specs=pl.BlockSpec((tm, tn), lambda j, i: (i, j)),
        out_shape=jax.ShapeDtypeStruct((m, n), BF16),
        compiler_params=_cparams(2),
        name="in_proj_b",
    )(h, w_in_t.reshape(-1, d))


def _q_kernel(a_ref, wm_ref, ws_ref, cos_ref, sin_ref, q_ref):
    a = a_ref[...]
    cos = cos_ref[...]
    sin = sin_ref[...]
    for hd in range(N_HEADS):
        main = _dot(a, wm_ref[:, hd * HEAD_W:(hd + 1) * HEAD_W])
        swap = _dot(a, ws_ref[:, hd * LANES:(hd + 1) * LANES])
        q_ref[:, hd * HEAD_W:hd * HEAD_W + LANES] = (main[:, :LANES] * Q_SCALE).astype(q_ref.dtype)
        rot = (main[:, LANES:] * cos + swap * sin) * Q_SCALE
        q_ref[:, hd * HEAD_W + LANES:(hd + 1) * HEAD_W] = rot.astype(q_ref.dtype)


def _q_proj(qa_n, w_main, w_swap, cos_t, sin_t, rows):
    k = qa_n.shape[1]
    tm = _tile(rows, 768, 640, 512, 256, 128)
    return pl.pallas_call(
        _q_kernel,
        grid=(rows // tm,),
        in_specs=[pl.BlockSpec((tm, k), lambda i: (i, 0)),
                  pl.BlockSpec(w_main.shape, lambda i: (0, 0)),
                  pl.BlockSpec(w_swap.shape, lambda i: (0, 0)),
                  pl.BlockSpec((tm, LANES), lambda i: (i, 0)),
                  pl.BlockSpec((tm, LANES), lambda i: (i, 0))],
        out_specs=pl.BlockSpec((tm, N_HEADS * HEAD_W), lambda i: (i, 0)),
        out_shape=jax.ShapeDtypeStruct((rows, N_HEADS * HEAD_W), BF16),
        compiler_params=_cparams(1),
        name="q_proj",
    )(qa_n, w_main, w_swap, cos_t, sin_t)


def _kv_kernel(c_ref, kr_ref, wkv_ref, k_ref, v_ref):
    c = c_ref[...]
    kr = kr_ref[...]
    lane = lax.broadcasted_iota(jnp.int32, (c.shape[0], LANES), 1)
    ones_col = jnp.where(lane == 0, 1.0, 0.0).astype(v_ref.dtype)
    for hd in range(N_HEADS):
        kv = _dot(c, wkv_ref[:, hd * HEAD_W:(hd + 1) * HEAD_W])
        k_ref[:, hd * HEAD_W:hd * HEAD_W + LANES] = kv[:, :QK_NOPE_DIM].astype(k_ref.dtype)
        k_ref[:, hd * HEAD_W + LANES:(hd + 1) * HEAD_W] = kr
        v_ref[:, hd * HEAD_W:hd * HEAD_W + LANES] = kv[:, QK_NOPE_DIM:].astype(v_ref.dtype)
        v_ref[:, hd * HEAD_W + LANES:(hd + 1) * HEAD_W] = ones_col


def _kv_proj(ckv_n, kr, w_kvb, layer):
    m, k = ckv_n.shape
    tm = _tile(m, 768, 640, 512, 256, 128)
    assert QK_NOPE_DIM == LANES and V_HEAD_DIM == LANES
    return pl.pallas_call(
        _kv_kernel,
        grid=(m // tm,),
        in_specs=[pl.BlockSpec((tm, k), lambda i: (i, 0)),
                  pl.BlockSpec((tm, LANES), lambda i: (i, 0)),
                  pl.BlockSpec((None,) + w_kvb.shape[1:], lambda i: (layer, 0, 0))],
        out_specs=[pl.BlockSpec((tm, N_HEADS * HEAD_W), lambda i: (i, 0)),
                   pl.BlockSpec((tm, N_HEADS * HEAD_W), lambda i: (i, 0))],
        out_shape=[jax.ShapeDtypeStruct((m, N_HEADS * HEAD_W), BF16),
                   jax.ShapeDtypeStruct((m, N_HEADS * HEAD_W), BF16)],
        compiler_params=_cparams(1),
        name="kv_proj",
    )(ckv_n, kr, w_kvb)


def _attn_kernel(q_ref, k_ref, v_ref, o_ref, sa_ref, sb_ref, *, tk, n_chunks):
    q = q_ref[...]
    tq = q.shape[0]

    def scores(c):
        return _dot_t(q, k_ref[c * tk:(c + 1) * tk, :])

    def absorb(s_ref, c, carry):
        m, acc = carry
        s = s_ref[...]
        m_new = jnp.maximum(m, jnp.max(s, axis=-1, keepdims=True))
        p = jnp.exp2(s - m_new).astype(v_ref.dtype)
        acc = jnp.exp2(m - m_new) * acc + _dot(p, v_ref[c * tk:(c + 1) * tk, :])
        return m_new, acc

    s_refs = (sa_ref, sb_ref)
    carry = (jnp.full((tq, 1), -jnp.inf, F32), jnp.zeros((tq, HEAD_W), F32))
    sa_ref[...] = scores(0)
    for c in range(n_chunks):
        if c + 1 < n_chunks:
            s_refs[(c + 1) % 2][...] = scores(c + 1)
        carry = absorb(s_refs[c % 2], c, carry)
    _, acc = carry
    o_ref[...] = (acc[:, :V_HEAD_DIM] / acc[:, V_HEAD_DIM:V_HEAD_DIM + 1]).astype(o_ref.dtype)


def _attention(q, k, v, q_row0, n_q, k_row0, n_k):
    tq = _tile(n_q, 1024, 512, 256, 128)
    tk = _tile(n_k, 768, 640, 512, 384, 256, 128)
    assert q_row0 % tq == 0 and k_row0 % n_k == 0
    q_blk0 = q_row0 // tq
    k_blk = k_row0 // n_k
    return pl.pallas_call(
        functools.partial(_attn_kernel, tk=tk, n_chunks=n_k // tk),
        grid=(N_HEADS, n_q // tq),
        in_specs=[pl.BlockSpec((tq, HEAD_W), lambda h, i: (q_blk0 + i, h)),
                  pl.BlockSpec((n_k, HEAD_W), lambda h, i: (k_blk, h)),
                  pl.BlockSpec((n_k, HEAD_W), lambda h, i: (k_blk, h))],
        out_specs=pl.BlockSpec((tq, V_HEAD_DIM), lambda h, i: (i, h)),
        out_shape=jax.ShapeDtypeStruct((n_q, N_HEADS * V_HEAD_DIM), BF16),
        scratch_shapes=[pltpu.VMEM((tq, tk), F32), pltpu.VMEM((tq, tk), F32)],
        compiler_params=_cparams(2),
        name="attention",
    )(q, k, v)


def _conv_gate_kernel(cx_ref, cb_ref, cc_ref, cxp_ref, ccp_ref, cxn_ref, ccn_ref, wc_ref, z_ref, *,
                      seg_starts, seg_ends, col_chunk):
    tm, width = z_ref.shape
    loc = lax.broadcasted_iota(jnp.int32, (tm, 1), 0)
    row = loc + pl.program_id(0) * tm
    first = functools.reduce(jnp.logical_or, [row == r for r in seg_starts])
    last = functools.reduce(jnp.logical_or, [row == r for r in seg_ends])
    for c0 in range(0, width, col_chunk):
        cs = slice(c0, c0 + col_chunk)
        u = cx_ref[:, cs].astype(F32) * cc_ref[:, cs].astype(F32)
        u_halo_prev = (cxp_ref[SUBLANES - 1:SUBLANES, cs].astype(F32)
                       * ccp_ref[SUBLANES - 1:SUBLANES, cs].astype(F32))
        u_halo_next = cxn_ref[0:1, cs].astype(F32) * ccn_ref[0:1, cs].astype(F32)
        u_prev = jnp.where(loc == 0, u_halo_prev, pltpu.roll(u, 1, 0))
        u_prev = jnp.where(first, 0.0, u_prev)
        u_next = jnp.where(loc == tm - 1, u_halo_next, pltpu.roll(u, tm - 1, 0))
        u_next = jnp.where(last, 0.0, u_next)
        conv = wc_ref[0:1, cs] * u_prev + wc_ref[1:2, cs] * u + wc_ref[2:3, cs] * u_next
        z_ref[:, cs] = (cb_ref[:, cs].astype(F32) * conv).astype(z_ref.dtype)


def _conv_gate(p, w_conv, rows, s_len, m_len):
    width = w_conv.shape[1]
    tm = _tile(rows, 768, 512, 384, 256, 128, 64)
    hb = tm // SUBLANES
    last_hb = p.shape[0] // SUBLANES - 1

    def prev_map(col):
        return lambda i: (jnp.maximum(i * hb - 1, 0), col)

    def next_map(col):
        return lambda i: (jnp.minimum((i + 1) * hb, last_hb), col)

    kern = functools.partial(_conv_gate_kernel, seg_starts=(0, s_len), seg_ends=(s_len - 1, m_len - 1),
                             col_chunk=_tile(width, 512, LANES))
    return pl.pallas_call(
        kern,
        grid=(rows // tm,),
        in_specs=[pl.BlockSpec((tm, width), lambda i: (i, 0)),
                  pl.BlockSpec((tm, width), lambda i: (i, 1)),
                  pl.BlockSpec((tm, width), lambda i: (i, 2)),
                  pl.BlockSpec((SUBLANES, width), prev_map(0)),
                  pl.BlockSpec((SUBLANES, width), prev_map(2)),
                  pl.BlockSpec((SUBLANES, width), next_map(0)),
                  pl.BlockSpec((SUBLANES, width), next_map(2)),
                  pl.BlockSpec((CONV_K, width), lambda i: (0, 0))],
        out_specs=pl.BlockSpec((tm, width), lambda i: (i, 0)),
        out_shape=jax.ShapeDtypeStruct((rows, width), BF16),
        compiler_params=_cparams(1),
        name="conv_gate",
    )(p, p, p, p, p, p, p, w_conv)


def _merge_kernel(attn_ref, z_ref, sga_ref, sgb_ref, woa_ref, wob_ref, o_ref):
    o_a = _dot(attn_ref[...], woa_ref[...])
    o_b = _dot(z_ref[...], wob_ref[...])
    o_ref[...] = (sga_ref[...].astype(F32) * o_a + sgb_ref[...].astype(F32) * o_b).astype(o_ref.dtype)


def _merge(attn, z, p, w_oa, w_ob, layer, rows):
    attn_w = w_oa.shape[1]
    _, width, d = w_ob.shape
    tm = _tile(rows, 768, 512, 384, 256, 128, 64)
    tn = _tile(d, 1024, 512, 256, LANES)
    ga_blk = 3 * width // tn
    gb_blk = (3 * width + d) // tn
    return pl.pallas_call(
        _merge_kernel,
        grid=(d // tn, rows // tm),
        in_specs=[pl.BlockSpec((tm, attn_w), lambda j, i: (i, 0)),
                  pl.BlockSpec((tm, width), lambda j, i: (i, 0)),
                  pl.BlockSpec((tm, tn), lambda j, i: (i, ga_blk + j)),
                  pl.BlockSpec((tm, tn), lambda j, i: (i, gb_blk + j)),
                  pl.BlockSpec((None, attn_w, tn), lambda j, i: (layer, 0, j)),
                  pl.BlockSpec((None, width, tn), lambda j, i: (layer, 0, j))],
        out_specs=pl.BlockSpec((tm, tn), lambda j, i: (i, j)),
        out_shape=jax.ShapeDtypeStruct((rows, d), BF16),
        compiler_params=_cparams(2),
        name="merge",
    )(attn, z, p, p, w_oa, w_ob)


def _mm_res_kernel(a_ref, w_ref, x_ref, gt_ref, o_ref, *, n_lat):
    tm = a_ref.shape[0]
    row = lax.broadcasted_iota(jnp.int32, (tm, 1), 0) + pl.program_id(1) * tm
    gate = jnp.where(row >= n_lat, gt_ref[1:2, :], gt_ref[0:1, :])
    o_ref[...] = x_ref[...] + gate * _dot(a_ref[...], _bf16(w_ref[...]))


def _matmul_residual(a, w, layer, x, mods, gate_blk, rows, n_lat):
    _, k, n = w.shape
    tm = _tile(rows, 768, 640, 512, 256, 128)
    tn = _tile(n, 1024 if k <= n else 512, 512, 256, LANES)
    nt = n // tn
    return pl.pallas_call(
        functools.partial(_mm_res_kernel, n_lat=n_lat),
        grid=(nt, rows // tm),
        in_specs=[pl.BlockSpec((tm, k), lambda j, i: (i, 0)),
                  pl.BlockSpec((None, k, tn), lambda j, i: (layer, 0, j)),
                  pl.BlockSpec((tm, tn), lambda j, i: (i, j)),
                  pl.BlockSpec((2, tn), lambda j, i: (0, gate_blk * nt + j))],
        out_specs=pl.BlockSpec((tm, tn), lambda j, i: (i, j)),
        out_shape=jax.ShapeDtypeStruct((rows, n), F32),
        compiler_params=_cparams(2),
        name="matmul_residual",
    )(a, w, x, mods)


def _block_in_use(eid_ref):
    return pl.program_id(1) < eid_ref[pl.num_programs(1)]


def _glu_kernel(eid_ref, a_ref, wg_ref, wu_ref, o_ref):
    @pl.when(_block_in_use(eid_ref))
    def _():
        a = _bf16(a_ref[...])
        gate = _dot(a, _bf16(wg_ref[0]))
        up = _dot(a, _bf16(wu_ref[0]))
        o_ref[...] = (gate * jax.nn.sigmoid(gate) * up).astype(o_ref.dtype)

    @pl.when(jnp.logical_not(_block_in_use(eid_ref)))
    def _():
        o_ref[...] = jnp.zeros_like(o_ref)


def _glu(a, w_gate, w_up, eid, tm):
    r, k = a.shape
    f = w_gate.shape[2]
    tn = _tile(f, 512, 1408, 256, LANES)
    grid_spec = pltpu.PrefetchScalarGridSpec(
        num_scalar_prefetch=1,
        grid=(f // tn, r // tm),
        in_specs=[pl.BlockSpec((tm, k), lambda j, i, e: (i, 0)),
                  pl.BlockSpec((1, k, tn), lambda j, i, e: (e[i], 0, j)),
                  pl.BlockSpec((1, k, tn), lambda j, i, e: (e[i], 0, j))],
        out_specs=pl.BlockSpec((tm, tn), lambda j, i, e: (i, j)),
    )
    return pl.pallas_call(
        _glu_kernel,
        grid_spec=grid_spec,
        out_shape=jax.ShapeDtypeStruct((r, f), BF16),
        compiler_params=_cparams(2),
        name="glu",
    )(eid, a, w_gate, w_up)


def _down_kernel(eid_ref, a_ref, w_ref, o_ref):
    @pl.when(_block_in_use(eid_ref))
    def _():
        o_ref[...] = _dot(a_ref[...], _bf16(w_ref[0]))

    @pl.when(jnp.logical_not(_block_in_use(eid_ref)))
    def _():
        o_ref[...] = jnp.zeros_like(o_ref)


def _down_grouped(a, w_down, eid, tm):
    r, f = a.shape
    d = w_down.shape[2]
    tn = _tile(d, 1024, 512, 256, LANES)
    grid_spec = pltpu.PrefetchScalarGridSpec(
        num_scalar_prefetch=1,
        grid=(d // tn, r // tm),
        in_specs=[pl.BlockSpec((tm, f), lambda j, i, e: (i, 0)),
                  pl.BlockSpec((1, f, tn), lambda j, i, e: (e[i], 0, j))],
        out_specs=pl.BlockSpec((tm, tn), lambda j, i, e: (i, j)),
    )
    return pl.pallas_call(
        _down_kernel,
        grid_spec=grid_spec,
        out_shape=jax.ShapeDtypeStruct((r, d), F32),
        compiler_params=_cparams(2),
        name="down_grouped",
    )(eid, a, w_down)


def _row_copies(idx_ref, idx_base, idx_stride, src_ref, dst_ref, sem, n_rows, start):
    if not start:
        pltpu.make_async_copy(src_ref.at[pl.ds(0, n_rows)], dst_ref, sem).wait()
        return

    def body(r2, c):
        for prio in range(2):
            r = 2 * r2 + prio
            row = idx_ref[idx_base + r * idx_stride]
            pltpu.make_async_copy(src_ref.at[pl.ds(row, 1)], dst_ref.at[pl.ds(r, 1)], sem).start(priority=prio)
        return c

    assert n_rows % 2 == 0 and dst_ref.shape[0] == n_rows
    lax.fori_loop(0, n_rows // 2, body, 0, unroll=DMA_ISSUE_UNROLL // 2)


def _prefetched_gather(copies):
    i = pl.program_id(0)
    slot = i % 2

    @pl.when(i == 0)
    def _():
        copies(0, 0, True)

    @pl.when(i + 1 < pl.num_programs(0))
    def _():
        copies(i + 1, 1 - slot, True)

    copies(i, slot, False)
    return slot


def _gather_rows_kernel(idx_ref, src_ref, o_ref, g_ref, sem):
    tm = o_ref.shape[0]

    def copies(step, slot, start):
        _row_copies(idx_ref, step * tm, 1, src_ref, g_ref.at[slot], sem.at[slot], tm, start)

    slot = _prefetched_gather(copies)
    o_ref[...] = g_ref[slot]


def _gather_rows(src, idx, tm):
    n = idx.shape[0]
    w = src.shape[1]
    grid_spec = pltpu.PrefetchScalarGridSpec(
        num_scalar_prefetch=1,
        grid=(n // tm,),
        in_specs=[pl.BlockSpec(memory_space=pl.ANY)],
        out_specs=pl.BlockSpec((tm, w), lambda i, idx_ref: (i, 0)),
        scratch_shapes=[pltpu.VMEM((2, tm, w), src.dtype), pltpu.SemaphoreType.DMA((2,))],
    )
    return pl.pallas_call(
        _gather_rows_kernel,
        grid_spec=grid_spec,
        out_shape=jax.ShapeDtypeStruct((n, w), src.dtype),
        compiler_params=_cparams(1),
        name="gather_rows",
    )(idx, src)


def _final_kernel(dest_ref, x_ref, y_hbm_ref, gate_ref, gt_ref, g_ref, o_ref, y_ref, sem):
    tm = x_ref.shape[0]

    def copies(step, slot, start):
        for kk in range(TOP_K):
            _row_copies(dest_ref, step * tm * TOP_K + kk, TOP_K, y_hbm_ref, y_ref.at[slot, kk], sem.at[slot],
                        tm, start)

    slot = _prefetched_gather(copies)
    gates = gate_ref[...]
    y = gates[:, 0:1] * y_ref[slot, 0] + gates[:, 1:2] * y_ref[slot, 1]
    x = x_ref[...] + gt_ref[0:1, :] * y
    o_ref[...] = x * lax.rsqrt(jnp.mean(x * x, axis=-1, keepdims=True) + NORM_EPS) * g_ref[...]


def _combine_final(x, ybuf, dest, gates, mods, gate_blk, g_final):
    r, d = x.shape
    tm = _tile(r, 256, 128, 64, SUBLANES)
    grid_spec = pltpu.PrefetchScalarGridSpec(
        num_scalar_prefetch=1,
        grid=(r // tm,),
        in_specs=[pl.BlockSpec((tm, d), lambda i, dest_ref: (i, 0)),
                  pl.BlockSpec(memory_space=pl.ANY),
                  pl.BlockSpec((tm, LANES), lambda i, dest_ref: (i, 0)),
                  pl.BlockSpec((2, d), lambda i, dest_ref: (0, gate_blk)),
                  pl.BlockSpec((1, d), lambda i, dest_ref: (0, 0))],
        out_specs=pl.BlockSpec((tm, d), lambda i, dest_ref: (i, 0)),
        scratch_shapes=[pltpu.VMEM((2, TOP_K, tm, d), F32), pltpu.SemaphoreType.DMA((2,))],
    )
    return pl.pallas_call(
        _final_kernel,
        grid_spec=grid_spec,
        out_shape=jax.ShapeDtypeStruct((r, d), F32),
        compiler_params=_cparams(1),
        name="combine_final",
    )(dest, x, ybuf, gates, mods, g_final.reshape(1, d))


def _rope_tables(s_len, c_len):
    quarter = QK_ROPE_DIM // 4
    tok = jnp.arange(s_len + c_len, dtype=jnp.int32)[:, None]
    lane = jnp.arange(LANES, dtype=jnp.int32)[None, :]
    group = lane // quarter
    pos = jnp.where(group < 2, tok // GRID_W, tok % GRID_W).astype(F32)
    inv_freq = jnp.power(ROPE_THETA, -(2 * (lane % quarter)).astype(F32) / (QK_ROPE_DIM // 2))
    ang = jnp.where(tok < s_len, pos * inv_freq, 0.0)
    live = group < 4
    cos = jnp.where(live, jnp.cos(ang), 0.0)
    sin = jnp.where(live, jnp.where(group % 2 == 0, -jnp.sin(ang), jnp.sin(ang)), 0.0)
    return cos, sin


def _rope_swap_perm():
    q = QK_ROPE_DIM // 4
    return jnp.concatenate([jnp.arange(q, 2 * q), jnp.arange(0, q), jnp.arange(3 * q, 4 * q), jnp.arange(2 * q, 3 * q)])


def _layer_weights(w_in_t, w_qb_all, li):
    d = w_in_t.shape[2]
    perm = _rope_swap_perm()
    kr_lo = Q_LORA_RANK + KV_LORA_RANK
    kr_hi = kr_lo + QK_ROPE_DIM
    w_kr = w_in_t[li, kr_lo:kr_hi]
    zpad = jnp.zeros((LANES - QK_ROPE_DIM, d), w_in_t.dtype)
    w_a = jnp.concatenate([w_kr, zpad, w_kr[perm], zpad], axis=0)
    w_qb = w_qb_all[li]
    qb = w_qb.reshape(Q_LORA_RANK, N_HEADS, QK_NOPE_DIM + QK_ROPE_DIM)
    q_rope = qb[:, :, QK_NOPE_DIM:]
    hpad = jnp.zeros((Q_LORA_RANK, N_HEADS, LANES - QK_ROPE_DIM), w_qb.dtype)
    w_q_main = jnp.concatenate([qb, hpad], axis=2).reshape(Q_LORA_RANK, N_HEADS * HEAD_W)
    w_q_swap = jnp.concatenate([q_rope[:, :, perm], hpad], axis=2).reshape(Q_LORA_RANK, N_HEADS * LANES)
    return tuple(w.astype(BF16) for w in (w_a, w_q_main, w_q_swap))


def _moe_slots(idx, n_blocks):
    e_flat = idx.reshape(-1)
    onehot = (e_flat[:, None] == jnp.arange(N_EXPERTS)[None, :]).astype(jnp.int32)
    csum = jnp.cumsum(onehot, axis=0)
    counts = csum[-1]
    rank = jnp.sum((csum - onehot) * onehot, axis=1)
    padded = (counts + MOE_ROWS - 1) // MOE_ROWS * MOE_ROWS
    p_end = jnp.cumsum(padded)
    p_start = p_end - padded
    dest = jnp.sum(onehot * p_start[None, :], axis=1) + rank
    blk_lo = jnp.arange(n_blocks) * MOE_ROWS
    block_expert = jnp.minimum(jnp.sum((blk_lo[:, None] >= p_end[None, :]).astype(jnp.int32), axis=1),
                               N_EXPERTS - 1)
    blocks_in_use = p_end[-1:] // MOE_ROWS
    return dest.astype(jnp.int32), jnp.concatenate([block_expert, blocks_in_use]).astype(jnp.int32)


def kernel(x, c, ctx, c_ctx, w_ada, b_ada, g_attn, w_in, g_qa, w_qb, g_kva, w_kvb, w_conv, w_oa, w_ob, w_o,
           g_ffn, w_gate_dense, w_up_dense, w_down_dense, w_router, w_gate_exp, w_up_exp, w_down_exp, g_final):
    _, s_len, d = x.shape
    c_len = ctx.shape[1]
    m_len = s_len + c_len
    depth = w_in.shape[0]
    width = w_conv.shape[2]
    assert depth == 2, "supported stack: dense-FFN layer with context updates, then a final expert-FFN layer"

    mods_all = _ada_mod(c, c_ctx, w_ada, b_ada)
    cos_t, sin_t = _rope_tables(s_len, c_len)
    w_in_t = jnp.swapaxes(w_in, 1, 2)
    w_oa_bf, w_ob_bf, w_kvb_bf = w_oa.astype(BF16), w_ob.astype(BF16), w_kvb.astype(BF16)
    xa = jnp.concatenate([x[0], ctx[0]], axis=0)

    for li in range(depth):
        last = li == depth - 1
        mods = mods_all[li]
        w_a, w_q_main, w_q_swap = _layer_weights(w_in_t, w_qb, li)
        g_a = jnp.concatenate([g_qa[li], g_kva[li]]).reshape(1, -1)
        rows = s_len if last else m_len

        h = _norm_modulate(xa, g_attn[li], mods, 0, 1, s_len, BF16)
        qa_n, ckv_n, kr = _in_proj_a(h, w_in_t, li, w_a, g_a, cos_t, sin_t)
        p = _in_proj_b(h, w_in_t, li, Q_LORA_RANK + KV_LORA_RANK + QK_ROPE_DIM, 3 * width)
        q = _q_proj(qa_n, w_q_main, w_q_swap, cos_t, sin_t, rows)
        k, v = _kv_proj(ckv_n, kr, w_kvb_bf, li)
        attn = _attention(q, k, v, 0, s_len, 0, m_len)
        if not last:
            attn = jnp.concatenate([attn, _attention(q, k, v, s_len, c_len, s_len, c_len)], axis=0)
        z = _conv_gate(p, w_conv[li], rows, s_len, m_len)
        merged = _merge(attn, z, p, w_oa_bf, w_ob_bf, li, rows)
        xa = _matmul_residual(merged, w_o, li, xa, mods, 2, rows, s_len)

        j = li // 2
        if li % 2 == 0:
            h2 = _norm_modulate(xa, g_ffn[li], mods, 3, 4, s_len, BF16)
            tm = _tile(rows, 1408, 768, 640, 512, 256, 128)
            eid = jnp.zeros((rows // tm + 1,), jnp.int32).at[-1].set(rows // tm)
            hid = _glu(h2, w_gate_dense[j][None], w_up_dense[j][None], eid, tm)
            xa = _matmul_residual(hid, w_down_dense.astype(BF16), j, xa, mods, 5, rows, s_len)
        else:
            h2, idx, gates = _norm_modulate_route(xa, g_ffn[li], mods, 3, 4, w_router[j])
            n_assign = rows * TOP_K
            n_blocks = -(-n_assign // MOE_ROWS) + N_EXPERTS
            dest, block_expert = _moe_slots(idx[:, :TOP_K], n_blocks)
            tok = jnp.arange(n_assign, dtype=jnp.int32) // TOP_K
            slot_tok = jnp.zeros((n_blocks * MOE_ROWS,), jnp.int32).at[dest].set(tok, unique_indices=True)
            buf = _gather_rows(h2, slot_tok, MOE_ROWS)
            hid = _glu(buf, w_gate_exp[j], w_up_exp[j], block_expert, MOE_ROWS)
            ybuf = _down_grouped(hid, w_down_exp[j], block_expert, MOE_ROWS)
            return _combine_final(xa, ybuf, dest, gates, mods, 5, g_final)[None]
    raise AssertionError("unreachable: the final layer returns")
```

```python
import functools

import jax
import jax.numpy as jnp
from jax import lax
from jax.experimental import pallas as pl
from jax.experimental.pallas import tpu as pltpu

F32 = jnp.float32
BF16 = jnp.bfloat16

N_HEADS = 16
QK_NOPE_DIM = 128
QK_ROPE_DIM = 64
V_HEAD_DIM = 128
Q_LORA_RANK = 512
KV_LORA_RANK = 512
GRID_W = 64
ROPE_THETA = 10000.0
ATTN_SCALE = (QK_NOPE_DIM + QK_ROPE_DIM) ** -0.5
Q_SCALE = ATTN_SCALE * 1.4426950408889634
CONV_K = 3
N_EXPERTS = 8
TOP_K = 2
NORM_EPS = 1e-6

LANES = 128
SUBLANES = 8
HEAD_W = 2 * LANES
VMEM_LIMIT_BYTES = 56 * 1024 * 1024

MOE_ROWS = 256
DMA_ISSUE_UNROLL = 8


def _tile(n, *cands):
    for c in cands:
        if n % c == 0:
            return c
    return n


def _cparams(n_axes):
    return pltpu.CompilerParams(dimension_semantics=("arbitrary",) * n_axes,
                                vmem_limit_bytes=VMEM_LIMIT_BYTES)


def _dot(a, b):
    return jnp.dot(a, b, preferred_element_type=F32)


def _dot_t(a, bt):
    return lax.dot_general(a, bt, (((1,), (1,)), ((), ())), preferred_element_type=F32)


def _bf16(w):
    return w if w.dtype == BF16 else w.astype(BF16)


def _ada_kernel(xt_ref, w_ref, b_ref, o_ref, *, k_chunk):
    d = xt_ref.shape[0]
    tn = o_ref.shape[-1]

    def body(k, acc):
        a0, a1 = acc
        ks = pl.multiple_of(k * k_chunk, k_chunk)
        xt = xt_ref[pl.ds(ks, k_chunk), :]
        s = xt * jax.nn.sigmoid(xt)
        w = w_ref[0, pl.ds(ks, k_chunk), :]
        a0 = a0 + jnp.sum(w * s[:, 0:1], axis=0, keepdims=True)
        a1 = a1 + jnp.sum(w * s[:, 1:2], axis=0, keepdims=True)
        return a0, a1

    z = jnp.zeros((1, tn), F32)
    a0, a1 = lax.fori_loop(0, d // k_chunk, body, (z, z))
    o_ref[0, 0:1, :] = a0 + b_ref[0]
    o_ref[0, 1:2, :] = a1 + b_ref[0]


def _ada_mod(c, c_ctx, w_ada, b_ada):
    depth, d, n = w_ada.shape
    xt = jnp.stack([c[0], c_ctx], axis=1)
    tn = _tile(n, 1024, 512, LANES)
    k_chunk = _tile(d, 256, SUBLANES)
    return pl.pallas_call(
        functools.partial(_ada_kernel, k_chunk=k_chunk),
        grid=(depth, n // tn),
        in_specs=[pl.BlockSpec((d, 2), lambda l, j: (0, 0)),
                  pl.BlockSpec((1, d, tn), lambda l, j: (l, 0, j)),
                  pl.BlockSpec((1, 1, tn), lambda l, j: (l, 0, j))],
        out_specs=pl.BlockSpec((1, 2, tn), lambda l, j: (l, 0, j)),
        out_shape=jax.ShapeDtypeStruct((depth, 2, n), F32),
        compiler_params=_cparams(2),
        name="ada_mod",
    )(xt, w_ada, b_ada.reshape(depth, 1, n))


def _norm_mod(x, g_ref, sh_ref, sc_ref, is_ctx):
    y = x * lax.rsqrt(jnp.mean(x * x, axis=-1, keepdims=True) + NORM_EPS) * g_ref[...]
    sh = jnp.where(is_ctx, sh_ref[1:2, :], sh_ref[0:1, :])
    sc = jnp.where(is_ctx, sc_ref[1:2, :], sc_ref[0:1, :])
    return y * (1.0 + sc) + sh


def _norm_kernel(x_ref, g_ref, sh_ref, sc_ref, o_ref, *, n_lat_tiles):
    is_ctx = pl.program_id(0) >= n_lat_tiles
    o_ref[...] = _norm_mod(x_ref[...], g_ref, sh_ref, sc_ref, is_ctx).astype(o_ref.dtype)


def _norm_router_kernel(x_ref, g_ref, sh_ref, sc_ref, wr_ref, h_ref, idx_ref, gate_ref):
    h = _norm_mod(x_ref[...], g_ref, sh_ref, sc_ref, False)
    h_ref[...] = h
    logits = jnp.dot(h, wr_ref[...], preferred_element_type=F32, precision=lax.Precision.HIGHEST)
    lane = lax.broadcasted_iota(jnp.int32, logits.shape, 1).astype(F32)
    neg = jnp.float32(-jnp.inf)
    l1 = jnp.where(lane < N_EXPERTS, logits, neg)
    v1 = jnp.max(l1, axis=-1, keepdims=True)
    i1 = jnp.min(jnp.where(l1 == v1, lane, float(LANES)), axis=-1, keepdims=True)
    l2 = jnp.where(lane == i1, neg, l1)
    v2 = jnp.max(l2, axis=-1, keepdims=True)
    i2 = jnp.min(jnp.where(l2 == v2, lane, float(LANES)), axis=-1, keepdims=True)
    e = jnp.exp(v2 - v1)
    g1 = 1.0 / (1.0 + e)
    g2 = e / (1.0 + e)
    idx_ref[...] = jnp.where(lane == 0, i1, jnp.where(lane == 1, i2, 0.0)).astype(jnp.int32)
    gate_ref[...] = jnp.where(lane == 0, g1, jnp.where(lane == 1, g2, 0.0))


def _norm_modulate(x, g, mods, sh_blk, sc_blk, n_lat, out_dtype):
    r, d = x.shape
    tm = _tile(n_lat, 256, 128, 64, SUBLANES) if r > n_lat else _tile(r, 256, 128, 64, SUBLANES)
    if r > n_lat:
        tm = _tile(r - n_lat, tm, 128, 64, SUBLANES)
    return pl.pallas_call(
        functools.partial(_norm_kernel, n_lat_tiles=n_lat // tm),
        grid=(r // tm,),
        in_specs=[pl.BlockSpec((tm, d), lambda i: (i, 0)),
                  pl.BlockSpec((1, d), lambda i: (0, 0)),
                  pl.BlockSpec((2, d), lambda i: (0, sh_blk)),
                  pl.BlockSpec((2, d), lambda i: (0, sc_blk))],
        out_specs=pl.BlockSpec((tm, d), lambda i: (i, 0)),
        out_shape=jax.ShapeDtypeStruct((r, d), out_dtype),
        compiler_params=_cparams(1),
        name="norm_modulate",
    )(x, g.reshape(1, d), mods, mods)


def _norm_modulate_route(x, g, mods, sh_blk, sc_blk, w_router):
    r, d = x.shape
    tm = _tile(r, 512, 256, 128, 64, SUBLANES)
    wr = jnp.pad(w_router, ((0, 0), (0, LANES - N_EXPERTS)))
    return pl.pallas_call(
        _norm_router_kernel,
        grid=(r // tm,),
        in_specs=[pl.BlockSpec((tm, d), lambda i: (i, 0)),
                  pl.BlockSpec((1, d), lambda i: (0, 0)),
                  pl.BlockSpec((2, d), lambda i: (0, sh_blk)),
                  pl.BlockSpec((2, d), lambda i: (0, sc_blk)),
                  pl.BlockSpec((d, LANES), lambda i: (0, 0))],
        out_specs=[pl.BlockSpec((tm, d), lambda i: (i, 0)),
                   pl.BlockSpec((tm, LANES), lambda i: (i, 0)),
                   pl.BlockSpec((tm, LANES), lambda i: (i, 0))],
        out_shape=[jax.ShapeDtypeStruct((r, d), F32),
                   jax.ShapeDtypeStruct((r, LANES), jnp.int32),
                   jax.ShapeDtypeStruct((r, LANES), F32)],
        compiler_params=_cparams(1),
        name="norm_modulate_route",
    )(x, g.reshape(1, d), mods, mods, wr)


def _in_a_kernel(h_ref, wt_ref, wkr_ref, g_ref, cos_ref, sin_ref, qa_ref, ckv_ref, kr_ref):
    h = h_ref[...]
    acc = _dot_t(h, _bf16(wt_ref[...]))
    acc_kr = _dot_t(h, wkr_ref[...])

    def rms(v, g):
        return v * lax.rsqrt(jnp.mean(v * v, axis=-1, keepdims=True) + NORM_EPS) * g

    q1 = Q_LORA_RANK
    c1 = q1 + KV_LORA_RANK
    qa_ref[...] = rms(acc[:, :q1], g_ref[:, :q1]).astype(qa_ref.dtype)
    ckv_ref[...] = rms(acc[:, q1:c1], g_ref[:, q1:c1]).astype(ckv_ref.dtype)
    kr = acc_kr[:, :LANES] * cos_ref[...] + acc_kr[:, LANES:] * sin_ref[...]
    kr_ref[...] = kr.astype(kr_ref.dtype)


def _in_proj_a(h, w_in_t, li, w_kr, g_a, cos_t, sin_t):
    m, d = h.shape
    n = Q_LORA_RANK + KV_LORA_RANK
    base = li * w_in_t.shape[1]
    assert base % SUBLANES == 0
    tm = _tile(m, 768, 640, 512, 256, 128)
    return pl.pallas_call(
        _in_a_kernel,
        grid=(m // tm,),
        in_specs=[pl.BlockSpec((tm, d), lambda i: (i, 0)),
                  pl.BlockSpec((pl.Element(n), pl.Element(d)), lambda i: (base, 0)),
                  pl.BlockSpec(w_kr.shape, lambda i: (0, 0)),
                  pl.BlockSpec((1, n), lambda i: (0, 0)),
                  pl.BlockSpec((tm, LANES), lambda i: (i, 0)),
                  pl.BlockSpec((tm, LANES), lambda i: (i, 0))],
        out_specs=[pl.BlockSpec((tm, Q_LORA_RANK), lambda i: (i, 0)),
                   pl.BlockSpec((tm, KV_LORA_RANK), lambda i: (i, 0)),
                   pl.BlockSpec((tm, LANES), lambda i: (i, 0))],
        out_shape=[jax.ShapeDtypeStruct((m, Q_LORA_RANK), BF16),
                   jax.ShapeDtypeStruct((m, KV_LORA_RANK), BF16),
                   jax.ShapeDtypeStruct((m, LANES), BF16)],
        compiler_params=_cparams(1),
        name="in_proj_a",
    )(h, w_in_t.reshape(-1, d), w_kr, g_a, cos_t, sin_t)


def _in_b_kernel(h_ref, wt_ref, o_ref, *, first_gate_tile):
    acc = _dot_t(h_ref[...], _bf16(wt_ref[...]))
    is_gate = pl.program_id(0) >= first_gate_tile

    @pl.when(is_gate)
    def _():
        o_ref[...] = jax.nn.sigmoid(acc).astype(o_ref.dtype)

    @pl.when(jnp.logical_not(is_gate))
    def _():
        o_ref[...] = acc.astype(o_ref.dtype)


def _in_proj_b(h, w_in_t, li, row0, first_gate_col):
    m, d = h.shape
    n = w_in_t.shape[1] - row0
    tm = _tile(m, 1408, 768, 640, 512, 256, 128)
    tn = _tile(first_gate_col, 1024, 512, 256, LANES)
    base = li * w_in_t.shape[1] + row0
    assert base % SUBLANES == 0 and n % tn == 0
    return pl.pallas_call(
        functools.partial(_in_b_kernel, first_gate_tile=first_gate_col // tn),
        grid=(n // tn, m // tm),
        in_specs=[pl.BlockSpec((tm, d), lambda j, i: (i, 0)),
                  pl.BlockSpec((pl.Element(tn), pl.Element(d)),
                               lambda j, i: (pl.multiple_of(base + j * tn, SUBLANES), 0))],
        out_specs=pl.BlockSpec((tm, tn), lambda j, i: (i, j)),
        out_shape=jax.ShapeDtypeStruct((m, n), BF16),
        compiler_params=_cparams(2),
        name="in_proj_b",
    )(h, w_in_t.reshape(-1, d))


def _q_kernel(a_ref, w_ref, cos_ref, sin_ref, q_ref):
    a = a_ref[...]
    cos = cos_ref[...]
    sin = sin_ref[...]
    for hd in range(N_HEADS):
        acc = _dot(a, w_ref[:, hd * HEAD_W:(hd + 1) * HEAD_W])
        q_ref[:, hd * HEAD_W:hd * HEAD_W + LANES] = (acc[:, :LANES] * Q_SCALE).astype(q_ref.dtype)
        rope = acc[:, LANES:]
        swapped = pltpu.roll(rope, LANES - QK_ROPE_DIM, 1)
        rot = (rope * cos + swapped * sin) * Q_SCALE
        q_ref[:, hd * HEAD_W + LANES:(hd + 1) * HEAD_W] = rot.astype(q_ref.dtype)


def _q_proj(qa_n, w_q, cos_t, sin_t, rows):
    k = qa_n.shape[1]
    tm = _tile(rows, 768, 640, 512, 256, 128)
    return pl.pallas_call(
        _q_kernel,
        grid=(rows // tm,),
        in_specs=[pl.BlockSpec((tm, k), lambda i: (i, 0)),
                  pl.BlockSpec(w_q.shape, lambda i: (0, 0)),
                  pl.BlockSpec((tm, LANES), lambda i: (i, 0)),
                  pl.BlockSpec((tm, LANES), lambda i: (i, 0))],
        out_specs=pl.BlockSpec((tm, N_HEADS * HEAD_W), lambda i: (i, 0)),
        out_shape=jax.ShapeDtypeStruct((rows, N_HEADS * HEAD_W), BF16),
        compiler_params=_cparams(1),
        name="q_proj",
    )(qa_n, w_q, cos_t, sin_t)


def _kv_kernel(c_ref, kr_ref, wkv_ref, k_ref, v_ref):
    c = c_ref[...]
    kr = kr_ref[...]
    lane = lax.broadcasted_iota(jnp.int32, (c.shape[0], LANES), 1)
    ones_col = jnp.where(lane == 0, 1.0, 0.0).astype(v_ref.dtype)
    for hd in range(N_HEADS):
        kv = _dot(c, wkv_ref[:, hd * HEAD_W:(hd + 1) * HEAD_W])
        k_ref[:, hd * HEAD_W:hd * HEAD_W + LANES] = kv[:, :QK_NOPE_DIM].astype(k_ref.dtype)
        k_ref[:, hd * HEAD_W + LANES:(hd + 1) * HEAD_W] = kr
        v_ref[:, hd * HEAD_W:hd * HEAD_W + LANES] = kv[:, QK_NOPE_DIM:].astype(v_ref.dtype)
        v_ref[:, hd * HEAD_W + LANES:(hd + 1) * HEAD_W] = ones_col


def _kv_proj(ckv_n, kr, w_kvb, layer):
    m, k = ckv_n.shape
    tm = _tile(m, 768, 640, 512, 256, 128)
    assert QK_NOPE_DIM == LANES and V_HEAD_DIM == LANES
    return pl.pallas_call(
        _kv_kernel,
        grid=(m // tm,),
        in_specs=[pl.BlockSpec((tm, k), lambda i: (i, 0)),
                  pl.BlockSpec((tm, LANES), lambda i: (i, 0)),
                  pl.BlockSpec((None,) + w_kvb.shape[1:], lambda i: (layer, 0, 0))],
        out_specs=[pl.BlockSpec((tm, N_HEADS * HEAD_W), lambda i: (i, 0)),
                   pl.BlockSpec((tm, N_HEADS * HEAD_W), lambda i: (i, 0))],
        out_shape=[jax.ShapeDtypeStruct((m, N_HEADS * HEAD_W), BF16),
                   jax.ShapeDtypeStruct((m, N_HEADS * HEAD_W), BF16)],
        compiler_params=_cparams(1),
        name="kv_proj",
    )(ckv_n, kr, w_kvb)


def _attn_kernel(q_ref, k_ref, v_ref, o_ref, sa_ref, sb_ref, *, tk, n_chunks):
    q = q_ref[...]
    tq = q.shape[0]

    def scores(c):
        return _dot_t(q, k_ref[c * tk:(c + 1) * tk, :])

    def absorb(s_ref, c, carry):
        m, acc = carry
        s = s_ref[...]
        m_new = jnp.maximum(m, jnp.max(s, axis=-1, keepdims=True))
        p = jnp.exp2(s - m_new).astype(v_ref.dtype)
        acc = jnp.exp2(m - m_new) * acc + _dot(p, v_ref[c * tk:(c + 1) * tk, :])
        return m_new, acc

    s_refs = (sa_ref, sb_ref)
    carry = (jnp.full((tq, 1), -jnp.inf, F32), jnp.zeros((tq, HEAD_W), F32))
    sa_ref[...] = scores(0)
    for c in range(n_chunks):
        if c + 1 < n_chunks:
            s_refs[(c + 1) % 2][...] = scores(c + 1)
        carry = absorb(s_refs[c % 2], c, carry)
    _, acc = carry
    o_ref[...] = (acc[:, :V_HEAD_DIM] / acc[:, V_HEAD_DIM:V_HEAD_DIM + 1]).astype(o_ref.dtype)


def _attention(q, k, v, q_row0, n_q, k_row0, n_k):
    tq = _tile(n_q, 1024, 512, 256, 128)
    tk = _tile(n_k, 768, 640, 512, 384, 256, 128)
    assert q_row0 % tq == 0 and k_row0 % n_k == 0
    q_blk0 = q_row0 // tq
    k_blk = k_row0 // n_k
    return pl.pallas_call(
        functools.partial(_attn_kernel, tk=tk, n_chunks=n_k // tk),
        grid=(N_HEADS, n_q // tq),
        in_specs=[pl.BlockSpec((tq, HEAD_W), lambda h, i: (q_blk0 + i, h)),
                  pl.BlockSpec((n_k, HEAD_W), lambda h, i: (k_blk, h)),
                  pl.BlockSpec((n_k, HEAD_W), lambda h, i: (k_blk, h))],
        out_specs=pl.BlockSpec((tq, V_HEAD_DIM), lambda h, i: (i, h)),
        out_shape=jax.ShapeDtypeStruct((n_q, N_HEADS * V_HEAD_DIM), BF16),
        scratch_shapes=[pltpu.VMEM((tq, tk), F32), pltpu.VMEM((tq, tk), F32)],
        compiler_params=_cparams(2),
        name="attention",
    )(q, k, v)


def _conv_gate_kernel(cx_ref, cb_ref, cc_ref, cxp_ref, ccp_ref, cxn_ref, ccn_ref, wc_ref, z_ref, *,
                      seg_starts, seg_ends, col_chunk):
    tm, width = z_ref.shape
    loc = lax.broadcasted_iota(jnp.int32, (tm, 1), 0)
    row = loc + pl.program_id(0) * tm
    first = functools.reduce(jnp.logical_or, [row == r for r in seg_starts])
    last = functools.reduce(jnp.logical_or, [row == r for r in seg_ends])
    for c0 in range(0, width, col_chunk):
        cs = slice(c0, c0 + col_chunk)
        u = cx_ref[:, cs].astype(F32) * cc_ref[:, cs].astype(F32)
        u_halo_prev = (cxp_ref[SUBLANES - 1:SUBLANES, cs].astype(F32)
                       * ccp_ref[SUBLANES - 1:SUBLANES, cs].astype(F32))
        u_halo_next = cxn_ref[0:1, cs].astype(F32) * ccn_ref[0:1, cs].astype(F32)
        u_prev = jnp.where(loc == 0, u_halo_prev, pltpu.roll(u, 1, 0))
        u_prev = jnp.where(first, 0.0, u_prev)
        u_next = jnp.where(loc == tm - 1, u_halo_next, pltpu.roll(u, tm - 1, 0))
        u_next = jnp.where(last, 0.0, u_next)
        conv = wc_ref[0:1, cs] * u_prev + wc_ref[1:2, cs] * u + wc_ref[2:3, cs] * u_next
        z_ref[:, cs] = (cb_ref[:, cs].astype(F32) * conv).astype(z_ref.dtype)


def _conv_gate(p, w_conv, rows, s_len, m_len):
    width = w_conv.shape[1]
    tm = _tile(rows, 768, 512, 384, 256, 128, 64)
    hb = tm // SUBLANES
    last_hb = p.shape[0] // SUBLANES - 1

    def prev_map(col):
        return lambda i: (jnp.maximum(i * hb - 1, 0), col)

    def next_map(col):
        return lambda i: (jnp.minimum((i + 1) * hb, last_hb), col)

    kern = functools.partial(_conv_gate_kernel, seg_starts=(0, s_len), seg_ends=(s_len - 1, m_len - 1),
                             col_chunk=_tile(width, 512, LANES))
    return pl.pallas_call(
        kern,
        grid=(rows // tm,),
        in_specs=[pl.BlockSpec((tm, width), lambda i: (i, 0)),
                  pl.BlockSpec((tm, width), lambda i: (i, 1)),
                  pl.BlockSpec((tm, width), lambda i: (i, 2)),
                  pl.BlockSpec((SUBLANES, width), prev_map(0)),
                  pl.BlockSpec((SUBLANES, width), prev_map(2)),
                  pl.BlockSpec((SUBLANES, width), next_map(0)),
                  pl.BlockSpec((SUBLANES, width), next_map(2)),
                  pl.BlockSpec((CONV_K, width), lambda i: (0, 0))],
        out_specs=pl.BlockSpec((tm, width), lambda i: (i, 0)),
        out_shape=jax.ShapeDtypeStruct((rows, width), BF16),
        compiler_params=_cparams(1),
        name="conv_gate",
    )(p, p, p, p, p, p, p, w_conv)


def _merge_kernel(attn_ref, z_ref, sga_ref, sgb_ref, woa_ref, wob_ref, o_ref):
    o_a = _dot(attn_ref[...], woa_ref[...])
    o_b = _dot(z_ref[...], wob_ref[...])
    o_ref[...] = (sga_ref[...].astype(F32) * o_a + sgb_ref[...].astype(F32) * o_b).astype(o_ref.dtype)


def _merge(attn, z, p, w_oa, w_ob, layer, rows):
    attn_w = w_oa.shape[1]
    _, width, d = w_ob.shape
    tm = _tile(rows, 768, 512, 384, 256, 128, 64)
    tn = _tile(d, 1024, 512, 256, LANES)
    ga_blk = 3 * width // tn
    gb_blk = (3 * width + d) // tn
    return pl.pallas_call(
        _merge_kernel,
        grid=(d // tn, rows // tm),
        in_specs=[pl.BlockSpec((tm, attn_w), lambda j, i: (i, 0)),
                  pl.BlockSpec((tm, width), lambda j, i: (i, 0)),
                  pl.BlockSpec((tm, tn), lambda j, i: (i, ga_blk + j)),
                  pl.BlockSpec((tm, tn), lambda j, i: (i, gb_blk + j)),
                  pl.BlockSpec((None, attn_w, tn), lambda j, i: (layer, 0, j)),
                  pl.BlockSpec((None, width, tn), lambda j, i: (layer, 0, j))],
        out_specs=pl.BlockSpec((tm, tn), lambda j, i: (i, j)),
        out_shape=jax.ShapeDtypeStruct((rows, d), BF16),
        compiler_params=_cparams(2),
        name="merge",
    )(attn, z, p, p, w_oa, w_ob)


def _mm_res_kernel(a_ref, w_ref, x_ref, gt_ref, o_ref, *, n_lat):
    tm = a_ref.shape[0]
    row = lax.broadcasted_iota(jnp.int32, (tm, 1), 0) + pl.program_id(1) * tm
    gate = jnp.where(row >= n_lat, gt_ref[1:2, :], gt_ref[0:1, :])
    o_ref[...] = x_ref[...] + gate * _dot(a_ref[...], _bf16(w_ref[...]))


def _matmul_residual(a, w, layer, x, mods, gate_blk, rows, n_lat):
    _, k, n = w.shape
    tm = _tile(rows, 768, 640, 512, 256, 128)
    tn = _tile(n, 1024 if k <= n else 512, 512, 256, LANES)
    nt = n // tn
    return pl.pallas_call(
        functools.partial(_mm_res_kernel, n_lat=n_lat),
        grid=(nt, rows // tm),
        in_specs=[pl.BlockSpec((tm, k), lambda j, i: (i, 0)),
                  pl.BlockSpec((None, k, tn), lambda j, i: (layer, 0, j)),
                  pl.BlockSpec((tm, tn), lambda j, i: (i, j)),
                  pl.BlockSpec((2, tn), lambda j, i: (0, gate_blk * nt + j))],
        out_specs=pl.BlockSpec((tm, tn), lambda j, i: (i, j)),
        out_shape=jax.ShapeDtypeStruct((rows, n), F32),
        compiler_params=_cparams(2),
        name="matmul_residual",
    )(a, w, x, mods)


def _block_in_use(eid_ref):
    return pl.program_id(1) < eid_ref[pl.num_programs(1)]


def _glu_kernel(eid_ref, a_ref, wg_ref, wu_ref, o_ref):
    @pl.when(_block_in_use(eid_ref))
    def _():
        a = _bf16(a_ref[...])
        gate = _dot(a, _bf16(wg_ref[0]))
        up = _dot(a, _bf16(wu_ref[0]))
        o_ref[...] = (gate * jax.nn.sigmoid(gate) * up).astype(o_ref.dtype)

    @pl.when(jnp.logical_not(_block_in_use(eid_ref)))
    def _():
        o_ref[...] = jnp.zeros_like(o_ref)


def _glu(a, w_gate, w_up, eid, tm):
    r, k = a.shape
    f = w_gate.shape[2]
    tn = _tile(f, 512, 1408, 256, LANES)
    grid_spec = pltpu.PrefetchScalarGridSpec(
        num_scalar_prefetch=1,
        grid=(f // tn, r // tm),
        in_specs=[pl.BlockSpec((tm, k), lambda j, i, e: (i, 0)),
                  pl.BlockSpec((1, k, tn), lambda j, i, e: (e[i], 0, j)),
                  pl.BlockSpec((1, k, tn), lambda j, i, e: (e[i], 0, j))],
        out_specs=pl.BlockSpec((tm, tn), lambda j, i, e: (i, j)),
    )
    return pl.pallas_call(
        _glu_kernel,
        grid_spec=grid_spec,
        out_shape=jax.ShapeDtypeStruct((r, f), BF16),
        compiler_params=_cparams(2),
        name="glu",
    )(eid, a, w_gate, w_up)


def _down_kernel(eid_ref, a_ref, w_ref, o_ref):
    @pl.when(_block_in_use(eid_ref))
    def _():
        o_ref[...] = _dot(a_ref[...], _bf16(w_ref[0]))

    @pl.when(jnp.logical_not(_block_in_use(eid_ref)))
    def _():
        o_ref[...] = jnp.zeros_like(o_ref)


def _down_grouped(a, w_down, eid, tm):
    r, f = a.shape
    d = w_down.shape[2]
    tn = _tile(d, 1024, 512, 256, LANES)
    grid_spec = pltpu.PrefetchScalarGridSpec(
        num_scalar_prefetch=1,
        grid=(d // tn, r // tm),
        in_specs=[pl.BlockSpec((tm, f), lambda j, i, e: (i, 0)),
                  pl.BlockSpec((1, f, tn), lambda j, i, e: (e[i], 0, j))],
        out_specs=pl.BlockSpec((tm, tn), lambda j, i, e: (i, j)),
    )
    return pl.pallas_call(
        _down_kernel,
        grid_spec=grid_spec,
        out_shape=jax.ShapeDtypeStruct((r, d), F32),
        compiler_params=_cparams(2),
        name="down_grouped",
    )(eid, a, w_down)


def _row_copies(idx_ref, idx_base, idx_stride, src_ref, dst_ref, sem, n_rows, start):
    if not start:
        pltpu.make_async_copy(src_ref.at[pl.ds(0, n_rows)], dst_ref, sem).wait()
        return

    def body(r2, c):
        for prio in range(2):
            r = 2 * r2 + prio
            row = idx_ref[idx_base + r * idx_stride]
            pltpu.make_async_copy(src_ref.at[pl.ds(row, 1)], dst_ref.at[pl.ds(r, 1)], sem).start(priority=prio)
        return c

    assert n_rows % 2 == 0 and dst_ref.shape[0] == n_rows
    lax.fori_loop(0, n_rows // 2, body, 0, unroll=DMA_ISSUE_UNROLL // 2)


def _prefetched_gather(copies):
    i = pl.program_id(0)
    slot = i % 2

    @pl.when(i == 0)
    def _():
        copies(0, 0, True)

    @pl.when(i + 1 < pl.num_programs(0))
    def _():
        copies(i + 1, 1 - slot, True)

    copies(i, slot, False)
    return slot


def _gather_rows_kernel(idx_ref, src_ref, o_ref, g_ref, sem):
    tm = o_ref.shape[0]

    def copies(step, slot, start):
        _row_copies(idx_ref, step * tm, 1, src_ref, g_ref.at[slot], sem.at[slot], tm, start)

    slot = _prefetched_gather(copies)
    o_ref[...] = g_ref[slot]


def _gather_rows(src, idx, tm):
    n = idx.shape[0]
    w = src.shape[1]
    grid_spec = pltpu.PrefetchScalarGridSpec(
        num_scalar_prefetch=1,
        grid=(n // tm,),
        in_specs=[pl.BlockSpec(memory_space=pl.ANY)],
        out_specs=pl.BlockSpec((tm, w), lambda i, idx_ref: (i, 0)),
        scratch_shapes=[pltpu.VMEM((2, tm, w), src.dtype), pltpu.SemaphoreType.DMA((2,))],
    )
    return pl.pallas_call(
        _gather_rows_kernel,
        grid_spec=grid_spec,
        out_shape=jax.ShapeDtypeStruct((n, w), src.dtype),
        compiler_params=_cparams(1),
        name="gather_rows",
    )(idx, src)


def _final_kernel(dest_ref, x_ref, y_hbm_ref, gate_ref, gt_ref, g_ref, o_ref, y_ref, sem):
    tm = x_ref.shape[0]

    def copies(step, slot, start):
        for kk in range(TOP_K):
            _row_copies(dest_ref, step * tm * TOP_K + kk, TOP_K, y_hbm_ref, y_ref.at[slot, kk], sem.at[slot],
                        tm, start)

    slot = _prefetched_gather(copies)
    gates = gate_ref[...]
    y = gates[:, 0:1] * y_ref[slot, 0] + gates[:, 1:2] * y_ref[slot, 1]
    x = x_ref[...] + gt_ref[0:1, :] * y
    o_ref[...] = x * lax.rsqrt(jnp.mean(x * x, axis=-1, keepdims=True) + NORM_EPS) * g_ref[...]


def _combine_final(x, ybuf, dest, gates, mods, gate_blk, g_final):
    r, d = x.shape
    tm = _tile(r, 256, 128, 64, SUBLANES)
    grid_spec = pltpu.PrefetchScalarGridSpec(
        num_scalar_prefetch=1,
        grid=(r // tm,),
        in_specs=[pl.BlockSpec((tm, d), lambda i, dest_ref: (i, 0)),
                  pl.BlockSpec(memory_space=pl.ANY),
                  pl.BlockSpec((tm, LANES), lambda i, dest_ref: (i, 0)),
                  pl.BlockSpec((2, d), lambda i, dest_ref: (0, gate_blk)),
                  pl.BlockSpec((1, d), lambda i, dest_ref: (0, 0))],
        out_specs=pl.BlockSpec((tm, d), lambda i, dest_ref: (i, 0)),
        scratch_shapes=[pltpu.VMEM((2, TOP_K, tm, d), F32), pltpu.SemaphoreType.DMA((2,))],
    )
    return pl.pallas_call(
        _final_kernel,
        grid_spec=grid_spec,
        out_shape=jax.ShapeDtypeStruct((r, d), F32),
        compiler_params=_cparams(1),
        name="combine_final",
    )(dest, x, ybuf, gates, mods, g_final.reshape(1, d))


def _rope_tables(s_len, c_len):
    quarter = QK_ROPE_DIM // 4
    tok = jnp.arange(s_len + c_len, dtype=jnp.int32)[:, None]
    lane = jnp.arange(LANES, dtype=jnp.int32)[None, :]
    group = lane // quarter
    pos = jnp.where(group < 2, tok // GRID_W, tok % GRID_W).astype(F32)
    inv_freq = jnp.power(ROPE_THETA, -(2 * (lane % quarter)).astype(F32) / (QK_ROPE_DIM // 2))
    ang = jnp.where(tok < s_len, pos * inv_freq, 0.0)
    live = group < 4
    cos = jnp.where(live, jnp.cos(ang), 0.0)
    sin = jnp.where(live, jnp.where(group % 2 == 0, -jnp.sin(ang), jnp.sin(ang)), 0.0)
    return cos, sin


def _rope_swap_perm():
    q = QK_ROPE_DIM // 4
    return jnp.concatenate([jnp.arange(q, 2 * q), jnp.arange(0, q), jnp.arange(3 * q, 4 * q), jnp.arange(2 * q, 3 * q)])


def _layer_weights(w_in_t, w_qb_all, li):
    d = w_in_t.shape[2]
    perm = _rope_swap_perm()
    kr_lo = Q_LORA_RANK + KV_LORA_RANK
    kr_hi = kr_lo + QK_ROPE_DIM
    w_kr = w_in_t[li, kr_lo:kr_hi]
    zpad = jnp.zeros((LANES - QK_ROPE_DIM, d), w_in_t.dtype)
    w_a = jnp.concatenate([w_kr, zpad, w_kr[perm], zpad], axis=0)
    w_qb = w_qb_all[li]
    qb = w_qb.reshape(Q_LORA_RANK, N_HEADS, QK_NOPE_DIM + QK_ROPE_DIM)
    q_rope = qb[:, :, QK_NOPE_DIM:]
    assert 2 * QK_ROPE_DIM == LANES
    w_q = jnp.concatenate([qb, q_rope[:, :, perm]], axis=2).reshape(Q_LORA_RANK, N_HEADS * HEAD_W)
    return w_a.astype(BF16), w_q.astype(BF16)


def _moe_slots(idx, n_blocks):
    e_flat = idx.reshape(-1)
    onehot = (e_flat[:, None] == jnp.arange(N_EXPERTS)[None, :]).astype(jnp.int32)
    csum = jnp.cumsum(onehot, axis=0)
    counts = csum[-1]
    rank = jnp.sum((csum - onehot) * onehot, axis=1)
    padded = (counts + MOE_ROWS - 1) // MOE_ROWS * MOE_ROWS
    p_end = jnp.cumsum(padded)
    p_start = p_end - padded
    dest = jnp.sum(onehot * p_start[None, :], axis=1) + rank
    blk_lo = jnp.arange(n_blocks) * MOE_ROWS
    block_expert = jnp.minimum(jnp.sum((blk_lo[:, None] >= p_end[None, :]).astype(jnp.int32), axis=1),
                               N_EXPERTS - 1)
    blocks_in_use = p_end[-1:] // MOE_ROWS
    return dest.astype(jnp.int32), jnp.concatenate([block_expert, blocks_in_use]).astype(jnp.int32)


def kernel(x, c, ctx, c_ctx, w_ada, b_ada, g_attn, w_in, g_qa, w_qb, g_kva, w_kvb, w_conv, w_oa, w_ob, w_o,
           g_ffn, w_gate_dense, w_up_dense, w_down_dense, w_router, w_gate_exp, w_up_exp, w_down_exp, g_final):
    _, s_len, d = x.shape
    c_len = ctx.shape[1]
    m_len = s_len + c_len
    depth = w_in.shape[0]
    width = w_conv.shape[2]
    assert depth == 2, "supported stack: dense-FFN layer with context updates, then a final expert-FFN layer"

    mods_all = _ada_mod(c, c_ctx, w_ada, b_ada)
    cos_t, sin_t = _rope_tables(s_len, c_len)
    w_in_t = jnp.swapaxes(w_in, 1, 2)
    w_oa_bf, w_ob_bf, w_kvb_bf = w_oa.astype(BF16), w_ob.astype(BF16), w_kvb.astype(BF16)
    xa = jnp.concatenate([x[0], ctx[0]], axis=0)

    for li in range(depth):
        last = li == depth - 1
        mods = mods_all[li]
        w_a, w_q = _layer_weights(w_in_t, w_qb, li)
        g_a = jnp.concatenate([g_qa[li], g_kva[li]]).reshape(1, -1)
        rows = s_len if last else m_len

        h = _norm_modulate(xa, g_attn[li], mods, 0, 1, s_len, BF16)
        qa_n, ckv_n, kr = _in_proj_a(h, w_in_t, li, w_a, g_a, cos_t, sin_t)
        p = _in_proj_b(h, w_in_t, li, Q_LORA_RANK + KV_LORA_RANK + QK_ROPE_DIM, 3 * width)
        q = _q_proj(qa_n, w_q, cos_t, sin_t, rows)
        k, v = _kv_proj(ckv_n, kr, w_kvb_bf, li)
        attn = _attention(q, k, v, 0, s_len, 0, m_len)
        if not last:
            attn = jnp.concatenate([attn, _attention(q, k, v, s_len, c_len, s_len, c_len)], axis=0)
        z = _conv_gate(p, w_conv[li], rows, s_len, m_len)
        merged = _merge(attn, z, p, w_oa_bf, w_ob_bf, li, rows)
        xa = _matmul_residual(merged, w_o, li, xa, mods, 2, rows, s_len)

        j = li // 2
        if li % 2 == 0:
            h2 = _norm_modulate(xa, g_ffn[li], mods, 3, 4, s_len, BF16)
            tm = _tile(rows, 1408, 768, 640, 512, 256, 128)
            eid = jnp.zeros((rows // tm + 1,), jnp.int32).at[-1].set(rows // tm)
            hid = _glu(h2, w_gate_dense[j][None], w_up_dense[j][None], eid, tm)
            xa = _matmul_residual(hid, w_down_dense.astype(BF16), j, xa, mods, 5, rows, s_len)
        else:
            h2, idx, gates = _norm_modulate_route(xa, g_ffn[li], mods, 3, 4, w_router[j])
            n_assign = rows * TOP_K
            n_blocks = -(-n_assign // MOE_ROWS) + N_EXPERTS
            dest, block_expert = _moe_slots(idx[:, :TOP_K], n_blocks)
            tok = jnp.arange(n_assign, dtype=jnp.int32) // TOP_K
            slot_tok = jnp.zeros((n_blocks * MOE_ROWS,), jnp.int32).at[dest].set(tok, unique_indices=True)
            buf = _gather_rows(h2, slot_tok, MOE_ROWS)
            hid = _glu(buf, w_gate_exp[j], w_up_exp[j], block_expert, MOE_ROWS)
            ybuf = _down_grouped(hid, w_down_exp[j], block_expert, MOE_ROWS)
            return _combine_final(xa, ybuf, dest, gates, mods, 5, g_final)[None]
    raise AssertionError("unreachable: the final layer returns")
```

```python
import functools

import jax
import jax.numpy as jnp
from jax import lax
from jax.experimental import pallas as pl
from jax.experimental.pallas import tpu as pltpu

F32 = jnp.float32
BF16 = jnp.bfloat16

N_HEADS = 16
QK_NOPE_DIM = 128
QK_ROPE_DIM = 64
V_HEAD_DIM = 128
Q_LORA_RANK = 512
KV_LORA_RANK = 512
GRID_W = 64
ROPE_THETA = 10000.0
ATTN_SCALE = (QK_NOPE_DIM + QK_ROPE_DIM) ** -0.5
Q_SCALE = ATTN_SCALE * 1.4426950408889634
CONV_K = 3
N_EXPERTS = 8
TOP_K = 2
NORM_EPS = 1e-6

LANES = 128
SUBLANES = 8
HEAD_W = 2 * LANES
VMEM_LIMIT_BYTES = 56 * 1024 * 1024

MOE_ROWS = 256
DMA_ISSUE_UNROLL = 8


def _tile(n, *cands):
    for c in cands:
        if n % c == 0:
            return c
    return n


def _cparams(n_axes):
    return pltpu.CompilerParams(dimension_semantics=("arbitrary",) * n_axes,
                                vmem_limit_bytes=VMEM_LIMIT_BYTES)


def _dot(a, b):
    return jnp.dot(a, b, preferred_element_type=F32)


def _dot_t(a, bt):
    return lax.dot_general(a, bt, (((1,), (1,)), ((), ())), preferred_element_type=F32)


def _bf16(w):
    return w if w.dtype == BF16 else w.astype(BF16)


def _ada_kernel(xt_ref, w_ref, b_ref, o_ref, *, k_chunk):
    d = xt_ref.shape[0]
    tn = o_ref.shape[-1]

    def body(k, acc):
        a0, a1 = acc
        ks = pl.multiple_of(k * k_chunk, k_chunk)
        xt = xt_ref[pl.ds(ks, k_chunk), :]
        s = xt * jax.nn.sigmoid(xt)
        w = w_ref[0, pl.ds(ks, k_chunk), :]
        a0 = a0 + jnp.sum(w * s[:, 0:1], axis=0, keepdims=True)
        a1 = a1 + jnp.sum(w * s[:, 1:2], axis=0, keepdims=True)
        return a0, a1

    z = jnp.zeros((1, tn), F32)
    a0, a1 = lax.fori_loop(0, d // k_chunk, body, (z, z))
    o_ref[0, 0:1, :] = a0 + b_ref[0]
    o_ref[0, 1:2, :] = a1 + b_ref[0]


def _ada_mod(c, c_ctx, w_ada, b_ada):
    depth, d, n = w_ada.shape
    xt = jnp.stack([c[0], c_ctx], axis=1)
    tn = _tile(n, 1024, 512, LANES)
    k_chunk = _tile(d, 256, SUBLANES)
    return pl.pallas_call(
        functools.partial(_ada_kernel, k_chunk=k_chunk),
        grid=(depth, n // tn),
        in_specs=[pl.BlockSpec((d, 2), lambda l, j: (0, 0)),
                  pl.BlockSpec((1, d, tn), lambda l, j: (l, 0, j)),
                  pl.BlockSpec((1, 1, tn), lambda l, j: (l, 0, j))],
        out_specs=pl.BlockSpec((1, 2, tn), lambda l, j: (l, 0, j)),
        out_shape=jax.ShapeDtypeStruct((depth, 2, n), F32),
        compiler_params=_cparams(2),
        name="ada_mod",
    )(xt, w_ada, b_ada.reshape(depth, 1, n))


def _norm_mod(x, g_ref, sh_ref, sc_ref, is_ctx):
    y = x * lax.rsqrt(jnp.mean(x * x, axis=-1, keepdims=True) + NORM_EPS) * g_ref[...]
    sh = jnp.where(is_ctx, sh_ref[1:2, :], sh_ref[0:1, :])
    sc = jnp.where(is_ctx, sc_ref[1:2, :], sc_ref[0:1, :])
    return y * (1.0 + sc) + sh


def _norm_kernel(x_ref, g_ref, sh_ref, sc_ref, o_ref, *, n_lat_tiles):
    is_ctx = pl.program_id(0) >= n_lat_tiles
    o_ref[...] = _norm_mod(x_ref[...], g_ref, sh_ref, sc_ref, is_ctx).astype(o_ref.dtype)


def _norm_router_kernel(x_ref, g_ref, sh_ref, sc_ref, wr_ref, h_ref, idx_ref, gate_ref):
    h = _norm_mod(x_ref[...], g_ref, sh_ref, sc_ref, False)
    h_ref[...] = h
    logits = jnp.dot(h, wr_ref[...], preferred_element_type=F32, precision=lax.Precision.HIGHEST)
    lane = lax.broadcasted_iota(jnp.int32, logits.shape, 1).astype(F32)
    neg = jnp.float32(-jnp.inf)
    l1 = jnp.where(lane < N_EXPERTS, logits, neg)
    v1 = jnp.max(l1, axis=-1, keepdims=True)
    i1 = jnp.min(jnp.where(l1 == v1, lane, float(LANES)), axis=-1, keepdims=True)
    l2 = jnp.where(lane == i1, neg, l1)
    v2 = jnp.max(l2, axis=-1, keepdims=True)
    i2 = jnp.min(jnp.where(l2 == v2, lane, float(LANES)), axis=-1, keepdims=True)
    e = jnp.exp(v2 - v1)
    g1 = 1.0 / (1.0 + e)
    g2 = e / (1.0 + e)
    idx_ref[...] = jnp.where(lane == 0, i1, jnp.where(lane == 1, i2, 0.0)).astype(jnp.int32)
    gate_ref[...] = jnp.where(lane == 0, g1, jnp.where(lane == 1, g2, 0.0))


def _norm_join_kernel(x_ref, c_ref, g_ref, sh_ref, sc_ref, o_ref, xa_ref, *, n_lat_tiles):
    is_ctx = pl.program_id(0) >= n_lat_tiles
    x = jnp.where(is_ctx, c_ref[...], x_ref[...])
    xa_ref[...] = x
    o_ref[...] = _norm_mod(x, g_ref, sh_ref, sc_ref, is_ctx).astype(o_ref.dtype)


def _norm_modulate_join(x, ctx, g, mods, sh_blk, sc_blk, out_dtype):
    n_lat, d = x.shape
    n_ctx = ctx.shape[0]
    tm = _tile(n_ctx, 256, 128, 64, SUBLANES)
    assert n_lat % tm == 0
    nl = n_lat // tm
    return pl.pallas_call(
        functools.partial(_norm_join_kernel, n_lat_tiles=nl),
        grid=((n_lat + n_ctx) // tm,),
        in_specs=[pl.BlockSpec((tm, d), lambda i: (jnp.minimum(i, nl - 1), 0)),
                  pl.BlockSpec((tm, d), lambda i: (jnp.maximum(i - nl, 0), 0)),
                  pl.BlockSpec((1, d), lambda i: (0, 0)),
                  pl.BlockSpec((2, d), lambda i: (0, sh_blk)),
                  pl.BlockSpec((2, d), lambda i: (0, sc_blk))],
        out_specs=[pl.BlockSpec((tm, d), lambda i: (i, 0)),
                   pl.BlockSpec((tm, d), lambda i: (i, 0))],
        out_shape=[jax.ShapeDtypeStruct((n_lat + n_ctx, d), out_dtype),
                   jax.ShapeDtypeStruct((n_lat + n_ctx, d), F32)],
        compiler_params=_cparams(1),
        name="norm_modulate_join",
    )(x, ctx, g.reshape(1, d), mods, mods)


def _norm_modulate(x, g, mods, sh_blk, sc_blk, n_lat, out_dtype):
    r, d = x.shape
    tm = _tile(n_lat, 256, 128, 64, SUBLANES) if r > n_lat else _tile(r, 256, 128, 64, SUBLANES)
    if r > n_lat:
        tm = _tile(r - n_lat, tm, 128, 64, SUBLANES)
    return pl.pallas_call(
        functools.partial(_norm_kernel, n_lat_tiles=n_lat // tm),
        grid=(r // tm,),
        in_specs=[pl.BlockSpec((tm, d), lambda i: (i, 0)),
                  pl.BlockSpec((1, d), lambda i: (0, 0)),
                  pl.BlockSpec((2, d), lambda i: (0, sh_blk)),
                  pl.BlockSpec((2, d), lambda i: (0, sc_blk))],
        out_specs=pl.BlockSpec((tm, d), lambda i: (i, 0)),
        out_shape=jax.ShapeDtypeStruct((r, d), out_dtype),
        compiler_params=_cparams(1),
        name="norm_modulate",
    )(x, g.reshape(1, d), mods, mods)


def _norm_modulate_route(x, g, mods, sh_blk, sc_blk, w_router):
    r, d = x.shape
    tm = _tile(r, 512, 256, 128, 64, SUBLANES)
    wr = jnp.pad(w_router, ((0, 0), (0, LANES - N_EXPERTS)))
    return pl.pallas_call(
        _norm_router_kernel,
        grid=(r // tm,),
        in_specs=[pl.BlockSpec((tm, d), lambda i: (i, 0)),
                  pl.BlockSpec((1, d), lambda i: (0, 0)),
                  pl.BlockSpec((2, d), lambda i: (0, sh_blk)),
                  pl.BlockSpec((2, d), lambda i: (0, sc_blk)),
                  pl.BlockSpec((d, LANES), lambda i: (0, 0))],
        out_specs=[pl.BlockSpec((tm, d), lambda i: (i, 0)),
                   pl.BlockSpec((tm, LANES), lambda i: (i, 0)),
                   pl.BlockSpec((tm, LANES), lambda i: (i, 0))],
        out_shape=[jax.ShapeDtypeStruct((r, d), F32),
                   jax.ShapeDtypeStruct((r, LANES), jnp.int32),
                   jax.ShapeDtypeStruct((r, LANES), F32)],
        compiler_params=_cparams(1),
        name="norm_modulate_route",
    )(x, g.reshape(1, d), mods, mods, wr)


def _in_a_kernel(h_ref, wt_ref, wkr_ref, g_ref, cos_ref, sin_ref, qa_ref, ckv_ref, kr_ref):
    h = h_ref[...]
    acc = _dot_t(h, _bf16(wt_ref[...]))
    acc_kr = _dot_t(h, wkr_ref[...])

    def rms(v, g):
        return v * lax.rsqrt(jnp.mean(v * v, axis=-1, keepdims=True) + NORM_EPS) * g

    q1 = Q_LORA_RANK
    c1 = q1 + KV_LORA_RANK
    qa_ref[...] = rms(acc[:, :q1], g_ref[:, :q1]).astype(qa_ref.dtype)
    ckv_ref[...] = rms(acc[:, q1:c1], g_ref[:, q1:c1]).astype(ckv_ref.dtype)
    kr = acc_kr[:, :LANES] * cos_ref[...] + acc_kr[:, LANES:] * sin_ref[...]
    kr_ref[...] = kr.astype(kr_ref.dtype)


def _in_proj_a(h, w_in_t, li, w_kr, g_a, cos_t, sin_t):
    m, d = h.shape
    n = Q_LORA_RANK + KV_LORA_RANK
    base = li * w_in_t.shape[1]
    assert base % SUBLANES == 0
    tm = _tile(m, 768, 640, 512, 256, 128)
    return pl.pallas_call(
        _in_a_kernel,
        grid=(m // tm,),
        in_specs=[pl.BlockSpec((tm, d), lambda i: (i, 0)),
                  pl.BlockSpec((pl.Element(n), pl.Element(d)), lambda i: (base, 0)),
                  pl.BlockSpec(w_kr.shape, lambda i: (0, 0)),
                  pl.BlockSpec((1, n), lambda i: (0, 0)),
                  pl.BlockSpec((tm, LANES), lambda i: (i, 0)),
                  pl.BlockSpec((tm, LANES), lambda i: (i, 0))],
        out_specs=[pl.BlockSpec((tm, Q_LORA_RANK), lambda i: (i, 0)),
                   pl.BlockSpec((tm, KV_LORA_RANK), lambda i: (i, 0)),
                   pl.BlockSpec((tm, LANES), lambda i: (i, 0))],
        out_shape=[jax.ShapeDtypeStruct((m, Q_LORA_RANK), BF16),
                   jax.ShapeDtypeStruct((m, KV_LORA_RANK), BF16),
                   jax.ShapeDtypeStruct((m, LANES), BF16)],
        compiler_params=_cparams(1),
        name="in_proj_a",
    )(h, w_in_t.reshape(-1, d), w_kr, g_a, cos_t, sin_t)


def _in_b_kernel(h_ref, wt_ref, o_ref, *, first_gate_tile):
    acc = _dot_t(h_ref[...], _bf16(wt_ref[...]))
    is_gate = pl.program_id(0) >= first_gate_tile

    @pl.when(is_gate)
    def _():
        o_ref[...] = jax.nn.sigmoid(acc).astype(o_ref.dtype)

    @pl.when(jnp.logical_not(is_gate))
    def _():
        o_ref[...] = acc.astype(o_ref.dtype)


def _in_proj_b(h, w_in_t, li, row0, first_gate_col):
    m, d = h.shape
    n = w_in_t.shape[1] - row0
    tm = _tile(m, 1408, 768, 640, 512, 256, 128)
    tn = _tile(first_gate_col, 1024, 512, 256, LANES)
    base = li * w_in_t.shape[1] + row0
    assert base % SUBLANES == 0 and n % tn == 0
    return pl.pallas_call(
        functools.partial(_in_b_kernel, first_gate_tile=first_gate_col // tn),
        grid=(n // tn, m // tm),
        in_specs=[pl.BlockSpec((tm, d), lambda j, i: (i, 0)),
                  pl.BlockSpec((pl.Element(tn), pl.Element(d)),
                               lambda j, i: (pl.multiple_of(base + j * tn, SUBLANES), 0))],
        out_specs=pl.BlockSpec((tm, tn), lambda j, i: (i, j)),
        out_shape=jax.ShapeDtypeStruct((m, n), BF16),
        compiler_params=_cparams(2),
        name="in_proj_b",
    )(h, w_in_t.reshape(-1, d))


def _q_kernel(a_ref, w_ref, cos_ref, sin_ref, q_ref):
    a = a_ref[...]
    cos = cos_ref[...]
    sin = sin_ref[...]
    for hd in range(N_HEADS):
        acc = _dot(a, w_ref[:, hd * HEAD_W:(hd + 1) * HEAD_W])
        q_ref[:, hd * HEAD_W:hd * HEAD_W + LANES] = (acc[:, :LANES] * Q_SCALE).astype(q_ref.dtype)
        rope = acc[:, LANES:]
        swapped = pltpu.roll(rope, LANES - QK_ROPE_DIM, 1)
        rot = (rope * cos + swapped * sin) * Q_SCALE
        q_ref[:, hd * HEAD_W + LANES:(hd + 1) * HEAD_W] = rot.astype(q_ref.dtype)


def _q_proj(qa_n, w_q, cos_t, sin_t, rows):
    k = qa_n.shape[1]
    tm = _tile(rows, 768, 640, 512, 256, 128)
    return pl.pallas_call(
        _q_kernel,
        grid=(rows // tm,),
        in_specs=[pl.BlockSpec((tm, k), lambda i: (i, 0)),
                  pl.BlockSpec(w_q.shape, lambda i: (0, 0)),
                  pl.BlockSpec((tm, LANES), lambda i: (i, 0)),
                  pl.BlockSpec((tm, LANES), lambda i: (i, 0))],
        out_specs=pl.BlockSpec((tm, N_HEADS * HEAD_W), lambda i: (i, 0)),
        out_shape=jax.ShapeDtypeStruct((rows, N_HEADS * HEAD_W), BF16),
        compiler_params=_cparams(1),
        name="q_proj",
    )(qa_n, w_q, cos_t, sin_t)


def _kv_kernel(c_ref, kr_ref, wkv_ref, k_ref, v_ref):
    c = c_ref[...]
    kr = kr_ref[...]
    lane = lax.broadcasted_iota(jnp.int32, (c.shape[0], LANES), 1)
    ones_col = jnp.where(lane == 0, 1.0, 0.0).astype(v_ref.dtype)
    for hd in range(N_HEADS):
        kv = _dot(c, wkv_ref[:, hd * HEAD_W:(hd + 1) * HEAD_W])
        k_ref[:, hd * HEAD_W:hd * HEAD_W + LANES] = kv[:, :QK_NOPE_DIM].astype(k_ref.dtype)
        k_ref[:, hd * HEAD_W + LANES:(hd + 1) * HEAD_W] = kr
        v_ref[:, hd * HEAD_W:hd * HEAD_W + LANES] = kv[:, QK_NOPE_DIM:].astype(v_ref.dtype)
        v_ref[:, hd * HEAD_W + LANES:(hd + 1) * HEAD_W] = ones_col


def _kv_proj(ckv_n, kr, w_kvb, layer):
    m, k = ckv_n.shape
    tm = _tile(m, 768, 640, 512, 256, 128)
    assert QK_NOPE_DIM == LANES and V_HEAD_DIM == LANES
    return pl.pallas_call(
        _kv_kernel,
        grid=(m // tm,),
        in_specs=[pl.BlockSpec((tm, k), lambda i: (i, 0)),
                  pl.BlockSpec((tm, LANES), lambda i: (i, 0)),
                  pl.BlockSpec((None,) + w_kvb.shape[1:], lambda i: (layer, 0, 0))],
        out_specs=[pl.BlockSpec((tm, N_HEADS * HEAD_W), lambda i: (i, 0)),
                   pl.BlockSpec((tm, N_HEADS * HEAD_W), lambda i: (i, 0))],
        out_shape=[jax.ShapeDtypeStruct((m, N_HEADS * HEAD_W), BF16),
                   jax.ShapeDtypeStruct((m, N_HEADS * HEAD_W), BF16)],
        compiler_params=_cparams(1),
        name="kv_proj",
    )(ckv_n, kr, w_kvb)


def _attn_kernel(q_ref, k_ref, v_ref, o_ref, sa_ref, sb_ref, *, tk, n_chunks):
    q = q_ref[...]
    tq = q.shape[0]

    def scores(c):
        return _dot_t(q, k_ref[c * tk:(c + 1) * tk, :])

    def absorb(s_ref, c, carry):
        m, acc = carry
        s = s_ref[...]
        m_new = jnp.maximum(m, jnp.max(s, axis=-1, keepdims=True))
        p = jnp.exp2(s - m_new).astype(v_ref.dtype)
        acc = jnp.exp2(m - m_new) * acc + _dot(p, v_ref[c * tk:(c + 1) * tk, :])
        return m_new, acc

    s_refs = (sa_ref, sb_ref)
    carry = (jnp.full((tq, 1), -jnp.inf, F32), jnp.zeros((tq, HEAD_W), F32))
    sa_ref[...] = scores(0)
    for c in range(n_chunks):
        if c + 1 < n_chunks:
            s_refs[(c + 1) % 2][...] = scores(c + 1)
        carry = absorb(s_refs[c % 2], c, carry)
    _, acc = carry
    o_ref[...] = (acc[:, :V_HEAD_DIM] / acc[:, V_HEAD_DIM:V_HEAD_DIM + 1]).astype(o_ref.dtype)


def _attention(q, k, v, q_row0, n_q, k_row0, n_k):
    tq = _tile(n_q, 1024, 512, 256, 128)
    tk = _tile(n_k, 768, 640, 512, 384, 256, 128)
    assert q_row0 % tq == 0 and k_row0 % n_k == 0
    q_blk0 = q_row0 // tq
    k_blk = k_row0 // n_k
    return pl.pallas_call(
        functools.partial(_attn_kernel, tk=tk, n_chunks=n_k // tk),
        grid=(N_HEADS, n_q // tq),
        in_specs=[pl.BlockSpec((tq, HEAD_W), lambda h, i: (q_blk0 + i, h)),
                  pl.BlockSpec((n_k, HEAD_W), lambda h, i: (k_blk, h)),
                  pl.BlockSpec((n_k, HEAD_W), lambda h, i: (k_blk, h))],
        out_specs=pl.BlockSpec((tq, V_HEAD_DIM), lambda h, i: (i, h)),
        out_shape=jax.ShapeDtypeStruct((n_q, N_HEADS * V_HEAD_DIM), BF16),
        scratch_shapes=[pltpu.VMEM((tq, tk), F32), pltpu.VMEM((tq, tk), F32)],
        compiler_params=_cparams(2),
        name="attention",
    )(q, k, v)


def _conv_gate_kernel(cx_ref, cb_ref, cc_ref, cxp_ref, ccp_ref, cxn_ref, ccn_ref, wc_ref, z_ref, *,
                      seg_starts, seg_ends, col_chunk):
    tm, width = z_ref.shape
    loc = lax.broadcasted_iota(jnp.int32, (tm, 1), 0)
    row = loc + pl.program_id(0) * tm
    first = functools.reduce(jnp.logical_or, [row == r for r in seg_starts])
    last = functools.reduce(jnp.logical_or, [row == r for r in seg_ends])
    for c0 in range(0, width, col_chunk):
        cs = slice(c0, c0 + col_chunk)
        u = cx_ref[:, cs].astype(F32) * cc_ref[:, cs].astype(F32)
        u_halo_prev = (cxp_ref[SUBLANES - 1:SUBLANES, cs].astype(F32)
                       * ccp_ref[SUBLANES - 1:SUBLANES, cs].astype(F32))
        u_halo_next = cxn_ref[0:1, cs].astype(F32) * ccn_ref[0:1, cs].astype(F32)
        u_prev = jnp.where(loc == 0, u_halo_prev, pltpu.roll(u, 1, 0))
        u_prev = jnp.where(first, 0.0, u_prev)
        u_next = jnp.where(loc == tm - 1, u_halo_next, pltpu.roll(u, tm - 1, 0))
        u_next = jnp.where(last, 0.0, u_next)
        conv = wc_ref[0:1, cs] * u_prev + wc_ref[1:2, cs] * u + wc_ref[2:3, cs] * u_next
        z_ref[:, cs] = (cb_ref[:, cs].astype(F32) * conv).astype(z_ref.dtype)


def _conv_gate(p, w_conv, rows, s_len, m_len):
    width = w_conv.shape[1]
    tm = _tile(rows, 768, 512, 384, 256, 128, 64)
    hb = tm // SUBLANES
    last_hb = p.shape[0] // SUBLANES - 1

    def prev_map(col):
        return lambda i: (jnp.maximum(i * hb - 1, 0), col)

    def next_map(col):
        return lambda i: (jnp.minimum((i + 1) * hb, last_hb), col)

    kern = functools.partial(_conv_gate_kernel, seg_starts=(0, s_len), seg_ends=(s_len - 1, m_len - 1),
                             col_chunk=_tile(width, 512, LANES))
    return pl.pallas_call(
        kern,
        grid=(rows // tm,),
        in_specs=[pl.BlockSpec((tm, width), lambda i: (i, 0)),
                  pl.BlockSpec((tm, width), lambda i: (i, 1)),
                  pl.BlockSpec((tm, width), lambda i: (i, 2)),
                  pl.BlockSpec((SUBLANES, width), prev_map(0)),
                  pl.BlockSpec((SUBLANES, width), prev_map(2)),
                  pl.BlockSpec((SUBLANES, width), next_map(0)),
                  pl.BlockSpec((SUBLANES, width), next_map(2)),
                  pl.BlockSpec((CONV_K, width), lambda i: (0, 0))],
        out_specs=pl.BlockSpec((tm, width), lambda i: (i, 0)),
        out_shape=jax.ShapeDtypeStruct((rows, width), BF16),
        compiler_params=_cparams(1),
        name="conv_gate",
    )(p, p, p, p, p, p, p, w_conv)


def _merge_kernel(attn_ref, z_ref, sga_ref, sgb_ref, woa_ref, wob_ref, o_ref):
    o_a = _dot(attn_ref[...], woa_ref[...])
    o_b = _dot(z_ref[...], wob_ref[...])
    o_ref[...] = (sga_ref[...].astype(F32) * o_a + sgb_ref[...].astype(F32) * o_b).astype(o_ref.dtype)


def _merge(attn, z, p, w_oa, w_ob, layer, rows):
    attn_w = w_oa.shape[1]
    _, width, d = w_ob.shape
    tm = _tile(rows, 768, 512, 384, 256, 128, 64)
    tn = _tile(d, 1024, 512, 256, LANES)
    ga_blk = 3 * width // tn
    gb_blk = (3 * width + d) // tn
    return pl.pallas_call(
        _merge_kernel,
        grid=(d // tn, rows // tm),
        in_specs=[pl.BlockSpec((tm, attn_w), lambda j, i: (i, 0)),
                  pl.BlockSpec((tm, width), lambda j, i: (i, 0)),
                  pl.BlockSpec((tm, tn), lambda j, i: (i, ga_blk + j)),
                  pl.BlockSpec((tm, tn), lambda j, i: (i, gb_blk + j)),
                  pl.BlockSpec((None, attn_w, tn), lambda j, i: (layer, 0, j)),
                  pl.BlockSpec((None, width, tn), lambda j, i: (layer, 0, j))],
        out_specs=pl.BlockSpec((tm, tn), lambda j, i: (i, j)),
        out_shape=jax.ShapeDtypeStruct((rows, d), BF16),
        compiler_params=_cparams(2),
        name="merge",
    )(attn, z, p, p, w_oa, w_ob)


def _mm_res_kernel(a_ref, w_ref, x_ref, gt_ref, o_ref, *, n_lat):
    tm = a_ref.shape[0]
    row = lax.broadcasted_iota(jnp.int32, (tm, 1), 0) + pl.program_id(1) * tm
    gate = jnp.where(row >= n_lat, gt_ref[1:2, :], gt_ref[0:1, :])
    o_ref[...] = x_ref[...] + gate * _dot(a_ref[...], _bf16(w_ref[...]))


def _matmul_residual(a, w, layer, x, mods, gate_blk, rows, n_lat):
    _, k, n = w.shape
    tm = _tile(rows, 768, 640, 512, 256, 128)
    tn = _tile(n, 1024 if k <= n else 512, 512, 256, LANES)
    nt = n // tn
    return pl.pallas_call(
        functools.partial(_mm_res_kernel, n_lat=n_lat),
        grid=(nt, rows // tm),
        in_specs=[pl.BlockSpec((tm, k), lambda j, i: (i, 0)),
                  pl.BlockSpec((None, k, tn), lambda j, i: (layer, 0, j)),
                  pl.BlockSpec((tm, tn), lambda j, i: (i, j)),
                  pl.BlockSpec((2, tn), lambda j, i: (0, gate_blk * nt + j))],
        out_specs=pl.BlockSpec((tm, tn), lambda j, i: (i, j)),
        out_shape=jax.ShapeDtypeStruct((rows, n), F32),
        compiler_params=_cparams(2),
        name="matmul_residual",
    )(a, w, x, mods)


def _block_in_use(eid_ref):
    return pl.program_id(1) < eid_ref[pl.num_programs(1)]


def _glu_kernel(eid_ref, a_ref, wg_ref, wu_ref, o_ref):
    @pl.when(_block_in_use(eid_ref))
    def _():
        a = _bf16(a_ref[...])
        gate = _dot(a, _bf16(wg_ref[0]))
        up = _dot(a, _bf16(wu_ref[0]))
        o_ref[...] = (gate * jax.nn.sigmoid(gate) * up).astype(o_ref.dtype)

    @pl.when(jnp.logical_not(_block_in_use(eid_ref)))
    def _():
        o_ref[...] = jnp.zeros_like(o_ref)


def _glu(a, w_gate, w_up, eid, tm):
    r, k = a.shape
    f = w_gate.shape[2]
    tn = _tile(f, 512, 1408, 256, LANES)
    grid_spec = pltpu.PrefetchScalarGridSpec(
        num_scalar_prefetch=1,
        grid=(f // tn, r // tm),
        in_specs=[pl.BlockSpec((tm, k), lambda j, i, e: (i, 0)),
                  pl.BlockSpec((1, k, tn), lambda j, i, e: (e[i], 0, j)),
                  pl.BlockSpec((1, k, tn), lambda j, i, e: (e[i], 0, j))],
        out_specs=pl.BlockSpec((tm, tn), lambda j, i, e: (i, j)),
    )
    return pl.pallas_call(
        _glu_kernel,
        grid_spec=grid_spec,
        out_shape=jax.ShapeDtypeStruct((r, f), BF16),
        compiler_params=_cparams(2),
        name="glu",
    )(eid, a, w_gate, w_up)


def _down_kernel(eid_ref, a_ref, w_ref, o_ref):
    @pl.when(_block_in_use(eid_ref))
    def _():
        o_ref[...] = _dot(a_ref[...], _bf16(w_ref[0]))

    @pl.when(jnp.logical_not(_block_in_use(eid_ref)))
    def _():
        o_ref[...] = jnp.zeros_like(o_ref)


def _down_grouped(a, w_down, eid, tm):
    r, f = a.shape
    d = w_down.shape[2]
    tn = _tile(d, 1024, 512, 256, LANES)
    grid_spec = pltpu.PrefetchScalarGridSpec(
        num_scalar_prefetch=1,
        grid=(d // tn, r // tm),
        in_specs=[pl.BlockSpec((tm, f), lambda j, i, e: (i, 0)),
                  pl.BlockSpec((1, f, tn), lambda j, i, e: (e[i], 0, j))],
        out_specs=pl.BlockSpec((tm, tn), lambda j, i, e: (i, j)),
    )
    return pl.pallas_call(
        _down_kernel,
        grid_spec=grid_spec,
        out_shape=jax.ShapeDtypeStruct((r, d), F32),
        compiler_params=_cparams(2),
        name="down_grouped",
    )(eid, a, w_down)


def _row_copies(idx_ref, idx_base, idx_stride, src_ref, dst_ref, sem, n_rows, start):
    if not start:
        pltpu.make_async_copy(src_ref.at[pl.ds(0, n_rows)], dst_ref, sem).wait()
        return

    def body(r2, c):
        for prio in range(2):
            r = 2 * r2 + prio
            row = idx_ref[idx_base + r * idx_stride]
            pltpu.make_async_copy(src_ref.at[pl.ds(row, 1)], dst_ref.at[pl.ds(r, 1)], sem).start(priority=prio)
        return c

    assert n_rows % 2 == 0 and dst_ref.shape[0] == n_rows
    lax.fori_loop(0, n_rows // 2, body, 0, unroll=DMA_ISSUE_UNROLL // 2)


def _prefetched_gather(copies):
    i = pl.program_id(0)
    slot = i % 2

    @pl.when(i == 0)
    def _():
        copies(0, 0, True)

    @pl.when(i + 1 < pl.num_programs(0))
    def _():
        copies(i + 1, 1 - slot, True)

    copies(i, slot, False)
    return slot


def _gather_rows_kernel(idx_ref, src_ref, o_ref, g_ref, sem):
    tm = o_ref.shape[0]

    def copies(step, slot, start):
        _row_copies(idx_ref, step * tm, 1, src_ref, g_ref.at[slot], sem.at[slot], tm, start)

    slot = _prefetched_gather(copies)
    o_ref[...] = g_ref[slot].astype(o_ref.dtype)


def _gather_rows(src, idx, tm, out_dtype):
    n = idx.shape[0]
    w = src.shape[1]
    grid_spec = pltpu.PrefetchScalarGridSpec(
        num_scalar_prefetch=1,
        grid=(n // tm,),
        in_specs=[pl.BlockSpec(memory_space=pl.ANY)],
        out_specs=pl.BlockSpec((tm, w), lambda i, idx_ref: (i, 0)),
        scratch_shapes=[pltpu.VMEM((2, tm, w), src.dtype), pltpu.SemaphoreType.DMA((2,))],
    )
    return pl.pallas_call(
        _gather_rows_kernel,
        grid_spec=grid_spec,
        out_shape=jax.ShapeDtypeStruct((n, w), out_dtype),
        compiler_params=_cparams(1),
        name="gather_rows",
    )(idx, src)


def _final_kernel(dest_ref, x_ref, y_hbm_ref, gate_ref, gt_ref, g_ref, o_ref, y_ref, sem):
    tm = x_ref.shape[0]

    def copies(step, slot, start):
        for kk in range(TOP_K):
            _row_copies(dest_ref, step * tm * TOP_K + kk, TOP_K, y_hbm_ref, y_ref.at[slot, kk], sem.at[slot],
                        tm, start)

    slot = _prefetched_gather(copies)
    gates = gate_ref[...]
    y = gates[:, 0:1] * y_ref[slot, 0] + gates[:, 1:2] * y_ref[slot, 1]
    x = x_ref[...] + gt_ref[0:1, :] * y
    o_ref[...] = x * lax.rsqrt(jnp.mean(x * x, axis=-1, keepdims=True) + NORM_EPS) * g_ref[...]


def _combine_final(x, ybuf, dest, gates, mods, gate_blk, g_final):
    r, d = x.shape
    tm = _tile(r, 256, 128, 64, SUBLANES)
    grid_spec = pltpu.PrefetchScalarGridSpec(
        num_scalar_prefetch=1,
        grid=(r // tm,),
        in_specs=[pl.BlockSpec((tm, d), lambda i, dest_ref: (i, 0)),
                  pl.BlockSpec(memory_space=pl.ANY),
                  pl.BlockSpec((tm, LANES), lambda i, dest_ref: (i, 0)),
                  pl.BlockSpec((2, d), lambda i, dest_ref: (0, gate_blk)),
                  pl.BlockSpec((1, d), lambda i, dest_ref: (0, 0))],
        out_specs=pl.BlockSpec((tm, d), lambda i, dest_ref: (i, 0)),
        scratch_shapes=[pltpu.VMEM((2, TOP_K, tm, d), F32), pltpu.SemaphoreType.DMA((2,))],
    )
    return pl.pallas_call(
        _final_kernel,
        grid_spec=grid_spec,
        out_shape=jax.ShapeDtypeStruct((r, d), F32),
        compiler_params=_cparams(1),
        name="combine_final",
    )(dest, x, ybuf, gates, mods, g_final.reshape(1, d))


def _rope_tables(s_len, c_len):
    quarter = QK_ROPE_DIM // 4
    tok = jnp.arange(s_len + c_len, dtype=jnp.int32)[:, None]
    lane = jnp.arange(LANES, dtype=jnp.int32)[None, :]
    group = lane // quarter
    pos = jnp.where(group < 2, tok // GRID_W, tok % GRID_W).astype(F32)
    inv_freq = jnp.power(ROPE_THETA, -(2 * (lane % quarter)).astype(F32) / (QK_ROPE_DIM // 2))
    ang = jnp.where(tok < s_len, pos * inv_freq, 0.0)
    live = group < 4
    cos = jnp.where(live, jnp.cos(ang), 0.0)
    sin = jnp.where(live, jnp.where(group % 2 == 0, -jnp.sin(ang), jnp.sin(ang)), 0.0)
    return cos, sin


def _rope_swap_perm():
    q = QK_ROPE_DIM // 4
    return jnp.concatenate([jnp.arange(q, 2 * q), jnp.arange(0, q), jnp.arange(3 * q, 4 * q), jnp.arange(2 * q, 3 * q)])


def _layer_weights(w_in_t, w_qb_all, li):
    d = w_in_t.shape[2]
    perm = _rope_swap_perm()
    kr_lo = Q_LORA_RANK + KV_LORA_RANK
    kr_hi = kr_lo + QK_ROPE_DIM
    w_kr = w_in_t[li, kr_lo:kr_hi]
    zpad = jnp.zeros((LANES - QK_ROPE_DIM, d), w_in_t.dtype)
    w_a = jnp.concatenate([w_kr, zpad, w_kr[perm], zpad], axis=0)
    w_qb = w_qb_all[li]
    qb = w_qb.reshape(Q_LORA_RANK, N_HEADS, QK_NOPE_DIM + QK_ROPE_DIM)
    q_rope = qb[:, :, QK_NOPE_DIM:]
    assert 2 * QK_ROPE_DIM == LANES
    w_q = jnp.concatenate([qb, q_rope[:, :, perm]], axis=2).reshape(Q_LORA_RANK, N_HEADS * HEAD_W)
    return w_a.astype(BF16), w_q.astype(BF16)


def _moe_slots(idx, n_blocks):
    e_flat = idx.reshape(-1)
    onehot = (e_flat[:, None] == jnp.arange(N_EXPERTS)[None, :]).astype(jnp.int32)
    csum = jnp.cumsum(onehot, axis=0)
    counts = csum[-1]
    rank = jnp.sum((csum - onehot) * onehot, axis=1)
    padded = (counts + MOE_ROWS - 1) // MOE_ROWS * MOE_ROWS
    p_end = jnp.cumsum(padded)
    p_start = p_end - padded
    dest = jnp.sum(onehot * p_start[None, :], axis=1) + rank
    blk_lo = jnp.arange(n_blocks) * MOE_ROWS
    block_expert = jnp.minimum(jnp.sum((blk_lo[:, None] >= p_end[None, :]).astype(jnp.int32), axis=1),
                               N_EXPERTS - 1)
    blocks_in_use = p_end[-1:] // MOE_ROWS
    return dest.astype(jnp.int32), jnp.concatenate([block_expert, blocks_in_use]).astype(jnp.int32)


def kernel(x, c, ctx, c_ctx, w_ada, b_ada, g_attn, w_in, g_qa, w_qb, g_kva, w_kvb, w_conv, w_oa, w_ob, w_o,
           g_ffn, w_gate_dense, w_up_dense, w_down_dense, w_router, w_gate_exp, w_up_exp, w_down_exp, g_final):
    _, s_len, d = x.shape
    c_len = ctx.shape[1]
    m_len = s_len + c_len
    depth = w_in.shape[0]
    width = w_conv.shape[2]
    assert depth == 2, "supported stack: dense-FFN layer with context updates, then a final expert-FFN layer"

    mods_all = _ada_mod(c, c_ctx, w_ada, b_ada)
    cos_t, sin_t = _rope_tables(s_len, c_len)
    w_in_t = jnp.swapaxes(w_in, 1, 2)
    w_oa_bf, w_ob_bf, w_kvb_bf = w_oa.astype(BF16), w_ob.astype(BF16), w_kvb.astype(BF16)

    for li in range(depth):
        last = li == depth - 1
        mods = mods_all[li]
        w_a, w_q = _layer_weights(w_in_t, w_qb, li)
        g_a = jnp.concatenate([g_qa[li], g_kva[li]]).reshape(1, -1)
        rows = s_len if last else m_len

        if li == 0:
            h, xa = _norm_modulate_join(x[0], ctx[0], g_attn[li], mods, 0, 1, BF16)
        else:
            h = _norm_modulate(xa, g_attn[li], mods, 0, 1, s_len, BF16)
        qa_n, ckv_n, kr = _in_proj_a(h, w_in_t, li, w_a, g_a, cos_t, sin_t)
        p = _in_proj_b(h, w_in_t, li, Q_LORA_RANK + KV_LORA_RANK + QK_ROPE_DIM, 3 * width)
        q = _q_proj(qa_n, w_q, cos_t, sin_t, rows)
        k, v = _kv_proj(ckv_n, kr, w_kvb_bf, li)
        attn = _attention(q, k, v, 0, s_len, 0, m_len)
        if not last:
            attn = jnp.concatenate([attn, _attention(q, k, v, s_len, c_len, s_len, c_len)], axis=0)
        z = _conv_gate(p, w_conv[li], rows, s_len, m_len)
        merged = _merge(attn, z, p, w_oa_bf, w_ob_bf, li, rows)
        xa = _matmul_residual(merged, w_o, li, xa, mods, 2, rows, s_len)

        j = li // 2
        if li % 2 == 0:
            h2 = _norm_modulate(xa, g_ffn[li], mods, 3, 4, s_len, BF16)
            tm = _tile(rows, 1408, 768, 640, 512, 256, 128)
            eid = jnp.zeros((rows // tm + 1,), jnp.int32).at[-1].set(rows // tm)
            hid = _glu(h2, w_gate_dense[j][None], w_up_dense[j][None], eid, tm)
            xa = _matmul_residual(hid, w_down_dense.astype(BF16), j, xa, mods, 5, rows, s_len)
        else:
            h2, idx, gates = _norm_modulate_route(xa, g_ffn[li], mods, 3, 4, w_router[j])
            n_assign = rows * TOP_K
            n_blocks = -(-n_assign // MOE_ROWS) + N_EXPERTS
            dest, block_expert = _moe_slots(idx[:, :TOP_K], n_blocks)
            tok = jnp.arange(n_assign, dtype=jnp.int32) // TOP_K
            slot_tok = jnp.zeros((n_blocks * MOE_ROWS,), jnp.int32).at[dest].set(tok, unique_indices=True)
            buf = _gather_rows(h2, slot_tok, MOE_ROWS, BF16)
            hid = _glu(buf, w_gate_exp[j], w_up_exp[j], block_expert, MOE_ROWS)
            ybuf = _down_grouped(hid, w_down_exp[j], block_expert, MOE_ROWS)
            return _combine_final(xa, ybuf, dest, gates, mods, 5, g_final)[None]
    raise AssertionError("unreachable: the final layer returns")
```

```python
import functools

import jax
import jax.numpy as jnp
from jax import lax
from jax.experimental import pallas as pl
from jax.experimental.pallas import tpu as pltpu

F32 = jnp.float32
BF16 = jnp.bfloat16

N_HEADS = 16
QK_NOPE_DIM = 128
QK_ROPE_DIM = 64
V_HEAD_DIM = 128
Q_LORA_RANK = 512
KV_LORA_RANK = 512
GRID_W = 64
ROPE_THETA = 10000.0
ATTN_SCALE = (QK_NOPE_DIM + QK_ROPE_DIM) ** -0.5
Q_SCALE = ATTN_SCALE * 1.4426950408889634
CONV_K = 3
N_EXPERTS = 8
TOP_K = 2
NORM_EPS = 1e-6

LANES = 128
SUBLANES = 8
HEAD_W = 2 * LANES
VMEM_LIMIT_BYTES = 56 * 1024 * 1024

MOE_ROWS = 512
DMA_ISSUE_UNROLL = 8


def _tile(n, *cands):
    for c in cands:
        if n % c == 0:
            return c
    return n


def _cparams(n_axes):
    return pltpu.CompilerParams(dimension_semantics=("arbitrary",) * n_axes,
                                vmem_limit_bytes=VMEM_LIMIT_BYTES)


def _dot(a, b):
    return jnp.dot(a, b, preferred_element_type=F32)


def _dot_t(a, bt):
    return lax.dot_general(a, bt, (((1,), (1,)), ((), ())), preferred_element_type=F32)


def _bf16(w):
    return w if w.dtype == BF16 else w.astype(BF16)


def _ada_kernel(xt_ref, w_ref, b_ref, o_ref, *, k_chunk):
    d = xt_ref.shape[0]
    tn = o_ref.shape[-1]

    def body(k, acc):
        a0, a1 = acc
        ks = pl.multiple_of(k * k_chunk, k_chunk)
        xt = xt_ref[pl.ds(ks, k_chunk), :]
        s = xt * jax.nn.sigmoid(xt)
        w = w_ref[0, pl.ds(ks, k_chunk), :]
        a0 = a0 + jnp.sum(w * s[:, 0:1], axis=0, keepdims=True)
        a1 = a1 + jnp.sum(w * s[:, 1:2], axis=0, keepdims=True)
        return a0, a1

    z = jnp.zeros((1, tn), F32)
    a0, a1 = lax.fori_loop(0, d // k_chunk, body, (z, z))
    o_ref[0, 0:1, :] = a0 + b_ref[0]
    o_ref[0, 1:2, :] = a1 + b_ref[0]


def _ada_mod(c, c_ctx, w_ada, b_ada):
    depth, d, n = w_ada.shape
    xt = jnp.stack([c[0], c_ctx], axis=1)
    tn = _tile(n, 1024, 512, LANES)
    k_chunk = _tile(d, 256, SUBLANES)
    return pl.pallas_call(
        functools.partial(_ada_kernel, k_chunk=k_chunk),
        grid=(depth, n // tn),
        in_specs=[pl.BlockSpec((d, 2), lambda l, j: (0, 0)),
                  pl.BlockSpec((1, d, tn), lambda l, j: (l, 0, j)),
                  pl.BlockSpec((1, 1, tn), lambda l, j: (l, 0, j))],
        out_specs=pl.BlockSpec((1, 2, tn), lambda l, j: (l, 0, j)),
        out_shape=jax.ShapeDtypeStruct((depth, 2, n), F32),
        compiler_params=_cparams(2),
        name="ada_mod",
    )(xt, w_ada, b_ada.reshape(depth, 1, n))


def _norm_mod(x, g_ref, sh_ref, sc_ref, is_ctx):
    y = x * lax.rsqrt(jnp.mean(x * x, axis=-1, keepdims=True) + NORM_EPS) * g_ref[...]
    sh = jnp.where(is_ctx, sh_ref[1:2, :], sh_ref[0:1, :])
    sc = jnp.where(is_ctx, sc_ref[1:2, :], sc_ref[0:1, :])
    return y * (1.0 + sc) + sh


def _norm_kernel(x_ref, g_ref, sh_ref, sc_ref, o_ref, *, n_lat_tiles):
    is_ctx = pl.program_id(0) >= n_lat_tiles
    o_ref[...] = _norm_mod(x_ref[...], g_ref, sh_ref, sc_ref, is_ctx).astype(o_ref.dtype)


def _norm_router_kernel(x_ref, g_ref, sh_ref, sc_ref, wr_ref, h_ref, idx_ref, gate_ref):
    h = _norm_mod(x_ref[...], g_ref, sh_ref, sc_ref, False)
    h_ref[...] = h
    logits = jnp.dot(h, wr_ref[...], preferred_element_type=F32, precision=lax.Precision.HIGHEST)
    lane = lax.broadcasted_iota(jnp.int32, logits.shape, 1).astype(F32)
    neg = jnp.float32(-jnp.inf)
    l1 = jnp.where(lane < N_EXPERTS, logits, neg)
    v1 = jnp.max(l1, axis=-1, keepdims=True)
    i1 = jnp.min(jnp.where(l1 == v1, lane, float(LANES)), axis=-1, keepdims=True)
    l2 = jnp.where(lane == i1, neg, l1)
    v2 = jnp.max(l2, axis=-1, keepdims=True)
    i2 = jnp.min(jnp.where(l2 == v2, lane, float(LANES)), axis=-1, keepdims=True)
    e = jnp.exp(v2 - v1)
    g1 = 1.0 / (1.0 + e)
    g2 = e / (1.0 + e)
    idx_ref[...] = jnp.where(lane == 0, i1, jnp.where(lane == 1, i2, 0.0)).astype(jnp.int32)
    gate_ref[...] = jnp.where(lane == 0, g1, jnp.where(lane == 1, g2, 0.0))


def _norm_join_kernel(x_ref, c_ref, g_ref, sh_ref, sc_ref, o_ref, xa_ref, *, n_lat_tiles):
    is_ctx = pl.program_id(0) >= n_lat_tiles
    x = jnp.where(is_ctx, c_ref[...], x_ref[...])
    xa_ref[...] = x
    o_ref[...] = _norm_mod(x, g_ref, sh_ref, sc_ref, is_ctx).astype(o_ref.dtype)


def _norm_modulate_join(x, ctx, g, mods, sh_blk, sc_blk, out_dtype):
    n_lat, d = x.shape
    n_ctx = ctx.shape[0]
    tm = _tile(n_ctx, 256, 128, 64, SUBLANES)
    assert n_lat % tm == 0
    nl = n_lat // tm
    return pl.pallas_call(
        functools.partial(_norm_join_kernel, n_lat_tiles=nl),
        grid=((n_lat + n_ctx) // tm,),
        in_specs=[pl.BlockSpec((tm, d), lambda i: (jnp.minimum(i, nl - 1), 0)),
                  pl.BlockSpec((tm, d), lambda i: (jnp.maximum(i - nl, 0), 0)),
                  pl.BlockSpec((1, d), lambda i: (0, 0)),
                  pl.BlockSpec((2, d), lambda i: (0, sh_blk)),
                  pl.BlockSpec((2, d), lambda i: (0, sc_blk))],
        out_specs=[pl.BlockSpec((tm, d), lambda i: (i, 0)),
                   pl.BlockSpec((tm, d), lambda i: (i, 0))],
        out_shape=[jax.ShapeDtypeStruct((n_lat + n_ctx, d), out_dtype),
                   jax.ShapeDtypeStruct((n_lat + n_ctx, d), F32)],
        compiler_params=_cparams(1),
        name="norm_modulate_join",
    )(x, ctx, g.reshape(1, d), mods, mods)


def _norm_modulate(x, g, mods, sh_blk, sc_blk, n_lat, out_dtype):
    r, d = x.shape
    tm = _tile(n_lat, 256, 128, 64, SUBLANES) if r > n_lat else _tile(r, 256, 128, 64, SUBLANES)
    if r > n_lat:
        tm = _tile(r - n_lat, tm, 128, 64, SUBLANES)
    return pl.pallas_call(
        functools.partial(_norm_kernel, n_lat_tiles=n_lat // tm),
        grid=(r // tm,),
        in_specs=[pl.BlockSpec((tm, d), lambda i: (i, 0)),
                  pl.BlockSpec((1, d), lambda i: (0, 0)),
                  pl.BlockSpec((2, d), lambda i: (0, sh_blk)),
                  pl.BlockSpec((2, d), lambda i: (0, sc_blk))],
        out_specs=pl.BlockSpec((tm, d), lambda i: (i, 0)),
        out_shape=jax.ShapeDtypeStruct((r, d), out_dtype),
        compiler_params=_cparams(1),
        name="norm_modulate",
    )(x, g.reshape(1, d), mods, mods)


def _norm_modulate_route(x, g, mods, sh_blk, sc_blk, w_router):
    r, d = x.shape
    tm = _tile(r, 512, 256, 128, 64, SUBLANES)
    wr = jnp.pad(w_router, ((0, 0), (0, LANES - N_EXPERTS)))
    return pl.pallas_call(
        _norm_router_kernel,
        grid=(r // tm,),
        in_specs=[pl.BlockSpec((tm, d), lambda i: (i, 0)),
                  pl.BlockSpec((1, d), lambda i: (0, 0)),
                  pl.BlockSpec((2, d), lambda i: (0, sh_blk)),
                  pl.BlockSpec((2, d), lambda i: (0, sc_blk)),
                  pl.BlockSpec((d, LANES), lambda i: (0, 0))],
        out_specs=[pl.BlockSpec((tm, d), lambda i: (i, 0)),
                   pl.BlockSpec((tm, LANES), lambda i: (i, 0)),
                   pl.BlockSpec((tm, LANES), lambda i: (i, 0))],
        out_shape=[jax.ShapeDtypeStruct((r, d), F32),
                   jax.ShapeDtypeStruct((r, LANES), jnp.int32),
                   jax.ShapeDtypeStruct((r, LANES), F32)],
        compiler_params=_cparams(1),
        name="norm_modulate_route",
    )(x, g.reshape(1, d), mods, mods, wr)


def _in_a_kernel(h_ref, wt_ref, wkr_ref, g_ref, cos_ref, sin_ref, qa_ref, ckv_ref, kr_ref):
    h = h_ref[...]
    acc = _dot_t(h, _bf16(wt_ref[...]))
    acc_kr = _dot_t(h, wkr_ref[...])

    def rms(v, g):
        return v * lax.rsqrt(jnp.mean(v * v, axis=-1, keepdims=True) + NORM_EPS) * g

    q1 = Q_LORA_RANK
    c1 = q1 + KV_LORA_RANK
    qa_ref[...] = rms(acc[:, :q1], g_ref[:, :q1]).astype(qa_ref.dtype)
    ckv_ref[...] = rms(acc[:, q1:c1], g_ref[:, q1:c1]).astype(ckv_ref.dtype)
    kr = acc_kr[:, :LANES] * cos_ref[...] + acc_kr[:, LANES:] * sin_ref[...]
    kr_ref[...] = kr.astype(kr_ref.dtype)


def _in_proj_a(h, w_in_t, li, w_kr, g_a, cos_t, sin_t):
    m, d = h.shape
    n = Q_LORA_RANK + KV_LORA_RANK
    base = li * w_in_t.shape[1]
    assert base % SUBLANES == 0
    tm = _tile(m, 768, 640, 512, 256, 128)
    return pl.pallas_call(
        _in_a_kernel,
        grid=(m // tm,),
        in_specs=[pl.BlockSpec((tm, d), lambda i: (i, 0)),
                  pl.BlockSpec((pl.Element(n), pl.Element(d)), lambda i: (base, 0)),
                  pl.BlockSpec(w_kr.shape, lambda i: (0, 0)),
                  pl.BlockSpec((1, n), lambda i: (0, 0)),
                  pl.BlockSpec((tm, LANES), lambda i: (i, 0)),
                  pl.BlockSpec((tm, LANES), lambda i: (i, 0))],
        out_specs=[pl.BlockSpec((tm, Q_LORA_RANK), lambda i: (i, 0)),
                   pl.BlockSpec((tm, KV_LORA_RANK), lambda i: (i, 0)),
                   pl.BlockSpec((tm, LANES), lambda i: (i, 0))],
        out_shape=[jax.ShapeDtypeStruct((m, Q_LORA_RANK), BF16),
                   jax.ShapeDtypeStruct((m, KV_LORA_RANK), BF16),
                   jax.ShapeDtypeStruct((m, LANES), BF16)],
        compiler_params=_cparams(1),
        name="in_proj_a",
    )(h, w_in_t.reshape(-1, d), w_kr, g_a, cos_t, sin_t)


def _in_b_kernel(h_ref, wt_ref, o_ref, *, first_gate_tile):
    acc = _dot_t(h_ref[...], _bf16(wt_ref[...]))
    is_gate = pl.program_id(0) >= first_gate_tile

    @pl.when(is_gate)
    def _():
        o_ref[...] = jax.nn.sigmoid(acc).astype(o_ref.dtype)

    @pl.when(jnp.logical_not(is_gate))
    def _():
        o_ref[...] = acc.astype(o_ref.dtype)


def _in_proj_b(h, w_in_t, li, row0, first_gate_col):
    m, d = h.shape
    n = w_in_t.shape[1] - row0
    tm = _tile(m, 1408, 768, 640, 512, 256, 128)
    tn = _tile(first_gate_col, 1024, 512, 256, LANES)
    base = li * w_in_t.shape[1] + row0
    assert base % SUBLANES == 0 and n % tn == 0
    return pl.pallas_call(
        functools.partial(_in_b_kernel, first_gate_tile=first_gate_col // tn),
        grid=(n // tn, m // tm),
        in_specs=[pl.BlockSpec((tm, d), lambda j, i: (i, 0)),
                  pl.BlockSpec((pl.Element(tn), pl.Element(d)),
                               lambda j, i: (pl.multiple_of(base + j * tn, SUBLANES), 0))],
        out_specs=pl.BlockSpec((tm, tn), lambda j, i: (i, j)),
        out_shape=jax.ShapeDtypeStruct((m, n), BF16),
        compiler_params=_cparams(2),
        name="in_proj_b",
    )(h, w_in_t.reshape(-1, d))


def _q_kernel(a_ref, w_ref, cos_ref, sin_ref, q_ref):
    a = a_ref[...]
    cos = cos_ref[...]
    sin = sin_ref[...]
    for hd in range(N_HEADS):
        acc = _dot(a, w_ref[:, hd * HEAD_W:(hd + 1) * HEAD_W])
        q_ref[:, hd * HEAD_W:hd * HEAD_W + LANES] = (acc[:, :LANES] * Q_SCALE).astype(q_ref.dtype)
        rope = acc[:, LANES:]
        swapped = pltpu.roll(rope, LANES - QK_ROPE_DIM, 1)
        rot = (rope * cos + swapped * sin) * Q_SCALE
        q_ref[:, hd * HEAD_W + LANES:(hd + 1) * HEAD_W] = rot.astype(q_ref.dtype)


def _q_proj(qa_n, w_q, cos_t, sin_t, rows):
    k = qa_n.shape[1]
    tm = _tile(rows, 768, 640, 512, 256, 128)
    return pl.pallas_call(
        _q_kernel,
        grid=(rows // tm,),
        in_specs=[pl.BlockSpec((tm, k), lambda i: (i, 0)),
                  pl.BlockSpec(w_q.shape, lambda i: (0, 0)),
                  pl.BlockSpec((tm, LANES), lambda i: (i, 0)),
                  pl.BlockSpec((tm, LANES), lambda i: (i, 0))],
        out_specs=pl.BlockSpec((tm, N_HEADS * HEAD_W), lambda i: (i, 0)),
        out_shape=jax.ShapeDtypeStruct((rows, N_HEADS * HEAD_W), BF16),
        compiler_params=_cparams(1),
        name="q_proj",
    )(qa_n, w_q, cos_t, sin_t)


def _kv_kernel(c_ref, kr_ref, wkv_ref, k_ref, v_ref):
    c = c_ref[...]
    kr = kr_ref[...]
    lane = lax.broadcasted_iota(jnp.int32, (c.shape[0], LANES), 1)
    ones_col = jnp.where(lane == 0, 1.0, 0.0).astype(v_ref.dtype)
    for hd in range(N_HEADS):
        kv = _dot(c, wkv_ref[:, hd * HEAD_W:(hd + 1) * HEAD_W])
        k_ref[:, hd * HEAD_W:hd * HEAD_W + LANES] = kv[:, :QK_NOPE_DIM].astype(k_ref.dtype)
        k_ref[:, hd * HEAD_W + LANES:(hd + 1) * HEAD_W] = kr
        v_ref[:, hd * HEAD_W:hd * HEAD_W + LANES] = kv[:, QK_NOPE_DIM:].astype(v_ref.dtype)
        v_ref[:, hd * HEAD_W + LANES:(hd + 1) * HEAD_W] = ones_col


def _kv_proj(ckv_n, kr, w_kvb, layer):
    m, k = ckv_n.shape
    tm = _tile(m, 768, 640, 512, 256, 128)
    assert QK_NOPE_DIM == LANES and V_HEAD_DIM == LANES
    return pl.pallas_call(
        _kv_kernel,
        grid=(m // tm,),
        in_specs=[pl.BlockSpec((tm, k), lambda i: (i, 0)),
                  pl.BlockSpec((tm, LANES), lambda i: (i, 0)),
                  pl.BlockSpec((None,) + w_kvb.shape[1:], lambda i: (layer, 0, 0))],
        out_specs=[pl.BlockSpec((tm, N_HEADS * HEAD_W), lambda i: (i, 0)),
                   pl.BlockSpec((tm, N_HEADS * HEAD_W), lambda i: (i, 0))],
        out_shape=[jax.ShapeDtypeStruct((m, N_HEADS * HEAD_W), BF16),
                   jax.ShapeDtypeStruct((m, N_HEADS * HEAD_W), BF16)],
        compiler_params=_cparams(1),
        name="kv_proj",
    )(ckv_n, kr, w_kvb)


def _attn_kernel(q_ref, k_ref, v_ref, o_ref, sa_ref, sb_ref, *, tk, n_chunks):
    q = q_ref[...]
    tq = q.shape[0]

    def scores(c):
        return _dot_t(q, k_ref[c * tk:(c + 1) * tk, :])

    def absorb(s_ref, c, carry):
        m, acc = carry
        s = s_ref[...]
        m_new = jnp.maximum(m, jnp.max(s, axis=-1, keepdims=True))
        p = jnp.exp2(s - m_new).astype(v_ref.dtype)
        acc = jnp.exp2(m - m_new) * acc + _dot(p, v_ref[c * tk:(c + 1) * tk, :])
        return m_new, acc

    s_refs = (sa_ref, sb_ref)
    carry = (jnp.full((tq, 1), -jnp.inf, F32), jnp.zeros((tq, HEAD_W), F32))
    sa_ref[...] = scores(0)
    for c in range(n_chunks):
        if c + 1 < n_chunks:
            s_refs[(c + 1) % 2][...] = scores(c + 1)
        carry = absorb(s_refs[c % 2], c, carry)
    _, acc = carry
    o_ref[...] = (acc[:, :V_HEAD_DIM] / acc[:, V_HEAD_DIM:V_HEAD_DIM + 1]).astype(o_ref.dtype)


def _attention(q, k, v, q_row0, n_q, k_row0, n_k):
    tq = _tile(n_q, 1024, 512, 256, 128)
    tk = _tile(n_k, 768, 640, 512, 384, 256, 128)
    assert q_row0 % tq == 0 and k_row0 % n_k == 0
    q_blk0 = q_row0 // tq
    k_blk = k_row0 // n_k
    return pl.pallas_call(
        functools.partial(_attn_kernel, tk=tk, n_chunks=n_k // tk),
        grid=(N_HEADS, n_q // tq),
        in_specs=[pl.BlockSpec((tq, HEAD_W), lambda h, i: (q_blk0 + i, h)),
                  pl.BlockSpec((n_k, HEAD_W), lambda h, i: (k_blk, h)),
                  pl.BlockSpec((n_k, HEAD_W), lambda h, i: (k_blk, h))],
        out_specs=pl.BlockSpec((tq, V_HEAD_DIM), lambda h, i: (i, h)),
        out_shape=jax.ShapeDtypeStruct((n_q, N_HEADS * V_HEAD_DIM), BF16),
        scratch_shapes=[pltpu.VMEM((tq, tk), F32), pltpu.VMEM((tq, tk), F32)],
        compiler_params=_cparams(2),
        name="attention",
    )(q, k, v)


def _conv_gate_kernel(cx_ref, cb_ref, cc_ref, cxp_ref, ccp_ref, cxn_ref, ccn_ref, wc_ref, z_ref, *,
                      seg_starts, seg_ends, col_chunk):
    tm, width = z_ref.shape
    loc = lax.broadcasted_iota(jnp.int32, (tm, 1), 0)
    row = loc + pl.program_id(0) * tm
    first = functools.reduce(jnp.logical_or, [row == r for r in seg_starts])
    last = functools.reduce(jnp.logical_or, [row == r for r in seg_ends])
    for c0 in range(0, width, col_chunk):
        cs = slice(c0, c0 + col_chunk)
        u = cx_ref[:, cs].astype(F32) * cc_ref[:, cs].astype(F32)
        u_halo_prev = (cxp_ref[SUBLANES - 1:SUBLANES, cs].astype(F32)
                       * ccp_ref[SUBLANES - 1:SUBLANES, cs].astype(F32))
        u_halo_next = cxn_ref[0:1, cs].astype(F32) * ccn_ref[0:1, cs].astype(F32)
        u_prev = jnp.where(loc == 0, u_halo_prev, pltpu.roll(u, 1, 0))
        u_prev = jnp.where(first, 0.0, u_prev)
        u_next = jnp.where(loc == tm - 1, u_halo_next, pltpu.roll(u, tm - 1, 0))
        u_next = jnp.where(last, 0.0, u_next)
        conv = wc_ref[0:1, cs] * u_prev + wc_ref[1:2, cs] * u + wc_ref[2:3, cs] * u_next
        z_ref[:, cs] = (cb_ref[:, cs].astype(F32) * conv).astype(z_ref.dtype)


def _conv_gate(p, w_conv, rows, s_len, m_len):
    width = w_conv.shape[1]
    tm = _tile(rows, 768, 512, 384, 256, 128, 64)
    hb = tm // SUBLANES
    last_hb = p.shape[0] // SUBLANES - 1

    def prev_map(col):
        return lambda i: (jnp.maximum(i * hb - 1, 0), col)

    def next_map(col):
        return lambda i: (jnp.minimum((i + 1) * hb, last_hb), col)

    kern = functools.partial(_conv_gate_kernel, seg_starts=(0, s_len), seg_ends=(s_len - 1, m_len - 1),
                             col_chunk=_tile(width, 512, LANES))
    return pl.pallas_call(
        kern,
        grid=(rows // tm,),
        in_specs=[pl.BlockSpec((tm, width), lambda i: (i, 0)),
                  pl.BlockSpec((tm, width), lambda i: (i, 1)),
                  pl.BlockSpec((tm, width), lambda i: (i, 2)),
                  pl.BlockSpec((SUBLANES, width), prev_map(0)),
                  pl.BlockSpec((SUBLANES, width), prev_map(2)),
                  pl.BlockSpec((SUBLANES, width), next_map(0)),
                  pl.BlockSpec((SUBLANES, width), next_map(2)),
                  pl.BlockSpec((CONV_K, width), lambda i: (0, 0))],
        out_specs=pl.BlockSpec((tm, width), lambda i: (i, 0)),
        out_shape=jax.ShapeDtypeStruct((rows, width), BF16),
        compiler_params=_cparams(1),
        name="conv_gate",
    )(p, p, p, p, p, p, p, w_conv)


def _merge_kernel(attn_ref, z_ref, sga_ref, sgb_ref, woa_ref, wob_ref, o_ref):
    o_a = _dot(attn_ref[...], woa_ref[...])
    o_b = _dot(z_ref[...], wob_ref[...])
    o_ref[...] = (sga_ref[...].astype(F32) * o_a + sgb_ref[...].astype(F32) * o_b).astype(o_ref.dtype)


def _merge(attn, z, p, w_oa, w_ob, layer, rows):
    attn_w = w_oa.shape[1]
    _, width, d = w_ob.shape
    tm = _tile(rows, 768, 512, 384, 256, 128, 64)
    tn = _tile(d, 1024, 512, 256, LANES)
    ga_blk = 3 * width // tn
    gb_blk = (3 * width + d) // tn
    return pl.pallas_call(
        _merge_kernel,
        grid=(d // tn, rows // tm),
        in_specs=[pl.BlockSpec((tm, attn_w), lambda j, i: (i, 0)),
                  pl.BlockSpec((tm, width), lambda j, i: (i, 0)),
                  pl.BlockSpec((tm, tn), lambda j, i: (i, ga_blk + j)),
                  pl.BlockSpec((tm, tn), lambda j, i: (i, gb_blk + j)),
                  pl.BlockSpec((None, attn_w, tn), lambda j, i: (layer, 0, j)),
                  pl.BlockSpec((None, width, tn), lambda j, i: (layer, 0, j))],
        out_specs=pl.BlockSpec((tm, tn), lambda j, i: (i, j)),
        out_shape=jax.ShapeDtypeStruct((rows, d), BF16),
        compiler_params=_cparams(2),
        name="merge",
    )(attn, z, p, p, w_oa, w_ob)


def _mm_res_kernel(a_ref, w_ref, x_ref, gt_ref, o_ref, *, n_lat):
    tm = a_ref.shape[0]
    row = lax.broadcasted_iota(jnp.int32, (tm, 1), 0) + pl.program_id(1) * tm
    gate = jnp.where(row >= n_lat, gt_ref[1:2, :], gt_ref[0:1, :])
    o_ref[...] = x_ref[...] + gate * _dot(a_ref[...], _bf16(w_ref[...]))


def _matmul_residual(a, w, layer, x, mods, gate_blk, rows, n_lat):
    _, k, n = w.shape
    tm = _tile(rows, 768, 640, 512, 256, 128)
    tn = _tile(n, 1024 if k <= n else 512, 512, 256, LANES)
    nt = n // tn
    return pl.pallas_call(
        functools.partial(_mm_res_kernel, n_lat=n_lat),
        grid=(nt, rows // tm),
        in_specs=[pl.BlockSpec((tm, k), lambda j, i: (i, 0)),
                  pl.BlockSpec((None, k, tn), lambda j, i: (layer, 0, j)),
                  pl.BlockSpec((tm, tn), lambda j, i: (i, j)),
                  pl.BlockSpec((2, tn), lambda j, i: (0, gate_blk * nt + j))],
        out_specs=pl.BlockSpec((tm, tn), lambda j, i: (i, j)),
        out_shape=jax.ShapeDtypeStruct((rows, n), F32),
        compiler_params=_cparams(2),
        name="matmul_residual",
    )(a, w, x, mods)


def _block_in_use(eid_ref):
    return pl.program_id(1) < eid_ref[pl.num_programs(1)]


def _glu_kernel(eid_ref, a_ref, wg_ref, wu_ref, o_ref):
    @pl.when(_block_in_use(eid_ref))
    def _():
        a = _bf16(a_ref[...])
        gate = _dot(a, _bf16(wg_ref[0]))
        up = _dot(a, _bf16(wu_ref[0]))
        o_ref[...] = (gate * jax.nn.sigmoid(gate) * up).astype(o_ref.dtype)

    @pl.when(jnp.logical_not(_block_in_use(eid_ref)))
    def _():
        o_ref[...] = jnp.zeros_like(o_ref)


def _glu(a, w_gate, w_up, eid, tm):
    r, k = a.shape
    f = w_gate.shape[2]
    tn = _tile(f, 512, 1408, 256, LANES)
    grid_spec = pltpu.PrefetchScalarGridSpec(
        num_scalar_prefetch=1,
        grid=(f // tn, r // tm),
        in_specs=[pl.BlockSpec((tm, k), lambda j, i, e: (i, 0)),
                  pl.BlockSpec((1, k, tn), lambda j, i, e: (e[i], 0, j)),
                  pl.BlockSpec((1, k, tn), lambda j, i, e: (e[i], 0, j))],
        out_specs=pl.BlockSpec((tm, tn), lambda j, i, e: (i, j)),
    )
    return pl.pallas_call(
        _glu_kernel,
        grid_spec=grid_spec,
        out_shape=jax.ShapeDtypeStruct((r, f), BF16),
        compiler_params=_cparams(2),
        name="glu",
    )(eid, a, w_gate, w_up)


def _down_kernel(eid_ref, a_ref, w_ref, o_ref):
    @pl.when(_block_in_use(eid_ref))
    def _():
        o_ref[...] = _dot(a_ref[...], _bf16(w_ref[0]))

    @pl.when(jnp.logical_not(_block_in_use(eid_ref)))
    def _():
        o_ref[...] = jnp.zeros_like(o_ref)


def _down_grouped(a, w_down, eid, tm):
    r, f = a.shape
    d = w_down.shape[2]
    tn = _tile(d, 1024, 512, 256, LANES)
    grid_spec = pltpu.PrefetchScalarGridSpec(
        num_scalar_prefetch=1,
        grid=(d // tn, r // tm),
        in_specs=[pl.BlockSpec((tm, f), lambda j, i, e: (i, 0)),
                  pl.BlockSpec((1, f, tn), lambda j, i, e: (e[i], 0, j))],
        out_specs=pl.BlockSpec((tm, tn), lambda j, i, e: (i, j)),
    )
    return pl.pallas_call(
        _down_kernel,
        grid_spec=grid_spec,
        out_shape=jax.ShapeDtypeStruct((r, d), F32),
        compiler_params=_cparams(2),
        name="down_grouped",
    )(eid, a, w_down)


def _row_copies(idx_ref, idx_base, idx_stride, src_ref, dst_ref, sem, n_rows, start):
    if not start:
        pltpu.make_async_copy(src_ref.at[pl.ds(0, n_rows)], dst_ref, sem).wait()
        return

    def body(r2, c):
        for prio in range(2):
            r = 2 * r2 + prio
            row = idx_ref[idx_base + r * idx_stride]
            pltpu.make_async_copy(src_ref.at[pl.ds(row, 1)], dst_ref.at[pl.ds(r, 1)], sem).start(priority=prio)
        return c

    assert n_rows % 2 == 0 and dst_ref.shape[0] == n_rows
    lax.fori_loop(0, n_rows // 2, body, 0, unroll=DMA_ISSUE_UNROLL // 2)


def _prefetched_gather(copies):
    i = pl.program_id(0)
    slot = i % 2

    @pl.when(i == 0)
    def _():
        copies(0, 0, True)

    @pl.when(i + 1 < pl.num_programs(0))
    def _():
        copies(i + 1, 1 - slot, True)

    copies(i, slot, False)
    return slot


def _gather_rows_kernel(idx_ref, src_ref, o_ref, g_ref, sem):
    tm = o_ref.shape[0]

    def copies(step, slot, start):
        _row_copies(idx_ref, step * tm, 1, src_ref, g_ref.at[slot], sem.at[slot], tm, start)

    slot = _prefetched_gather(copies)
    o_ref[...] = g_ref[slot].astype(o_ref.dtype)


def _gather_rows(src, idx, tm, out_dtype):
    n = idx.shape[0]
    w = src.shape[1]
    grid_spec = pltpu.PrefetchScalarGridSpec(
        num_scalar_prefetch=1,
        grid=(n // tm,),
        in_specs=[pl.BlockSpec(memory_space=pl.ANY)],
        out_specs=pl.BlockSpec((tm, w), lambda i, idx_ref: (i, 0)),
        scratch_shapes=[pltpu.VMEM((2, tm, w), src.dtype), pltpu.SemaphoreType.DMA((2,))],
    )
    return pl.pallas_call(
        _gather_rows_kernel,
        grid_spec=grid_spec,
        out_shape=jax.ShapeDtypeStruct((n, w), out_dtype),
        compiler_params=_cparams(1),
        name="gather_rows",
    )(idx, src)


def _final_kernel(dest_ref, x_ref, y_hbm_ref, gate_ref, gt_ref, g_ref, o_ref, y_ref, sem):
    tm = x_ref.shape[0]

    def copies(step, slot, start):
        for kk in range(TOP_K):
            _row_copies(dest_ref, step * tm * TOP_K + kk, TOP_K, y_hbm_ref, y_ref.at[slot, kk], sem.at[slot],
                        tm, start)

    slot = _prefetched_gather(copies)
    gates = gate_ref[...]
    y = gates[:, 0:1] * y_ref[slot, 0] + gates[:, 1:2] * y_ref[slot, 1]
    x = x_ref[...] + gt_ref[0:1, :] * y
    o_ref[...] = x * lax.rsqrt(jnp.mean(x * x, axis=-1, keepdims=True) + NORM_EPS) * g_ref[...]


def _combine_final(x, ybuf, dest, gates, mods, gate_blk, g_final):
    r, d = x.shape
    tm = _tile(r, 256, 128, 64, SUBLANES)
    grid_spec = pltpu.PrefetchScalarGridSpec(
        num_scalar_prefetch=1,
        grid=(r // tm,),
        in_specs=[pl.BlockSpec((tm, d), lambda i, dest_ref: (i, 0)),
                  pl.BlockSpec(memory_space=pl.ANY),
                  pl.BlockSpec((tm, LANES), lambda i, dest_ref: (i, 0)),
                  pl.BlockSpec((2, d), lambda i, dest_ref: (0, gate_blk)),
                  pl.BlockSpec((1, d), lambda i, dest_ref: (0, 0))],
        out_specs=pl.BlockSpec((tm, d), lambda i, dest_ref: (i, 0)),
        scratch_shapes=[pltpu.VMEM((2, TOP_K, tm, d), F32), pltpu.SemaphoreType.DMA((2,))],
    )
    return pl.pallas_call(
        _final_kernel,
        grid_spec=grid_spec,
        out_shape=jax.ShapeDtypeStruct((r, d), F32),
        compiler_params=_cparams(1),
        name="combine_final",
    )(dest, x, ybuf, gates, mods, g_final.reshape(1, d))


def _rope_tables(s_len, c_len):
    quarter = QK_ROPE_DIM // 4
    tok = jnp.arange(s_len + c_len, dtype=jnp.int32)[:, None]
    lane = jnp.arange(LANES, dtype=jnp.int32)[None, :]
    group = lane // quarter
    pos = jnp.where(group < 2, tok // GRID_W, tok % GRID_W).astype(F32)
    inv_freq = jnp.power(ROPE_THETA, -(2 * (lane % quarter)).astype(F32) / (QK_ROPE_DIM // 2))
    ang = jnp.where(tok < s_len, pos * inv_freq, 0.0)
    live = group < 4
    cos = jnp.where(live, jnp.cos(ang), 0.0)
    sin = jnp.where(live, jnp.where(group % 2 == 0, -jnp.sin(ang), jnp.sin(ang)), 0.0)
    return cos, sin


def _rope_swap_perm():
    q = QK_ROPE_DIM // 4
    return jnp.concatenate([jnp.arange(q, 2 * q), jnp.arange(0, q), jnp.arange(3 * q, 4 * q), jnp.arange(2 * q, 3 * q)])


def _layer_weights(w_in_t, w_qb_all, li):
    d = w_in_t.shape[2]
    perm = _rope_swap_perm()
    kr_lo = Q_LORA_RANK + KV_LORA_RANK
    kr_hi = kr_lo + QK_ROPE_DIM
    w_kr = w_in_t[li, kr_lo:kr_hi]
    zpad = jnp.zeros((LANES - QK_ROPE_DIM, d), w_in_t.dtype)
    w_a = jnp.concatenate([w_kr, zpad, w_kr[perm], zpad], axis=0)
    w_qb = w_qb_all[li]
    qb = w_qb.reshape(Q_LORA_RANK, N_HEADS, QK_NOPE_DIM + QK_ROPE_DIM)
    q_rope = qb[:, :, QK_NOPE_DIM:]
    assert 2 * QK_ROPE_DIM == LANES
    w_q = jnp.concatenate([qb, q_rope[:, :, perm]], axis=2).reshape(Q_LORA_RANK, N_HEADS * HEAD_W)
    return w_a.astype(BF16), w_q.astype(BF16)


def _moe_slots(idx, n_blocks):
    e_flat = idx.reshape(-1)
    onehot = (e_flat[:, None] == jnp.arange(N_EXPERTS)[None, :]).astype(jnp.int32)
    csum = jnp.cumsum(onehot, axis=0)
    counts = csum[-1]
    rank = jnp.sum((csum - onehot) * onehot, axis=1)
    padded = (counts + MOE_ROWS - 1) // MOE_ROWS * MOE_ROWS
    p_end = jnp.cumsum(padded)
    p_start = p_end - padded
    dest = jnp.sum(onehot * p_start[None, :], axis=1) + rank
    blk_lo = jnp.arange(n_blocks) * MOE_ROWS
    block_expert = jnp.minimum(jnp.sum((blk_lo[:, None] >= p_end[None, :]).astype(jnp.int32), axis=1),
                               N_EXPERTS - 1)
    blocks_in_use = p_end[-1:] // MOE_ROWS
    return dest.astype(jnp.int32), jnp.concatenate([block_expert, blocks_in_use]).astype(jnp.int32)


def kernel(x, c, ctx, c_ctx, w_ada, b_ada, g_attn, w_in, g_qa, w_qb, g_kva, w_kvb, w_conv, w_oa, w_ob, w_o,
           g_ffn, w_gate_dense, w_up_dense, w_down_dense, w_router, w_gate_exp, w_up_exp, w_down_exp, g_final):
    _, s_len, d = x.shape
    c_len = ctx.shape[1]
    m_len = s_len + c_len
    depth = w_in.shape[0]
    width = w_conv.shape[2]
    assert depth == 2, "supported stack: dense-FFN layer with context updates, then a final expert-FFN layer"

    mods_all = _ada_mod(c, c_ctx, w_ada, b_ada)
    cos_t, sin_t = _rope_tables(s_len, c_len)
    w_in_t = jnp.swapaxes(w_in, 1, 2)
    w_oa_bf, w_ob_bf, w_kvb_bf = w_oa.astype(BF16), w_ob.astype(BF16), w_kvb.astype(BF16)

    for li in range(depth):
        last = li == depth - 1
        mods = mods_all[li]
        w_a, w_q = _layer_weights(w_in_t, w_qb, li)
        g_a = jnp.concatenate([g_qa[li], g_kva[li]]).reshape(1, -1)
        rows = s_len if last else m_len

        if li == 0:
            h, xa = _norm_modulate_join(x[0], ctx[0], g_attn[li], mods, 0, 1, BF16)
        else:
            h = _norm_modulate(xa, g_attn[li], mods, 0, 1, s_len, BF16)
        qa_n, ckv_n, kr = _in_proj_a(h, w_in_t, li, w_a, g_a, cos_t, sin_t)
        p = _in_proj_b(h, w_in_t, li, Q_LORA_RANK + KV_LORA_RANK + QK_ROPE_DIM, 3 * width)
        q = _q_proj(qa_n, w_q, cos_t, sin_t, rows)
        k, v = _kv_proj(ckv_n, kr, w_kvb_bf, li)
        attn = _attention(q, k, v, 0, s_len, 0, m_len)
        if not last:
            attn = jnp.concatenate([attn, _attention(q, k, v, s_len, c_len, s_len, c_len)], axis=0)
        z = _conv_gate(p, w_conv[li], rows, s_len, m_len)
        merged = _merge(attn, z, p, w_oa_bf, w_ob_bf, li, rows)
        xa = _matmul_residual(merged, w_o, li, xa, mods, 2, rows, s_len)

        j = li // 2
        if li % 2 == 0:
            h2 = _norm_modulate(xa, g_ffn[li], mods, 3, 4, s_len, BF16)
            tm = _tile(rows, 1408, 768, 640, 512, 256, 128)
            eid = jnp.zeros((rows // tm + 1,), jnp.int32).at[-1].set(rows // tm)
            hid = _glu(h2, w_gate_dense[j][None], w_up_dense[j][None], eid, tm)
            xa = _matmul_residual(hid, w_down_dense.astype(BF16), j, xa, mods, 5, rows, s_len)
        else:
            h2, idx, gates = _norm_modulate_route(xa, g_ffn[li], mods, 3, 4, w_router[j])
            n_assign = rows * TOP_K
            n_blocks = -(-n_assign // MOE_ROWS) + N_EXPERTS
            dest, block_expert = _moe_slots(idx[:, :TOP_K], n_blocks)
            tok = jnp.arange(n_assign, dtype=jnp.int32) // TOP_K
            slot_tok = jnp.zeros((n_blocks * MOE_ROWS,), jnp.int32).at[dest].set(tok, unique_indices=True)
            buf = _gather_rows(h2, slot_tok, MOE_ROWS, BF16)
            hid = _glu(buf, w_gate_exp[j], w_up_exp[j], block_expert, MOE_ROWS)
            ybuf = _down_grouped(hid, w_down_exp[j], block_expert, MOE_ROWS)
            return _combine_final(xa, ybuf, dest, gates, mods, 5, g_final)[None]
    raise AssertionError("unreachable: the final layer returns")
```

```python
import functools

import jax
import jax.numpy as jnp
from jax import lax
from jax.experimental import pallas as pl
from jax.experimental.pallas import tpu as pltpu

F32 = jnp.float32
BF16 = jnp.bfloat16

N_HEADS = 16
QK_NOPE_DIM = 128
QK_ROPE_DIM = 64
V_HEAD_DIM = 128
Q_LORA_RANK = 512
KV_LORA_RANK = 512
GRID_W = 64
ROPE_THETA = 10000.0
ATTN_SCALE = (QK_NOPE_DIM + QK_ROPE_DIM) ** -0.5
Q_SCALE = ATTN_SCALE * 1.4426950408889634
CONV_K = 3
N_EXPERTS = 8
TOP_K = 2
NORM_EPS = 1e-6

LANES = 128
SUBLANES = 8
HEAD_W = 2 * LANES
VMEM_LIMIT_BYTES = 56 * 1024 * 1024

MOE_ROWS = 512
GATHER_ROWS = 256
DMA_ISSUE_UNROLL = 8


def _tile(n, *cands):
    for c in cands:
        if n % c == 0:
            return c
    return n


def _cparams(n_axes):
    return pltpu.CompilerParams(dimension_semantics=("arbitrary",) * n_axes,
                                vmem_limit_bytes=VMEM_LIMIT_BYTES)


def _dot(a, b):
    return jnp.dot(a, b, preferred_element_type=F32)


def _dot_t(a, bt):
    return lax.dot_general(a, bt, (((1,), (1,)), ((), ())), preferred_element_type=F32)


def _bf16(w):
    return w if w.dtype == BF16 else w.astype(BF16)


def _ada_kernel(xt_ref, w_ref, b_ref, o_ref, *, k_chunk):
    d = xt_ref.shape[0]
    tn = o_ref.shape[-1]

    def body(k, acc):
        a0, a1 = acc
        ks = pl.multiple_of(k * k_chunk, k_chunk)
        xt = xt_ref[pl.ds(ks, k_chunk), :]
        s = xt * jax.nn.sigmoid(xt)
        w = w_ref[0, pl.ds(ks, k_chunk), :]
        a0 = a0 + jnp.sum(w * s[:, 0:1], axis=0, keepdims=True)
        a1 = a1 + jnp.sum(w * s[:, 1:2], axis=0, keepdims=True)
        return a0, a1

    z = jnp.zeros((1, tn), F32)
    a0, a1 = lax.fori_loop(0, d // k_chunk, body, (z, z))
    o_ref[0, 0:1, :] = a0 + b_ref[0]
    o_ref[0, 1:2, :] = a1 + b_ref[0]


def _ada_mod(c, c_ctx, w_ada, b_ada):
    depth, d, n = w_ada.shape
    xt = jnp.stack([c[0], c_ctx], axis=1)
    tn = _tile(n, 1024, 512, LANES)
    k_chunk = _tile(d, 256, SUBLANES)
    return pl.pallas_call(
        functools.partial(_ada_kernel, k_chunk=k_chunk),
        grid=(depth, n // tn),
        in_specs=[pl.BlockSpec((d, 2), lambda l, j: (0, 0)),
                  pl.BlockSpec((1, d, tn), lambda l, j: (l, 0, j)),
                  pl.BlockSpec((1, 1, tn), lambda l, j: (l, 0, j))],
        out_specs=pl.BlockSpec((1, 2, tn), lambda l, j: (l, 0, j)),
        out_shape=jax.ShapeDtypeStruct((depth, 2, n), F32),
        compiler_params=_cparams(2),
        name="ada_mod",
    )(xt, w_ada, b_ada.reshape(depth, 1, n))


def _norm_mod(x, g_ref, sh_ref, sc_ref, is_ctx):
    y = x * lax.rsqrt(jnp.mean(x * x, axis=-1, keepdims=True) + NORM_EPS) * g_ref[...]
    sh = jnp.where(is_ctx, sh_ref[1:2, :], sh_ref[0:1, :])
    sc = jnp.where(is_ctx, sc_ref[1:2, :], sc_ref[0:1, :])
    return y * (1.0 + sc) + sh


def _norm_kernel(x_ref, g_ref, sh_ref, sc_ref, o_ref, *, n_lat_tiles):
    is_ctx = pl.program_id(0) >= n_lat_tiles
    o_ref[...] = _norm_mod(x_ref[...], g_ref, sh_ref, sc_ref, is_ctx).astype(o_ref.dtype)


def _norm_router_kernel(x_ref, g_ref, sh_ref, sc_ref, wr_ref, h_ref, idx_ref, gate_ref):
    h = _norm_mod(x_ref[...], g_ref, sh_ref, sc_ref, False)
    h_ref[...] = h
    logits = jnp.dot(h, wr_ref[...], preferred_element_type=F32, precision=lax.Precision.HIGHEST)
    lane = lax.broadcasted_iota(jnp.int32, logits.shape, 1).astype(F32)
    neg = jnp.float32(-jnp.inf)
    l1 = jnp.where(lane < N_EXPERTS, logits, neg)
    v1 = jnp.max(l1, axis=-1, keepdims=True)
    i1 = jnp.min(jnp.where(l1 == v1, lane, float(LANES)), axis=-1, keepdims=True)
    l2 = jnp.where(lane == i1, neg, l1)
    v2 = jnp.max(l2, axis=-1, keepdims=True)
    i2 = jnp.min(jnp.where(l2 == v2, lane, float(LANES)), axis=-1, keepdims=True)
    e = jnp.exp(v2 - v1)
    g1 = 1.0 / (1.0 + e)
    g2 = e / (1.0 + e)
    idx_ref[...] = jnp.where(lane == 0, i1, jnp.where(lane == 1, i2, 0.0)).astype(jnp.int32)
    gate_ref[...] = jnp.where(lane == 0, g1, jnp.where(lane == 1, g2, 0.0))


def _norm_join_kernel(x_ref, c_ref, g_ref, sh_ref, sc_ref, o_ref, xa_ref, *, n_lat_tiles):
    is_ctx = pl.program_id(0) >= n_lat_tiles
    x = jnp.where(is_ctx, c_ref[...], x_ref[...])
    xa_ref[...] = x
    o_ref[...] = _norm_mod(x, g_ref, sh_ref, sc_ref, is_ctx).astype(o_ref.dtype)


def _norm_modulate_join(x, ctx, g, mods, sh_blk, sc_blk, out_dtype):
    n_lat, d = x.shape
    n_ctx = ctx.shape[0]
    tm = _tile(n_ctx, 256, 128, 64, SUBLANES)
    assert n_lat % tm == 0
    nl = n_lat // tm
    return pl.pallas_call(
        functools.partial(_norm_join_kernel, n_lat_tiles=nl),
        grid=((n_lat + n_ctx) // tm,),
        in_specs=[pl.BlockSpec((tm, d), lambda i: (jnp.minimum(i, nl - 1), 0)),
                  pl.BlockSpec((tm, d), lambda i: (jnp.maximum(i - nl, 0), 0)),
                  pl.BlockSpec((1, d), lambda i: (0, 0)),
                  pl.BlockSpec((2, d), lambda i: (0, sh_blk)),
                  pl.BlockSpec((2, d), lambda i: (0, sc_blk))],
        out_specs=[pl.BlockSpec((tm, d), lambda i: (i, 0)),
                   pl.BlockSpec((tm, d), lambda i: (i, 0))],
        out_shape=[jax.ShapeDtypeStruct((n_lat + n_ctx, d), out_dtype),
                   jax.ShapeDtypeStruct((n_lat + n_ctx, d), F32)],
        compiler_params=_cparams(1),
        name="norm_modulate_join",
    )(x, ctx, g.reshape(1, d), mods, mods)


def _norm_modulate(x, g, mods, sh_blk, sc_blk, n_lat, out_dtype):
    r, d = x.shape
    tm = _tile(n_lat, 256, 128, 64, SUBLANES) if r > n_lat else _tile(r, 256, 128, 64, SUBLANES)
    if r > n_lat:
        tm = _tile(r - n_lat, tm, 128, 64, SUBLANES)
    return pl.pallas_call(
        functools.partial(_norm_kernel, n_lat_tiles=n_lat // tm),
        grid=(r // tm,),
        in_specs=[pl.BlockSpec((tm, d), lambda i: (i, 0)),
                  pl.BlockSpec((1, d), lambda i: (0, 0)),
                  pl.BlockSpec((2, d), lambda i: (0, sh_blk)),
                  pl.BlockSpec((2, d), lambda i: (0, sc_blk))],
        out_specs=pl.BlockSpec((tm, d), lambda i: (i, 0)),
        out_shape=jax.ShapeDtypeStruct((r, d), out_dtype),
        compiler_params=_cparams(1),
        name="norm_modulate",
    )(x, g.reshape(1, d), mods, mods)


def _norm_modulate_route(x, g, mods, sh_blk, sc_blk, w_router):
    r, d = x.shape
    tm = _tile(r, 512, 256, 128, 64, SUBLANES)
    wr = jnp.pad(w_router, ((0, 0), (0, LANES - N_EXPERTS)))
    return pl.pallas_call(
        _norm_router_kernel,
        grid=(r // tm,),
        in_specs=[pl.BlockSpec((tm, d), lambda i: (i, 0)),
                  pl.BlockSpec((1, d), lambda i: (0, 0)),
                  pl.BlockSpec((2, d), lambda i: (0, sh_blk)),
                  pl.BlockSpec((2, d), lambda i: (0, sc_blk)),
                  pl.BlockSpec((d, LANES), lambda i: (0, 0))],
        out_specs=[pl.BlockSpec((tm, d), lambda i: (i, 0)),
                   pl.BlockSpec((tm, LANES), lambda i: (i, 0)),
                   pl.BlockSpec((tm, LANES), lambda i: (i, 0))],
        out_shape=[jax.ShapeDtypeStruct((r, d), F32),
                   jax.ShapeDtypeStruct((r, LANES), jnp.int32),
                   jax.ShapeDtypeStruct((r, LANES), F32)],
        compiler_params=_cparams(1),
        name="norm_modulate_route",
    )(x, g.reshape(1, d), mods, mods, wr)


def _in_a_kernel(h_ref, wt_ref, wkr_ref, g_ref, cos_ref, sin_ref, qa_ref, ckv_ref, kr_ref):
    h = h_ref[...]
    acc = _dot_t(h, _bf16(wt_ref[...]))
    acc_kr = _dot_t(h, wkr_ref[...])

    def rms(v, g):
        return v * lax.rsqrt(jnp.mean(v * v, axis=-1, keepdims=True) + NORM_EPS) * g

    q1 = Q_LORA_RANK
    c1 = q1 + KV_LORA_RANK
    qa_ref[...] = rms(acc[:, :q1], g_ref[:, :q1]).astype(qa_ref.dtype)
    ckv_ref[...] = rms(acc[:, q1:c1], g_ref[:, q1:c1]).astype(ckv_ref.dtype)
    kr = acc_kr[:, :LANES] * cos_ref[...] + acc_kr[:, LANES:] * sin_ref[...]
    kr_ref[...] = kr.astype(kr_ref.dtype)


def _in_proj_a(h, w_in_t, li, w_kr, g_a, cos_t, sin_t):
    m, d = h.shape
    n = Q_LORA_RANK + KV_LORA_RANK
    base = li * w_in_t.shape[1]
    assert base % SUBLANES == 0
    tm = _tile(m, 768, 640, 512, 256, 128)
    return pl.pallas_call(
        _in_a_kernel,
        grid=(m // tm,),
        in_specs=[pl.BlockSpec((tm, d), lambda i: (i, 0)),
                  pl.BlockSpec((pl.Element(n), pl.Element(d)), lambda i: (base, 0)),
                  pl.BlockSpec(w_kr.shape, lambda i: (0, 0)),
                  pl.BlockSpec((1, n), lambda i: (0, 0)),
                  pl.BlockSpec((tm, LANES), lambda i: (i, 0)),
                  pl.BlockSpec((tm, LANES), lambda i: (i, 0))],
        out_specs=[pl.BlockSpec((tm, Q_LORA_RANK), lambda i: (i, 0)),
                   pl.BlockSpec((tm, KV_LORA_RANK), lambda i: (i, 0)),
                   pl.BlockSpec((tm, LANES), lambda i: (i, 0))],
        out_shape=[jax.ShapeDtypeStruct((m, Q_LORA_RANK), BF16),
                   jax.ShapeDtypeStruct((m, KV_LORA_RANK), BF16),
                   jax.ShapeDtypeStruct((m, LANES), BF16)],
        compiler_params=_cparams(1),
        name="in_proj_a",
    )(h, w_in_t.reshape(-1, d), w_kr, g_a, cos_t, sin_t)


def _in_b_kernel(h_ref, wt_ref, o_ref, *, first_gate_tile):
    acc = _dot_t(h_ref[...], _bf16(wt_ref[...]))
    is_gate = pl.program_id(0) >= first_gate_tile

    @pl.when(is_gate)
    def _():
        o_ref[...] = jax.nn.sigmoid(acc).astype(o_ref.dtype)

    @pl.when(jnp.logical_not(is_gate))
    def _():
        o_ref[...] = acc.astype(o_ref.dtype)


def _in_proj_b(h, w_in_t, li, row0, first_gate_col):
    m, d = h.shape
    n = w_in_t.shape[1] - row0
    tm = _tile(m, 1408, 768, 640, 512, 256, 128)
    tn = _tile(first_gate_col, 1024, 512, 256, LANES)
    base = li * w_in_t.shape[1] + row0
    assert base % SUBLANES == 0 and n % tn == 0
    return pl.pallas_call(
        functools.partial(_in_b_kernel, first_gate_tile=first_gate_col // tn),
        grid=(n // tn, m // tm),
        in_specs=[pl.BlockSpec((tm, d), lambda j, i: (i, 0)),
                  pl.BlockSpec((pl.Element(tn), pl.Element(d)),
                               lambda j, i: (pl.multiple_of(base + j * tn, SUBLANES), 0))],
        out_specs=pl.BlockSpec((tm, tn), lambda j, i: (i, j)),
        out_shape=jax.ShapeDtypeStruct((m, n), BF16),
        compiler_params=_cparams(2),
        name="in_proj_b",
    )(h, w_in_t.reshape(-1, d))


def _q_kernel(a_ref, w_ref, cos_ref, sin_ref, q_ref):
    a = a_ref[...]
    cos = cos_ref[...]
    sin = sin_ref[...]
    for hd in range(N_HEADS):
        acc = _dot(a, w_ref[:, hd * HEAD_W:(hd + 1) * HEAD_W])
        q_ref[:, hd * HEAD_W:hd * HEAD_W + LANES] = (acc[:, :LANES] * Q_SCALE).astype(q_ref.dtype)
        rope = acc[:, LANES:]
        swapped = pltpu.roll(rope, LANES - QK_ROPE_DIM, 1)
        rot = (rope * cos + swapped * sin) * Q_SCALE
        q_ref[:, hd * HEAD_W + LANES:(hd + 1) * HEAD_W] = rot.astype(q_ref.dtype)


def _q_proj(qa_n, w_q, cos_t, sin_t, rows):
    k = qa_n.shape[1]
    tm = _tile(rows, 768, 640, 512, 256, 128)
    return pl.pallas_call(
        _q_kernel,
        grid=(rows // tm,),
        in_specs=[pl.BlockSpec((tm, k), lambda i: (i, 0)),
                  pl.BlockSpec(w_q.shape, lambda i: (0, 0)),
                  pl.BlockSpec((tm, LANES), lambda i: (i, 0)),
                  pl.BlockSpec((tm, LANES), lambda i: (i, 0))],
        out_specs=pl.BlockSpec((tm, N_HEADS * HEAD_W), lambda i: (i, 0)),
        out_shape=jax.ShapeDtypeStruct((rows, N_HEADS * HEAD_W), BF16),
        compiler_params=_cparams(1),
        name="q_proj",
    )(qa_n, w_q, cos_t, sin_t)


def _kv_kernel(c_ref, kr_ref, wkv_ref, k_ref, v_ref):
    c = c_ref[...]
    kr = kr_ref[...]
    lane = lax.broadcasted_iota(jnp.int32, (c.shape[0], LANES), 1)
    ones_col = jnp.where(lane == 0, 1.0, 0.0).astype(v_ref.dtype)
    for hd in range(N_HEADS):
        kv = _dot(c, wkv_ref[:, hd * HEAD_W:(hd + 1) * HEAD_W])
        k_ref[:, hd * HEAD_W:hd * HEAD_W + LANES] = kv[:, :QK_NOPE_DIM].astype(k_ref.dtype)
        k_ref[:, hd * HEAD_W + LANES:(hd + 1) * HEAD_W] = kr
        v_ref[:, hd * HEAD_W:hd * HEAD_W + LANES] = kv[:, QK_NOPE_DIM:].astype(v_ref.dtype)
        v_ref[:, hd * HEAD_W + LANES:(hd + 1) * HEAD_W] = ones_col


def _kv_proj(ckv_n, kr, w_kvb, layer):
    m, k = ckv_n.shape
    tm = _tile(m, 768, 640, 512, 256, 128)
    assert QK_NOPE_DIM == LANES and V_HEAD_DIM == LANES
    return pl.pallas_call(
        _kv_kernel,
        grid=(m // tm,),
        in_specs=[pl.BlockSpec((tm, k), lambda i: (i, 0)),
                  pl.BlockSpec((tm, LANES), lambda i: (i, 0)),
                  pl.BlockSpec((None,) + w_kvb.shape[1:], lambda i: (layer, 0, 0))],
        out_specs=[pl.BlockSpec((tm, N_HEADS * HEAD_W), lambda i: (i, 0)),
                   pl.BlockSpec((tm, N_HEADS * HEAD_W), lambda i: (i, 0))],
        out_shape=[jax.ShapeDtypeStruct((m, N_HEADS * HEAD_W), BF16),
                   jax.ShapeDtypeStruct((m, N_HEADS * HEAD_W), BF16)],
        compiler_params=_cparams(1),
        name="kv_proj",
    )(ckv_n, kr, w_kvb)


def _attn_kernel(q_ref, k_ref, v_ref, o_ref, sa_ref, sb_ref, *, tk, n_chunks):
    q = q_ref[...]
    tq = q.shape[0]

    def scores(c):
        return _dot_t(q, k_ref[c * tk:(c + 1) * tk, :])

    def absorb(s_ref, c, carry):
        m, acc = carry
        s = s_ref[...]
        m_new = jnp.maximum(m, jnp.max(s, axis=-1, keepdims=True))
        p = jnp.exp2(s - m_new).astype(v_ref.dtype)
        acc = jnp.exp2(m - m_new) * acc + _dot(p, v_ref[c * tk:(c + 1) * tk, :])
        return m_new, acc

    s_refs = (sa_ref, sb_ref)
    carry = (jnp.full((tq, 1), -jnp.inf, F32), jnp.zeros((tq, HEAD_W), F32))
    sa_ref[...] = scores(0)
    for c in range(n_chunks):
        if c + 1 < n_chunks:
            s_refs[(c + 1) % 2][...] = scores(c + 1)
        carry = absorb(s_refs[c % 2], c, carry)
    _, acc = carry
    o_ref[...] = (acc[:, :V_HEAD_DIM] / acc[:, V_HEAD_DIM:V_HEAD_DIM + 1]).astype(o_ref.dtype)


def _attention(q, k, v, q_row0, n_q, k_row0, n_k):
    tq = _tile(n_q, 1024, 512, 256, 128)
    tk = _tile(n_k, 768, 640, 512, 384, 256, 128)
    assert q_row0 % tq == 0 and k_row0 % n_k == 0
    q_blk0 = q_row0 // tq
    k_blk = k_row0 // n_k
    return pl.pallas_call(
        functools.partial(_attn_kernel, tk=tk, n_chunks=n_k // tk),
        grid=(N_HEADS, n_q // tq),
        in_specs=[pl.BlockSpec((tq, HEAD_W), lambda h, i: (q_blk0 + i, h)),
                  pl.BlockSpec((n_k, HEAD_W), lambda h, i: (k_blk, h)),
                  pl.BlockSpec((n_k, HEAD_W), lambda h, i: (k_blk, h))],
        out_specs=pl.BlockSpec((tq, V_HEAD_DIM), lambda h, i: (i, h)),
        out_shape=jax.ShapeDtypeStruct((n_q, N_HEADS * V_HEAD_DIM), BF16),
        scratch_shapes=[pltpu.VMEM((tq, tk), F32), pltpu.VMEM((tq, tk), F32)],
        compiler_params=_cparams(2),
        name="attention",
    )(q, k, v)


def _conv_gate_kernel(cx_ref, cb_ref, cc_ref, cxp_ref, ccp_ref, cxn_ref, ccn_ref, wc_ref, z_ref, *,
                      seg_starts, seg_ends, col_chunk):
    tm, width = z_ref.shape
    loc = lax.broadcasted_iota(jnp.int32, (tm, 1), 0)
    row = loc + pl.program_id(0) * tm
    first = functools.reduce(jnp.logical_or, [row == r for r in seg_starts])
    last = functools.reduce(jnp.logical_or, [row == r for r in seg_ends])
    for c0 in range(0, width, col_chunk):
        cs = slice(c0, c0 + col_chunk)
        u = cx_ref[:, cs].astype(F32) * cc_ref[:, cs].astype(F32)
        u_halo_prev = (cxp_ref[SUBLANES - 1:SUBLANES, cs].astype(F32)
                       * ccp_ref[SUBLANES - 1:SUBLANES, cs].astype(F32))
        u_halo_next = cxn_ref[0:1, cs].astype(F32) * ccn_ref[0:1, cs].astype(F32)
        u_prev = jnp.where(loc == 0, u_halo_prev, pltpu.roll(u, 1, 0))
        u_prev = jnp.where(first, 0.0, u_prev)
        u_next = jnp.where(loc == tm - 1, u_halo_next, pltpu.roll(u, tm - 1, 0))
        u_next = jnp.where(last, 0.0, u_next)
        conv = wc_ref[0:1, cs] * u_prev + wc_ref[1:2, cs] * u + wc_ref[2:3, cs] * u_next
        z_ref[:, cs] = (cb_ref[:, cs].astype(F32) * conv).astype(z_ref.dtype)


def _conv_gate(p, w_conv, rows, s_len, m_len):
    width = w_conv.shape[1]
    tm = _tile(rows, 768, 512, 384, 256, 128, 64)
    hb = tm // SUBLANES
    last_hb = p.shape[0] // SUBLANES - 1

    def prev_map(col):
        return lambda i: (jnp.maximum(i * hb - 1, 0), col)

    def next_map(col):
        return lambda i: (jnp.minimum((i + 1) * hb, last_hb), col)

    kern = functools.partial(_conv_gate_kernel, seg_starts=(0, s_len), seg_ends=(s_len - 1, m_len - 1),
                             col_chunk=_tile(width, 512, LANES))
    return pl.pallas_call(
        kern,
        grid=(rows // tm,),
        in_specs=[pl.BlockSpec((tm, width), lambda i: (i, 0)),
                  pl.BlockSpec((tm, width), lambda i: (i, 1)),
                  pl.BlockSpec((tm, width), lambda i: (i, 2)),
                  pl.BlockSpec((SUBLANES, width), prev_map(0)),
                  pl.BlockSpec((SUBLANES, width), prev_map(2)),
                  pl.BlockSpec((SUBLANES, width), next_map(0)),
                  pl.BlockSpec((SUBLANES, width), next_map(2)),
                  pl.BlockSpec((CONV_K, width), lambda i: (0, 0))],
        out_specs=pl.BlockSpec((tm, width), lambda i: (i, 0)),
        out_shape=jax.ShapeDtypeStruct((rows, width), BF16),
        compiler_params=_cparams(1),
        name="conv_gate",
    )(p, p, p, p, p, p, p, w_conv)


def _merge_kernel(attn_ref, z_ref, sga_ref, sgb_ref, woa_ref, wob_ref, o_ref):
    o_a = _dot(attn_ref[...], woa_ref[...])
    o_b = _dot(z_ref[...], wob_ref[...])
    o_ref[...] = (sga_ref[...].astype(F32) * o_a + sgb_ref[...].astype(F32) * o_b).astype(o_ref.dtype)


def _merge(attn, z, p, w_oa, w_ob, layer, rows):
    attn_w = w_oa.shape[1]
    _, width, d = w_ob.shape
    tm = _tile(rows, 768, 512, 384, 256, 128, 64)
    tn = _tile(d, 1024, 512, 256, LANES)
    ga_blk = 3 * width // tn
    gb_blk = (3 * width + d) // tn
    return pl.pallas_call(
        _merge_kernel,
        grid=(d // tn, rows // tm),
        in_specs=[pl.BlockSpec((tm, attn_w), lambda j, i: (i, 0)),
                  pl.BlockSpec((tm, width), lambda j, i: (i, 0)),
                  pl.BlockSpec((tm, tn), lambda j, i: (i, ga_blk + j)),
                  pl.BlockSpec((tm, tn), lambda j, i: (i, gb_blk + j)),
                  pl.BlockSpec((None, attn_w, tn), lambda j, i: (layer, 0, j)),
                  pl.BlockSpec((None, width, tn), lambda j, i: (layer, 0, j))],
        out_specs=pl.BlockSpec((tm, tn), lambda j, i: (i, j)),
        out_shape=jax.ShapeDtypeStruct((rows, d), BF16),
        compiler_params=_cparams(2),
        name="merge",
    )(attn, z, p, p, w_oa, w_ob)


def _mm_res_kernel(a_ref, w_ref, x_ref, gt_ref, o_ref, *, n_lat):
    tm = a_ref.shape[0]
    row = lax.broadcasted_iota(jnp.int32, (tm, 1), 0) + pl.program_id(1) * tm
    gate = jnp.where(row >= n_lat, gt_ref[1:2, :], gt_ref[0:1, :])
    o_ref[...] = x_ref[...] + gate * _dot(a_ref[...], _bf16(w_ref[...]))


def _matmul_residual(a, w, layer, x, mods, gate_blk, rows, n_lat):
    _, k, n = w.shape
    tm = _tile(rows, 768, 640, 512, 256, 128)
    tn = _tile(n, 1024 if k <= n else 512, 512, 256, LANES)
    nt = n // tn
    return pl.pallas_call(
        functools.partial(_mm_res_kernel, n_lat=n_lat),
        grid=(nt, rows // tm),
        in_specs=[pl.BlockSpec((tm, k), lambda j, i: (i, 0)),
                  pl.BlockSpec((None, k, tn), lambda j, i: (layer, 0, j)),
                  pl.BlockSpec((tm, tn), lambda j, i: (i, j)),
                  pl.BlockSpec((2, tn), lambda j, i: (0, gate_blk * nt + j))],
        out_specs=pl.BlockSpec((tm, tn), lambda j, i: (i, j)),
        out_shape=jax.ShapeDtypeStruct((rows, n), F32),
        compiler_params=_cparams(2),
        name="matmul_residual",
    )(a, w, x, mods)


def _block_in_use(eid_ref):
    return pl.program_id(1) < eid_ref[pl.num_programs(1)]


def _glu_kernel(eid_ref, a_ref, wg_ref, wu_ref, o_ref):
    @pl.when(_block_in_use(eid_ref))
    def _():
        a = _bf16(a_ref[...])
        gate = _dot(a, _bf16(wg_ref[0]))
        up = _dot(a, _bf16(wu_ref[0]))
        o_ref[...] = (gate * jax.nn.sigmoid(gate) * up).astype(o_ref.dtype)

    @pl.when(jnp.logical_not(_block_in_use(eid_ref)))
    def _():
        o_ref[...] = jnp.zeros_like(o_ref)


def _glu(a, w_gate, w_up, eid, tm):
    r, k = a.shape
    f = w_gate.shape[2]
    tn = _tile(f, 512, 1408, 256, LANES)
    grid_spec = pltpu.PrefetchScalarGridSpec(
        num_scalar_prefetch=1,
        grid=(f // tn, r // tm),
        in_specs=[pl.BlockSpec((tm, k), lambda j, i, e: (i, 0)),
                  pl.BlockSpec((1, k, tn), lambda j, i, e: (e[i], 0, j)),
                  pl.BlockSpec((1, k, tn), lambda j, i, e: (e[i], 0, j))],
        out_specs=pl.BlockSpec((tm, tn), lambda j, i, e: (i, j)),
    )
    return pl.pallas_call(
        _glu_kernel,
        grid_spec=grid_spec,
        out_shape=jax.ShapeDtypeStruct((r, f), BF16),
        compiler_params=_cparams(2),
        name="glu",
    )(eid, a, w_gate, w_up)


def _down_kernel(eid_ref, a_ref, w_ref, o_ref):
    @pl.when(_block_in_use(eid_ref))
    def _():
        o_ref[...] = _dot(a_ref[...], _bf16(w_ref[0]))

    @pl.when(jnp.logical_not(_block_in_use(eid_ref)))
    def _():
        o_ref[...] = jnp.zeros_like(o_ref)


def _down_grouped(a, w_down, eid, tm):
    r, f = a.shape
    d = w_down.shape[2]
    tn = _tile(d, 1024, 512, 256, LANES)
    grid_spec = pltpu.PrefetchScalarGridSpec(
        num_scalar_prefetch=1,
        grid=(d // tn, r // tm),
        in_specs=[pl.BlockSpec((tm, f), lambda j, i, e: (i, 0)),
                  pl.BlockSpec((1, f, tn), lambda j, i, e: (e[i], 0, j))],
        out_specs=pl.BlockSpec((tm, tn), lambda j, i, e: (i, j)),
    )
    return pl.pallas_call(
        _down_kernel,
        grid_spec=grid_spec,
        out_shape=jax.ShapeDtypeStruct((r, d), F32),
        compiler_params=_cparams(2),
        name="down_grouped",
    )(eid, a, w_down)


def _row_copies(idx_ref, idx_base, idx_stride, src_ref, dst_ref, sem, n_rows, start):
    if not start:
        pltpu.make_async_copy(src_ref.at[pl.ds(0, n_rows)], dst_ref, sem).wait()
        return

    def body(r2, c):
        for prio in range(2):
            r = 2 * r2 + prio
            row = idx_ref[idx_base + r * idx_stride]
            pltpu.make_async_copy(src_ref.at[pl.ds(row, 1)], dst_ref.at[pl.ds(r, 1)], sem).start(priority=prio)
        return c

    assert n_rows % 2 == 0 and dst_ref.shape[0] == n_rows
    lax.fori_loop(0, n_rows // 2, body, 0, unroll=DMA_ISSUE_UNROLL // 2)


def _prefetched_gather(copies):
    i = pl.program_id(0)
    slot = i % 2

    @pl.when(i == 0)
    def _():
        copies(0, 0, True)

    @pl.when(i + 1 < pl.num_programs(0))
    def _():
        copies(i + 1, 1 - slot, True)

    copies(i, slot, False)
    return slot


def _gather_rows_kernel(idx_ref, src_ref, o_ref, g_ref, sem):
    tm = o_ref.shape[0]

    def copies(step, slot, start):
        _row_copies(idx_ref, step * tm, 1, src_ref, g_ref.at[slot], sem.at[slot], tm, start)

    slot = _prefetched_gather(copies)
    o_ref[...] = g_ref[slot].astype(o_ref.dtype)


def _gather_rows(src, idx, tm, out_dtype):
    n = idx.shape[0]
    w = src.shape[1]
    grid_spec = pltpu.PrefetchScalarGridSpec(
        num_scalar_prefetch=1,
        grid=(n // tm,),
        in_specs=[pl.BlockSpec(memory_space=pl.ANY)],
        out_specs=pl.BlockSpec((tm, w), lambda i, idx_ref: (i, 0)),
        scratch_shapes=[pltpu.VMEM((2, tm, w), src.dtype), pltpu.SemaphoreType.DMA((2,))],
    )
    return pl.pallas_call(
        _gather_rows_kernel,
        grid_spec=grid_spec,
        out_shape=jax.ShapeDtypeStruct((n, w), out_dtype),
        compiler_params=_cparams(1),
        name="gather_rows",
    )(idx, src)


def _final_kernel(dest_ref, x_ref, y_hbm_ref, gate_ref, gt_ref, g_ref, o_ref, y_ref, sem):
    tm = x_ref.shape[0]

    def copies(step, slot, start):
        for kk in range(TOP_K):
            _row_copies(dest_ref, step * tm * TOP_K + kk, TOP_K, y_hbm_ref, y_ref.at[slot, kk], sem.at[slot],
                        tm, start)

    slot = _prefetched_gather(copies)
    gates = gate_ref[...]
    y = gates[:, 0:1] * y_ref[slot, 0] + gates[:, 1:2] * y_ref[slot, 1]
    x = x_ref[...] + gt_ref[0:1, :] * y
    o_ref[...] = x * lax.rsqrt(jnp.mean(x * x, axis=-1, keepdims=True) + NORM_EPS) * g_ref[...]


def _combine_final(x, ybuf, dest, gates, mods, gate_blk, g_final):
    r, d = x.shape
    tm = _tile(r, 256, 128, 64, SUBLANES)
    grid_spec = pltpu.PrefetchScalarGridSpec(
        num_scalar_prefetch=1,
        grid=(r // tm,),
        in_specs=[pl.BlockSpec((tm, d), lambda i, dest_ref: (i, 0)),
                  pl.BlockSpec(memory_space=pl.ANY),
                  pl.BlockSpec((tm, LANES), lambda i, dest_ref: (i, 0)),
                  pl.BlockSpec((2, d), lambda i, dest_ref: (0, gate_blk)),
                  pl.BlockSpec((1, d), lambda i, dest_ref: (0, 0))],
        out_specs=pl.BlockSpec((tm, d), lambda i, dest_ref: (i, 0)),
        scratch_shapes=[pltpu.VMEM((2, TOP_K, tm, d), F32), pltpu.SemaphoreType.DMA((2,))],
    )
    return pl.pallas_call(
        _final_kernel,
        grid_spec=grid_spec,
        out_shape=jax.ShapeDtypeStruct((r, d), F32),
        compiler_params=_cparams(1),
        name="combine_final",
    )(dest, x, ybuf, gates, mods, g_final.reshape(1, d))


def _rope_tables(s_len, c_len):
    quarter = QK_ROPE_DIM // 4
    tok = jnp.arange(s_len + c_len, dtype=jnp.int32)[:, None]
    lane = jnp.arange(LANES, dtype=jnp.int32)[None, :]
    group = lane // quarter
    pos = jnp.where(group < 2, tok // GRID_W, tok % GRID_W).astype(F32)
    inv_freq = jnp.power(ROPE_THETA, -(2 * (lane % quarter)).astype(F32) / (QK_ROPE_DIM // 2))
    ang = jnp.where(tok < s_len, pos * inv_freq, 0.0)
    live = group < 4
    cos = jnp.where(live, jnp.cos(ang), 0.0)
    sin = jnp.where(live, jnp.where(group % 2 == 0, -jnp.sin(ang), jnp.sin(ang)), 0.0)
    return cos, sin


def _rope_swap_perm():
    q = QK_ROPE_DIM // 4
    return jnp.concatenate([jnp.arange(q, 2 * q), jnp.arange(0, q), jnp.arange(3 * q, 4 * q), jnp.arange(2 * q, 3 * q)])


def _layer_weights(w_in_t, w_qb_all, li):
    d = w_in_t.shape[2]
    perm = _rope_swap_perm()
    kr_lo = Q_LORA_RANK + KV_LORA_RANK
    kr_hi = kr_lo + QK_ROPE_DIM
    w_kr = w_in_t[li, kr_lo:kr_hi]
    zpad = jnp.zeros((LANES - QK_ROPE_DIM, d), w_in_t.dtype)
    w_a = jnp.concatenate([w_kr, zpad, w_kr[perm], zpad], axis=0)
    w_qb = w_qb_all[li]
    qb = w_qb.reshape(Q_LORA_RANK, N_HEADS, QK_NOPE_DIM + QK_ROPE_DIM)
    q_rope = qb[:, :, QK_NOPE_DIM:]
    assert 2 * QK_ROPE_DIM == LANES
    w_q = jnp.concatenate([qb, q_rope[:, :, perm]], axis=2).reshape(Q_LORA_RANK, N_HEADS * HEAD_W)
    return w_a.astype(BF16), w_q.astype(BF16)


def _moe_slots(idx, n_blocks):
    e_flat = idx.reshape(-1)
    onehot = (e_flat[:, None] == jnp.arange(N_EXPERTS)[None, :]).astype(jnp.int32)
    csum = jnp.cumsum(onehot, axis=0)
    counts = csum[-1]
    rank = jnp.sum((csum - onehot) * onehot, axis=1)
    padded = (counts + MOE_ROWS - 1) // MOE_ROWS * MOE_ROWS
    p_end = jnp.cumsum(padded)
    p_start = p_end - padded
    dest = jnp.sum(onehot * p_start[None, :], axis=1) + rank
    blk_lo = jnp.arange(n_blocks) * MOE_ROWS
    block_expert = jnp.minimum(jnp.sum((blk_lo[:, None] >= p_end[None, :]).astype(jnp.int32), axis=1),
                               N_EXPERTS - 1)
    blocks_in_use = p_end[-1:] // MOE_ROWS
    return dest.astype(jnp.int32), jnp.concatenate([block_expert, blocks_in_use]).astype(jnp.int32)


def kernel(x, c, ctx, c_ctx, w_ada, b_ada, g_attn, w_in, g_qa, w_qb, g_kva, w_kvb, w_conv, w_oa, w_ob, w_o,
           g_ffn, w_gate_dense, w_up_dense, w_down_dense, w_router, w_gate_exp, w_up_exp, w_down_exp, g_final):
    _, s_len, d = x.shape
    c_len = ctx.shape[1]
    m_len = s_len + c_len
    depth = w_in.shape[0]
    width = w_conv.shape[2]
    assert depth == 2, "supported stack: dense-FFN layer with context updates, then a final expert-FFN layer"

    mods_all = _ada_mod(c, c_ctx, w_ada, b_ada)
    cos_t, sin_t = _rope_tables(s_len, c_len)
    w_in_t = jnp.swapaxes(w_in, 1, 2)
    w_oa_bf, w_ob_bf, w_kvb_bf = w_oa.astype(BF16), w_ob.astype(BF16), w_kvb.astype(BF16)

    for li in range(depth):
        last = li == depth - 1
        mods = mods_all[li]
        w_a, w_q = _layer_weights(w_in_t, w_qb, li)
        g_a = jnp.concatenate([g_qa[li], g_kva[li]]).reshape(1, -1)
        rows = s_len if last else m_len

        if li == 0:
            h, xa = _norm_modulate_join(x[0], ctx[0], g_attn[li], mods, 0, 1, BF16)
        else:
            h = _norm_modulate(xa, g_attn[li], mods, 0, 1, s_len, BF16)
        qa_n, ckv_n, kr = _in_proj_a(h, w_in_t, li, w_a, g_a, cos_t, sin_t)
        p = _in_proj_b(h, w_in_t, li, Q_LORA_RANK + KV_LORA_RANK + QK_ROPE_DIM, 3 * width)
        q = _q_proj(qa_n, w_q, cos_t, sin_t, rows)
        k, v = _kv_proj(ckv_n, kr, w_kvb_bf, li)
        attn = _attention(q, k, v, 0, s_len, 0, m_len)
        if not last:
            attn = jnp.concatenate([attn, _attention(q, k, v, s_len, c_len, s_len, c_len)], axis=0)
        z = _conv_gate(p, w_conv[li], rows, s_len, m_len)
        merged = _merge(attn, z, p, w_oa_bf, w_ob_bf, li, rows)
        xa = _matmul_residual(merged, w_o, li, xa, mods, 2, rows, s_len)

        j = li // 2
        if li % 2 == 0:
            h2 = _norm_modulate(xa, g_ffn[li], mods, 3, 4, s_len, BF16)
            tm = _tile(rows, 1408, 768, 640, 512, 256, 128)
            eid = jnp.zeros((rows // tm + 1,), jnp.int32).at[-1].set(rows // tm)
            hid = _glu(h2, w_gate_dense[j][None], w_up_dense[j][None], eid, tm)
            xa = _matmul_residual(hid, w_down_dense.astype(BF16), j, xa, mods, 5, rows, s_len)
        else:
            h2, idx, gates = _norm_modulate_route(xa, g_ffn[li], mods, 3, 4, w_router[j])
            n_assign = rows * TOP_K
            n_blocks = -(-n_assign // MOE_ROWS) + N_EXPERTS
            dest, block_expert = _moe_slots(idx[:, :TOP_K], n_blocks)
            tok = jnp.arange(n_assign, dtype=jnp.int32) // TOP_K
            slot_tok = jnp.zeros((n_blocks * MOE_ROWS,), jnp.int32).at[dest].set(tok, unique_indices=True)
            buf = _gather_rows(h2, slot_tok, GATHER_ROWS, BF16)
            hid = _glu(buf, w_gate_exp[j], w_up_exp[j], block_expert, MOE_ROWS)
            ybuf = _down_grouped(hid, w_down_exp[j], block_expert, MOE_ROWS)
            return _combine_final(xa, ybuf, dest, gates, mods, 5, g_final)[None]
    raise AssertionError("unreachable: the final layer returns")
```

```python
import functools

import jax
import jax.numpy as jnp
from jax import lax
from jax.experimental import pallas as pl
from jax.experimental.pallas import tpu as pltpu

F32 = jnp.float32
BF16 = jnp.bfloat16

N_HEADS = 16
QK_NOPE_DIM = 128
QK_ROPE_DIM = 64
V_HEAD_DIM = 128
Q_LORA_RANK = 512
KV_LORA_RANK = 512
GRID_W = 64
ROPE_THETA = 10000.0
ATTN_SCALE = (QK_NOPE_DIM + QK_ROPE_DIM) ** -0.5
Q_SCALE = ATTN_SCALE * 1.4426950408889634
CONV_K = 3
N_EXPERTS = 8
TOP_K = 2
NORM_EPS = 1e-6

LANES = 128
SUBLANES = 8
HEAD_W = 2 * LANES
VMEM_LIMIT_BYTES = 56 * 1024 * 1024

MOE_ROWS = 512
GATHER_ROWS = 256
DMA_ISSUE_UNROLL = 8


def _tile(n, *cands):
    for c in cands:
        if n % c == 0:
            return c
    return n


def _cparams(n_axes):
    return pltpu.CompilerParams(dimension_semantics=("arbitrary",) * n_axes,
                                vmem_limit_bytes=VMEM_LIMIT_BYTES)


def _dot(a, b):
    return jnp.dot(a, b, preferred_element_type=F32)


def _dot_t(a, bt):
    return lax.dot_general(a, bt, (((1,), (1,)), ((), ())), preferred_element_type=F32)


def _bf16(w):
    return w if w.dtype == BF16 else w.astype(BF16)


def _ada_kernel(xt_ref, w_ref, b_ref, o_ref, *, k_chunk):
    d = xt_ref.shape[0]
    tn = o_ref.shape[-1]

    def body(k, acc):
        a0, a1 = acc
        ks = pl.multiple_of(k * k_chunk, k_chunk)
        xt = xt_ref[pl.ds(ks, k_chunk), :]
        s = xt * jax.nn.sigmoid(xt)
        w = w_ref[0, pl.ds(ks, k_chunk), :]
        a0 = a0 + jnp.sum(w * s[:, 0:1], axis=0, keepdims=True)
        a1 = a1 + jnp.sum(w * s[:, 1:2], axis=0, keepdims=True)
        return a0, a1

    z = jnp.zeros((1, tn), F32)
    a0, a1 = lax.fori_loop(0, d // k_chunk, body, (z, z))
    o_ref[0, 0:1, :] = a0 + b_ref[0]
    o_ref[0, 1:2, :] = a1 + b_ref[0]


def _ada_mod(c, c_ctx, w_ada, b_ada):
    depth, d, n = w_ada.shape
    xt = jnp.stack([c[0], c_ctx], axis=1)
    tn = _tile(n, 1024, 512, LANES)
    k_chunk = _tile(d, 256, SUBLANES)
    return pl.pallas_call(
        functools.partial(_ada_kernel, k_chunk=k_chunk),
        grid=(depth, n // tn),
        in_specs=[pl.BlockSpec((d, 2), lambda l, j: (0, 0)),
                  pl.BlockSpec((1, d, tn), lambda l, j: (l, 0, j)),
                  pl.BlockSpec((1, 1, tn), lambda l, j: (l, 0, j))],
        out_specs=pl.BlockSpec((1, 2, tn), lambda l, j: (l, 0, j)),
        out_shape=jax.ShapeDtypeStruct((depth, 2, n), F32),
        compiler_params=_cparams(2),
        name="ada_mod",
    )(xt, w_ada, b_ada.reshape(depth, 1, n))


def _norm_mod(x, g_ref, sh_ref, sc_ref, is_ctx):
    y = x * lax.rsqrt(jnp.mean(x * x, axis=-1, keepdims=True) + NORM_EPS) * g_ref[...]
    sh = jnp.where(is_ctx, sh_ref[1:2, :], sh_ref[0:1, :])
    sc = jnp.where(is_ctx, sc_ref[1:2, :], sc_ref[0:1, :])
    return y * (1.0 + sc) + sh


def _norm_kernel(x_ref, g_ref, sh_ref, sc_ref, o_ref, *, n_lat_tiles):
    is_ctx = pl.program_id(0) >= n_lat_tiles
    o_ref[...] = _norm_mod(x_ref[...], g_ref, sh_ref, sc_ref, is_ctx).astype(o_ref.dtype)


def _norm_router_kernel(x_ref, g_ref, sh_ref, sc_ref, wr_ref, h_ref, idx_ref, gate_ref):
    h = _norm_mod(x_ref[...], g_ref, sh_ref, sc_ref, False)
    h_ref[...] = h
    logits = jnp.dot(h, wr_ref[...], preferred_element_type=F32, precision=lax.Precision.HIGHEST)
    lane = lax.broadcasted_iota(jnp.int32, logits.shape, 1).astype(F32)
    neg = jnp.float32(-jnp.inf)
    l1 = jnp.where(lane < N_EXPERTS, logits, neg)
    v1 = jnp.max(l1, axis=-1, keepdims=True)
    i1 = jnp.min(jnp.where(l1 == v1, lane, float(LANES)), axis=-1, keepdims=True)
    l2 = jnp.where(lane == i1, neg, l1)
    v2 = jnp.max(l2, axis=-1, keepdims=True)
    i2 = jnp.min(jnp.where(l2 == v2, lane, float(LANES)), axis=-1, keepdims=True)
    e = jnp.exp(v2 - v1)
    g1 = 1.0 / (1.0 + e)
    g2 = e / (1.0 + e)
    idx_ref[...] = jnp.where(lane == 0, i1, jnp.where(lane == 1, i2, 0.0)).astype(jnp.int32)
    gate_ref[...] = jnp.where(lane == 0, g1, jnp.where(lane == 1, g2, 0.0))


def _norm_join_kernel(x_ref, c_ref, g_ref, sh_ref, sc_ref, o_ref, xa_ref, *, n_lat_tiles):
    is_ctx = pl.program_id(0) >= n_lat_tiles
    x = jnp.where(is_ctx, c_ref[...], x_ref[...])
    xa_ref[...] = x
    o_ref[...] = _norm_mod(x, g_ref, sh_ref, sc_ref, is_ctx).astype(o_ref.dtype)


def _norm_modulate_join(x, ctx, g, mods, sh_blk, sc_blk, out_dtype):
    n_lat, d = x.shape
    n_ctx = ctx.shape[0]
    tm = _tile(n_ctx, 256, 128, 64, SUBLANES)
    assert n_lat % tm == 0
    nl = n_lat // tm
    return pl.pallas_call(
        functools.partial(_norm_join_kernel, n_lat_tiles=nl),
        grid=((n_lat + n_ctx) // tm,),
        in_specs=[pl.BlockSpec((tm, d), lambda i: (jnp.minimum(i, nl - 1), 0)),
                  pl.BlockSpec((tm, d), lambda i: (jnp.maximum(i - nl, 0), 0)),
                  pl.BlockSpec((1, d), lambda i: (0, 0)),
                  pl.BlockSpec((2, d), lambda i: (0, sh_blk)),
                  pl.BlockSpec((2, d), lambda i: (0, sc_blk))],
        out_specs=[pl.BlockSpec((tm, d), lambda i: (i, 0)),
                   pl.BlockSpec((tm, d), lambda i: (i, 0))],
        out_shape=[jax.ShapeDtypeStruct((n_lat + n_ctx, d), out_dtype),
                   jax.ShapeDtypeStruct((n_lat + n_ctx, d), F32)],
        compiler_params=_cparams(1),
        name="norm_modulate_join",
    )(x, ctx, g.reshape(1, d), mods, mods)


def _norm_modulate(x, g, mods, sh_blk, sc_blk, n_lat, out_dtype):
    r, d = x.shape
    tm = _tile(n_lat, 256, 128, 64, SUBLANES) if r > n_lat else _tile(r, 256, 128, 64, SUBLANES)
    if r > n_lat:
        tm = _tile(r - n_lat, tm, 128, 64, SUBLANES)
    return pl.pallas_call(
        functools.partial(_norm_kernel, n_lat_tiles=n_lat // tm),
        grid=(r // tm,),
        in_specs=[pl.BlockSpec((tm, d), lambda i: (i, 0)),
                  pl.BlockSpec((1, d), lambda i: (0, 0)),
                  pl.BlockSpec((2, d), lambda i: (0, sh_blk)),
                  pl.BlockSpec((2, d), lambda i: (0, sc_blk))],
        out_specs=pl.BlockSpec((tm, d), lambda i: (i, 0)),
        out_shape=jax.ShapeDtypeStruct((r, d), out_dtype),
        compiler_params=_cparams(1),
        name="norm_modulate",
    )(x, g.reshape(1, d), mods, mods)


def _norm_modulate_route(x, g, mods, sh_blk, sc_blk, w_router):
    r, d = x.shape
    tm = _tile(r, 512, 256, 128, 64, SUBLANES)
    wr = jnp.pad(w_router, ((0, 0), (0, LANES - N_EXPERTS)))
    return pl.pallas_call(
        _norm_router_kernel,
        grid=(r // tm,),
        in_specs=[pl.BlockSpec((tm, d), lambda i: (i, 0)),
                  pl.BlockSpec((1, d), lambda i: (0, 0)),
                  pl.BlockSpec((2, d), lambda i: (0, sh_blk)),
                  pl.BlockSpec((2, d), lambda i: (0, sc_blk)),
                  pl.BlockSpec((d, LANES), lambda i: (0, 0))],
        out_specs=[pl.BlockSpec((tm, d), lambda i: (i, 0)),
                   pl.BlockSpec((tm, LANES), lambda i: (i, 0)),
                   pl.BlockSpec((tm, LANES), lambda i: (i, 0))],
        out_shape=[jax.ShapeDtypeStruct((r, d), F32),
                   jax.ShapeDtypeStruct((r, LANES), jnp.int32),
                   jax.ShapeDtypeStruct((r, LANES), F32)],
        compiler_params=_cparams(1),
        name="norm_modulate_route",
    )(x, g.reshape(1, d), mods, mods, wr)


def _in_a_kernel(h_ref, wt_ref, wkr_ref, g_ref, cos_ref, sin_ref, qa_ref, ckv_ref, kr_ref):
    h = h_ref[...]
    acc = _dot_t(h, _bf16(wt_ref[...]))
    acc_kr = _dot_t(h, wkr_ref[...])

    def rms(v, g):
        return v * lax.rsqrt(jnp.mean(v * v, axis=-1, keepdims=True) + NORM_EPS) * g

    q1 = Q_LORA_RANK
    c1 = q1 + KV_LORA_RANK
    qa_ref[...] = rms(acc[:, :q1], g_ref[:, :q1]).astype(qa_ref.dtype)
    ckv_ref[...] = rms(acc[:, q1:c1], g_ref[:, q1:c1]).astype(ckv_ref.dtype)
    kr = acc_kr[:, :LANES] * cos_ref[...] + acc_kr[:, LANES:] * sin_ref[...]
    kr_ref[...] = kr.astype(kr_ref.dtype)


def _in_proj_a(h, w_in_t, li, w_kr, g_a, cos_t, sin_t):
    m, d = h.shape
    n = Q_LORA_RANK + KV_LORA_RANK
    base = li * w_in_t.shape[1]
    assert base % SUBLANES == 0
    tm = _tile(m, 768, 640, 512, 256, 128)
    return pl.pallas_call(
        _in_a_kernel,
        grid=(m // tm,),
        in_specs=[pl.BlockSpec((tm, d), lambda i: (i, 0)),
                  pl.BlockSpec((pl.Element(n), pl.Element(d)), lambda i: (base, 0)),
                  pl.BlockSpec(w_kr.shape, lambda i: (0, 0)),
                  pl.BlockSpec((1, n), lambda i: (0, 0)),
                  pl.BlockSpec((tm, LANES), lambda i: (i, 0)),
                  pl.BlockSpec((tm, LANES), lambda i: (i, 0))],
        out_specs=[pl.BlockSpec((tm, Q_LORA_RANK), lambda i: (i, 0)),
                   pl.BlockSpec((tm, KV_LORA_RANK), lambda i: (i, 0)),
                   pl.BlockSpec((tm, LANES), lambda i: (i, 0))],
        out_shape=[jax.ShapeDtypeStruct((m, Q_LORA_RANK), BF16),
                   jax.ShapeDtypeStruct((m, KV_LORA_RANK), BF16),
                   jax.ShapeDtypeStruct((m, LANES), BF16)],
        compiler_params=_cparams(1),
        name="in_proj_a",
    )(h, w_in_t.reshape(-1, d), w_kr, g_a, cos_t, sin_t)


def _in_b_kernel(h_ref, wt_ref, o_ref, *, first_gate_tile):
    acc = _dot_t(h_ref[...], _bf16(wt_ref[...]))
    is_gate = pl.program_id(0) >= first_gate_tile

    @pl.when(is_gate)
    def _():
        o_ref[...] = jax.nn.sigmoid(acc).astype(o_ref.dtype)

    @pl.when(jnp.logical_not(is_gate))
    def _():
        o_ref[...] = acc.astype(o_ref.dtype)


def _in_proj_b(h, w_in_t, li, row0, first_gate_col):
    m, d = h.shape
    n = w_in_t.shape[1] - row0
    tm = _tile(m, 1408, 768, 640, 512, 256, 128)
    tn = _tile(first_gate_col, 1024, 512, 256, LANES)
    base = li * w_in_t.shape[1] + row0
    assert base % SUBLANES == 0 and n % tn == 0
    return pl.pallas_call(
        functools.partial(_in_b_kernel, first_gate_tile=first_gate_col // tn),
        grid=(n // tn, m // tm),
        in_specs=[pl.BlockSpec((tm, d), lambda j, i: (i, 0)),
                  pl.BlockSpec((pl.Element(tn), pl.Element(d)),
                               lambda j, i: (pl.multiple_of(base + j * tn, SUBLANES), 0))],
        out_specs=pl.BlockSpec((tm, tn), lambda j, i: (i, j)),
        out_shape=jax.ShapeDtypeStruct((m, n), BF16),
        compiler_params=_cparams(2),
        name="in_proj_b",
    )(h, w_in_t.reshape(-1, d))


def _q_kernel(a_ref, w_ref, cos_ref, sin_ref, q_ref):
    a = a_ref[...]
    cos = cos_ref[...]
    sin = sin_ref[...]
    for hd in range(N_HEADS):
        acc = _dot(a, w_ref[:, hd * HEAD_W:(hd + 1) * HEAD_W])
        q_ref[:, hd * HEAD_W:hd * HEAD_W + LANES] = (acc[:, :LANES] * Q_SCALE).astype(q_ref.dtype)
        rope = acc[:, LANES:]
        swapped = pltpu.roll(rope, LANES - QK_ROPE_DIM, 1)
        rot = (rope * cos + swapped * sin) * Q_SCALE
        q_ref[:, hd * HEAD_W + LANES:(hd + 1) * HEAD_W] = rot.astype(q_ref.dtype)


def _q_proj(qa_n, w_q, cos_t, sin_t, rows):
    k = qa_n.shape[1]
    tm = _tile(rows, 768, 640, 512, 256, 128)
    return pl.pallas_call(
        _q_kernel,
        grid=(rows // tm,),
        in_specs=[pl.BlockSpec((tm, k), lambda i: (i, 0)),
                  pl.BlockSpec(w_q.shape, lambda i: (0, 0)),
                  pl.BlockSpec((tm, LANES), lambda i: (i, 0)),
                  pl.BlockSpec((tm, LANES), lambda i: (i, 0))],
        out_specs=pl.BlockSpec((tm, N_HEADS * HEAD_W), lambda i: (i, 0)),
        out_shape=jax.ShapeDtypeStruct((rows, N_HEADS * HEAD_W), BF16),
        compiler_params=_cparams(1),
        name="q_proj",
    )(qa_n, w_q, cos_t, sin_t)


def _kv_kernel(c_ref, kr_ref, wkv_ref, k_ref, v_ref):
    c = c_ref[...]
    kr = kr_ref[...]
    lane = lax.broadcasted_iota(jnp.int32, (c.shape[0], LANES), 1)
    ones_col = jnp.where(lane == 0, 1.0, 0.0).astype(v_ref.dtype)
    for hd in range(N_HEADS):
        kv = _dot(c, wkv_ref[:, hd * HEAD_W:(hd + 1) * HEAD_W])
        k_ref[:, hd * HEAD_W:hd * HEAD_W + LANES] = kv[:, :QK_NOPE_DIM].astype(k_ref.dtype)
        k_ref[:, hd * HEAD_W + LANES:(hd + 1) * HEAD_W] = kr
        v_ref[:, hd * HEAD_W:hd * HEAD_W + LANES] = kv[:, QK_NOPE_DIM:].astype(v_ref.dtype)
        v_ref[:, hd * HEAD_W + LANES:(hd + 1) * HEAD_W] = ones_col


def _kv_proj(ckv_n, kr, w_kvb, layer):
    m, k = ckv_n.shape
    tm = _tile(m, 768, 640, 512, 256, 128)
    assert QK_NOPE_DIM == LANES and V_HEAD_DIM == LANES
    return pl.pallas_call(
        _kv_kernel,
        grid=(m // tm,),
        in_specs=[pl.BlockSpec((tm, k), lambda i: (i, 0)),
                  pl.BlockSpec((tm, LANES), lambda i: (i, 0)),
                  pl.BlockSpec((None,) + w_kvb.shape[1:], lambda i: (layer, 0, 0))],
        out_specs=[pl.BlockSpec((tm, N_HEADS * HEAD_W), lambda i: (i, 0)),
                   pl.BlockSpec((tm, N_HEADS * HEAD_W), lambda i: (i, 0))],
        out_shape=[jax.ShapeDtypeStruct((m, N_HEADS * HEAD_W), BF16),
                   jax.ShapeDtypeStruct((m, N_HEADS * HEAD_W), BF16)],
        compiler_params=_cparams(1),
        name="kv_proj",
    )(ckv_n, kr, w_kvb)


def _attn_kernel(q_ref, k_ref, v_ref, o_ref, sa_ref, sb_ref, *, tk, n_chunks):
    q = q_ref[...]
    tq = q.shape[0]

    def scores(c):
        return _dot_t(q, k_ref[c * tk:(c + 1) * tk, :])

    def absorb(s_ref, c, carry):
        m, acc = carry
        s = s_ref[...]
        m_new = jnp.maximum(m, jnp.max(s, axis=-1, keepdims=True))
        p = jnp.exp2(s - m_new).astype(v_ref.dtype)
        acc = jnp.exp2(m - m_new) * acc + _dot(p, v_ref[c * tk:(c + 1) * tk, :])
        return m_new, acc

    s_refs = (sa_ref, sb_ref)
    carry = (jnp.full((tq, 1), -jnp.inf, F32), jnp.zeros((tq, HEAD_W), F32))
    sa_ref[...] = scores(0)
    for c in range(n_chunks):
        if c + 1 < n_chunks:
            s_refs[(c + 1) % 2][...] = scores(c + 1)
        carry = absorb(s_refs[c % 2], c, carry)
    _, acc = carry
    o_ref[...] = (acc[:, :V_HEAD_DIM] / acc[:, V_HEAD_DIM:V_HEAD_DIM + 1]).astype(o_ref.dtype)


def _attention(q, k, v, q_row0, n_q, k_row0, n_k):
    tq = _tile(n_q, 1024, 512, 256, 128)
    tk = _tile(n_k, 768, 640, 512, 384, 256, 128)
    assert q_row0 % tq == 0 and k_row0 % n_k == 0
    q_blk0 = q_row0 // tq
    k_blk = k_row0 // n_k
    return pl.pallas_call(
        functools.partial(_attn_kernel, tk=tk, n_chunks=n_k // tk),
        grid=(N_HEADS, n_q // tq),
        in_specs=[pl.BlockSpec((tq, HEAD_W), lambda h, i: (q_blk0 + i, h)),
                  pl.BlockSpec((n_k, HEAD_W), lambda h, i: (k_blk, h)),
                  pl.BlockSpec((n_k, HEAD_W), lambda h, i: (k_blk, h))],
        out_specs=pl.BlockSpec((tq, V_HEAD_DIM), lambda h, i: (i, h)),
        out_shape=jax.ShapeDtypeStruct((n_q, N_HEADS * V_HEAD_DIM), BF16),
        scratch_shapes=[pltpu.VMEM((tq, tk), F32), pltpu.VMEM((tq, tk), F32)],
        compiler_params=_cparams(2),
        name="attention",
    )(q, k, v)


def _conv_gate_kernel(cx_ref, cb_ref, cc_ref, cxp_ref, ccp_ref, cxn_ref, ccn_ref, wc_ref, z_ref, *,
                      seg_starts, seg_ends, col_chunk):
    tm, width = z_ref.shape
    loc = lax.broadcasted_iota(jnp.int32, (tm, 1), 0)
    row = loc + pl.program_id(0) * tm
    first = functools.reduce(jnp.logical_or, [row == r for r in seg_starts])
    last = functools.reduce(jnp.logical_or, [row == r for r in seg_ends])
    for c0 in range(0, width, col_chunk):
        cs = slice(c0, c0 + col_chunk)
        u = cx_ref[:, cs].astype(F32) * cc_ref[:, cs].astype(F32)
        u_halo_prev = (cxp_ref[SUBLANES - 1:SUBLANES, cs].astype(F32)
                       * ccp_ref[SUBLANES - 1:SUBLANES, cs].astype(F32))
        u_halo_next = cxn_ref[0:1, cs].astype(F32) * ccn_ref[0:1, cs].astype(F32)
        u_prev = jnp.where(loc == 0, u_halo_prev, pltpu.roll(u, 1, 0))
        u_prev = jnp.where(first, 0.0, u_prev)
        u_next = jnp.where(loc == tm - 1, u_halo_next, pltpu.roll(u, tm - 1, 0))
        u_next = jnp.where(last, 0.0, u_next)
        conv = wc_ref[0:1, cs] * u_prev + wc_ref[1:2, cs] * u + wc_ref[2:3, cs] * u_next
        z_ref[:, cs] = (cb_ref[:, cs].astype(F32) * conv).astype(z_ref.dtype)


def _conv_gate(p, w_conv, rows, s_len, m_len):
    width = w_conv.shape[1]
    tm = _tile(rows, 768, 512, 384, 256, 128, 64)
    hb = tm // SUBLANES
    last_hb = p.shape[0] // SUBLANES - 1

    def prev_map(col):
        return lambda i: (jnp.maximum(i * hb - 1, 0), col)

    def next_map(col):
        return lambda i: (jnp.minimum((i + 1) * hb, last_hb), col)

    kern = functools.partial(_conv_gate_kernel, seg_starts=(0, s_len), seg_ends=(s_len - 1, m_len - 1),
                             col_chunk=_tile(width, 512, LANES))
    return pl.pallas_call(
        kern,
        grid=(rows // tm,),
        in_specs=[pl.BlockSpec((tm, width), lambda i: (i, 0)),
                  pl.BlockSpec((tm, width), lambda i: (i, 1)),
                  pl.BlockSpec((tm, width), lambda i: (i, 2)),
                  pl.BlockSpec((SUBLANES, width), prev_map(0)),
                  pl.BlockSpec((SUBLANES, width), prev_map(2)),
                  pl.BlockSpec((SUBLANES, width), next_map(0)),
                  pl.BlockSpec((SUBLANES, width), next_map(2)),
                  pl.BlockSpec((CONV_K, width), lambda i: (0, 0))],
        out_specs=pl.BlockSpec((tm, width), lambda i: (i, 0)),
        out_shape=jax.ShapeDtypeStruct((rows, width), BF16),
        compiler_params=_cparams(1),
        name="conv_gate",
    )(p, p, p, p, p, p, p, w_conv)


def _merge_kernel(attn_ref, z_ref, sga_ref, sgb_ref, woa_ref, wob_ref, o_ref):
    o_a = _dot(attn_ref[...], woa_ref[...])
    o_b = _dot(z_ref[...], wob_ref[...])
    o_ref[...] = (sga_ref[...].astype(F32) * o_a + sgb_ref[...].astype(F32) * o_b).astype(o_ref.dtype)


def _merge(attn, z, p, w_oa, w_ob, layer, rows):
    attn_w = w_oa.shape[1]
    _, width, d = w_ob.shape
    tm = _tile(rows, 768, 512, 384, 256, 128, 64)
    tn = _tile(d, 1024, 512, 256, LANES)
    ga_blk = 3 * width // tn
    gb_blk = (3 * width + d) // tn
    return pl.pallas_call(
        _merge_kernel,
        grid=(d // tn, rows // tm),
        in_specs=[pl.BlockSpec((tm, attn_w), lambda j, i: (i, 0)),
                  pl.BlockSpec((tm, width), lambda j, i: (i, 0)),
                  pl.BlockSpec((tm, tn), lambda j, i: (i, ga_blk + j)),
                  pl.BlockSpec((tm, tn), lambda j, i: (i, gb_blk + j)),
                  pl.BlockSpec((None, attn_w, tn), lambda j, i: (layer, 0, j)),
                  pl.BlockSpec((None, width, tn), lambda j, i: (layer, 0, j))],
        out_specs=pl.BlockSpec((tm, tn), lambda j, i: (i, j)),
        out_shape=jax.ShapeDtypeStruct((rows, d), BF16),
        compiler_params=_cparams(2),
        name="merge",
    )(attn, z, p, p, w_oa, w_ob)


def _mm_res_kernel(a_ref, w_ref, x_ref, gt_ref, o_ref, *, n_lat):
    tm = a_ref.shape[0]
    row = lax.broadcasted_iota(jnp.int32, (tm, 1), 0) + pl.program_id(1) * tm
    gate = jnp.where(row >= n_lat, gt_ref[1:2, :], gt_ref[0:1, :])
    o_ref[...] = x_ref[...] + gate * _dot(a_ref[...], _bf16(w_ref[...]))


def _matmul_residual(a, w, layer, x, mods, gate_blk, rows, n_lat):
    _, k, n = w.shape
    tm = _tile(rows, 768, 640, 512, 256, 128)
    tn = _tile(n, 1024 if k <= n else 512, 512, 256, LANES)
    nt = n // tn
    return pl.pallas_call(
        functools.partial(_mm_res_kernel, n_lat=n_lat),
        grid=(nt, rows // tm),
        in_specs=[pl.BlockSpec((tm, k), lambda j, i: (i, 0)),
                  pl.BlockSpec((None, k, tn), lambda j, i: (layer, 0, j)),
                  pl.BlockSpec((tm, tn), lambda j, i: (i, j)),
                  pl.BlockSpec((2, tn), lambda j, i: (0, gate_blk * nt + j))],
        out_specs=pl.BlockSpec((tm, tn), lambda j, i: (i, j)),
        out_shape=jax.ShapeDtypeStruct((rows, n), F32),
        compiler_params=_cparams(2),
        name="matmul_residual",
    )(a, w, x, mods)


def _block_in_use(eid_ref):
    return pl.program_id(1) < eid_ref[pl.num_programs(1)]


def _glu_kernel(eid_ref, a_ref, wg_ref, wu_ref, o_ref):
    @pl.when(_block_in_use(eid_ref))
    def _():
        a = _bf16(a_ref[...])
        gate = _dot(a, _bf16(wg_ref[0]))
        up = _dot(a, _bf16(wu_ref[0]))
        o_ref[...] = (gate * jax.nn.sigmoid(gate) * up).astype(o_ref.dtype)

    @pl.when(jnp.logical_not(_block_in_use(eid_ref)))
    def _():
        o_ref[...] = jnp.zeros_like(o_ref)


def _glu(a, w_gate, w_up, eid, tm):
    r, k = a.shape
    f = w_gate.shape[2]
    tn = _tile(f, 512, 1408, 256, LANES)
    grid_spec = pltpu.PrefetchScalarGridSpec(
        num_scalar_prefetch=1,
        grid=(f // tn, r // tm),
        in_specs=[pl.BlockSpec((tm, k), lambda j, i, e: (i, 0)),
                  pl.BlockSpec((1, k, tn), lambda j, i, e: (e[i], 0, j)),
                  pl.BlockSpec((1, k, tn), lambda j, i, e: (e[i], 0, j))],
        out_specs=pl.BlockSpec((tm, tn), lambda j, i, e: (i, j)),
    )
    return pl.pallas_call(
        _glu_kernel,
        grid_spec=grid_spec,
        out_shape=jax.ShapeDtypeStruct((r, f), BF16),
        compiler_params=_cparams(2),
        name="glu",
    )(eid, a, w_gate, w_up)


def _down_kernel(eid_ref, a_ref, w_ref, o_ref):
    @pl.when(_block_in_use(eid_ref))
    def _():
        o_ref[...] = _dot(a_ref[...], _bf16(w_ref[0]))

    @pl.when(jnp.logical_not(_block_in_use(eid_ref)))
    def _():
        o_ref[...] = jnp.zeros_like(o_ref)


def _down_grouped(a, w_down, eid, tm):
    r, f = a.shape
    d = w_down.shape[2]
    tn = _tile(d, 1024, 512, 256, LANES)
    grid_spec = pltpu.PrefetchScalarGridSpec(
        num_scalar_prefetch=1,
        grid=(d // tn, r // tm),
        in_specs=[pl.BlockSpec((tm, f), lambda j, i, e: (i, 0)),
                  pl.BlockSpec((1, f, tn), lambda j, i, e: (e[i], 0, j))],
        out_specs=pl.BlockSpec((tm, tn), lambda j, i, e: (i, j)),
    )
    return pl.pallas_call(
        _down_kernel,
        grid_spec=grid_spec,
        out_shape=jax.ShapeDtypeStruct((r, d), F32),
        compiler_params=_cparams(2),
        name="down_grouped",
    )(eid, a, w_down)


def _row_copies(idx_ref, idx_base, idx_stride, src_ref, dst_ref, sem, n_rows, start):
    if not start:
        pltpu.make_async_copy(src_ref.at[pl.ds(0, n_rows)], dst_ref, sem).wait()
        return

    def body(r2, c):
        for prio in range(2):
            r = 2 * r2 + prio
            row = idx_ref[idx_base + r * idx_stride]
            pltpu.make_async_copy(src_ref.at[pl.ds(row, 1)], dst_ref.at[pl.ds(r, 1)], sem).start(priority=prio)
        return c

    assert n_rows % 2 == 0 and dst_ref.shape[0] == n_rows
    lax.fori_loop(0, n_rows // 2, body, 0, unroll=DMA_ISSUE_UNROLL // 2)


def _prefetched_gather(copies):
    i = pl.program_id(0)
    slot = i % 2

    @pl.when(i == 0)
    def _():
        copies(0, 0, True)

    @pl.when(i + 1 < pl.num_programs(0))
    def _():
        copies(i + 1, 1 - slot, True)

    copies(i, slot, False)
    return slot


def _gather_rows_kernel(idx_ref, src_ref, o_ref, g_ref, sem):
    tm = o_ref.shape[0]

    def copies(step, slot, start):
        _row_copies(idx_ref, step * tm, 1, src_ref, g_ref.at[slot], sem.at[slot], tm, start)

    slot = _prefetched_gather(copies)
    o_ref[...] = g_ref[slot].astype(o_ref.dtype)


def _gather_rows(src, idx, tm, out_dtype):
    n = idx.shape[0]
    w = src.shape[1]
    grid_spec = pltpu.PrefetchScalarGridSpec(
        num_scalar_prefetch=1,
        grid=(n // tm,),
        in_specs=[pl.BlockSpec(memory_space=pl.ANY)],
        out_specs=pl.BlockSpec((tm, w), lambda i, idx_ref: (i, 0)),
        scratch_shapes=[pltpu.VMEM((2, tm, w), src.dtype), pltpu.SemaphoreType.DMA((2,))],
    )
    return pl.pallas_call(
        _gather_rows_kernel,
        grid_spec=grid_spec,
        out_shape=jax.ShapeDtypeStruct((n, w), out_dtype),
        compiler_params=_cparams(1),
        name="gather_rows",
    )(idx, src)


def _final_kernel(dest_ref, x_ref, y_hbm_ref, gate_ref, gt_ref, g_ref, o_ref, y_ref, sem):
    tm = x_ref.shape[0]

    def copies(step, slot, start):
        for kk in range(TOP_K):
            _row_copies(dest_ref, step * tm * TOP_K + kk, TOP_K, y_hbm_ref, y_ref.at[slot, kk], sem.at[slot],
                        tm, start)

    slot = _prefetched_gather(copies)
    gates = gate_ref[...]
    y = gates[:, 0:1] * y_ref[slot, 0] + gates[:, 1:2] * y_ref[slot, 1]
    x = x_ref[...] + gt_ref[0:1, :] * y
    o_ref[...] = x * lax.rsqrt(jnp.mean(x * x, axis=-1, keepdims=True) + NORM_EPS) * g_ref[...]


def _combine_final(x, ybuf, dest, gates, mods, gate_blk, g_final):
    r, d = x.shape
    tm = _tile(r, 256, 128, 64, SUBLANES)
    grid_spec = pltpu.PrefetchScalarGridSpec(
        num_scalar_prefetch=1,
        grid=(r // tm,),
        in_specs=[pl.BlockSpec((tm, d), lambda i, dest_ref: (i, 0)),
                  pl.BlockSpec(memory_space=pl.ANY),
                  pl.BlockSpec((tm, LANES), lambda i, dest_ref: (i, 0)),
                  pl.BlockSpec((2, d), lambda i, dest_ref: (0, gate_blk)),
                  pl.BlockSpec((1, d), lambda i, dest_ref: (0, 0))],
        out_specs=pl.BlockSpec((tm, d), lambda i, dest_ref: (i, 0)),
        scratch_shapes=[pltpu.VMEM((2, TOP_K, tm, d), F32), pltpu.SemaphoreType.DMA((2,))],
    )
    return pl.pallas_call(
        _final_kernel,
        grid_spec=grid_spec,
        out_shape=jax.ShapeDtypeStruct((r, d), F32),
        compiler_params=_cparams(1),
        name="combine_final",
    )(dest, x, ybuf, gates, mods, g_final.reshape(1, d))


def _rope_tables(s_len, c_len):
    quarter = QK_ROPE_DIM // 4
    tok = jnp.arange(s_len + c_len, dtype=jnp.int32)[:, None]
    lane = jnp.arange(LANES, dtype=jnp.int32)[None, :]
    group = lane // quarter
    pos = jnp.where(group < 2, tok // GRID_W, tok % GRID_W).astype(F32)
    inv_freq = jnp.power(ROPE_THETA, -(2 * (lane % quarter)).astype(F32) / (QK_ROPE_DIM // 2))
    ang = jnp.where(tok < s_len, pos * inv_freq, 0.0)
    live = group < 4
    cos = jnp.where(live, jnp.cos(ang), 0.0)
    sin = jnp.where(live, jnp.where(group % 2 == 0, -jnp.sin(ang), jnp.sin(ang)), 0.0)
    return cos, sin


def _rope_swap_perm():
    q = QK_ROPE_DIM // 4
    return jnp.concatenate([jnp.arange(q, 2 * q), jnp.arange(0, q), jnp.arange(3 * q, 4 * q), jnp.arange(2 * q, 3 * q)])


def _layer_weights(w_in_t, w_qb_all, li):
    d = w_in_t.shape[2]
    perm = _rope_swap_perm()
    kr_lo = Q_LORA_RANK + KV_LORA_RANK
    kr_hi = kr_lo + QK_ROPE_DIM
    w_kr = w_in_t[li, kr_lo:kr_hi]
    zpad = jnp.zeros((LANES - QK_ROPE_DIM, d), w_in_t.dtype)
    w_a = jnp.concatenate([w_kr, zpad, w_kr[perm], zpad], axis=0)
    w_qb = w_qb_all[li]
    qb = w_qb.reshape(Q_LORA_RANK, N_HEADS, QK_NOPE_DIM + QK_ROPE_DIM)
    q_rope = qb[:, :, QK_NOPE_DIM:]
    assert 2 * QK_ROPE_DIM == LANES
    w_q = jnp.concatenate([qb, q_rope[:, :, perm]], axis=2).reshape(Q_LORA_RANK, N_HEADS * HEAD_W)
    return w_a.astype(BF16), w_q.astype(BF16)


def _moe_slots(idx, n_blocks):
    e_flat = idx.reshape(-1)
    onehot = (e_flat[:, None] == jnp.arange(N_EXPERTS)[None, :]).astype(jnp.int32)
    csum = jnp.cumsum(onehot, axis=0)
    counts = csum[-1]
    rank = jnp.sum((csum - onehot) * onehot, axis=1)
    padded = (counts + MOE_ROWS - 1) // MOE_ROWS * MOE_ROWS
    p_end = jnp.cumsum(padded)
    p_start = p_end - padded
    dest = jnp.sum(onehot * p_start[None, :], axis=1) + rank
    blk_lo = jnp.arange(n_blocks) * MOE_ROWS
    block_expert = jnp.minimum(jnp.sum((blk_lo[:, None] >= p_end[None, :]).astype(jnp.int32), axis=1),
                               N_EXPERTS - 1)
    blocks_in_use = p_end[-1:] // MOE_ROWS
    return dest.astype(jnp.int32), jnp.concatenate([block_expert, blocks_in_use]).astype(jnp.int32)


def kernel(x, c, ctx, c_ctx, w_ada, b_ada, g_attn, w_in, g_qa, w_qb, g_kva, w_kvb, w_conv, w_oa, w_ob, w_o,
           g_ffn, w_gate_dense, w_up_dense, w_down_dense, w_router, w_gate_exp, w_up_exp, w_down_exp, g_final):
    _, s_len, d = x.shape
    c_len = ctx.shape[1]
    m_len = s_len + c_len
    depth = w_in.shape[0]
    width = w_conv.shape[2]
    assert depth == 2, "supported stack: dense-FFN layer with context updates, then a final expert-FFN layer"

    mods_all = _ada_mod(c, c_ctx, w_ada, b_ada)
    cos_t, sin_t = _rope_tables(s_len, c_len)
    w_in_t = jnp.swapaxes(w_in, 1, 2)
    w_oa_bf, w_ob_bf, w_kvb_bf = w_oa.astype(BF16), w_ob.astype(BF16), w_kvb.astype(BF16)

    for li in range(depth):
        last = li == depth - 1
        mods = mods_all[li]
        w_a, w_q = _layer_weights(w_in_t, w_qb, li)
        g_a = jnp.concatenate([g_qa[li], g_kva[li]]).reshape(1, -1)
        rows = s_len if last else m_len

        if li == 0:
            h, xa = _norm_modulate_join(x[0], ctx[0], g_attn[li], mods, 0, 1, BF16)
        else:
            h = _norm_modulate(xa, g_attn[li], mods, 0, 1, s_len, BF16)
        qa_n, ckv_n, kr = _in_proj_a(h, w_in_t, li, w_a, g_a, cos_t, sin_t)
        p = _in_proj_b(h, w_in_t, li, Q_LORA_RANK + KV_LORA_RANK + QK_ROPE_DIM, 3 * width)
        q = _q_proj(qa_n, w_q, cos_t, sin_t, rows)
        k, v = _kv_proj(ckv_n, kr, w_kvb_bf, li)
        attn = _attention(q, k, v, 0, s_len, 0, m_len)
        if not last:
            attn = jnp.concatenate([attn, _attention(q, k, v, s_len, c_len, s_len, c_len)], axis=0)
        z = _conv_gate(p, w_conv[li], rows, s_len, m_len)
        merged = _merge(attn, z, p, w_oa_bf, w_ob_bf, li, rows)
        xa = _matmul_residual(merged, w_o, li, xa, mods, 2, rows, s_len)

        j = li // 2
        if li % 2 == 0:
            h2 = _norm_modulate(xa, g_ffn[li], mods, 3, 4, s_len, BF16)
            tm = _tile(rows, 1408, 768, 640, 512, 256, 128)
            eid = jnp.zeros((rows // tm + 1,), jnp.int32).at[-1].set(rows // tm)
            hid = _glu(h2, w_gate_dense[j][None], w_up_dense[j][None], eid, tm)
            xa = _matmul_residual(hid, w_down_dense.astype(BF16), j, xa, mods, 5, rows, s_len)
        else:
            h2, idx, gates = _norm_modulate_route(xa, g_ffn[li], mods, 3, 4, w_router[j])
            n_assign = rows * TOP_K
            n_blocks = -(-n_assign // MOE_ROWS) + N_EXPERTS
            dest, block_expert = _moe_slots(idx[:, :TOP_K], n_blocks)
            tok = jnp.arange(n_assign, dtype=jnp.int32) // TOP_K
            pad_tok = jnp.arange(n_blocks * MOE_ROWS, dtype=jnp.int32) % rows
            slot_tok = pad_tok.at[dest].set(tok, unique_indices=True)
            buf = _gather_rows(h2, slot_tok, GATHER_ROWS, BF16)
            hid = _glu(buf, w_gate_exp[j], w_up_exp[j], block_expert, MOE_ROWS)
            ybuf = _down_grouped(hid, w_down_exp[j], block_expert, MOE_ROWS)
            return _combine_final(xa, ybuf, dest, gates, mods, 5, g_final)[None]
    raise AssertionError("unreachable: the final layer returns")
```
